```python
import jax, jax.numpy as jnp
from jax import lax
import numpy as np

D_MODEL = 1024
BATCH = 8
SEQ = 2048
DEPTH = 1

DSW_GROUPS = ((128, 1), (512, 4), (2048, 16))
N_DSW_GROUPS = 3
DSW_HEADS_PER_GROUP = 4
DSW_HEAD_DIM = 64
DSW_BLOCK = 128
DSW_QKV_WIDTH = N_DSW_GROUPS * DSW_HEADS_PER_GROUP * DSW_HEAD_DIM
DSW_OUT_WIDTH = DSW_HEADS_PER_GROUP * DSW_HEAD_DIM
ROPE_THETA = 10000.0

GDN_HEADS = 8
GDN_HEAD_DIM = 128
GDN_WIDTH = GDN_HEADS * GDN_HEAD_DIM
GDN_CONV = 4
GDN_CHUNK = 64

D_FF = 2816
EPS = 1e-6

IN_WIDTHS = (DSW_QKV_WIDTH, DSW_QKV_WIDTH, DSW_QKV_WIDTH,
             3 * GDN_WIDTH,
             GDN_HEADS, GDN_HEADS,
             GDN_WIDTH,
             D_MODEL, D_MODEL)
D_IN = sum(IN_WIDTHS)

kernel_name = "hybrid_dilated_swa_gated_deltanet_macaron"


def rmsnorm(x, g):
    xf = x.astype(jnp.float32)
    y = xf * lax.rsqrt(jnp.mean(xf * xf, axis=-1, keepdims=True) + EPS)
    return (y * g.astype(jnp.float32)).astype(x.dtype)


def swiglu(x, w_gate, w_up, w_down):
    return (jax.nn.silu(x @ w_gate) * (x @ w_up)) @ w_down


def rope(x, pos):
    half = x.shape[-1] // 2
    inv_freq = ROPE_THETA ** (-jnp.arange(half, dtype=jnp.float32) / half)
    ang = pos.astype(jnp.float32)[:, None] * inv_freq[None, :]
    cos = jnp.cos(ang)[None, :, None, :]
    sin = jnp.sin(ang)[None, :, None, :]
    xf = x.astype(jnp.float32)
    x1, x2 = xf[..., :half], xf[..., half:]
    return jnp.concatenate([x1 * cos - x2 * sin, x2 * cos + x1 * sin], axis=-1).astype(x.dtype)


def dilated_window_attention(q, k, v, window, dilation):
    B, S, H, Dh = q.shape
    L = S // dilation
    span = window // dilation
    nb = -(-L // DSW_BLOCK)
    Lp = nb * DSW_BLOCK

    def to_blocks(t):
        t = t.reshape(B, L, dilation, H, Dh).transpose(0, 2, 1, 3, 4)
        t = jnp.pad(t, ((0, 0), (0, 0), (0, Lp - L), (0, 0), (0, 0)))
        return t.reshape(B, dilation, nb, DSW_BLOCK, H, Dh)

    def with_prev(t):
        prev = jnp.pad(t, ((0, 0), (0, 0), (1, 0), (0, 0), (0, 0), (0, 0)))[:, :, :-1]
        return jnp.concatenate([prev, t], axis=3)

    qb = to_blocks(q)
    kw = with_prev(to_blocks(k))
    vw = with_prev(to_blocks(v))
    s = jnp.einsum('brnqhd,brnkhd->brnhqk', qb, kw).astype(jnp.float32) * (Dh ** -0.5)
    qi = jnp.arange(nb)[:, None] * DSW_BLOCK + jnp.arange(DSW_BLOCK)[None, :]
    ki = (jnp.arange(nb)[:, None] - 1) * DSW_BLOCK + jnp.arange(2 * DSW_BLOCK)[None, :]
    dist = qi[:, :, None] - ki[:, None, :]
    valid = (dist >= 0) & (dist <= span) & (ki[:, None, :] >= 0)
    s = jnp.where(valid[None, None, :, None], s, -jnp.inf)
    m = jnp.max(s, axis=-1, keepdims=True)
    p = jnp.exp(s - m)
    l = jnp.sum(p, axis=-1, keepdims=True)
    o = jnp.einsum('brnhqk,brnkhd->brnqhd', (p / l).astype(v.dtype), vw)
    lse = (m + jnp.log(l))[..., 0]
    o = o.reshape(B, dilation, Lp, H, Dh)[:, :, :L].transpose(0, 2, 1, 3, 4).reshape(B, S, H, Dh)
    lse = lse.transpose(0, 1, 2, 4, 3).reshape(B, dilation, Lp, H)[:, :, :L]
    lse = lse.transpose(0, 2, 1, 3).reshape(B, S, H)
    return o, lse


def causal_depthwise_conv(x, w):
    K = w.shape[0]
    S = x.shape[1]
    xp = jnp.pad(x, ((0, 0), (K - 1, 0), (0, 0)))
    y = xp[:, 0:S] * w[0]
    for i in range(1, K):
        y = y + xp[:, i:i + S] * w[i]
    return y


def l2norm(x):
    return x * lax.rsqrt(jnp.sum(x * x, axis=-1, keepdims=True) + EPS)


def gated_delta_rule(q, k, v, beta, g):
    B, S, H, Dk = q.shape
    Dv = v.shape[-1]
    C = GDN_CHUNK
    n = S // C
    q = q * (Dk ** -0.5)
    qc, kc, vc = [t.transpose(0, 2, 1, 3).reshape(B, H, n, C, t.shape[-1]) for t in (q, k, v)]
    bc, gc = [t.transpose(0, 2, 1).reshape(B, H, n, C) for t in (beta, g)]
    gcum = jnp.cumsum(gc, axis=-1)
    tri = jnp.tril(jnp.ones((C, C), dtype=bool))
    tri_strict = jnp.tril(jnp.ones((C, C), dtype=bool), -1)
    decay = jnp.exp(jnp.where(tri, gcum[..., :, None] - gcum[..., None, :], -jnp.inf))
    kbeta = kc * bc[..., None]
    vbeta = vc * bc[..., None]
    M = jnp.where(tri_strict, jnp.einsum('bhnid,bhnjd->bhnij', kbeta, kc) * decay, 0.0)
    A = M + jnp.eye(C, dtype=M.dtype)
    rhs = jnp.concatenate([vbeta, kbeta * jnp.exp(gcum)[..., None]], axis=-1)
    sol = lax.linalg.triangular_solve(A, rhs, left_side=True, lower=True, unit_diagonal=True)
    u, w = sol[..., :Dv], sol[..., Dv:]
    a_qk = jnp.einsum('bhnid,bhnjd->bhnij', qc, kc) * decay
    q_dec = qc * jnp.exp(gcum)[..., None]
    g_last = gcum[..., -1]
    k_dec = kc * jnp.exp(g_last[..., None] - gcum)[..., None]
    state_dec = jnp.exp(g_last)

    def step(state, inp):
        q_i, k_i, u_i, w_i, a_i, sd_i = inp
        v_new = u_i - jnp.einsum('bhck,bhkv->bhcv', w_i, state)
        o_i = jnp.einsum('bhck,bhkv->bhcv', q_i, state) + jnp.einsum('bhij,bhjv->bhiv', a_i, v_new)
        state = state * sd_i[..., None, None] + jnp.einsum('bhck,bhcv->bhkv', k_i, v_new)
        return state, o_i

    xs = tuple(jnp.moveaxis(t, 2, 0) for t in (q_dec, k_dec, u, w, a_qk, state_dec))
    state0 = jnp.zeros((B, H, Dk, Dv), dtype=jnp.float32)
    _, o = lax.scan(step, state0, xs)
    return jnp.moveaxis(o, 0, 2).reshape(B, H, S, Dv).transpose(0, 2, 1, 3)


def hybrid_mixer(h, pos, w_in, gdn_conv_w, gdn_a_log, gdn_dt_bias, gdn_out_norm,
                 w_branch_a, w_branch_b, w_out):
    B, S, _ = h.shape
    offsets = []
    acc = 0
    for wdt in IN_WIDTHS[:-1]:
        acc += wdt
        offsets.append(acc)
    qa, ka, va, qkv_b, beta_raw, decay_raw, gdn_gate, gate_a, gate_b = jnp.split(h @ w_in, offsets, axis=-1)

    n_a = N_DSW_GROUPS * DSW_HEADS_PER_GROUP
    qa = rope(qa.reshape(B, S, n_a, DSW_HEAD_DIM), pos)
    ka = rope(ka.reshape(B, S, n_a, DSW_HEAD_DIM), pos)
    va = va.reshape(B, S, n_a, DSW_HEAD_DIM)
    outs, lses = [], []
    for gi, (window, dilation) in enumerate(DSW_GROUPS):
        sl = slice(gi * DSW_HEADS_PER_GROUP, (gi + 1) * DSW_HEADS_PER_GROUP)
        o, lse = dilated_window_attention(qa[:, :, sl], ka[:, :, sl], va[:, :, sl], window, dilation)
        outs.append(o)
        lses.append(lse)
    wts = jax.nn.softmax(jnp.stack(lses, axis=0), axis=0)
    ya = jnp.einsum('gbsh,gbshd->bshd', wts.astype(h.dtype), jnp.stack(outs, axis=0))
    ya = ya.reshape(B, S, DSW_OUT_WIDTH)

    qkv = jax.nn.silu(causal_depthwise_conv(qkv_b, gdn_conv_w)).astype(jnp.float32)
    qb, kb, vb = jnp.split(qkv, 3, axis=-1)
    qb = l2norm(qb.reshape(B, S, GDN_HEADS, GDN_HEAD_DIM))
    kb = l2norm(kb.reshape(B, S, GDN_HEADS, GDN_HEAD_DIM))
    vb = vb.reshape(B, S, GDN_HEADS, GDN_HEAD_DIM)
    beta = jax.nn.sigmoid(beta_raw.astype(jnp.float32))
    g = -jnp.exp(gdn_a_log.astype(jnp.float32)) * jax.nn.softplus(
        decay_raw.astype(jnp.float32) + gdn_dt_bias.astype(jnp.float32))
    ob = gated_delta_rule(qb, kb, vb, beta, g)
    ob = rmsnorm(ob, gdn_out_norm) * jax.nn.silu(
        gdn_gate.astype(jnp.float32).reshape(B, S, GDN_HEADS, GDN_HEAD_DIM))
    yb = ob.reshape(B, S, GDN_WIDTH).astype(h.dtype)

    merged = jax.nn.sigmoid(gate_a) * (ya @ w_branch_a) + jax.nn.sigmoid(gate_b) * (yb @ w_branch_b)
    return merged @ w_out


def setup_inputs(seed: int = 0) -> dict:
    key = jax.random.key(seed)
    ks = jax.random.split(key, 20)
    f32 = jnp.float32

    def nrm(k, shape, fan_in):
        return jax.random.normal(k, shape, f32) * (fan_in ** -0.5)

    def gain(k, shape):
        return 1.0 + 0.01 * jax.random.normal(k, shape, f32)

    dt = jnp.exp(jax.random.uniform(ks[9], (DEPTH, GDN_HEADS), f32, np.log(1e-3), np.log(1e-1)))
    return {
        "x": jax.random.normal(ks[0], (BATCH, SEQ, D_MODEL), f32),
        "ffn1_norm": gain(ks[1], (DEPTH, D_MODEL)),
        "ffn1_w_gate": nrm(ks[2], (DEPTH, D_MODEL, D_FF), D_MODEL),
        "ffn1_w_up": nrm(ks[3], (DEPTH, D_MODEL, D_FF), D_MODEL),
        "ffn1_w_down": nrm(ks[4], (DEPTH, D_FF, D_MODEL), D_FF),
        "mix_norm": gain(ks[5], (DEPTH, D_MODEL)),
        "w_in": nrm(ks[6], (DEPTH, D_MODEL, D_IN), D_MODEL),
        "gdn_conv_w": nrm(ks[7], (DEPTH, GDN_CONV, 3 * GDN_WIDTH), GDN_CONV),
        "gdn_a_log": jnp.log(jax.random.uniform(ks[8], (DEPTH, GDN_HEADS), f32, 1.0, 16.0)),
        "gdn_dt_bias": dt + jnp.log(-jnp.expm1(-dt)),
        "gdn_out_norm": gain(ks[10], (DEPTH, GDN_HEAD_DIM)),
        "w_branch_a": nrm(ks[11], (DEPTH, DSW_OUT_WIDTH, D_MODEL), DSW_OUT_WIDTH),
        "w_branch_b": nrm(ks[12], (DEPTH, GDN_WIDTH, D_MODEL), GDN_WIDTH),
        "w_out": nrm(ks[13], (DEPTH, D_MODEL, D_MODEL), D_MODEL),
        "ffn2_norm": gain(ks[14], (DEPTH, D_MODEL)),
        "ffn2_w_gate": nrm(ks[15], (DEPTH, D_MODEL, D_FF), D_MODEL),
        "ffn2_w_up": nrm(ks[16], (DEPTH, D_MODEL, D_FF), D_MODEL),
        "ffn2_w_down": nrm(ks[17], (DEPTH, D_FF, D_MODEL), D_FF),
        "final_norm": gain(ks[18], (D_MODEL,)),
    }


def reference(x, ffn1_norm, ffn1_w_gate, ffn1_w_up, ffn1_w_down, mix_norm, w_in, gdn_conv_w,
              gdn_a_log, gdn_dt_bias, gdn_out_norm, w_branch_a, w_branch_b, w_out,
              ffn2_norm, ffn2_w_gate, ffn2_w_up, ffn2_w_down, final_norm):
    pos = jnp.arange(x.shape[1])
    for layer in range(DEPTH):
        x = x + 0.5 * swiglu(rmsnorm(x, ffn1_norm[layer]),
                             ffn1_w_gate[layer], ffn1_w_up[layer], ffn1_w_down[layer])
        h = rmsnorm(x, mix_norm[layer])
        x = x + hybrid_mixer(h, pos, w_in[layer], gdn_conv_w[layer], gdn_a_log[layer],
                             gdn_dt_bias[layer], gdn_out_norm[layer], w_branch_a[layer],
                             w_branch_b[layer], w_out[layer])
        x = x + 0.5 * swiglu(rmsnorm(x, ffn2_norm[layer]),
                             ffn2_w_gate[layer], ffn2_w_up[layer], ffn2_w_down[layer])
    return rmsnorm(x, final_norm)
```

```python
import functools

import jax
import jax.numpy as jnp
from jax import lax
from jax.experimental import pallas as pl
from jax.experimental.pallas import tpu as pltpu

F32 = jnp.float32
BF16 = jnp.bfloat16

D_MODEL = 1024
D_FF = 2816
EPS = 1e-6

ATT_GROUPS = ((128, 1), (512, 4), (2048, 16))
ATT_HEADS_PER_GROUP = 4
ATT_HEAD_DIM = 64
ATT_BLOCK = 128
ATT_GROUP_WIDTH = ATT_HEADS_PER_GROUP * ATT_HEAD_DIM
ATT_QKV_WIDTH = len(ATT_GROUPS) * ATT_GROUP_WIDTH
ROPE_THETA = 10000.0

GDN_HEADS = 8
GDN_HEAD_DIM = 128
GDN_WIDTH = GDN_HEADS * GDN_HEAD_DIM
GDN_CONV = 4
GDN_CHUNK = 64

LANES = 128
SUBLANES = 8
VMEM_LIMIT_BYTES = 56 * 1024 * 1024

W_MAIN_WIDTH = 3 * ATT_QKV_WIDTH + 3 * GDN_WIDTH
W_BD_OFFSET = W_MAIN_WIDTH
W_GATES_OFFSET = W_MAIN_WIDTH + 2 * GDN_HEADS
FFN_CHUNKS = ((0, 768), (768, 1536), (1536, 2304), (2304, 2816))
NEG_BIG = -1e30


def _resident(shape):
    nd = len(shape)
    return pl.BlockSpec(shape, lambda *_: (0,) * nd, pipeline_mode=pl.Buffered(1))


def _rmsnorm(x, g):
    return x * lax.rsqrt(jnp.mean(x * x, axis=-1, keepdims=True) + EPS) * g


def _sigmoid(x):
    return 1.0 / (1.0 + jnp.exp(-x))


def _dot(a, b):
    return jnp.dot(a, b, preferred_element_type=F32)


def _dot_nt(a, b):
    return lax.dot_general(a, b, (((1,), (1,)), ((), ())), preferred_element_type=F32)


def _dot_tn(a, b):
    return lax.dot_general(a, b, (((0,), (0,)), ((), ())), preferred_element_type=F32)


def _dot_exact(a, b):
    return jnp.dot(a, b, preferred_element_type=F32, precision=lax.Precision.HIGHEST)


def _swiglu_residual(x, g, wg_ref, wu_ref, wd_ref):
    h = _rmsnorm(x, g).astype(BF16)
    acc = x
    for lo, hi in FFN_CHUNKS:
        gate = _dot(h, wg_ref[:, lo:hi])
        up = _dot(h, wu_ref[:, lo:hi])
        act = (0.5 * gate * _sigmoid(gate) * up).astype(BF16)
        acc = acc + _dot(act, wd_ref[lo:hi, :])
    return acc


def _ffn_kernel(x_ref, g_ref, wg_ref, wu_ref, wd_ref, fin_ref, o_ref, *, final_norm):
    y = _swiglu_residual(x_ref[...], g_ref[...], wg_ref, wu_ref, wd_ref)
    if final_norm:
        y = _rmsnorm(y, fin_ref[...])
    o_ref[...] = y


def _ffn(x, norm_g, wg, wu, wd, fin_g, *, final_norm, tm):
    n = x.shape[0]
    row = pl.BlockSpec((tm, D_MODEL), lambda i: (i, 0))
    return pl.pallas_call(
        functools.partial(_ffn_kernel, final_norm=final_norm),
        grid=(n // tm,),
        in_specs=[row, _resident((1, D_MODEL)), _resident((D_MODEL, D_FF)), _resident((D_MODEL, D_FF)),
                  _resident((D_FF, D_MODEL)), _resident((1, D_MODEL))],
        out_specs=row,
        out_shape=jax.ShapeDtypeStruct((n, D_MODEL), F32),
        compiler_params=pltpu.CompilerParams(dimension_semantics=("arbitrary",),
                                             vmem_limit_bytes=VMEM_LIMIT_BYTES),
        name="ffn_final" if final_norm else "ffn",
    )(x, norm_g, wg, wu, wd, fin_g)


def _mixer_in_kernel(x_ref, g_ref, wm_ref, wbd_ref, convw_ref, alog_ref, dtb_ref, cos_ref, sin_ref,
                     a0_ref, a1_ref, a2_ref, qb_ref, kb_ref, vb_ref, bd_ref, conv_ref, *, tm):
    h = _rmsnorm(x_ref[0], g_ref[...]).astype(BF16)

    cos = cos_ref[...]
    sin = sin_ref[...]
    lane = lax.broadcasted_iota(jnp.int32, (1, LANES), 1)
    first_half = (lane % ATT_HEAD_DIM) < (ATT_HEAD_DIM // 2)
    att_refs = (a0_ref, a1_ref, a2_ref)
    ya = _dot(h, wm_ref[:, 0:3 * ATT_QKV_WIDTH])
    blocks_per_group = ATT_GROUP_WIDTH // LANES
    for part in range(3):
        for gi in range(len(ATT_GROUPS)):
            for j in range(blocks_per_group):
                col = part * ATT_QKV_WIDTH + gi * ATT_GROUP_WIDTH + j * LANES
                blk = ya[:, col:col + LANES]
                if part < 2:
                    swapped = jnp.where(first_half, pltpu.roll(blk, LANES - ATT_HEAD_DIM // 2, 1),
                                        pltpu.roll(blk, ATT_HEAD_DIM // 2, 1))
                    blk = blk * cos + swapped * sin
                if part == 0:
                    blk = blk * (ATT_HEAD_DIM ** -0.5)
                dst = part * ATT_GROUP_WIDTH + j * LANES
                att_refs[gi][0, :, dst:dst + LANES] = blk.astype(BF16)

    @pl.when(pl.program_id(1) == 0)
    def _():
        conv_ref[0:SUBLANES, :] = jnp.zeros((SUBLANES, 3 * GDN_WIDTH), F32)

    for part in range(3):
        lo = 3 * ATT_QKV_WIDTH + part * GDN_WIDTH
        conv_ref[SUBLANES:SUBLANES + tm, part * GDN_WIDTH:(part + 1) * GDN_WIDTH] = _dot(h, wm_ref[:, lo:lo + GDN_WIDTH])
    out_refs = (qb_ref, kb_ref, vb_ref)
    for part in range(3):
        for hh in range(GDN_HEADS):
            col = part * GDN_WIDTH + hh * GDN_HEAD_DIM
            acc = None
            for t in range(GDN_CONV):
                start = SUBLANES - (GDN_CONV - 1) + t
                term = conv_ref[start:start + tm, col:col + GDN_HEAD_DIM] * convw_ref[t:t + 1, col:col + GDN_HEAD_DIM]
                acc = term if acc is None else acc + term
            act = acc * _sigmoid(acc)
            if part < 2:
                act = act * lax.rsqrt(jnp.sum(act * act, axis=-1, keepdims=True) + EPS)
            if part == 0:
                act = act * (GDN_HEAD_DIM ** -0.5)
            out_refs[part][0, :, hh * GDN_HEAD_DIM:(hh + 1) * GDN_HEAD_DIM] = act
    conv_ref[0:SUBLANES, :] = conv_ref[tm:tm + SUBLANES, :]

    raw = _dot(h, wbd_ref[...])
    z = raw + dtb_ref[...]
    softplus = jnp.maximum(z, 0.0) + jnp.log1p(jnp.exp(-jnp.abs(z)))
    g = -jnp.exp(alog_ref[...]) * softplus
    bd_ref[0] = jnp.where(lane < GDN_HEADS, _sigmoid(raw), jnp.where(lane < 2 * GDN_HEADS, g, 0.0))


def _mixer_in(x1, norm_g, w_main, w_bd, conv_w, a_log, dt_bias, cos_t, sin_t, *, tm):
    b, s, _ = x1.shape
    tile = lambda w: pl.BlockSpec((1, tm, w), lambda bi, i: (bi, i, 0))
    table = pl.BlockSpec((tm, LANES), lambda bi, i: (i, 0))
    att = jax.ShapeDtypeStruct((b, s, 3 * ATT_GROUP_WIDTH), BF16)
    gdn = jax.ShapeDtypeStruct((b, s, GDN_WIDTH), F32)
    return pl.pallas_call(
        functools.partial(_mixer_in_kernel, tm=tm),
        grid=(b, s // tm),
        in_specs=[tile(D_MODEL), _resident((1, D_MODEL)), _resident((D_MODEL, W_MAIN_WIDTH)),
                  _resident((D_MODEL, LANES)), _resident((GDN_CONV, 3 * GDN_WIDTH)),
                  _resident((1, LANES)), _resident((1, LANES)), table, table],
        out_specs=[tile(3 * ATT_GROUP_WIDTH)] * 3 + [tile(GDN_WIDTH)] * 3 + [tile(LANES)],
        out_shape=[att] * 3 + [gdn] * 3 + [jax.ShapeDtypeStruct((b, s, LANES), F32)],
        scratch_shapes=[pltpu.VMEM((tm + SUBLANES, 3 * GDN_WIDTH), F32)],
        compiler_params=pltpu.CompilerParams(dimension_semantics=("arbitrary", "arbitrary"),
                                             vmem_limit_bytes=VMEM_LIMIT_BYTES),
        name="mixer_in",
    )(x1, norm_g, w_main, w_bd, conv_w, a_log, dt_bias, cos_t, sin_t)


def _attention_block(q, k, v, valid):
    lane = lax.broadcasted_iota(jnp.int32, (1, ATT_GROUP_WIDTH), 1)
    o = jnp.zeros((ATT_BLOCK, ATT_GROUP_WIDTH), F32)
    inv_l = jnp.zeros((ATT_BLOCK, ATT_GROUP_WIDTH), F32)
    lse = jnp.zeros((ATT_BLOCK, ATT_GROUP_WIDTH), F32)
    for hh in range(ATT_HEADS_PER_GROUP):
        in_head = (lane // ATT_HEAD_DIM) == hh
        keep = jnp.where(in_head, 1.0, 0.0).astype(BF16)
        s = _dot_nt(q * keep, k)
        s = jnp.where(valid, s, NEG_BIG)
        m = jnp.max(s, axis=-1, keepdims=True)
        p = jnp.exp(s - m)
        l = jnp.sum(p, axis=-1, keepdims=True)
        o = o + _dot(p.astype(BF16), v * keep)
        inv_l = jnp.where(in_head, 1.0 / l, inv_l)
        lse = jnp.where(in_head, m + jnp.log(l), lse)
    return o * inv_l, lse


def _attention_kernel(a0_ref, a1_ref, a2_ref, ya_ref, o0, l0, o1, l1, o2, l2, *, seq):
    in_refs = (a0_ref, a1_ref, a2_ref)
    o_refs = (o0, o1, o2)
    l_refs = (l0, l1, l2)
    qi = lax.broadcasted_iota(jnp.int32, (ATT_BLOCK, ATT_BLOCK), 0)
    kj = lax.broadcasted_iota(jnp.int32, (ATT_BLOCK, ATT_BLOCK), 1)
    causal = kj <= qi
    qi2 = lax.broadcasted_iota(jnp.int32, (ATT_BLOCK, 2 * ATT_BLOCK), 0)
    kj2 = lax.broadcasted_iota(jnp.int32, (ATT_BLOCK, 2 * ATT_BLOCK), 1)
    band = (kj2 >= qi2) & (kj2 - ATT_BLOCK <= qi2)

    for gi, (window, dil) in enumerate(ATT_GROUPS):
        assert window // dil == ATT_BLOCK
        length = seq // dil
        nblk = length // ATT_BLOCK
        src, o_ref, l_ref = in_refs[gi], o_refs[gi], l_refs[gi]
        wq = 3 * ATT_GROUP_WIDTH

        def store(o_ref, l_ref, dil, r, n, o, lse):
            if dil == 1:
                rows = pl.ds(pl.multiple_of(n * ATT_BLOCK, ATT_BLOCK), ATT_BLOCK)
            else:
                rows = pl.ds(n * ATT_BLOCK * dil + r, ATT_BLOCK, stride=dil)
            for half in range(ATT_GROUP_WIDTH // LANES):
                o_ref[half, rows, :] = o[:, half * LANES:(half + 1) * LANES]
                l_ref[half, rows, :] = lse[:, half * LANES:(half + 1) * LANES]

        for r in range(dil):
            base = r * wq
            q = src[0, 0:ATT_BLOCK, base:base + ATT_GROUP_WIDTH]
            k = src[0, 0:ATT_BLOCK, base + ATT_GROUP_WIDTH:base + 2 * ATT_GROUP_WIDTH]
            v = src[0, 0:ATT_BLOCK, base + 2 * ATT_GROUP_WIDTH:base + 3 * ATT_GROUP_WIDTH]
            o, lse = _attention_block(q, k, v, causal)
            store(o_ref, l_ref, dil, r, 0, o, lse)

            if nblk > 1:
                def body(n, carry, src=src, base=base, o_ref=o_ref, l_ref=l_ref, dil=dil, r=r):
                    qrows = pl.ds(pl.multiple_of(n * ATT_BLOCK, ATT_BLOCK), ATT_BLOCK)
                    krows = pl.ds(pl.multiple_of((n - 1) * ATT_BLOCK, ATT_BLOCK), 2 * ATT_BLOCK)
                    q = src[0, qrows, base:base + ATT_GROUP_WIDTH]
                    k = src[0, krows, base + ATT_GROUP_WIDTH:base + 2 * ATT_GROUP_WIDTH]
                    v = src[0, krows, base + 2 * ATT_GROUP_WIDTH:base + 3 * ATT_GROUP_WIDTH]
                    o, lse = _attention_block(q, k, v, band)
                    store(o_ref, l_ref, dil, r, n, o, lse)
                    return carry
                lax.fori_loop(1, nblk, body, 0)

    rows_per_step = 256

    def merge(i, carry):
        rows = pl.ds(pl.multiple_of(i * rows_per_step, rows_per_step), rows_per_step)
        for half in range(ATT_GROUP_WIDTH // LANES):
            la, lb, lc = l0[half, rows, :], l1[half, rows, :], l2[half, rows, :]
            m = jnp.maximum(jnp.maximum(la, lb), lc)
            ea, eb, ec = jnp.exp(la - m), jnp.exp(lb - m), jnp.exp(lc - m)
            num = ea * o0[half, rows, :] + eb * o1[half, rows, :] + ec * o2[half, rows, :]
            ya_ref[0, rows, half * LANES:(half + 1) * LANES] = num / (ea + eb + ec)
        return carry
    lax.fori_loop(0, seq // rows_per_step, merge, 0)


def _attention(a0, a1, a2):
    b, s, w = a0.shape
    views = []
    specs = []
    for arr, (_, dil) in zip((a0, a1, a2), ATT_GROUPS):
        views.append(arr.reshape(b, s // dil, dil * w))
        specs.append(pl.BlockSpec((1, s // dil, dil * w), lambda bi: (bi, 0, 0)))
    scratch = [pltpu.VMEM((ATT_GROUP_WIDTH // LANES, s, LANES), F32) for _ in range(6)]
    return pl.pallas_call(
        functools.partial(_attention_kernel, seq=s),
        grid=(b,),
        in_specs=specs,
        out_specs=pl.BlockSpec((1, s, ATT_GROUP_WIDTH), lambda bi: (bi, 0, 0)),
        out_shape=jax.ShapeDtypeStruct((b, s, ATT_GROUP_WIDTH), F32),
        scratch_shapes=scratch,
        compiler_params=pltpu.CompilerParams(dimension_semantics=("arbitrary",),
                                             vmem_limit_bytes=VMEM_LIMIT_BYTES),
        name="dilated_attention",
    )(*views)


def _deltanet_kernel(q_ref, k_ref, v_ref, bd_ref, gn_ref, o_ref, state_ref, *, tile):
    c = GDN_CHUNK
    d = GDN_HEAD_DIM

    @pl.when(pl.program_id(1) == 0)
    def _():
        state_ref[...] = jnp.zeros(state_ref.shape, F32)

    ii = lax.broadcasted_iota(jnp.int32, (c, c), 0)
    jj = lax.broadcasted_iota(jnp.int32, (c, c), 1)
    lower = ii >= jj
    strict = ii > jj
    tri_ones = jnp.where(lower, 1.0, 0.0).astype(F32)
    all_ones = jnp.ones((c, c), F32)
    gnorm = gn_ref[...]

    def chunk(ci, carry):
        rows = pl.ds(pl.multiple_of(ci * c, c), c)
        bd = bd_ref[0, rows, :]
        gcum = _dot_exact(tri_ones, bd)
        gtot = _dot_exact(all_ones, bd)
        gcum_t = gcum.T
        e_cum = jnp.exp(gcum)
        e_rest = jnp.exp(gtot - gcum)
        e_tot = jnp.exp(gtot)
        for hh in range(GDN_HEADS):
            cols = slice(hh * d, (hh + 1) * d)
            q = q_ref[0, rows, cols]
            k = k_ref[0, rows, cols]
            v = v_ref[0, rows, cols]
            beta = bd[:, hh:hh + 1]
            gl = GDN_HEADS + hh
            diff = gcum[:, gl:gl + 1] - gcum_t[gl:gl + 1, :]
            decay = jnp.exp(jnp.where(lower, diff, NEG_BIG))
            kbeta = k * beta
            kq = _dot_nt(jnp.concatenate([kbeta, q], axis=0).astype(BF16), k.astype(BF16))
            m = jnp.where(strict, kq[0:c] * decay, 0.0)
            a_qk = kq[c:2 * c] * decay
            n = -m
            p = m
            for _ in range(5):
                pb = p.astype(BF16)
                p = _dot(pb, pb)
                n = n + p + _dot(p.astype(BF16), n.astype(BF16))
            rhs = jnp.concatenate([v * beta, kbeta * e_cum[:, gl:gl + 1]], axis=1)
            sol = rhs + _dot(n.astype(BF16), rhs.astype(BF16))
            u = sol[:, 0:d]
            w = sol[:, d:2 * d]
            state = state_ref[hh]
            wq = jnp.concatenate([w, q * e_cum[:, gl:gl + 1]], axis=0).astype(BF16)
            ws = _dot(wq, state.astype(BF16))
            v_new = u - ws[0:c]
            v_new_b = v_new.astype(BF16)
            o = ws[c:2 * c] + _dot(a_qk.astype(BF16), v_new_b)
            k_dec = (k * e_rest[:, gl:gl + 1]).astype(BF16)
            state_ref[hh] = state * e_tot[0:1, gl:gl + 1] + _dot_tn(k_dec, v_new_b)
            o_ref[0, rows, cols] = _rmsnorm(o, gnorm)
        return carry

    lax.fori_loop(0, tile // c, chunk, 0)


def _deltanet(qb, kb, vb, bd, out_norm, *, tile):
    b, s, _ = qb.shape
    row = pl.BlockSpec((1, tile, GDN_WIDTH), lambda bi, i: (bi, i, 0))
    return pl.pallas_call(
        functools.partial(_deltanet_kernel, tile=tile),
        grid=(b, s // tile),
        in_specs=[row, row, row, pl.BlockSpec((1, tile, LANES), lambda bi, i: (bi, i, 0)),
                  _resident((1, GDN_HEAD_DIM))],
        out_specs=row,
        out_shape=jax.ShapeDtypeStruct((b, s, GDN_WIDTH), F32),
        scratch_shapes=[pltpu.VMEM((GDN_HEADS, GDN_HEAD_DIM, GDN_HEAD_DIM), F32)],
        compiler_params=pltpu.CompilerParams(dimension_semantics=("arbitrary", "arbitrary"),
                                             vmem_limit_bytes=VMEM_LIMIT_BYTES),
        name="gated_deltanet",
    )(qb, kb, vb, bd, out_norm)


def _mixer_out_kernel(x_ref, ya_ref, ob_ref, g_ref, wgt_ref, wa_ref, wb_ref, wo_ref, o_ref):
    x = x_ref[...]
    h = _rmsnorm(x, g_ref[...]).astype(BF16)
    gdn_gate = _dot(h, wgt_ref[:, 0:GDN_WIDTH])
    yb = (ob_ref[...] * (gdn_gate * _sigmoid(gdn_gate))).astype(BF16)
    gate_a = _dot(h, wgt_ref[:, GDN_WIDTH:GDN_WIDTH + D_MODEL])
    merged = _sigmoid(gate_a) * _dot(ya_ref[...].astype(BF16), wa_ref[...])
    gate_b = _dot(h, wgt_ref[:, GDN_WIDTH + D_MODEL:GDN_WIDTH + 2 * D_MODEL])
    merged = merged + _sigmoid(gate_b) * _dot(yb, wb_ref[...])
    o_ref[...] = x + _dot(merged.astype(BF16), wo_ref[...])


def _mixer_out(x1, ya, ob, norm_g, w_gates, w_a, w_b, w_o, *, tm):
    n = x1.shape[0]
    row = lambda w: pl.BlockSpec((tm, w), lambda i: (i, 0))
    return pl.pallas_call(
        _mixer_out_kernel,
        grid=(n // tm,),
        in_specs=[row(D_MODEL), row(ATT_GROUP_WIDTH), row(GDN_WIDTH), _resident((1, D_MODEL)),
                  _resident(w_gates.shape), _resident(w_a.shape), _resident(w_b.shape), _resident(w_o.shape)],
        out_specs=row(D_MODEL),
        out_shape=jax.ShapeDtypeStruct((n, D_MODEL), F32),
        compiler_params=pltpu.CompilerParams(dimension_semantics=("arbitrary",),
                                             vmem_limit_bytes=VMEM_LIMIT_BYTES),
        name="mixer_out",
    )(x1, ya, ob, norm_g, w_gates, w_a, w_b, w_o)


def _rope_tables(seq):
    half = ATT_HEAD_DIM // 2
    inv_freq = ROPE_THETA ** (-jnp.arange(half, dtype=F32) / half)
    ang = jnp.arange(seq, dtype=F32)[:, None] * inv_freq[None, :]
    cos, sin = jnp.cos(ang), jnp.sin(ang)
    reps = LANES // ATT_HEAD_DIM
    return jnp.tile(jnp.concatenate([cos, cos], axis=-1), (1, reps)), jnp.tile(jnp.concatenate([-sin, sin], axis=-1), (1, reps))


def _pad_lanes(row, offset):
    return jnp.zeros((1, LANES), F32).at[0, offset:offset + row.shape[0]].set(row.astype(F32))


def _layer(x, ffn1_norm, ffn1_w_gate, ffn1_w_up, ffn1_w_down, mix_norm, w_in, gdn_conv_w, gdn_a_log, gdn_dt_bias,
           gdn_out_norm, w_branch_a, w_branch_b, w_out, ffn2_norm, ffn2_w_gate, ffn2_w_up, ffn2_w_down, fin_g,
           *, final_norm, tm_ffn, tm_mix, gdn_tile):
    b, s, _ = x.shape
    n = b * s
    row = lambda v: v.reshape(1, -1).astype(F32)
    x1 = _ffn(x.reshape(n, D_MODEL), row(ffn1_norm), ffn1_w_gate.astype(BF16), ffn1_w_up.astype(BF16),
              ffn1_w_down.astype(BF16), fin_g, final_norm=False, tm=tm_ffn)

    w_main = w_in[:, :W_MAIN_WIDTH].astype(BF16)
    w_bd = jnp.pad(w_in[:, W_BD_OFFSET:W_GATES_OFFSET], ((0, 0), (0, LANES - 2 * GDN_HEADS))).astype(BF16)
    w_gates = w_in[:, W_GATES_OFFSET:].astype(BF16)
    cos_t, sin_t = _rope_tables(s)
    a0, a1, a2, qb, kb, vb, bd = _mixer_in(
        x1.reshape(b, s, D_MODEL), row(mix_norm), w_main, w_bd, gdn_conv_w.astype(F32),
        _pad_lanes(gdn_a_log, GDN_HEADS), _pad_lanes(gdn_dt_bias, GDN_HEADS), cos_t, sin_t, tm=tm_mix)

    ya = _attention(a0, a1, a2)
    ob = _deltanet(qb, kb, vb, bd, row(gdn_out_norm), tile=gdn_tile)

    x2 = _mixer_out(x1, ya.reshape(n, ATT_GROUP_WIDTH), ob.reshape(n, GDN_WIDTH), row(mix_norm), w_gates,
                    w_branch_a.astype(BF16), w_branch_b.astype(BF16), w_out.astype(BF16), tm=tm_mix)
    x3 = _ffn(x2, row(ffn2_norm), ffn2_w_gate.astype(BF16), ffn2_w_up.astype(BF16), ffn2_w_down.astype(BF16),
              fin_g, final_norm=final_norm, tm=tm_ffn)
    return x3.reshape(b, s, D_MODEL)


def kernel(x, ffn1_norm, ffn1_w_gate, ffn1_w_up, ffn1_w_down, mix_norm, w_in, gdn_conv_w, gdn_a_log, gdn_dt_bias,
           gdn_out_norm, w_branch_a, w_branch_b, w_out, ffn2_norm, ffn2_w_gate, ffn2_w_up, ffn2_w_down, final_norm):
    depth = ffn1_norm.shape[0]
    fin_g = final_norm.reshape(1, -1).astype(F32)
    for layer in range(depth):
        x = _layer(x, ffn1_norm[layer], ffn1_w_gate[layer], ffn1_w_up[layer], ffn1_w_down[layer], mix_norm[layer],
                   w_in[layer], gdn_conv_w[layer], gdn_a_log[layer], gdn_dt_bias[layer], gdn_out_norm[layer],
                   w_branch_a[layer], w_branch_b[layer], w_out[layer], ffn2_norm[layer], ffn2_w_gate[layer],
                   ffn2_w_up[layer], ffn2_w_down[layer], fin_g, final_norm=(layer == depth - 1),
                   tm_ffn=512, tm_mix=512, gdn_tile=512)
    return x
```

```python
import functools

import jax
import jax.numpy as jnp
from jax import lax
from jax.experimental import pallas as pl
from jax.experimental.pallas import tpu as pltpu

F32 = jnp.float32
BF16 = jnp.bfloat16

D_MODEL = 1024
D_FF = 2816
EPS = 1e-6

ATT_GROUPS = ((128, 1), (512, 4), (2048, 16))
ATT_HEADS_PER_GROUP = 4
ATT_HEAD_DIM = 64
ATT_BLOCK = 128
ATT_GROUP_WIDTH = ATT_HEADS_PER_GROUP * ATT_HEAD_DIM
ATT_QKV_WIDTH = len(ATT_GROUPS) * ATT_GROUP_WIDTH
ROPE_THETA = 10000.0

GDN_HEADS = 8
GDN_HEAD_DIM = 128
GDN_WIDTH = GDN_HEADS * GDN_HEAD_DIM
GDN_CONV = 4
GDN_CHUNK = 64

LANES = 128
SUBLANES = 8
VMEM_LIMIT_BYTES = 56 * 1024 * 1024

W_MAIN_WIDTH = 3 * ATT_QKV_WIDTH + 3 * GDN_WIDTH
W_BD_OFFSET = W_MAIN_WIDTH
W_GATES_OFFSET = W_MAIN_WIDTH + 2 * GDN_HEADS
FFN_CHUNKS = ((0, 768), (768, 1536), (1536, 2304), (2304, 2816))
NEG_BIG = -1e30


def _resident(shape):
    nd = len(shape)
    return pl.BlockSpec(shape, lambda *_: (0,) * nd, pipeline_mode=pl.Buffered(1))


def _rmsnorm(x, g):
    return x * lax.rsqrt(jnp.mean(x * x, axis=-1, keepdims=True) + EPS) * g


def _sigmoid(x):
    return 1.0 / (1.0 + jnp.exp(-x))


def _dot(a, b):
    return jnp.dot(a, b, preferred_element_type=F32)


def _dot_nt(a, b):
    return lax.dot_general(a, b, (((1,), (1,)), ((), ())), preferred_element_type=F32)


def _dot_tn(a, b):
    return lax.dot_general(a, b, (((0,), (0,)), ((), ())), preferred_element_type=F32)


def _dot_exact(a, b):
    return jnp.dot(a, b, preferred_element_type=F32, precision=lax.Precision.HIGHEST)


def _swiglu_residual(x, g, wg_ref, wu_ref, wd_ref):
    h = _rmsnorm(x, g).astype(BF16)
    acc = x
    for lo, hi in FFN_CHUNKS:
        gate = _dot(h, wg_ref[:, lo:hi])
        up = _dot(h, wu_ref[:, lo:hi])
        act = (0.5 * gate * _sigmoid(gate) * up).astype(BF16)
        acc = acc + _dot(act, wd_ref[lo:hi, :])
    return acc


def _ffn_kernel(x_ref, g_ref, wg_ref, wu_ref, wd_ref, fin_ref, o_ref, *, final_norm):
    y = _swiglu_residual(x_ref[...], g_ref[...], wg_ref, wu_ref, wd_ref)
    if final_norm:
        y = _rmsnorm(y, fin_ref[...])
    o_ref[...] = y


def _ffn(x, norm_g, wg, wu, wd, fin_g, *, final_norm, tm):
    n = x.shape[0]
    row = pl.BlockSpec((tm, D_MODEL), lambda i: (i, 0))
    return pl.pallas_call(
        functools.partial(_ffn_kernel, final_norm=final_norm),
        grid=(n // tm,),
        in_specs=[row, _resident((1, D_MODEL)), _resident((D_MODEL, D_FF)), _resident((D_MODEL, D_FF)),
                  _resident((D_FF, D_MODEL)), _resident((1, D_MODEL))],
        out_specs=row,
        out_shape=jax.ShapeDtypeStruct((n, D_MODEL), F32),
        compiler_params=pltpu.CompilerParams(dimension_semantics=("arbitrary",),
                                             vmem_limit_bytes=VMEM_LIMIT_BYTES),
        name="ffn_final" if final_norm else "ffn",
    )(x, norm_g, wg, wu, wd, fin_g)


def _mixer_in_kernel(x_ref, g_ref, wm_ref, wbd_ref, convw_ref, alog_ref, dtb_ref, cos_ref, sin_ref,
                     a0_ref, a1_ref, a2_ref, qb_ref, kb_ref, vb_ref, bd_ref, conv_ref, *, tm):
    h = _rmsnorm(x_ref[0], g_ref[...]).astype(BF16)

    cos = cos_ref[...]
    sin = sin_ref[...]
    lane = lax.broadcasted_iota(jnp.int32, (1, LANES), 1)
    first_half = (lane % ATT_HEAD_DIM) < (ATT_HEAD_DIM // 2)
    att_refs = (a0_ref, a1_ref, a2_ref)
    ya = _dot(h, wm_ref[:, 0:3 * ATT_QKV_WIDTH])
    blocks_per_group = ATT_GROUP_WIDTH // LANES
    for part in range(3):
        for gi in range(len(ATT_GROUPS)):
            for j in range(blocks_per_group):
                col = part * ATT_QKV_WIDTH + gi * ATT_GROUP_WIDTH + j * LANES
                blk = ya[:, col:col + LANES]
                if part < 2:
                    swapped = jnp.where(first_half, pltpu.roll(blk, LANES - ATT_HEAD_DIM // 2, 1),
                                        pltpu.roll(blk, ATT_HEAD_DIM // 2, 1))
                    blk = blk * cos + swapped * sin
                if part == 0:
                    blk = blk * (ATT_HEAD_DIM ** -0.5)
                dst = part * ATT_GROUP_WIDTH + j * LANES
                att_refs[gi][0, :, dst:dst + LANES] = blk.astype(BF16)

    @pl.when(pl.program_id(1) == 0)
    def _():
        conv_ref[0:SUBLANES, :] = jnp.zeros((SUBLANES, 3 * GDN_WIDTH), F32)

    for part in range(3):
        lo = 3 * ATT_QKV_WIDTH + part * GDN_WIDTH
        conv_ref[SUBLANES:SUBLANES + tm, part * GDN_WIDTH:(part + 1) * GDN_WIDTH] = _dot(h, wm_ref[:, lo:lo + GDN_WIDTH])
    out_refs = (qb_ref, kb_ref, vb_ref)
    for part in range(3):
        for hh in range(GDN_HEADS):
            col = part * GDN_WIDTH + hh * GDN_HEAD_DIM
            acc = None
            for t in range(GDN_CONV):
                start = SUBLANES - (GDN_CONV - 1) + t
                term = conv_ref[start:start + tm, col:col + GDN_HEAD_DIM] * convw_ref[t:t + 1, col:col + GDN_HEAD_DIM]
                acc = term if acc is None else acc + term
            act = acc * _sigmoid(acc)
            if part < 2:
                act = act * lax.rsqrt(jnp.sum(act * act, axis=-1, keepdims=True) + EPS)
            if part == 0:
                act = act * (GDN_HEAD_DIM ** -0.5)
            out_refs[part][0, :, hh * GDN_HEAD_DIM:(hh + 1) * GDN_HEAD_DIM] = act
    conv_ref[0:SUBLANES, :] = conv_ref[tm:tm + SUBLANES, :]

    raw = _dot(h, wbd_ref[...])
    z = raw + dtb_ref[...]
    softplus = jnp.maximum(z, 0.0) + jnp.log1p(jnp.exp(-jnp.abs(z)))
    g = -jnp.exp(alog_ref[...]) * softplus
    bd_ref[0] = jnp.where(lane < GDN_HEADS, _sigmoid(raw), jnp.where(lane < 2 * GDN_HEADS, g, 0.0))


def _mixer_in(x1, norm_g, w_main, w_bd, conv_w, a_log, dt_bias, cos_t, sin_t, *, tm):
    b, s, _ = x1.shape
    tile = lambda w: pl.BlockSpec((1, tm, w), lambda bi, i: (bi, i, 0))
    table = pl.BlockSpec((tm, LANES), lambda bi, i: (i, 0))
    att = jax.ShapeDtypeStruct((b, s, 3 * ATT_GROUP_WIDTH), BF16)
    gdn = jax.ShapeDtypeStruct((b, s, GDN_WIDTH), F32)
    return pl.pallas_call(
        functools.partial(_mixer_in_kernel, tm=tm),
        grid=(b, s // tm),
        in_specs=[tile(D_MODEL), _resident((1, D_MODEL)), _resident((D_MODEL, W_MAIN_WIDTH)),
                  _resident((D_MODEL, LANES)), _resident((GDN_CONV, 3 * GDN_WIDTH)),
                  _resident((1, LANES)), _resident((1, LANES)), table, table],
        out_specs=[tile(3 * ATT_GROUP_WIDTH)] * 3 + [tile(GDN_WIDTH)] * 3 + [tile(LANES)],
        out_shape=[att] * 3 + [gdn] * 3 + [jax.ShapeDtypeStruct((b, s, LANES), F32)],
        scratch_shapes=[pltpu.VMEM((tm + SUBLANES, 3 * GDN_WIDTH), F32)],
        compiler_params=pltpu.CompilerParams(dimension_semantics=("arbitrary", "arbitrary"),
                                             vmem_limit_bytes=VMEM_LIMIT_BYTES),
        name="mixer_in",
    )(x1, norm_g, w_main, w_bd, conv_w, a_log, dt_bias, cos_t, sin_t)


def _attention_block(q, k, v, valid):
    lane = lax.broadcasted_iota(jnp.int32, (1, ATT_GROUP_WIDTH), 1)
    o = jnp.zeros((ATT_BLOCK, ATT_GROUP_WIDTH), F32)
    inv_l = jnp.zeros((ATT_BLOCK, ATT_GROUP_WIDTH), F32)
    lse = jnp.zeros((ATT_BLOCK, ATT_GROUP_WIDTH), F32)
    for hh in range(ATT_HEADS_PER_GROUP):
        in_head = (lane // ATT_HEAD_DIM) == hh
        keep = jnp.where(in_head, 1.0, 0.0).astype(BF16)
        s = _dot_nt(q * keep, k)
        s = jnp.where(valid, s, NEG_BIG)
        m = jnp.max(s, axis=-1, keepdims=True)
        p = jnp.exp(s - m)
        l = jnp.sum(p, axis=-1, keepdims=True)
        o = o + _dot(p.astype(BF16), v * keep)
        inv_l = jnp.where(in_head, 1.0 / l, inv_l)
        lse = jnp.where(in_head, m + jnp.log(l), lse)
    return o * inv_l, lse


def _attention_kernel(a0_ref, a1_ref, a2_ref, ya_ref, o0, l0, o1, l1, o2, l2, *, seq):
    in_refs = (a0_ref, a1_ref, a2_ref)
    o_refs = (o0, o1, o2)
    l_refs = (l0, l1, l2)
    qi = lax.broadcasted_iota(jnp.int32, (ATT_BLOCK, ATT_BLOCK), 0)
    kj = lax.broadcasted_iota(jnp.int32, (ATT_BLOCK, ATT_BLOCK), 1)
    causal = kj <= qi
    qi2 = lax.broadcasted_iota(jnp.int32, (ATT_BLOCK, 2 * ATT_BLOCK), 0)
    kj2 = lax.broadcasted_iota(jnp.int32, (ATT_BLOCK, 2 * ATT_BLOCK), 1)
    band = (kj2 >= qi2) & (kj2 - ATT_BLOCK <= qi2)

    for gi, (window, dil) in enumerate(ATT_GROUPS):
        assert window // dil == ATT_BLOCK
        length = seq // dil
        nblk = length // ATT_BLOCK
        src, o_ref, l_ref = in_refs[gi], o_refs[gi], l_refs[gi]
        wq = 3 * ATT_GROUP_WIDTH

        def store(o_ref, l_ref, dil, r, n, o, lse):
            if dil == 1:
                rows = pl.ds(pl.multiple_of(n * ATT_BLOCK, ATT_BLOCK), ATT_BLOCK)
            else:
                rows = pl.ds(n * ATT_BLOCK * dil + r, ATT_BLOCK, stride=dil)
            for half in range(ATT_GROUP_WIDTH // LANES):
                o_ref[half, rows, :] = o[:, half * LANES:(half + 1) * LANES]
                l_ref[half, rows, :] = lse[:, half * LANES:(half + 1) * LANES]

        for r in range(dil):
            base = r * wq
            q = src[0, 0:ATT_BLOCK, base:base + ATT_GROUP_WIDTH]
            k = src[0, 0:ATT_BLOCK, base + ATT_GROUP_WIDTH:base + 2 * ATT_GROUP_WIDTH]
            v = src[0, 0:ATT_BLOCK, base + 2 * ATT_GROUP_WIDTH:base + 3 * ATT_GROUP_WIDTH]
            o, lse = _attention_block(q, k, v, causal)
            store(o_ref, l_ref, dil, r, 0, o, lse)

            if nblk > 1:
                def body(n, carry, src=src, base=base, o_ref=o_ref, l_ref=l_ref, dil=dil, r=r):
                    qrows = pl.ds(pl.multiple_of(n * ATT_BLOCK, ATT_BLOCK), ATT_BLOCK)
                    krows = pl.ds(pl.multiple_of((n - 1) * ATT_BLOCK, ATT_BLOCK), 2 * ATT_BLOCK)
                    q = src[0, qrows, base:base + ATT_GROUP_WIDTH]
                    k = src[0, krows, base + ATT_GROUP_WIDTH:base + 2 * ATT_GROUP_WIDTH]
                    v = src[0, krows, base + 2 * ATT_GROUP_WIDTH:base + 3 * ATT_GROUP_WIDTH]
                    o, lse = _attention_block(q, k, v, band)
                    store(o_ref, l_ref, dil, r, n, o, lse)
                    return carry
                lax.fori_loop(1, nblk, body, 0)

    rows_per_step = 256

    def merge(i, carry):
        rows = pl.ds(pl.multiple_of(i * rows_per_step, rows_per_step), rows_per_step)
        for half in range(ATT_GROUP_WIDTH // LANES):
            la, lb, lc = l0[half, rows, :], l1[half, rows, :], l2[half, rows, :]
            m = jnp.maximum(jnp.maximum(la, lb), lc)
            ea, eb, ec = jnp.exp(la - m), jnp.exp(lb - m), jnp.exp(lc - m)
            num = ea * o0[half, rows, :] + eb * o1[half, rows, :] + ec * o2[half, rows, :]
            ya_ref[0, rows, half * LANES:(half + 1) * LANES] = num / (ea + eb + ec)
        return carry
    lax.fori_loop(0, seq // rows_per_step, merge, 0)


def _attention(a0, a1, a2):
    b, s, w = a0.shape
    views = []
    specs = []
    for arr, (_, dil) in zip((a0, a1, a2), ATT_GROUPS):
        views.append(arr.reshape(b, s // dil, dil * w))
        specs.append(pl.BlockSpec((1, s // dil, dil * w), lambda bi: (bi, 0, 0)))
    scratch = [pltpu.VMEM((ATT_GROUP_WIDTH // LANES, s, LANES), F32) for _ in range(6)]
    return pl.pallas_call(
        functools.partial(_attention_kernel, seq=s),
        grid=(b,),
        in_specs=specs,
        out_specs=pl.BlockSpec((1, s, ATT_GROUP_WIDTH), lambda bi: (bi, 0, 0)),
        out_shape=jax.ShapeDtypeStruct((b, s, ATT_GROUP_WIDTH), F32),
        scratch_shapes=scratch,
        compiler_params=pltpu.CompilerParams(dimension_semantics=("arbitrary",),
                                             vmem_limit_bytes=VMEM_LIMIT_BYTES),
        name="dilated_attention",
    )(*views)


def _deltanet_kernel(q_ref, k_ref, v_ref, bd_ref, gn_ref, o_ref, state_ref, *, tile):
    c = GDN_CHUNK
    d = GDN_HEAD_DIM

    @pl.when(pl.program_id(1) == 0)
    def _():
        state_ref[...] = jnp.zeros(state_ref.shape, F32)

    ii = lax.broadcasted_iota(jnp.int32, (c, c), 0)
    jj = lax.broadcasted_iota(jnp.int32, (c, c), 1)
    lower = ii >= jj
    strict = ii > jj
    tri_ones = jnp.where(lower, 1.0, 0.0).astype(F32)
    all_ones = jnp.ones((c, c), F32)
    gnorm = gn_ref[...]

    def chunk(ci, carry):
        rows = pl.ds(pl.multiple_of(ci * c, c), c)
        bd = bd_ref[0, rows, :]
        gcum = _dot_exact(tri_ones, bd)
        gtot = _dot_exact(all_ones, bd)
        gcum_t = gcum.T
        e_cum_all = jnp.exp(gcum)
        e_rest_all = jnp.exp(gtot - gcum)
        e_tot_all = jnp.exp(gtot)

        heads = range(GDN_HEADS)
        cols = [slice(hh * d, (hh + 1) * d) for hh in heads]
        glane = [GDN_HEADS + hh for hh in heads]
        q = [q_ref[0, rows, cols[hh]] for hh in heads]
        k = [k_ref[0, rows, cols[hh]] for hh in heads]
        v = [v_ref[0, rows, cols[hh]] for hh in heads]
        beta = [bd[:, hh:hh + 1] for hh in heads]
        e_cum = [e_cum_all[:, gl:gl + 1] for gl in glane]
        kbeta = [k[hh] * beta[hh] for hh in heads]
        kq = [_dot_nt(jnp.concatenate([kbeta[hh], q[hh]], axis=0).astype(BF16), k[hh].astype(BF16))
              for hh in heads]
        decay = [jnp.exp(jnp.where(lower, gcum[:, gl:gl + 1] - gcum_t[gl:gl + 1, :], NEG_BIG)) for gl in glane]
        m = [jnp.where(strict, kq[hh][0:c] * decay[hh], 0.0) for hh in heads]
        a_qk = [(kq[hh][c:2 * c] * decay[hh]).astype(BF16) for hh in heads]
        n = [-m[hh] for hh in heads]
        p = m
        for _ in range(5):
            pb = [p[hh].astype(BF16) for hh in heads]
            p = [_dot(pb[hh], pb[hh]) for hh in heads]
            n = [n[hh] + p[hh] + _dot(p[hh].astype(BF16), n[hh].astype(BF16)) for hh in heads]
        rhs = [jnp.concatenate([v[hh] * beta[hh], kbeta[hh] * e_cum[hh]], axis=1) for hh in heads]
        sol = [rhs[hh] + _dot(n[hh].astype(BF16), rhs[hh].astype(BF16)) for hh in heads]
        state = [state_ref[hh] for hh in heads]
        wq = [jnp.concatenate([sol[hh][:, d:2 * d], q[hh] * e_cum[hh]], axis=0).astype(BF16) for hh in heads]
        ws = [_dot(wq[hh], state[hh].astype(BF16)) for hh in heads]
        v_new = [(sol[hh][:, 0:d] - ws[hh][0:c]).astype(BF16) for hh in heads]
        o = [ws[hh][c:2 * c] + _dot(a_qk[hh], v_new[hh]) for hh in heads]
        k_dec = [(k[hh] * e_rest_all[:, gl:gl + 1]).astype(BF16) for hh, gl in zip(heads, glane)]
        for hh, gl in zip(heads, glane):
            state_ref[hh] = state[hh] * e_tot_all[0:1, gl:gl + 1] + _dot_tn(k_dec[hh], v_new[hh])
        for hh in heads:
            o_ref[0, rows, cols[hh]] = _rmsnorm(o[hh], gnorm)
        return carry

    lax.fori_loop(0, tile // c, chunk, 0)


def _deltanet(qb, kb, vb, bd, out_norm, *, tile):
    b, s, _ = qb.shape
    row = pl.BlockSpec((1, tile, GDN_WIDTH), lambda bi, i: (bi, i, 0))
    return pl.pallas_call(
        functools.partial(_deltanet_kernel, tile=tile),
        grid=(b, s // tile),
        in_specs=[row, row, row, pl.BlockSpec((1, tile, LANES), lambda bi, i: (bi, i, 0)),
                  _resident((1, GDN_HEAD_DIM))],
        out_specs=row,
        out_shape=jax.ShapeDtypeStruct((b, s, GDN_WIDTH), F32),
        scratch_shapes=[pltpu.VMEM((GDN_HEADS, GDN_HEAD_DIM, GDN_HEAD_DIM), F32)],
        compiler_params=pltpu.CompilerParams(dimension_semantics=("arbitrary", "arbitrary"),
                                             vmem_limit_bytes=VMEM_LIMIT_BYTES),
        name="gated_deltanet",
    )(qb, kb, vb, bd, out_norm)


def _mixer_out_kernel(x_ref, ya_ref, ob_ref, g_ref, wgt_ref, wa_ref, wb_ref, wo_ref, o_ref):
    x = x_ref[...]
    h = _rmsnorm(x, g_ref[...]).astype(BF16)
    gdn_gate = _dot(h, wgt_ref[:, 0:GDN_WIDTH])
    yb = (ob_ref[...] * (gdn_gate * _sigmoid(gdn_gate))).astype(BF16)
    gate_a = _dot(h, wgt_ref[:, GDN_WIDTH:GDN_WIDTH + D_MODEL])
    merged = _sigmoid(gate_a) * _dot(ya_ref[...].astype(BF16), wa_ref[...])
    gate_b = _dot(h, wgt_ref[:, GDN_WIDTH + D_MODEL:GDN_WIDTH + 2 * D_MODEL])
    merged = merged + _sigmoid(gate_b) * _dot(yb, wb_ref[...])
    o_ref[...] = x + _dot(merged.astype(BF16), wo_ref[...])


def _mixer_out(x1, ya, ob, norm_g, w_gates, w_a, w_b, w_o, *, tm):
    n = x1.shape[0]
    row = lambda w: pl.BlockSpec((tm, w), lambda i: (i, 0))
    return pl.pallas_call(
        _mixer_out_kernel,
        grid=(n // tm,),
        in_specs=[row(D_MODEL), row(ATT_GROUP_WIDTH), row(GDN_WIDTH), _resident((1, D_MODEL)),
                  _resident(w_gates.shape), _resident(w_a.shape), _resident(w_b.shape), _resident(w_o.shape)],
        out_specs=row(D_MODEL),
        out_shape=jax.ShapeDtypeStruct((n, D_MODEL), F32),
        compiler_params=pltpu.CompilerParams(dimension_semantics=("arbitrary",),
                                             vmem_limit_bytes=VMEM_LIMIT_BYTES),
        name="mixer_out",
    )(x1, ya, ob, norm_g, w_gates, w_a, w_b, w_o)


def _rope_tables(seq):
    half = ATT_HEAD_DIM // 2
    inv_freq = ROPE_THETA ** (-jnp.arange(half, dtype=F32) / half)
    ang = jnp.arange(seq, dtype=F32)[:, None] * inv_freq[None, :]
    cos, sin = jnp.cos(ang), jnp.sin(ang)
    reps = LANES // ATT_HEAD_DIM
    return jnp.tile(jnp.concatenate([cos, cos], axis=-1), (1, reps)), jnp.tile(jnp.concatenate([-sin, sin], axis=-1), (1, reps))


def _pad_lanes(row, offset):
    return jnp.zeros((1, LANES), F32).at[0, offset:offset + row.shape[0]].set(row.astype(F32))


def _layer(x, ffn1_norm, ffn1_w_gate, ffn1_w_up, ffn1_w_down, mix_norm, w_in, gdn_conv_w, gdn_a_log, gdn_dt_bias,
           gdn_out_norm, w_branch_a, w_branch_b, w_out, ffn2_norm, ffn2_w_gate, ffn2_w_up, ffn2_w_down, fin_g,
           *, final_norm, tm_ffn, tm_mix, gdn_tile):
    b, s, _ = x.shape
    n = b * s
    row = lambda v: v.reshape(1, -1).astype(F32)
    x1 = _ffn(x.reshape(n, D_MODEL), row(ffn1_norm), ffn1_w_gate.astype(BF16), ffn1_w_up.astype(BF16),
              ffn1_w_down.astype(BF16), fin_g, final_norm=False, tm=tm_ffn)

    w_main = w_in[:, :W_MAIN_WIDTH].astype(BF16)
    w_bd = jnp.pad(w_in[:, W_BD_OFFSET:W_GATES_OFFSET], ((0, 0), (0, LANES - 2 * GDN_HEADS))).astype(BF16)
    w_gates = w_in[:, W_GATES_OFFSET:].astype(BF16)
    cos_t, sin_t = _rope_tables(s)
    a0, a1, a2, qb, kb, vb, bd = _mixer_in(
        x1.reshape(b, s, D_MODEL), row(mix_norm), w_main, w_bd, gdn_conv_w.astype(F32),
        _pad_lanes(gdn_a_log, GDN_HEADS), _pad_lanes(gdn_dt_bias, GDN_HEADS), cos_t, sin_t, tm=tm_mix)

    ya = _attention(a0, a1, a2)
    ob = _deltanet(qb, kb, vb, bd, row(gdn_out_norm), tile=gdn_tile)

    x2 = _mixer_out(x1, ya.reshape(n, ATT_GROUP_WIDTH), ob.reshape(n, GDN_WIDTH), row(mix_norm), w_gates,
                    w_branch_a.astype(BF16), w_branch_b.astype(BF16), w_out.astype(BF16), tm=tm_mix)
    x3 = _ffn(x2, row(ffn2_norm), ffn2_w_gate.astype(BF16), ffn2_w_up.astype(BF16), ffn2_w_down.astype(BF16),
              fin_g, final_norm=final_norm, tm=tm_ffn)
    return x3.reshape(b, s, D_MODEL)


def kernel(x, ffn1_norm, ffn1_w_gate, ffn1_w_up, ffn1_w_down, mix_norm, w_in, gdn_conv_w, gdn_a_log, gdn_dt_bias,
           gdn_out_norm, w_branch_a, w_branch_b, w_out, ffn2_norm, ffn2_w_gate, ffn2_w_up, ffn2_w_down, final_norm):
    depth = ffn1_norm.shape[0]
    fin_g = final_norm.reshape(1, -1).astype(F32)
    for layer in range(depth):
        x = _layer(x, ffn1_norm[layer], ffn1_w_gate[layer], ffn1_w_up[layer], ffn1_w_down[layer], mix_norm[layer],
                   w_in[layer], gdn_conv_w[layer], gdn_a_log[layer], gdn_dt_bias[layer], gdn_out_norm[layer],
                   w_branch_a[layer], w_branch_b[layer], w_out[layer], ffn2_norm[layer], ffn2_w_gate[layer],
                   ffn2_w_up[layer], ffn2_w_down[layer], fin_g, final_norm=(layer == depth - 1),
                   tm_ffn=512, tm_mix=512, gdn_tile=512)
    return x
```

```python
import functools

import jax
import jax.numpy as jnp
from jax import lax
from jax.experimental import pallas as pl
from jax.experimental.pallas import tpu as pltpu

F32 = jnp.float32
BF16 = jnp.bfloat16

D_MODEL = 1024
D_FF = 2816
EPS = 1e-6

ATT_GROUPS = ((128, 1), (512, 4), (2048, 16))
ATT_HEADS_PER_GROUP = 4
ATT_HEAD_DIM = 64
ATT_BLOCK = 128
ATT_GROUP_WIDTH = ATT_HEADS_PER_GROUP * ATT_HEAD_DIM
ATT_QKV_WIDTH = len(ATT_GROUPS) * ATT_GROUP_WIDTH
ROPE_THETA = 10000.0

GDN_HEADS = 8
GDN_HEAD_DIM = 128
GDN_WIDTH = GDN_HEADS * GDN_HEAD_DIM
GDN_CONV = 4
GDN_CHUNK = 64

LANES = 128
SUBLANES = 8
VMEM_LIMIT_BYTES = 56 * 1024 * 1024

W_IN_GDN = 3 * ATT_QKV_WIDTH
W_IN_BD = W_IN_GDN + 3 * GDN_WIDTH
W_IN_GATES = W_IN_BD + 2 * GDN_HEADS
W_ATT = 0
W_BD = 3 * ATT_QKV_WIDTH
W_GDN = 3 * GDN_WIDTH
W_GATES = 2 * W_GDN
FFN_CHUNKS = ((0, 768), (768, 1536), (1536, 2304), (2304, 2816))
NEG_BIG = -1e30


def _resident(shape):
    nd = len(shape)
    return pl.BlockSpec(shape, lambda *_: (0,) * nd, pipeline_mode=pl.Buffered(1))


def _rmsnorm(x, g):
    return x * lax.rsqrt(jnp.mean(x * x, axis=-1, keepdims=True) + EPS) * g


def _sigmoid(x):
    return 1.0 / (1.0 + jnp.exp(-x))


def _dot(a, b):
    return jnp.dot(a, b, preferred_element_type=F32)


def _dot_nt(a, b):
    return lax.dot_general(a, b, (((1,), (1,)), ((), ())), preferred_element_type=F32)


def _dot_tn(a, b):
    return lax.dot_general(a, b, (((0,), (0,)), ((), ())), preferred_element_type=F32)


def _dot_exact(a, b):
    return jnp.dot(a, b, preferred_element_type=F32, precision=lax.Precision.HIGHEST)


def _swiglu_residual(x, g, wg_ref, wu_ref, wd_ref):
    h = _rmsnorm(x, g).astype(BF16)
    acc = x
    for lo, hi in FFN_CHUNKS:
        gate = _dot(h, wg_ref[:, lo:hi])
        up = _dot(h, wu_ref[:, lo:hi])
        act = (0.5 * gate * _sigmoid(gate) * up).astype(BF16)
        acc = acc + _dot(act, wd_ref[lo:hi, :])
    return acc


def _ffn_kernel(x_ref, g_ref, wg_ref, wu_ref, wd_ref, fin_ref, o_ref, *, final_norm):
    y = _swiglu_residual(x_ref[...], g_ref[...], wg_ref, wu_ref, wd_ref)
    if final_norm:
        y = _rmsnorm(y, fin_ref[...])
    o_ref[...] = y


def _ffn(x, norm_g, wg, wu, wd, fin_g, *, final_norm, tm):
    n = x.shape[0]
    row = pl.BlockSpec((tm, D_MODEL), lambda i: (i, 0))
    return pl.pallas_call(
        functools.partial(_ffn_kernel, final_norm=final_norm),
        grid=(n // tm,),
        in_specs=[row, _resident((1, D_MODEL)), _resident((D_MODEL, D_FF)), _resident((D_MODEL, D_FF)),
                  _resident((D_FF, D_MODEL)), _resident((1, D_MODEL))],
        out_specs=row,
        out_shape=jax.ShapeDtypeStruct((n, D_MODEL), F32),
        compiler_params=pltpu.CompilerParams(dimension_semantics=("arbitrary",),
                                             vmem_limit_bytes=VMEM_LIMIT_BYTES),
        name="ffn_final" if final_norm else "ffn",
    )(x, norm_g, wg, wu, wd, fin_g)


def _mixer_in_kernel(x_ref, g_ref, wm_ref, convw_ref, alog_ref, dtb_ref, cos_ref, sin_ref,
                     a0_ref, a1_ref, a2_ref, qb_ref, kb_ref, vb_ref, bd_ref, conv_ref, stage_ref, *, tm):
    @pl.when(pl.program_id(1) == 0)
    def _():
        conv_ref[0:SUBLANES, :] = jnp.zeros((SUBLANES, 3 * GDN_WIDTH), F32)

    h = _rmsnorm(x_ref[0], g_ref[...]).astype(BF16)
    for part in range(3):
        lo = W_GDN + part * GDN_WIDTH
        conv_ref[SUBLANES:SUBLANES + tm, part * GDN_WIDTH:(part + 1) * GDN_WIDTH] = _dot(h, wm_ref[:, lo:lo + GDN_WIDTH])
    ya = _dot(h, wm_ref[:, W_ATT:W_ATT + 3 * ATT_QKV_WIDTH])
    raw = _dot(h, wm_ref[:, W_BD:W_BD + LANES])

    out_refs = (qb_ref, kb_ref, vb_ref)
    for part in range(3):
        for hh in range(GDN_HEADS):
            col = part * GDN_WIDTH + hh * GDN_HEAD_DIM
            acc = None
            for t in range(GDN_CONV):
                start = SUBLANES - (GDN_CONV - 1) + t
                term = conv_ref[start:start + tm, col:col + GDN_HEAD_DIM] * convw_ref[t:t + 1, col:col + GDN_HEAD_DIM]
                acc = term if acc is None else acc + term
            act = acc * _sigmoid(acc)
            if part < 2:
                act = act * lax.rsqrt(jnp.sum(act * act, axis=-1, keepdims=True) + EPS)
            if part == 0:
                act = act * (GDN_HEAD_DIM ** -0.5)
            out_refs[part][0, :, hh * GDN_HEAD_DIM:(hh + 1) * GDN_HEAD_DIM] = act
    conv_ref[0:SUBLANES, :] = conv_ref[tm:tm + SUBLANES, :]

    cos = cos_ref[...]
    sin = sin_ref[...]
    lane = lax.broadcasted_iota(jnp.int32, (1, LANES), 1)
    first_half = (lane % ATT_HEAD_DIM) < (ATT_HEAD_DIM // 2)
    att_refs = (a0_ref, a1_ref, a2_ref)
    blocks_per_group = ATT_GROUP_WIDTH // LANES
    slot = 0
    for part in range(3):
        for gi, (_, dil) in enumerate(ATT_GROUPS):
            for j in range(blocks_per_group):
                col = part * ATT_QKV_WIDTH + gi * ATT_GROUP_WIDTH + j * LANES
                blk = ya[:, col:col + LANES]
                if part < 2:
                    swapped = jnp.where(first_half, pltpu.roll(blk, LANES - ATT_HEAD_DIM // 2, 1),
                                        pltpu.roll(blk, ATT_HEAD_DIM // 2, 1))
                    blk = blk * cos + swapped * sin
                if part == 0:
                    blk = blk * (ATT_HEAD_DIM ** -0.5)
                dst = part * ATT_GROUP_WIDTH + j * LANES
                if dil == 1:
                    att_refs[gi][0, :, dst:dst + LANES] = blk.astype(BF16)
                else:
                    stage_ref[slot] = blk
                    for r in range(dil):
                        rows = stage_ref[slot, pl.ds(r, tm // dil, stride=dil), :]
                        lo = r * 3 * ATT_GROUP_WIDTH + dst
                        att_refs[gi][0, :, lo:lo + LANES] = rows.astype(BF16)
                    slot += 1

    z = raw + dtb_ref[...]
    softplus = jnp.maximum(z, 0.0) + jnp.log1p(jnp.exp(-jnp.abs(z)))
    g = -jnp.exp(alog_ref[...]) * softplus
    bd_ref[0] = jnp.where(lane < GDN_HEADS, _sigmoid(raw), jnp.where(lane < 2 * GDN_HEADS, g, 0.0))


def _mixer_in(x1, norm_g, w_all, conv_w, a_log, dt_bias, cos_t, sin_t, *, tm):
    b, s, _ = x1.shape
    tile = lambda w: pl.BlockSpec((1, tm, w), lambda bi, i: (bi, i, 0))
    table = pl.BlockSpec((tm, LANES), lambda bi, i: (i, 0))
    wq = 3 * ATT_GROUP_WIDTH
    att_specs = [pl.BlockSpec((1, tm // dil, dil * wq), lambda bi, i: (bi, i, 0)) for _, dil in ATT_GROUPS]
    att_shapes = [jax.ShapeDtypeStruct((b, s // dil, dil * wq), BF16) for _, dil in ATT_GROUPS]
    gdn = jax.ShapeDtypeStruct((b, s, GDN_WIDTH), F32)
    n_staged = 3 * (len(ATT_GROUPS) - 1) * (ATT_GROUP_WIDTH // LANES)
    return pl.pallas_call(
        functools.partial(_mixer_in_kernel, tm=tm),
        grid=(b, s // tm),
        in_specs=[tile(D_MODEL), _resident((1, D_MODEL)),
                  pl.BlockSpec((D_MODEL, W_GATES), lambda bi, i: (0, 0), pipeline_mode=pl.Buffered(1)),
                  _resident((GDN_CONV, 3 * GDN_WIDTH)), _resident((1, LANES)), _resident((1, LANES)), table, table],
        out_specs=att_specs + [tile(GDN_WIDTH)] * 3 + [tile(LANES)],
        out_shape=att_shapes + [gdn] * 3 + [jax.ShapeDtypeStruct((b, s, LANES), F32)],
        scratch_shapes=[pltpu.VMEM((tm + SUBLANES, 3 * GDN_WIDTH), F32), pltpu.VMEM((n_staged, tm, LANES), F32)],
        compiler_params=pltpu.CompilerParams(dimension_semantics=("arbitrary", "arbitrary"),
                                             vmem_limit_bytes=VMEM_LIMIT_BYTES),
        name="mixer_in",
    )(x1, norm_g, w_all, conv_w, a_log, dt_bias, cos_t, sin_t)


def _attention_block(q, k, v, valid):
    lane = lax.broadcasted_iota(jnp.int32, (1, ATT_GROUP_WIDTH), 1)
    o = jnp.zeros((ATT_BLOCK, ATT_GROUP_WIDTH), F32)
    inv_l = jnp.zeros((ATT_BLOCK, ATT_GROUP_WIDTH), F32)
    lse = jnp.zeros((ATT_BLOCK, ATT_GROUP_WIDTH), F32)
    for hh in range(ATT_HEADS_PER_GROUP):
        in_head = (lane // ATT_HEAD_DIM) == hh
        keep = jnp.where(in_head, 1.0, 0.0).astype(BF16)
        s = _dot_nt(q * keep, k)
        s = jnp.where(valid, s, NEG_BIG)
        m = jnp.max(s, axis=-1, keepdims=True)
        p = jnp.exp(s - m)
        l = jnp.sum(p, axis=-1, keepdims=True)
        o = o + _dot(p.astype(BF16), v * keep)
        inv_l = jnp.where(in_head, 1.0 / l, inv_l)
        lse = jnp.where(in_head, m + jnp.log(l), lse)
    return o * inv_l, lse


def _attention_kernel(a0_ref, a1_ref, a2_ref, ya_ref, o0, l0, o1, l1, o2, l2, *, seq):
    in_refs = (a0_ref, a1_ref, a2_ref)
    o_refs = (o0, o1, o2)
    l_refs = (l0, l1, l2)
    qi = lax.broadcasted_iota(jnp.int32, (ATT_BLOCK, ATT_BLOCK), 0)
    kj = lax.broadcasted_iota(jnp.int32, (ATT_BLOCK, ATT_BLOCK), 1)
    causal = kj <= qi
    qi2 = lax.broadcasted_iota(jnp.int32, (ATT_BLOCK, 2 * ATT_BLOCK), 0)
    kj2 = lax.broadcasted_iota(jnp.int32, (ATT_BLOCK, 2 * ATT_BLOCK), 1)
    band = (kj2 >= qi2) & (kj2 - ATT_BLOCK <= qi2)

    for gi, (window, dil) in enumerate(ATT_GROUPS):
        assert window // dil == ATT_BLOCK
        length = seq // dil
        nblk = length // ATT_BLOCK
        src, o_ref, l_ref = in_refs[gi], o_refs[gi], l_refs[gi]
        wq = 3 * ATT_GROUP_WIDTH

        def store(o_ref, l_ref, dil, r, n, o, lse):
            if dil == 1:
                rows = pl.ds(pl.multiple_of(n * ATT_BLOCK, ATT_BLOCK), ATT_BLOCK)
            else:
                rows = pl.ds(n * ATT_BLOCK * dil + r, ATT_BLOCK, stride=dil)
            for half in range(ATT_GROUP_WIDTH // LANES):
                o_ref[half, rows, :] = o[:, half * LANES:(half + 1) * LANES]
                l_ref[half, rows, :] = lse[:, half * LANES:(half + 1) * LANES]

        for r in range(dil):
            base = r * wq
            q = src[0, 0:ATT_BLOCK, base:base + ATT_GROUP_WIDTH]
            k = src[0, 0:ATT_BLOCK, base + ATT_GROUP_WIDTH:base + 2 * ATT_GROUP_WIDTH]
            v = src[0, 0:ATT_BLOCK, base + 2 * ATT_GROUP_WIDTH:base + 3 * ATT_GROUP_WIDTH]
            o, lse = _attention_block(q, k, v, causal)
            store(o_ref, l_ref, dil, r, 0, o, lse)

            if nblk > 1:
                def body(n, carry, src=src, base=base, o_ref=o_ref, l_ref=l_ref, dil=dil, r=r):
                    qrows = pl.ds(pl.multiple_of(n * ATT_BLOCK, ATT_BLOCK), ATT_BLOCK)
                    krows = pl.ds(pl.multiple_of((n - 1) * ATT_BLOCK, ATT_BLOCK), 2 * ATT_BLOCK)
                    q = src[0, qrows, base:base + ATT_GROUP_WIDTH]
                    k = src[0, krows, base + ATT_GROUP_WIDTH:base + 2 * ATT_GROUP_WIDTH]
                    v = src[0, krows, base + 2 * ATT_GROUP_WIDTH:base + 3 * ATT_GROUP_WIDTH]
                    o, lse = _attention_block(q, k, v, band)
                    store(o_ref, l_ref, dil, r, n, o, lse)
                    return carry
                lax.fori_loop(1, nblk, body, 0)

    rows_per_step = 256

    def merge(i, carry):
        rows = pl.ds(pl.multiple_of(i * rows_per_step, rows_per_step), rows_per_step)
        for half in range(ATT_GROUP_WIDTH // LANES):
            la, lb, lc = l0[half, rows, :], l1[half, rows, :], l2[half, rows, :]
            m = jnp.maximum(jnp.maximum(la, lb), lc)
            ea, eb, ec = jnp.exp(la - m), jnp.exp(lb - m), jnp.exp(lc - m)
            num = ea * o0[half, rows, :] + eb * o1[half, rows, :] + ec * o2[half, rows, :]
            ya_ref[0, rows, half * LANES:(half + 1) * LANES] = num / (ea + eb + ec)
        return carry
    lax.fori_loop(0, seq // rows_per_step, merge, 0)


def _attention(a0, a1, a2):
    b, s, _ = a0.shape
    views = (a0, a1, a2)
    specs = [pl.BlockSpec((1,) + arr.shape[1:], lambda bi: (bi, 0, 0)) for arr in views]
    scratch = [pltpu.VMEM((ATT_GROUP_WIDTH // LANES, s, LANES), F32) for _ in range(6)]
    return pl.pallas_call(
        functools.partial(_attention_kernel, seq=s),
        grid=(b,),
        in_specs=specs,
        out_specs=pl.BlockSpec((1, s, ATT_GROUP_WIDTH), lambda bi: (bi, 0, 0)),
        out_shape=jax.ShapeDtypeStruct((b, s, ATT_GROUP_WIDTH), F32),
        scratch_shapes=scratch,
        compiler_params=pltpu.CompilerParams(dimension_semantics=("arbitrary",),
                                             vmem_limit_bytes=VMEM_LIMIT_BYTES),
        name="dilated_attention",
    )(*views)


def _deltanet_kernel(q_ref, k_ref, v_ref, bd_ref, gn_ref, o_ref, state_ref, *, tile):
    c = GDN_CHUNK
    d = GDN_HEAD_DIM

    @pl.when(pl.program_id(1) == 0)
    def _():
        state_ref[...] = jnp.zeros(state_ref.shape, F32)

    ii = lax.broadcasted_iota(jnp.int32, (c, c), 0)
    jj = lax.broadcasted_iota(jnp.int32, (c, c), 1)
    lower = ii >= jj
    strict = ii > jj
    tri_ones = jnp.where(lower, 1.0, 0.0).astype(F32)
    all_ones = jnp.ones((c, c), F32)
    gnorm = gn_ref[...]

    def chunk(ci, carry):
        rows = pl.ds(pl.multiple_of(ci * c, c), c)
        bd = bd_ref[0, rows, :]
        gcum = _dot_exact(tri_ones, bd)
        gtot = _dot_exact(all_ones, bd)
        gcum_t = gcum.T
        e_cum_all = jnp.exp(gcum)
        e_rest_all = jnp.exp(gtot - gcum)
        e_tot_all = jnp.exp(gtot)

        heads = range(GDN_HEADS)
        cols = [slice(hh * d, (hh + 1) * d) for hh in heads]
        glane = [GDN_HEADS + hh for hh in heads]
        q = [q_ref[0, rows, cols[hh]] for hh in heads]
        k = [k_ref[0, rows, cols[hh]] for hh in heads]
        v = [v_ref[0, rows, cols[hh]] for hh in heads]
        beta = [bd[:, hh:hh + 1] for hh in heads]
        e_cum = [e_cum_all[:, gl:gl + 1] for gl in glane]
        kbeta = [k[hh] * beta[hh] for hh in heads]
        kq = [_dot_nt(jnp.concatenate([kbeta[hh], q[hh]], axis=0).astype(BF16), k[hh].astype(BF16))
              for hh in heads]
        decay = [jnp.exp(jnp.where(lower, gcum[:, gl:gl + 1] - gcum_t[gl:gl + 1, :], NEG_BIG)) for gl in glane]
        m = [jnp.where(strict, kq[hh][0:c] * decay[hh], 0.0) for hh in heads]
        a_qk = [(kq[hh][c:2 * c] * decay[hh]).astype(BF16) for hh in heads]
        n = [-m[hh] for hh in heads]
        p = m
        for _ in range(5):
            pb = [p[hh].astype(BF16) for hh in heads]
            p = [_dot(pb[hh], pb[hh]) for hh in heads]
            n = [n[hh] + p[hh] + _dot(p[hh].astype(BF16), n[hh].astype(BF16)) for hh in heads]
        rhs = [jnp.concatenate([v[hh] * beta[hh], kbeta[hh] * e_cum[hh]], axis=1) for hh in heads]
        sol = [rhs[hh] + _dot(n[hh].astype(BF16), rhs[hh].astype(BF16)) for hh in heads]
        state = [state_ref[hh] for hh in heads]
        wq = [jnp.concatenate([sol[hh][:, d:2 * d], q[hh] * e_cum[hh]], axis=0).astype(BF16) for hh in heads]
        ws = [_dot(wq[hh], state[hh].astype(BF16)) for hh in heads]
        v_new = [(sol[hh][:, 0:d] - ws[hh][0:c]).astype(BF16) for hh in heads]
        o = [ws[hh][c:2 * c] + _dot(a_qk[hh], v_new[hh]) for hh in heads]
        k_dec = [(k[hh] * e_rest_all[:, gl:gl + 1]).astype(BF16) for hh, gl in zip(heads, glane)]
        for hh, gl in zip(heads, glane):
            state_ref[hh] = state[hh] * e_tot_all[0:1, gl:gl + 1] + _dot_tn(k_dec[hh], v_new[hh])
        for hh in heads:
            o_ref[0, rows, cols[hh]] = _rmsnorm(o[hh], gnorm)
        return carry

    lax.fori_loop(0, tile // c, chunk, 0)


def _deltanet(qb, kb, vb, bd, out_norm, *, tile):
    b, s, _ = qb.shape
    row = pl.BlockSpec((1, tile, GDN_WIDTH), lambda bi, i: (bi, i, 0))
    return pl.pallas_call(
        functools.partial(_deltanet_kernel, tile=tile),
        grid=(b, s // tile),
        in_specs=[row, row, row, pl.BlockSpec((1, tile, LANES), lambda bi, i: (bi, i, 0)),
                  _resident((1, GDN_HEAD_DIM))],
        out_specs=row,
        out_shape=jax.ShapeDtypeStruct((b, s, GDN_WIDTH), F32),
        scratch_shapes=[pltpu.VMEM((GDN_HEADS, GDN_HEAD_DIM, GDN_HEAD_DIM), F32)],
        compiler_params=pltpu.CompilerParams(dimension_semantics=("arbitrary", "arbitrary"),
                                             vmem_limit_bytes=VMEM_LIMIT_BYTES),
        name="gated_deltanet",
    )(qb, kb, vb, bd, out_norm)


def _mixer_out_kernel(x_ref, ya_ref, ob_ref, g_ref, wgt_ref, wa_ref, wb_ref, wo_ref, o_ref):
    x = x_ref[...]
    h = _rmsnorm(x, g_ref[...]).astype(BF16)
    gdn_gate = _dot(h, wgt_ref[:, 0:GDN_WIDTH])
    yb = (ob_ref[...] * (gdn_gate * _sigmoid(gdn_gate))).astype(BF16)
    gate_a = _dot(h, wgt_ref[:, GDN_WIDTH:GDN_WIDTH + D_MODEL])
    merged = _sigmoid(gate_a) * _dot(ya_ref[...].astype(BF16), wa_ref[...])
    gate_b = _dot(h, wgt_ref[:, GDN_WIDTH + D_MODEL:GDN_WIDTH + 2 * D_MODEL])
    merged = merged + _sigmoid(gate_b) * _dot(yb, wb_ref[...])
    o_ref[...] = x + _dot(merged.astype(BF16), wo_ref[...])


def _mixer_out(x1, ya, ob, norm_g, w_all, w_a, w_b, w_o, *, tm):
    n = x1.shape[0]
    row = lambda w: pl.BlockSpec((tm, w), lambda i: (i, 0))
    return pl.pallas_call(
        _mixer_out_kernel,
        grid=(n // tm,),
        in_specs=[row(D_MODEL), row(ATT_GROUP_WIDTH), row(GDN_WIDTH), _resident((1, D_MODEL)),
                  pl.BlockSpec((D_MODEL, W_GDN), lambda i: (0, W_GATES // W_GDN), pipeline_mode=pl.Buffered(1)),
                  _resident(w_a.shape), _resident(w_b.shape), _resident(w_o.shape)],
        out_specs=row(D_MODEL),
        out_shape=jax.ShapeDtypeStruct((n, D_MODEL), F32),
        compiler_params=pltpu.CompilerParams(dimension_semantics=("arbitrary",),
                                             vmem_limit_bytes=VMEM_LIMIT_BYTES),
        name="mixer_out",
    )(x1, ya, ob, norm_g, w_all, w_a, w_b, w_o)


def _rope_tables(seq):
    half = ATT_HEAD_DIM // 2
    inv_freq = ROPE_THETA ** (-jnp.arange(half, dtype=F32) / half)
    ang = jnp.arange(seq, dtype=F32)[:, None] * inv_freq[None, :]
    cos, sin = jnp.cos(ang), jnp.sin(ang)
    reps = LANES // ATT_HEAD_DIM
    return jnp.tile(jnp.concatenate([cos, cos], axis=-1), (1, reps)), jnp.tile(jnp.concatenate([-sin, sin], axis=-1), (1, reps))


def _pad_lanes(row, offset):
    return jnp.zeros((1, LANES), F32).at[0, offset:offset + row.shape[0]].set(row.astype(F32))


def _layer(x, ffn1_norm, ffn1_w_gate, ffn1_w_up, ffn1_w_down, mix_norm, w_in, gdn_conv_w, gdn_a_log, gdn_dt_bias,
           gdn_out_norm, w_branch_a, w_branch_b, w_out, ffn2_norm, ffn2_w_gate, ffn2_w_up, ffn2_w_down, fin_g,
           *, final_norm, tm_ffn, tm_mix, gdn_tile):
    b, s, _ = x.shape
    n = b * s
    row = lambda v: v.reshape(1, -1).astype(F32)
    x1 = _ffn(x.reshape(n, D_MODEL), row(ffn1_norm), ffn1_w_gate.astype(BF16), ffn1_w_up.astype(BF16),
              ffn1_w_down.astype(BF16), fin_g, final_norm=False, tm=tm_ffn)

    w_all = jnp.concatenate(
        [w_in[:, :W_IN_GDN], w_in[:, W_IN_BD:W_IN_GATES], jnp.zeros((D_MODEL, W_GDN - W_BD - 2 * GDN_HEADS), w_in.dtype),
         w_in[:, W_IN_GDN:W_IN_BD], w_in[:, W_IN_GATES:]], axis=1).astype(BF16)
    cos_t, sin_t = _rope_tables(s)
    a0, a1, a2, qb, kb, vb, bd = _mixer_in(
        x1.reshape(b, s, D_MODEL), row(mix_norm), w_all, gdn_conv_w.astype(F32),
        _pad_lanes(gdn_a_log, GDN_HEADS), _pad_lanes(gdn_dt_bias, GDN_HEADS), cos_t, sin_t, tm=tm_mix)

    ya = _attention(a0, a1, a2)
    ob = _deltanet(qb, kb, vb, bd, row(gdn_out_norm), tile=gdn_tile)

    x2 = _mixer_out(x1, ya.reshape(n, ATT_GROUP_WIDTH), ob.reshape(n, GDN_WIDTH), row(mix_norm), w_all,
                    w_branch_a.astype(BF16), w_branch_b.astype(BF16), w_out.astype(BF16), tm=tm_mix)
    x3 = _ffn(x2, row(ffn2_norm), ffn2_w_gate.astype(BF16), ffn2_w_up.astype(BF16), ffn2_w_down.astype(BF16),
              fin_g, final_norm=final_norm, tm=tm_ffn)
    return x3.reshape(b, s, D_MODEL)


def kernel(x, ffn1_norm, ffn1_w_gate, ffn1_w_up, ffn1_w_down, mix_norm, w_in, gdn_conv_w, gdn_a_log, gdn_dt_bias,
           gdn_out_norm, w_branch_a, w_branch_b, w_out, ffn2_norm, ffn2_w_gate, ffn2_w_up, ffn2_w_down, final_norm):
    depth = ffn1_norm.shape[0]
    fin_g = final_norm.reshape(1, -1).astype(F32)
    for layer in range(depth):
        x = _layer(x, ffn1_norm[layer], ffn1_w_gate[layer], ffn1_w_up[layer], ffn1_w_down[layer], mix_norm[layer],
                   w_in[layer], gdn_conv_w[layer], gdn_a_log[layer], gdn_dt_bias[layer], gdn_out_norm[layer],
                   w_branch_a[layer], w_branch_b[layer], w_out[layer], ffn2_norm[layer], ffn2_w_gate[layer],
                   ffn2_w_up[layer], ffn2_w_down[layer], fin_g, final_norm=(layer == depth - 1),
                   tm_ffn=512, tm_mix=512, gdn_tile=512)
    return x
```

```python
import functools

import jax
import jax.numpy as jnp
from jax import lax
from jax.experimental import pallas as pl
from jax.experimental.pallas import tpu as pltpu

F32 = jnp.float32
BF16 = jnp.bfloat16

D_MODEL = 1024
D_FF = 2816
EPS = 1e-6

ATT_GROUPS = ((128, 1), (512, 4), (2048, 16))
ATT_HEADS_PER_GROUP = 4
ATT_HEAD_DIM = 64
ATT_BLOCK = 128
ATT_GROUP_WIDTH = ATT_HEADS_PER_GROUP * ATT_HEAD_DIM
ATT_QKV_WIDTH = len(ATT_GROUPS) * ATT_GROUP_WIDTH
ROPE_THETA = 10000.0

GDN_HEADS = 8
GDN_HEAD_DIM = 128
GDN_WIDTH = GDN_HEADS * GDN_HEAD_DIM
GDN_CONV = 4
GDN_CHUNK = 64

LANES = 128
SUBLANES = 8
VMEM_LIMIT_BYTES = 56 * 1024 * 1024

W_IN_GDN = 3 * ATT_QKV_WIDTH
W_IN_BD = W_IN_GDN + 3 * GDN_WIDTH
W_IN_GATES = W_IN_BD + 2 * GDN_HEADS
W_ATT = 0
W_BD = 3 * ATT_QKV_WIDTH
W_GDN = 3 * GDN_WIDTH
W_GATES = 2 * W_GDN
FFN_CHUNKS = ((0, 768), (768, 1536), (1536, 2304), (2304, 2816))
NEG_BIG = -1e30


def _resident(shape):
    nd = len(shape)
    return pl.BlockSpec(shape, lambda *_: (0,) * nd, pipeline_mode=pl.Buffered(1))


def _rmsnorm(x, g):
    return x * lax.rsqrt(jnp.mean(x * x, axis=-1, keepdims=True) + EPS) * g


def _sigmoid(x):
    return 1.0 / (1.0 + jnp.exp(-x))


def _dot(a, b):
    return jnp.dot(a, b, preferred_element_type=F32)


def _dot_nt(a, b):
    return lax.dot_general(a, b, (((1,), (1,)), ((), ())), preferred_element_type=F32)


def _dot_tn(a, b):
    return lax.dot_general(a, b, (((0,), (0,)), ((), ())), preferred_element_type=F32)


def _dot_exact(a, b):
    return jnp.dot(a, b, preferred_element_type=F32, precision=lax.Precision.HIGHEST)


def _swiglu_residual(x, g, wg_ref, wu_ref, wd_ref):
    h = _rmsnorm(x, g).astype(BF16)
    acc = x
    for lo, hi in FFN_CHUNKS:
        gate = _dot(h, wg_ref[:, lo:hi])
        up = _dot(h, wu_ref[:, lo:hi])
        act = (0.5 * gate * _sigmoid(gate) * up).astype(BF16)
        acc = acc + _dot(act, wd_ref[lo:hi, :])
    return acc


def _ffn_kernel(x_ref, g_ref, wg_ref, wu_ref, wd_ref, fin_ref, o_ref, *, final_norm):
    y = _swiglu_residual(x_ref[...], g_ref[...], wg_ref, wu_ref, wd_ref)
    if final_norm:
        y = _rmsnorm(y, fin_ref[...])
    o_ref[...] = y


def _ffn(x, norm_g, wg, wu, wd, fin_g, *, final_norm, tm):
    n = x.shape[0]
    row = pl.BlockSpec((tm, D_MODEL), lambda i: (i, 0))
    return pl.pallas_call(
        functools.partial(_ffn_kernel, final_norm=final_norm),
        grid=(n // tm,),
        in_specs=[row, _resident((1, D_MODEL)), _resident((D_MODEL, D_FF)), _resident((D_MODEL, D_FF)),
                  _resident((D_FF, D_MODEL)), _resident((1, D_MODEL))],
        out_specs=row,
        out_shape=jax.ShapeDtypeStruct((n, D_MODEL), F32),
        compiler_params=pltpu.CompilerParams(dimension_semantics=("arbitrary",),
                                             vmem_limit_bytes=VMEM_LIMIT_BYTES),
        name="ffn_final" if final_norm else "ffn",
    )(x, norm_g, wg, wu, wd, fin_g)


def _mixer_in_kernel(x_ref, g_ref, wm_ref, convw_ref, alog_ref, dtb_ref, cos_ref, sin_ref,
                     a0_ref, a1_ref, a2_ref, qb_ref, kb_ref, vb_ref, bd_ref, conv_ref, stage_ref, *, tm):
    @pl.when(pl.program_id(1) == 0)
    def _():
        conv_ref[0:SUBLANES, :] = jnp.zeros((SUBLANES, 3 * GDN_WIDTH), F32)

    h = _rmsnorm(x_ref[0], g_ref[...]).astype(BF16)
    for part in range(3):
        lo = W_GDN + part * GDN_WIDTH
        conv_ref[SUBLANES:SUBLANES + tm, part * GDN_WIDTH:(part + 1) * GDN_WIDTH] = _dot(h, wm_ref[:, lo:lo + GDN_WIDTH])
    ya = _dot(h, wm_ref[:, W_ATT:W_ATT + 3 * ATT_QKV_WIDTH])
    raw = _dot(h, wm_ref[:, W_BD:W_BD + LANES])

    out_refs = (qb_ref, kb_ref, vb_ref)
    for part in range(3):
        for hh in range(GDN_HEADS):
            col = part * GDN_WIDTH + hh * GDN_HEAD_DIM
            acc = None
            for t in range(GDN_CONV):
                start = SUBLANES - (GDN_CONV - 1) + t
                term = conv_ref[start:start + tm, col:col + GDN_HEAD_DIM] * convw_ref[t:t + 1, col:col + GDN_HEAD_DIM]
                acc = term if acc is None else acc + term
            act = acc * _sigmoid(acc)
            if part < 2:
                act = act * lax.rsqrt(jnp.sum(act * act, axis=-1, keepdims=True) + EPS)
            if part == 0:
                act = act * (GDN_HEAD_DIM ** -0.5)
            out_refs[part][0, :, hh * GDN_HEAD_DIM:(hh + 1) * GDN_HEAD_DIM] = act
    conv_ref[0:SUBLANES, :] = conv_ref[tm:tm + SUBLANES, :]

    cos = cos_ref[...]
    sin = sin_ref[...]
    lane = lax.broadcasted_iota(jnp.int32, (1, LANES), 1)
    first_half = (lane % ATT_HEAD_DIM) < (ATT_HEAD_DIM // 2)
    att_refs = (a0_ref, a1_ref, a2_ref)
    blocks_per_group = ATT_GROUP_WIDTH // LANES
    slot = 0
    for part in range(3):
        for gi, (_, dil) in enumerate(ATT_GROUPS):
            for j in range(blocks_per_group):
                col = part * ATT_QKV_WIDTH + gi * ATT_GROUP_WIDTH + j * LANES
                blk = ya[:, col:col + LANES]
                if part < 2:
                    swapped = jnp.where(first_half, pltpu.roll(blk, LANES - ATT_HEAD_DIM // 2, 1),
                                        pltpu.roll(blk, ATT_HEAD_DIM // 2, 1))
                    blk = blk * cos + swapped * sin
                if part == 0:
                    blk = blk * (ATT_HEAD_DIM ** -0.5)
                dst = part * ATT_GROUP_WIDTH + j * LANES
                if dil == 1:
                    att_refs[gi][0, :, dst:dst + LANES] = blk.astype(BF16)
                else:
                    stage_ref[slot] = blk
                    for r in range(dil):
                        rows = stage_ref[slot, pl.ds(r, tm // dil, stride=dil), :]
                        lo = r * 3 * ATT_GROUP_WIDTH + dst
                        att_refs[gi][0, :, lo:lo + LANES] = rows.astype(BF16)
                    slot += 1

    z = raw + dtb_ref[...]
    softplus = jnp.maximum(z, 0.0) + jnp.log1p(jnp.exp(-jnp.abs(z)))
    g = -jnp.exp(alog_ref[...]) * softplus
    bd_ref[0] = jnp.where(lane < GDN_HEADS, _sigmoid(raw), jnp.where(lane < 2 * GDN_HEADS, g, 0.0))


def _mixer_in(x1, norm_g, w_all, conv_w, a_log, dt_bias, cos_t, sin_t, *, tm):
    b, s, _ = x1.shape
    tile = lambda w: pl.BlockSpec((1, tm, w), lambda bi, i: (bi, i, 0))
    table = pl.BlockSpec((tm, LANES), lambda bi, i: (i, 0))
    wq = 3 * ATT_GROUP_WIDTH
    att_specs = [pl.BlockSpec((1, tm // dil, dil * wq), lambda bi, i: (bi, i, 0)) for _, dil in ATT_GROUPS]
    att_shapes = [jax.ShapeDtypeStruct((b, s // dil, dil * wq), BF16) for _, dil in ATT_GROUPS]
    gdn = jax.ShapeDtypeStruct((b, s, GDN_WIDTH), F32)
    n_staged = 3 * (len(ATT_GROUPS) - 1) * (ATT_GROUP_WIDTH // LANES)
    return pl.pallas_call(
        functools.partial(_mixer_in_kernel, tm=tm),
        grid=(b, s // tm),
        in_specs=[tile(D_MODEL), _resident((1, D_MODEL)),
                  pl.BlockSpec((D_MODEL, W_GATES), lambda bi, i: (0, 0), pipeline_mode=pl.Buffered(1)),
                  _resident((GDN_CONV, 3 * GDN_WIDTH)), _resident((1, LANES)), _resident((1, LANES)), table, table],
        out_specs=att_specs + [tile(GDN_WIDTH)] * 3 + [tile(LANES)],
        out_shape=att_shapes + [gdn] * 3 + [jax.ShapeDtypeStruct((b, s, LANES), F32)],
        scratch_shapes=[pltpu.VMEM((tm + SUBLANES, 3 * GDN_WIDTH), F32), pltpu.VMEM((n_staged, tm, LANES), F32)],
        compiler_params=pltpu.CompilerParams(dimension_semantics=("arbitrary", "arbitrary"),
                                             vmem_limit_bytes=VMEM_LIMIT_BYTES),
        name="mixer_in",
    )(x1, norm_g, w_all, conv_w, a_log, dt_bias, cos_t, sin_t)


def _attention_block(q, k, v, valid):
    lane = lax.broadcasted_iota(jnp.int32, (1, ATT_GROUP_WIDTH), 1)
    o = jnp.zeros((ATT_BLOCK, ATT_GROUP_WIDTH), F32)
    inv_l = jnp.zeros((ATT_BLOCK, ATT_GROUP_WIDTH), F32)
    lse = jnp.zeros((ATT_BLOCK, ATT_GROUP_WIDTH), F32)
    for hh in range(ATT_HEADS_PER_GROUP):
        in_head = (lane // ATT_HEAD_DIM) == hh
        keep = jnp.where(in_head, 1.0, 0.0).astype(BF16)
        s = _dot_nt(q * keep, k)
        s = jnp.where(valid, s, NEG_BIG)
        m = jnp.max(s, axis=-1, keepdims=True)
        p = jnp.exp(s - m)
        l = jnp.sum(p, axis=-1, keepdims=True)
        o = o + _dot(p.astype(BF16), v * keep)
        inv_l = jnp.where(in_head, 1.0 / l, inv_l)
        lse = jnp.where(in_head, m + jnp.log(l), lse)
    return o * inv_l, lse


def _attention_kernel(a0_ref, a1_ref, a2_ref, ya_ref, o0, l0, o1, l1, o2, l2, *, seq):
    in_refs = (a0_ref, a1_ref, a2_ref)
    o_refs = (o0, o1, o2)
    l_refs = (l0, l1, l2)
    qi = lax.broadcasted_iota(jnp.int32, (ATT_BLOCK, ATT_BLOCK), 0)
    kj = lax.broadcasted_iota(jnp.int32, (ATT_BLOCK, ATT_BLOCK), 1)
    causal = kj <= qi
    qi2 = lax.broadcasted_iota(jnp.int32, (ATT_BLOCK, 2 * ATT_BLOCK), 0)
    kj2 = lax.broadcasted_iota(jnp.int32, (ATT_BLOCK, 2 * ATT_BLOCK), 1)
    band = (kj2 >= qi2) & (kj2 - ATT_BLOCK <= qi2)

    for gi, (window, dil) in enumerate(ATT_GROUPS):
        assert window // dil == ATT_BLOCK
        length = seq // dil
        nblk = length // ATT_BLOCK
        src, o_ref, l_ref = in_refs[gi], o_refs[gi], l_refs[gi]
        wq = 3 * ATT_GROUP_WIDTH

        def store(o_ref, l_ref, dil, r, n, o, lse):
            if dil == 1:
                rows = pl.ds(pl.multiple_of(n * ATT_BLOCK, ATT_BLOCK), ATT_BLOCK)
            else:
                rows = pl.ds(n * ATT_BLOCK * dil + r, ATT_BLOCK, stride=dil)
            for half in range(ATT_GROUP_WIDTH // LANES):
                o_ref[half, rows, :] = o[:, half * LANES:(half + 1) * LANES]
                l_ref[half, rows, :] = lse[:, half * LANES:(half + 1) * LANES]

        for r in range(dil):
            base = r * wq
            q = src[0, 0:ATT_BLOCK, base:base + ATT_GROUP_WIDTH]
            k = src[0, 0:ATT_BLOCK, base + ATT_GROUP_WIDTH:base + 2 * ATT_GROUP_WIDTH]
            v = src[0, 0:ATT_BLOCK, base + 2 * ATT_GROUP_WIDTH:base + 3 * ATT_GROUP_WIDTH]
            o, lse = _attention_block(q, k, v, causal)
            store(o_ref, l_ref, dil, r, 0, o, lse)

            if nblk > 1:
                def body(n, carry, src=src, base=base, o_ref=o_ref, l_ref=l_ref, dil=dil, r=r):
                    qrows = pl.ds(pl.multiple_of(n * ATT_BLOCK, ATT_BLOCK), ATT_BLOCK)
                    krows = pl.ds(pl.multiple_of((n - 1) * ATT_BLOCK, ATT_BLOCK), 2 * ATT_BLOCK)
                    q = src[0, qrows, base:base + ATT_GROUP_WIDTH]
                    k = src[0, krows, base + ATT_GROUP_WIDTH:base + 2 * ATT_GROUP_WIDTH]
                    v = src[0, krows, base + 2 * ATT_GROUP_WIDTH:base + 3 * ATT_GROUP_WIDTH]
                    o, lse = _attention_block(q, k, v, band)
                    store(o_ref, l_ref, dil, r, n, o, lse)
                    return carry
                lax.fori_loop(1, nblk, body, 0)

    rows_per_step = 256

    def merge(i, carry):
        rows = pl.ds(pl.multiple_of(i * rows_per_step, rows_per_step), rows_per_step)
        for half in range(ATT_GROUP_WIDTH // LANES):
            la, lb, lc = l0[half, rows, :], l1[half, rows, :], l2[half, rows, :]
            m = jnp.maximum(jnp.maximum(la, lb), lc)
            ea, eb, ec = jnp.exp(la - m), jnp.exp(lb - m), jnp.exp(lc - m)
            num = ea * o0[half, rows, :] + eb * o1[half, rows, :] + ec * o2[half, rows, :]
            ya_ref[0, rows, half * LANES:(half + 1) * LANES] = num / (ea + eb + ec)
        return carry
    lax.fori_loop(0, seq // rows_per_step, merge, 0)


def _attention(a0, a1, a2):
    b, s, _ = a0.shape
    views = (a0, a1, a2)
    specs = [pl.BlockSpec((1,) + arr.shape[1:], lambda bi: (bi, 0, 0)) for arr in views]
    scratch = [pltpu.VMEM((ATT_GROUP_WIDTH // LANES, s, LANES), F32) for _ in range(6)]
    return pl.pallas_call(
        functools.partial(_attention_kernel, seq=s),
        grid=(b,),
        in_specs=specs,
        out_specs=pl.BlockSpec((1, s, ATT_GROUP_WIDTH), lambda bi: (bi, 0, 0)),
        out_shape=jax.ShapeDtypeStruct((b, s, ATT_GROUP_WIDTH), F32),
        scratch_shapes=scratch,
        compiler_params=pltpu.CompilerParams(dimension_semantics=("arbitrary",),
                                             vmem_limit_bytes=VMEM_LIMIT_BYTES),
        name="dilated_attention",
    )(*views)


def _deltanet_stages(q_ref, k_ref, v_ref, bd_ref, gnorm, state_ref, ob_ref, slot, *, tile):
    c = GDN_CHUNK
    d = GDN_HEAD_DIM
    ii = lax.broadcasted_iota(jnp.int32, (c, c), 0)
    jj = lax.broadcasted_iota(jnp.int32, (c, c), 1)
    lower = ii >= jj
    strict = ii > jj
    tri_ones = jnp.where(lower, 1.0, 0.0).astype(F32)
    all_ones = jnp.ones((c, c), F32)
    heads = range(GDN_HEADS)
    cols = [slice(hh * d, (hh + 1) * d) for hh in heads]
    glane = [GDN_HEADS + hh for hh in heads]

    for ci in range(tile // c):
        rows = slice(ci * c, (ci + 1) * c)
        bd = bd_ref[rows, :]
        gcum = _dot_exact(tri_ones, bd)
        gtot = _dot_exact(all_ones, bd)
        yield
        gcum_t = gcum.T
        e_cum_all = jnp.exp(gcum)
        e_rest_all = jnp.exp(gtot - gcum)
        e_tot_all = jnp.exp(gtot)
        q = [q_ref[rows, cols[hh]] for hh in heads]
        k = [k_ref[rows, cols[hh]] for hh in heads]
        v = [v_ref[rows, cols[hh]] for hh in heads]
        beta = [bd[:, hh:hh + 1] for hh in heads]
        e_cum = [e_cum_all[:, gl:gl + 1] for gl in glane]
        kbeta = [k[hh] * beta[hh] for hh in heads]
        kq = [_dot_nt(jnp.concatenate([kbeta[hh], q[hh]], axis=0).astype(BF16), k[hh].astype(BF16))
              for hh in heads]
        yield
        decay = [jnp.exp(jnp.where(lower, gcum[:, gl:gl + 1] - gcum_t[gl:gl + 1, :], NEG_BIG)) for gl in glane]
        m = [jnp.where(strict, kq[hh][0:c] * decay[hh], 0.0) for hh in heads]
        a_qk = [(kq[hh][c:2 * c] * decay[hh]).astype(BF16) for hh in heads]
        n = [-m[hh] for hh in heads]
        pb = [m[hh].astype(BF16) for hh in heads]
        p = [_dot(pb[hh], pb[hh]) for hh in heads]
        yield
        rounds = 5
        for r in range(rounds):
            pb = [p[hh].astype(BF16) for hh in heads]
            upd = [_dot(pb[hh], n[hh].astype(BF16)) for hh in heads]
            p_next = [_dot(pb[hh], pb[hh]) for hh in heads] if r + 1 < rounds else None
            yield
            n = [n[hh] + p[hh] + upd[hh] for hh in heads]
            p = p_next
        rhs = [jnp.concatenate([v[hh] * beta[hh], kbeta[hh] * e_cum[hh]], axis=1) for hh in heads]
        nr = [_dot(n[hh].astype(BF16), rhs[hh].astype(BF16)) for hh in heads]
        yield
        sol = [rhs[hh] + nr[hh] for hh in heads]
        state = [state_ref[hh] for hh in heads]
        wq = [jnp.concatenate([sol[hh][:, d:2 * d], q[hh] * e_cum[hh]], axis=0).astype(BF16) for hh in heads]
        ws = [_dot(wq[hh], state[hh].astype(BF16)) for hh in heads]
        yield
        v_new = [(sol[hh][:, 0:d] - ws[hh][0:c]).astype(BF16) for hh in heads]
        k_dec = [(k[hh] * e_rest_all[:, gl:gl + 1]).astype(BF16) for hh, gl in zip(heads, glane)]
        kv = [_dot_tn(k_dec[hh], v_new[hh]) for hh in heads]
        av = [_dot(a_qk[hh], v_new[hh]) for hh in heads]
        yield
        for hh, gl in zip(heads, glane):
            state_ref[hh] = state[hh] * e_tot_all[0:1, gl:gl + 1] + kv[hh]
        for hh in heads:
            ob_ref[slot, rows, cols[hh]] = _rmsnorm(ws[hh][c:2 * c] + av[hh], gnorm)


GDN_LAYERS_PER_CHUNK = 11


MIX_BLOCK = 256


def _mixer_out_stages(x_ref, ya_ref, ob_ref, slot, g_ref, wgt_ref, wa_ref, wb_ref, wo_ref, o_ref):
    x = x_ref[...]
    h = _rmsnorm(x, g_ref[...]).astype(BF16)
    ya = ya_ref[...].astype(BF16)
    blocks = [slice(j * MIX_BLOCK, (j + 1) * MIX_BLOCK) for j in range(D_MODEL // MIX_BLOCK)]
    gate_cols = lambda which, blk: slice(which * D_MODEL + blk.start, which * D_MODEL + blk.stop)
    yb = []
    for blk in blocks:
        gdn_gate = _dot(h, wgt_ref[:, gate_cols(0, blk)])
        yield
        yb.append((ob_ref[slot, :, blk] * (gdn_gate * _sigmoid(gdn_gate))).astype(BF16))
    yb = jnp.concatenate(yb, axis=1)
    merged = []
    for blk in blocks:
        gate_a = _dot(h, wgt_ref[:, gate_cols(1, blk)])
        branch_a = _dot(ya, wa_ref[:, blk])
        yield
        gate_b = _dot(h, wgt_ref[:, gate_cols(2, blk)])
        yield
        branch_b = _dot(yb, wb_ref[:, blk])
        yield
        merged.append((_sigmoid(gate_a) * branch_a + _sigmoid(gate_b) * branch_b).astype(BF16))
    merged = jnp.concatenate(merged, axis=1)
    for blk in blocks:
        o_ref[:, blk] = x[:, blk] + _dot(merged, wo_ref[:, blk])
        yield


MIX_GRANULES = 5 * (D_MODEL // MIX_BLOCK)


def _interleave(primary, n_primary, secondary, n_secondary):
    done = 0
    for i in range(n_primary):
        next(primary)
        while done * n_primary < (i + 1) * n_secondary:
            next(secondary)
            done += 1
    for gen in (primary, secondary):
        for _ in gen:
            pass


def _mixer_tail_kernel(q_ref, k_ref, v_ref, bd_ref, gn_ref, x_ref, ya_ref, g_ref, wgt_ref, wa_ref, wb_ref, wo_ref,
                       o_ref, state_ref, ob_ref, *, tile, tiles_per_seq, n_tiles):
    step = pl.program_id(0)

    @pl.when(step == 0)
    def _():
        ob_ref[...] = jnp.zeros(ob_ref.shape, F32)

    @pl.when(jnp.minimum(step, n_tiles - 1) % tiles_per_seq == 0)
    def _():
        state_ref[...] = jnp.zeros(state_ref.shape, F32)

    slot = step % 2
    gdn = _deltanet_stages(q_ref, k_ref, v_ref, bd_ref, gn_ref[...], state_ref, ob_ref, slot, tile=tile)
    mix = _mixer_out_stages(x_ref, ya_ref, ob_ref, 1 - slot, g_ref, wgt_ref, wa_ref, wb_ref, wo_ref, o_ref)
    _interleave(gdn, GDN_LAYERS_PER_CHUNK * (tile // GDN_CHUNK), mix, MIX_GRANULES)


def _mixer_tail(qb, kb, vb, bd, out_norm, x1, ya, norm_g, w_all, w_a, w_b, w_o, *, tile, seq):
    n = x1.shape[0]
    n_tiles = n // tile
    cur = lambda w: pl.BlockSpec((tile, w), lambda s: (jnp.minimum(s, n_tiles - 1), 0))
    prev = lambda w: pl.BlockSpec((tile, w), lambda s: (jnp.maximum(s - 1, 0), 0))
    return pl.pallas_call(
        functools.partial(_mixer_tail_kernel, tile=tile, tiles_per_seq=seq // tile, n_tiles=n_tiles),
        grid=(n_tiles + 1,),
        in_specs=[cur(GDN_WIDTH), cur(GDN_WIDTH), cur(GDN_WIDTH), cur(LANES), _resident((1, GDN_HEAD_DIM)),
                  prev(D_MODEL), prev(ATT_GROUP_WIDTH), _resident((1, D_MODEL)),
                  pl.BlockSpec((D_MODEL, W_GDN), lambda s: (0, W_GATES // W_GDN), pipeline_mode=pl.Buffered(1)),
                  _resident(w_a.shape), _resident(w_b.shape), _resident(w_o.shape)],
        out_specs=prev(D_MODEL),
        out_shape=jax.ShapeDtypeStruct((n, D_MODEL), F32),
        scratch_shapes=[pltpu.VMEM((GDN_HEADS, GDN_HEAD_DIM, GDN_HEAD_DIM), F32),
                        pltpu.VMEM((2, tile, GDN_WIDTH), F32)],
        compiler_params=pltpu.CompilerParams(dimension_semantics=("arbitrary",),
                                             vmem_limit_bytes=VMEM_LIMIT_BYTES),
        name="deltanet_mixer_out",
    )(qb, kb, vb, bd, out_norm, x1, ya, norm_g, w_all, w_a, w_b, w_o)


def _rope_tables(seq):
    half = ATT_HEAD_DIM // 2
    inv_freq = ROPE_THETA ** (-jnp.arange(half, dtype=F32) / half)
    ang = jnp.arange(seq, dtype=F32)[:, None] * inv_freq[None, :]
    cos, sin = jnp.cos(ang), jnp.sin(ang)
    reps = LANES // ATT_HEAD_DIM
    return jnp.tile(jnp.concatenate([cos, cos], axis=-1), (1, reps)), jnp.tile(jnp.concatenate([-sin, sin], axis=-1), (1, reps))


def _pad_lanes(row, offset):
    return jnp.zeros((1, LANES), F32).at[0, offset:offset + row.shape[0]].set(row.astype(F32))


def _layer(x, ffn1_norm, ffn1_w_gate, ffn1_w_up, ffn1_w_down, mix_norm, w_in, gdn_conv_w, gdn_a_log, gdn_dt_bias,
           gdn_out_norm, w_branch_a, w_branch_b, w_out, ffn2_norm, ffn2_w_gate, ffn2_w_up, ffn2_w_down, fin_g,
           *, final_norm, tm_ffn, tm_mix, gdn_tile):
    b, s, _ = x.shape
    n = b * s
    row = lambda v: v.reshape(1, -1).astype(F32)
    x1 = _ffn(x.reshape(n, D_MODEL), row(ffn1_norm), ffn1_w_gate.astype(BF16), ffn1_w_up.astype(BF16),
              ffn1_w_down.astype(BF16), fin_g, final_norm=False, tm=tm_ffn)

    w_all = jnp.concatenate(
        [w_in[:, :W_IN_GDN], w_in[:, W_IN_BD:W_IN_GATES], jnp.zeros((D_MODEL, W_GDN - W_BD - 2 * GDN_HEADS), w_in.dtype),
         w_in[:, W_IN_GDN:W_IN_BD], w_in[:, W_IN_GATES:]], axis=1).astype(BF16)
    cos_t, sin_t = _rope_tables(s)
    a0, a1, a2, qb, kb, vb, bd = _mixer_in(
        x1.reshape(b, s, D_MODEL), row(mix_norm), w_all, gdn_conv_w.astype(F32),
        _pad_lanes(gdn_a_log, GDN_HEADS), _pad_lanes(gdn_dt_bias, GDN_HEADS), cos_t, sin_t, tm=tm_mix)

    ya = _attention(a0, a1, a2)
    flat = lambda a: a.reshape(n, a.shape[-1])
    x2 = _mixer_tail(flat(qb), flat(kb), flat(vb), flat(bd), row(gdn_out_norm), x1, flat(ya), row(mix_norm), w_all,
                     w_branch_a.astype(BF16), w_branch_b.astype(BF16), w_out.astype(BF16), tile=gdn_tile, seq=s)
    x3 = _ffn(x2, row(ffn2_norm), ffn2_w_gate.astype(BF16), ffn2_w_up.astype(BF16), ffn2_w_down.astype(BF16),
              fin_g, final_norm=final_norm, tm=tm_ffn)
    return x3.reshape(b, s, D_MODEL)


def kernel(x, ffn1_norm, ffn1_w_gate, ffn1_w_up, ffn1_w_down, mix_norm, w_in, gdn_conv_w, gdn_a_log, gdn_dt_bias,
           gdn_out_norm, w_branch_a, w_branch_b, w_out, ffn2_norm, ffn2_w_gate, ffn2_w_up, ffn2_w_down, final_norm):
    depth = ffn1_norm.shape[0]
    fin_g = final_norm.reshape(1, -1).astype(F32)
    for layer in range(depth):
        x = _layer(x, ffn1_norm[layer], ffn1_w_gate[layer], ffn1_w_up[layer], ffn1_w_down[layer], mix_norm[layer],
                   w_in[layer], gdn_conv_w[layer], gdn_a_log[layer], gdn_dt_bias[layer], gdn_out_norm[layer],
                   w_branch_a[layer], w_branch_b[layer], w_out[layer], ffn2_norm[layer], ffn2_w_gate[layer],
                   ffn2_w_up[layer], ffn2_w_down[layer], fin_g, final_norm=(layer == depth - 1),
                   tm_ffn=512, tm_mix=512, gdn_tile=512)
    return x
```

```python
import functools

import jax
import jax.numpy as jnp
from jax import lax
from jax.experimental import pallas as pl
from jax.experimental.pallas import tpu as pltpu

F32 = jnp.float32
BF16 = jnp.bfloat16

D_MODEL = 1024
D_FF = 2816
EPS = 1e-6

ATT_GROUPS = ((128, 1), (512, 4), (2048, 16))
ATT_HEADS_PER_GROUP = 4
ATT_HEAD_DIM = 64
ATT_BLOCK = 128
ATT_GROUP_WIDTH = ATT_HEADS_PER_GROUP * ATT_HEAD_DIM
ATT_QKV_WIDTH = len(ATT_GROUPS) * ATT_GROUP_WIDTH
ROPE_THETA = 10000.0

GDN_HEADS = 8
GDN_HEAD_DIM = 128
GDN_WIDTH = GDN_HEADS * GDN_HEAD_DIM
GDN_CONV = 4
GDN_CHUNK = 64

LANES = 128
SUBLANES = 8
VMEM_LIMIT_BYTES = 56 * 1024 * 1024

W_IN_GDN = 3 * ATT_QKV_WIDTH
W_IN_BD = W_IN_GDN + 3 * GDN_WIDTH
W_IN_GATES = W_IN_BD + 2 * GDN_HEADS
W_ATT = 0
W_BD = 3 * ATT_QKV_WIDTH
W_GDN = 3 * GDN_WIDTH
W_GATES = 2 * W_GDN
FFN_CHUNKS = ((0, 768), (768, 1536), (1536, 2304), (2304, 2816))
NEG_BIG = -1e30


def _resident(shape):
    nd = len(shape)
    return pl.BlockSpec(shape, lambda *_: (0,) * nd, pipeline_mode=pl.Buffered(1))


def _rmsnorm(x, g):
    return x * lax.rsqrt(jnp.mean(x * x, axis=-1, keepdims=True) + EPS) * g


def _sigmoid(x):
    return 1.0 / (1.0 + jnp.exp(-x))


def _dot(a, b):
    return jnp.dot(a, b, preferred_element_type=F32)


def _dot_nt(a, b):
    return lax.dot_general(a, b, (((1,), (1,)), ((), ())), preferred_element_type=F32)


def _dot_tn(a, b):
    return lax.dot_general(a, b, (((0,), (0,)), ((), ())), preferred_element_type=F32)


def _swiglu_residual(x, g, wg_ref, wu_ref, wd_ref):
    h = _rmsnorm(x, g).astype(BF16)
    acc = x
    for lo, hi in FFN_CHUNKS:
        gate = _dot(h, wg_ref[:, lo:hi])
        up = _dot(h, wu_ref[:, lo:hi])
        act = (0.5 * gate * _sigmoid(gate) * up).astype(BF16)
        acc = acc + _dot(act, wd_ref[lo:hi, :])
    return acc


def _ffn_kernel(x_ref, g_ref, wg_ref, wu_ref, wd_ref, fin_ref, o_ref, *, final_norm):
    y = _swiglu_residual(x_ref[...], g_ref[...], wg_ref, wu_ref, wd_ref)
    if final_norm:
        y = _rmsnorm(y, fin_ref[...])
    o_ref[...] = y


def _ffn(x, norm_g, wg, wu, wd, fin_g, *, final_norm, tm):
    n = x.shape[0]
    row = pl.BlockSpec((tm, D_MODEL), lambda i: (i, 0))
    return pl.pallas_call(
        functools.partial(_ffn_kernel, final_norm=final_norm),
        grid=(n // tm,),
        in_specs=[row, _resident((1, D_MODEL)), _resident((D_MODEL, D_FF)), _resident((D_MODEL, D_FF)),
                  _resident((D_FF, D_MODEL)), _resident((1, D_MODEL))],
        out_specs=row,
        out_shape=jax.ShapeDtypeStruct((n, D_MODEL), F32),
        compiler_params=pltpu.CompilerParams(dimension_semantics=("arbitrary",),
                                             vmem_limit_bytes=VMEM_LIMIT_BYTES),
        name="ffn_final" if final_norm else "ffn",
    )(x, norm_g, wg, wu, wd, fin_g)


def _mixer_in_kernel(x_ref, g_ref, wm_ref, convw_ref, alog_ref, dtb_ref, cos_ref, sin_ref,
                     a0_ref, a1_ref, a2_ref, qb_ref, kb_ref, vb_ref, bd_ref, conv_ref, stage_ref, *, tm):
    @pl.when(pl.program_id(1) == 0)
    def _():
        conv_ref[0:SUBLANES, :] = jnp.zeros((SUBLANES, 3 * GDN_WIDTH), F32)

    h = _rmsnorm(x_ref[0], g_ref[...]).astype(BF16)
    for part in range(3):
        lo = W_GDN + part * GDN_WIDTH
        conv_ref[SUBLANES:SUBLANES + tm, part * GDN_WIDTH:(part + 1) * GDN_WIDTH] = _dot(h, wm_ref[:, lo:lo + GDN_WIDTH])
    ya = _dot(h, wm_ref[:, W_ATT:W_ATT + 3 * ATT_QKV_WIDTH])
    raw = _dot(h, wm_ref[:, W_BD:W_BD + LANES])

    out_refs = (qb_ref, kb_ref, vb_ref)
    for part in range(3):
        for hh in range(GDN_HEADS):
            col = part * GDN_WIDTH + hh * GDN_HEAD_DIM
            acc = None
            for t in range(GDN_CONV):
                start = SUBLANES - (GDN_CONV - 1) + t
                term = conv_ref[start:start + tm, col:col + GDN_HEAD_DIM] * convw_ref[t:t + 1, col:col + GDN_HEAD_DIM]
                acc = term if acc is None else acc + term
            act = acc * _sigmoid(acc)
            if part < 2:
                act = act * lax.rsqrt(jnp.sum(act * act, axis=-1, keepdims=True) + EPS)
            if part == 0:
                act = act * (GDN_HEAD_DIM ** -0.5)
            out_refs[part][0, :, hh * GDN_HEAD_DIM:(hh + 1) * GDN_HEAD_DIM] = act
    conv_ref[0:SUBLANES, :] = conv_ref[tm:tm + SUBLANES, :]

    cos = cos_ref[...]
    sin = sin_ref[...]
    lane = lax.broadcasted_iota(jnp.int32, (1, LANES), 1)
    first_half = (lane % ATT_HEAD_DIM) < (ATT_HEAD_DIM // 2)
    att_refs = (a0_ref, a1_ref, a2_ref)
    blocks_per_group = ATT_GROUP_WIDTH // LANES
    slot = 0
    for part in range(3):
        for gi, (_, dil) in enumerate(ATT_GROUPS):
            for j in range(blocks_per_group):
                col = part * ATT_QKV_WIDTH + gi * ATT_GROUP_WIDTH + j * LANES
                blk = ya[:, col:col + LANES]
                if part < 2:
                    swapped = jnp.where(first_half, pltpu.roll(blk, LANES - ATT_HEAD_DIM // 2, 1),
                                        pltpu.roll(blk, ATT_HEAD_DIM // 2, 1))
                    blk = blk * cos + swapped * sin
                if part == 0:
                    blk = blk * (ATT_HEAD_DIM ** -0.5)
                dst = part * ATT_GROUP_WIDTH + j * LANES
                if dil == 1:
                    att_refs[gi][0, :, dst:dst + LANES] = blk.astype(BF16)
                else:
                    stage_ref[slot] = blk
                    for r in range(dil):
                        rows = stage_ref[slot, pl.ds(r, tm // dil, stride=dil), :]
                        lo = r * 3 * ATT_GROUP_WIDTH + dst
                        att_refs[gi][0, :, lo:lo + LANES] = rows.astype(BF16)
                    slot += 1

    z = raw + dtb_ref[...]
    softplus = jnp.maximum(z, 0.0) + jnp.log1p(jnp.exp(-jnp.abs(z)))
    g = -jnp.exp(alog_ref[...]) * softplus
    bd_ref[0] = jnp.where(lane < GDN_HEADS, _sigmoid(raw), jnp.where(lane < 2 * GDN_HEADS, g, 0.0))


def _mixer_in(x1, norm_g, w_all, conv_w, a_log, dt_bias, cos_t, sin_t, *, tm):
    b, s, _ = x1.shape
    tile = lambda w: pl.BlockSpec((1, tm, w), lambda bi, i: (bi, i, 0))
    table = pl.BlockSpec((tm, LANES), lambda bi, i: (i, 0))
    wq = 3 * ATT_GROUP_WIDTH
    att_specs = [pl.BlockSpec((1, tm // dil, dil * wq), lambda bi, i: (bi, i, 0)) for _, dil in ATT_GROUPS]
    att_shapes = [jax.ShapeDtypeStruct((b, s // dil, dil * wq), BF16) for _, dil in ATT_GROUPS]
    gdn = jax.ShapeDtypeStruct((b, s, GDN_WIDTH), F32)
    n_staged = 3 * (len(ATT_GROUPS) - 1) * (ATT_GROUP_WIDTH // LANES)
    return pl.pallas_call(
        functools.partial(_mixer_in_kernel, tm=tm),
        grid=(b, s // tm),
        in_specs=[tile(D_MODEL), _resident((1, D_MODEL)),
                  pl.BlockSpec((D_MODEL, W_GATES), lambda bi, i: (0, 0), pipeline_mode=pl.Buffered(1)),
                  _resident((GDN_CONV, 3 * GDN_WIDTH)), _resident((1, LANES)), _resident((1, LANES)), table, table],
        out_specs=att_specs + [tile(GDN_WIDTH)] * 3 + [tile(LANES)],
        out_shape=att_shapes + [gdn] * 3 + [jax.ShapeDtypeStruct((b, s, LANES), F32)],
        scratch_shapes=[pltpu.VMEM((tm + SUBLANES, 3 * GDN_WIDTH), F32), pltpu.VMEM((n_staged, tm, LANES), F32)],
        compiler_params=pltpu.CompilerParams(dimension_semantics=("arbitrary", "arbitrary"),
                                             vmem_limit_bytes=VMEM_LIMIT_BYTES),
        name="mixer_in",
    )(x1, norm_g, w_all, conv_w, a_log, dt_bias, cos_t, sin_t)


def _attention_block(q, k, v, valid):
    lane = lax.broadcasted_iota(jnp.int32, (1, ATT_GROUP_WIDTH), 1)
    o = jnp.zeros((ATT_BLOCK, ATT_GROUP_WIDTH), F32)
    inv_l = jnp.zeros((ATT_BLOCK, ATT_GROUP_WIDTH), F32)
    lse = jnp.zeros((ATT_BLOCK, ATT_GROUP_WIDTH), F32)
    for hh in range(ATT_HEADS_PER_GROUP):
        in_head = (lane // ATT_HEAD_DIM) == hh
        keep = jnp.where(in_head, 1.0, 0.0).astype(BF16)
        s = _dot_nt(q * keep, k)
        s = jnp.where(valid, s, NEG_BIG)
        m = jnp.max(s, axis=-1, keepdims=True)
        p = jnp.exp(s - m)
        l = jnp.sum(p, axis=-1, keepdims=True)
        o = o + _dot(p.astype(BF16), v * keep)
        inv_l = jnp.where(in_head, 1.0 / l, inv_l)
        lse = jnp.where(in_head, m + jnp.log(l), lse)
    return o * inv_l, lse


def _attention_kernel(a0_ref, a1_ref, a2_ref, ya_ref, o0, l0, o1, l1, o2, l2, *, seq):
    in_refs = (a0_ref, a1_ref, a2_ref)
    o_refs = (o0, o1, o2)
    l_refs = (l0, l1, l2)
    qi = lax.broadcasted_iota(jnp.int32, (ATT_BLOCK, ATT_BLOCK), 0)
    kj = lax.broadcasted_iota(jnp.int32, (ATT_BLOCK, ATT_BLOCK), 1)
    causal = kj <= qi
    qi2 = lax.broadcasted_iota(jnp.int32, (ATT_BLOCK, 2 * ATT_BLOCK), 0)
    kj2 = lax.broadcasted_iota(jnp.int32, (ATT_BLOCK, 2 * ATT_BLOCK), 1)
    band = (kj2 >= qi2) & (kj2 - ATT_BLOCK <= qi2)

    for gi, (window, dil) in enumerate(ATT_GROUPS):
        assert window // dil == ATT_BLOCK
        length = seq // dil
        nblk = length // ATT_BLOCK
        src, o_ref, l_ref = in_refs[gi], o_refs[gi], l_refs[gi]
        wq = 3 * ATT_GROUP_WIDTH

        def store(o_ref, l_ref, dil, r, n, o, lse):
            if dil == 1:
                rows = pl.ds(pl.multiple_of(n * ATT_BLOCK, ATT_BLOCK), ATT_BLOCK)
            else:
                rows = pl.ds(n * ATT_BLOCK * dil + r, ATT_BLOCK, stride=dil)
            for half in range(ATT_GROUP_WIDTH // LANES):
                o_ref[half, rows, :] = o[:, half * LANES:(half + 1) * LANES]
                l_ref[half, rows, :] = lse[:, half * LANES:(half + 1) * LANES]

        for r in range(dil):
            base = r * wq
            q = src[0, 0:ATT_BLOCK, base:base + ATT_GROUP_WIDTH]
            k = src[0, 0:ATT_BLOCK, base + ATT_GROUP_WIDTH:base + 2 * ATT_GROUP_WIDTH]
            v = src[0, 0:ATT_BLOCK, base + 2 * ATT_GROUP_WIDTH:base + 3 * ATT_GROUP_WIDTH]
            o, lse = _attention_block(q, k, v, causal)
            store(o_ref, l_ref, dil, r, 0, o, lse)

            if nblk > 1:
                def body(n, carry, src=src, base=base, o_ref=o_ref, l_ref=l_ref, dil=dil, r=r):
                    qrows = pl.ds(pl.multiple_of(n * ATT_BLOCK, ATT_BLOCK), ATT_BLOCK)
                    krows = pl.ds(pl.multiple_of((n - 1) * ATT_BLOCK, ATT_BLOCK), 2 * ATT_BLOCK)
                    q = src[0, qrows, base:base + ATT_GROUP_WIDTH]
                    k = src[0, krows, base + ATT_GROUP_WIDTH:base + 2 * ATT_GROUP_WIDTH]
                    v = src[0, krows, base + 2 * ATT_GROUP_WIDTH:base + 3 * ATT_GROUP_WIDTH]
                    o, lse = _attention_block(q, k, v, band)
                    store(o_ref, l_ref, dil, r, n, o, lse)
                    return carry
                lax.fori_loop(1, nblk, body, 0)

    rows_per_step = 256

    def merge(i, carry):
        rows = pl.ds(pl.multiple_of(i * rows_per_step, rows_per_step), rows_per_step)
        for half in range(ATT_GROUP_WIDTH // LANES):
            la, lb, lc = l0[half, rows, :], l1[half, rows, :], l2[half, rows, :]
            m = jnp.maximum(jnp.maximum(la, lb), lc)
            ea, eb, ec = jnp.exp(la - m), jnp.exp(lb - m), jnp.exp(lc - m)
            num = ea * o0[half, rows, :] + eb * o1[half, rows, :] + ec * o2[half, rows, :]
            ya_ref[0, rows, half * LANES:(half + 1) * LANES] = num / (ea + eb + ec)
        return carry
    lax.fori_loop(0, seq // rows_per_step, merge, 0)


def _attention(a0, a1, a2):
    b, s, _ = a0.shape
    views = (a0, a1, a2)
    specs = [pl.BlockSpec((1,) + arr.shape[1:], lambda bi: (bi, 0, 0)) for arr in views]
    scratch = [pltpu.VMEM((ATT_GROUP_WIDTH // LANES, s, LANES), F32) for _ in range(6)]
    return pl.pallas_call(
        functools.partial(_attention_kernel, seq=s),
        grid=(b,),
        in_specs=specs,
        out_specs=pl.BlockSpec((1, s, ATT_GROUP_WIDTH), lambda bi: (bi, 0, 0)),
        out_shape=jax.ShapeDtypeStruct((b, s, ATT_GROUP_WIDTH), F32),
        scratch_shapes=scratch,
        compiler_params=pltpu.CompilerParams(dimension_semantics=("arbitrary",),
                                             vmem_limit_bytes=VMEM_LIMIT_BYTES),
        name="dilated_attention",
    )(*views)


def _deltanet_stages(q_ref, k_ref, v_ref, bd_ref, gnorm, state_ref, ob_ref, slot, *, tile):
    c = GDN_CHUNK
    d = GDN_HEAD_DIM
    heads = range(GDN_HEADS)
    pairs = [(2 * pp, 2 * pp + 1) for pp in range(GDN_HEADS // 2)]
    cols = [slice(hh * d, (hh + 1) * d) for hh in heads]
    glane = [GDN_HEADS + hh for hh in heads]
    ii = lax.broadcasted_iota(jnp.int32, (c, 2 * c), 0)
    ll = lax.broadcasted_iota(jnp.int32, (c, 2 * c), 1)
    jj = ll % c
    lower = ii >= jj
    strict = ii > jj
    left = ll < c
    left_row = lax.broadcasted_iota(jnp.int32, (1, 2 * c), 1) < c
    keep_left = jnp.where(left, 1.0, 0.0).astype(BF16)
    keep_right = jnp.where(left, 0.0, 1.0).astype(BF16)
    ti = lax.broadcasted_iota(jnp.int32, (c, c), 0)
    tj = lax.broadcasted_iota(jnp.int32, (c, c), 1)
    tri_ones = jnp.where(ti >= tj, 1.0, 0.0).astype(BF16)

    def blockdiag(x):
        return jnp.concatenate([x * keep_left, x * keep_right], axis=0)

    def stack_diag(xa, xb):
        zero = jnp.zeros_like(xa)
        return jnp.concatenate([jnp.concatenate([xa, zero], axis=1), jnp.concatenate([zero, xb], axis=1)], axis=0)

    def prepare(ci, out):
        rows = slice(ci * c, (ci + 1) * c)
        bd = bd_ref[rows, :]
        bd_hi = bd.astype(BF16)
        bd_rest = bd - bd_hi.astype(F32)
        bd_mid = bd_rest.astype(BF16)
        bd_lo = (bd_rest - bd_mid.astype(F32)).astype(BF16)
        gcum = _dot(tri_ones, bd_hi) + _dot(tri_ones, bd_mid) + _dot(tri_ones, bd_lo)
        yield
        gtot = jnp.broadcast_to(gcum[c - 1:c, :], (c, LANES))
        gcum_t = jnp.concatenate([gcum, gcum], axis=0).T
        e_cum_all = jnp.exp(gcum)
        e_rest_all = jnp.exp(gtot - gcum)
        e_tot_all = jnp.exp(gtot)
        q = [q_ref[rows, cols[hh]] for hh in heads]
        k = [k_ref[rows, cols[hh]] for hh in heads]
        v = [v_ref[rows, cols[hh]] for hh in heads]
        beta = [bd[:, hh:hh + 1] for hh in heads]
        e_cum = [e_cum_all[:, gl:gl + 1] for gl in glane]
        kbeta = [k[hh] * beta[hh] for hh in heads]
        kq = [_dot_nt(jnp.concatenate([jnp.concatenate([kbeta[a], kbeta[b]], axis=1),
                                       jnp.concatenate([q[a], q[b]], axis=1)], axis=0).astype(BF16),
                      stack_diag(k[a].astype(BF16), k[b].astype(BF16)))
              for a, b in pairs]
        yield
        decay = [jnp.exp(jnp.where(lower,
                                   jnp.where(left, gcum[:, glane[a]:glane[a] + 1], gcum[:, glane[b]:glane[b] + 1])
                                   - jnp.where(left_row, gcum_t[glane[a]:glane[a] + 1, :], gcum_t[glane[b]:glane[b] + 1, :]),
                                   NEG_BIG)) for a, b in pairs]
        m = [jnp.where(strict, kq[pp][0:c] * decay[pp], 0.0) for pp in range(len(pairs))]
        n = [-mm for mm in m]
        pb = [mm.astype(BF16) for mm in m]
        p = [_dot(x, blockdiag(x)) for x in pb]
        yield
        rounds = 5
        for r in range(rounds):
            pb = [x.astype(BF16) for x in p]
            upd = [_dot(x, blockdiag(y.astype(BF16))) for x, y in zip(pb, n)]
            p_next = [_dot(x, blockdiag(x)) for x in pb] if r + 1 < rounds else None
            yield
            n = [y + x + u for y, x, u in zip(n, p, upd)]
            p = p_next
        rhs = [jnp.concatenate([v[hh] * beta[hh], kbeta[hh] * e_cum[hh]], axis=1) for hh in heads]
        nr = [_dot(n[pp].astype(BF16), stack_diag(rhs[a].astype(BF16), rhs[b].astype(BF16)))
              for pp, (a, b) in enumerate(pairs)]
        yield
        sol = [rhs[hh] + nr[hh // 2][:, (hh % 2) * 2 * d:(hh % 2 + 1) * 2 * d] for hh in heads]
        out.update(
            rows=rows,
            u=[sol[hh][:, 0:d] for hh in heads],
            wq=[jnp.concatenate([sol[hh][:, d:2 * d], q[hh] * e_cum[hh]], axis=0).astype(BF16) for hh in heads],
            a_qk=[(kq[pp][c:2 * c] * decay[pp]).astype(BF16) for pp in range(len(pairs))],
            k_dec=[(k[hh] * e_rest_all[:, gl:gl + 1]).astype(BF16) for hh, gl in zip(heads, glane)],
            e_tot=[e_tot_all[0:1, gl:gl + 1] for gl in glane])

    for first in range(0, tile // c, GDN_GROUP):
        group = [dict() for _ in range(GDN_GROUP)]
        gens = [prepare(first + gi, group[gi]) for gi in range(GDN_GROUP)]
        for _ in range(GDN_PREP_LAYERS):
            for gen in gens:
                next(gen)
            yield
        for gen in gens:
            for _ in gen:
                pass
        for pre in group:
            state = [state_ref[hh] for hh in heads]
            ws = [_dot(pre["wq"][hh], state[hh].astype(BF16)) for hh in heads]
            yield
            v_new = [(pre["u"][hh] - ws[hh][0:c]).astype(BF16) for hh in heads]
            kv = [_dot_tn(pre["k_dec"][hh], v_new[hh]) for hh in heads]
            av = [_dot(pre["a_qk"][pp], stack_diag(v_new[a], v_new[b])) for pp, (a, b) in enumerate(pairs)]
            yield
            for hh in heads:
                state_ref[hh] = state[hh] * pre["e_tot"][hh] + kv[hh]
            for hh in heads:
                o = ws[hh][c:2 * c] + av[hh // 2][:, (hh % 2) * d:(hh % 2 + 1) * d]
                ob_ref[slot, pre["rows"], cols[hh]] = _rmsnorm(o, gnorm)


GDN_GROUP = 4
GDN_PREP_LAYERS = 9
GDN_LAYERS_PER_GROUP = GDN_PREP_LAYERS + 2 * GDN_GROUP


MIX_BLOCK = 256


def _mixer_out_stages(x_ref, ya_ref, ob_ref, slot, g_ref, wgt_ref, wa_ref, wb_ref, wo_ref, o_ref):
    x = x_ref[...]
    h = _rmsnorm(x, g_ref[...]).astype(BF16)
    ya = ya_ref[...].astype(BF16)
    blocks = [slice(j * MIX_BLOCK, (j + 1) * MIX_BLOCK) for j in range(D_MODEL // MIX_BLOCK)]
    gate_cols = lambda which, blk: slice(which * D_MODEL + blk.start, which * D_MODEL + blk.stop)
    yb = []
    for blk in blocks:
        gdn_gate = _dot(h, wgt_ref[:, gate_cols(0, blk)])
        yield
        yb.append((ob_ref[slot, :, blk] * (gdn_gate * _sigmoid(gdn_gate))).astype(BF16))
    yb = jnp.concatenate(yb, axis=1)
    merged = []
    for blk in blocks:
        gate_a = _dot(h, wgt_ref[:, gate_cols(1, blk)])
        branch_a = _dot(ya, wa_ref[:, blk])
        yield
        gate_b = _dot(h, wgt_ref[:, gate_cols(2, blk)])
        yield
        branch_b = _dot(yb, wb_ref[:, blk])
        yield
        merged.append((_sigmoid(gate_a) * branch_a + _sigmoid(gate_b) * branch_b).astype(BF16))
    merged = jnp.concatenate(merged, axis=1)
    for blk in blocks:
        o_ref[:, blk] = x[:, blk] + _dot(merged, wo_ref[:, blk])
        yield


MIX_GRANULES = 5 * (D_MODEL // MIX_BLOCK)


def _interleave(primary, n_primary, secondary, n_secondary):
    done = 0
    for i in range(n_primary):
        next(primary)
        while done * n_primary < (i + 1) * n_secondary:
            next(secondary)
            done += 1
    for gen in (primary, secondary):
        for _ in gen:
            pass


def _mixer_tail_kernel(q_ref, k_ref, v_ref, bd_ref, gn_ref, x_ref, ya_ref, g_ref, wgt_ref, wa_ref, wb_ref, wo_ref,
                       o_ref, state_ref, ob_ref, *, tile, tiles_per_seq, n_tiles):
    step = pl.program_id(0)

    @pl.when(step == 0)
    def _():
        ob_ref[...] = jnp.zeros(ob_ref.shape, F32)

    @pl.when(jnp.minimum(step, n_tiles - 1) % tiles_per_seq == 0)
    def _():
        state_ref[...] = jnp.zeros(state_ref.shape, F32)

    slot = step % 2
    gdn = _deltanet_stages(q_ref, k_ref, v_ref, bd_ref, gn_ref[...], state_ref, ob_ref, slot, tile=tile)
    mix = _mixer_out_stages(x_ref, ya_ref, ob_ref, 1 - slot, g_ref, wgt_ref, wa_ref, wb_ref, wo_ref, o_ref)
    _interleave(gdn, GDN_LAYERS_PER_GROUP * (tile // (GDN_CHUNK * GDN_GROUP)), mix, MIX_GRANULES)


def _mixer_tail(qb, kb, vb, bd, out_norm, x1, ya, norm_g, w_all, w_a, w_b, w_o, *, tile, seq):
    n = x1.shape[0]
    n_tiles = n // tile
    cur = lambda w: pl.BlockSpec((tile, w), lambda s: (jnp.minimum(s, n_tiles - 1), 0))
    prev = lambda w: pl.BlockSpec((tile, w), lambda s: (jnp.maximum(s - 1, 0), 0))
    return pl.pallas_call(
        functools.partial(_mixer_tail_kernel, tile=tile, tiles_per_seq=seq // tile, n_tiles=n_tiles),
        grid=(n_tiles + 1,),
        in_specs=[cur(GDN_WIDTH), cur(GDN_WIDTH), cur(GDN_WIDTH), cur(LANES), _resident((1, GDN_HEAD_DIM)),
                  prev(D_MODEL), prev(ATT_GROUP_WIDTH), _resident((1, D_MODEL)),
                  pl.BlockSpec((D_MODEL, W_GDN), lambda s: (0, W_GATES // W_GDN), pipeline_mode=pl.Buffered(1)),
                  _resident(w_a.shape), _resident(w_b.shape), _resident(w_o.shape)],
        out_specs=prev(D_MODEL),
        out_shape=jax.ShapeDtypeStruct((n, D_MODEL), F32),
        scratch_shapes=[pltpu.VMEM((GDN_HEADS, GDN_HEAD_DIM, GDN_HEAD_DIM), F32),
                        pltpu.VMEM((2, tile, GDN_WIDTH), F32)],
        compiler_params=pltpu.CompilerParams(dimension_semantics=("arbitrary",),
                                             vmem_limit_bytes=VMEM_LIMIT_BYTES),
        name="deltanet_mixer_out",
    )(qb, kb, vb, bd, out_norm, x1, ya, norm_g, w_all, w_a, w_b, w_o)


def _rope_tables(seq):
    half = ATT_HEAD_DIM // 2
    inv_freq = ROPE_THETA ** (-jnp.arange(half, dtype=F32) / half)
    ang = jnp.arange(seq, dtype=F32)[:, None] * inv_freq[None, :]
    cos, sin = jnp.cos(ang), jnp.sin(ang)
    reps = LANES // ATT_HEAD_DIM
    return jnp.tile(jnp.concatenate([cos, cos], axis=-1), (1, reps)), jnp.tile(jnp.concatenate([-sin, sin], axis=-1), (1, reps))


def _pad_lanes(row, offset):
    return jnp.zeros((1, LANES), F32).at[0, offset:offset + row.shape[0]].set(row.astype(F32))


def _layer(x, ffn1_norm, ffn1_w_gate, ffn1_w_up, ffn1_w_down, mix_norm, w_in, gdn_conv_w, gdn_a_log, gdn_dt_bias,
           gdn_out_norm, w_branch_a, w_branch_b, w_out, ffn2_norm, ffn2_w_gate, ffn2_w_up, ffn2_w_down, fin_g,
           *, final_norm, tm_ffn, tm_mix, gdn_tile):
    b, s, _ = x.shape
    n = b * s
    row = lambda v: v.reshape(1, -1).astype(F32)
    x1 = _ffn(x.reshape(n, D_MODEL), row(ffn1_norm), ffn1_w_gate.astype(BF16), ffn1_w_up.astype(BF16),
              ffn1_w_down.astype(BF16), fin_g, final_norm=False, tm=tm_ffn)

    w_all = jnp.concatenate(
        [w_in[:, :W_IN_GDN], w_in[:, W_IN_BD:W_IN_GATES], jnp.zeros((D_MODEL, W_GDN - W_BD - 2 * GDN_HEADS), w_in.dtype),
         w_in[:, W_IN_GDN:W_IN_BD], w_in[:, W_IN_GATES:]], axis=1).astype(BF16)
    cos_t, sin_t = _rope_tables(s)
    a0, a1, a2, qb, kb, vb, bd = _mixer_in(
        x1.reshape(b, s, D_MODEL), row(mix_norm), w_all, gdn_conv_w.astype(F32),
        _pad_lanes(gdn_a_log, GDN_HEADS), _pad_lanes(gdn_dt_bias, GDN_HEADS), cos_t, sin_t, tm=tm_mix)

    ya = _attention(a0, a1, a2)
    flat = lambda a: a.reshape(n, a.shape[-1])
    x2 = _mixer_tail(flat(qb), flat(kb), flat(vb), flat(bd), row(gdn_out_norm), x1, flat(ya), row(mix_norm), w_all,
                     w_branch_a.astype(BF16), w_branch_b.astype(BF16), w_out.astype(BF16), tile=gdn_tile, seq=s)
    x3 = _ffn(x2, row(ffn2_norm), ffn2_w_gate.astype(BF16), ffn2_w_up.astype(BF16), ffn2_w_down.astype(BF16),
              fin_g, final_norm=final_norm, tm=tm_ffn)
    return x3.reshape(b, s, D_MODEL)


def kernel(x, ffn1_norm, ffn1_w_gate, ffn1_w_up, ffn1_w_down, mix_norm, w_in, gdn_conv_w, gdn_a_log, gdn_dt_bias,
           gdn_out_norm, w_branch_a, w_branch_b, w_out, ffn2_norm, ffn2_w_gate, ffn2_w_up, ffn2_w_down, final_norm):
    depth = ffn1_norm.shape[0]
    fin_g = final_norm.reshape(1, -1).astype(F32)
    for layer in range(depth):
        x = _layer(x, ffn1_norm[layer], ffn1_w_gate[layer], ffn1_w_up[layer], ffn1_w_down[layer], mix_norm[layer],
                   w_in[layer], gdn_conv_w[layer], gdn_a_log[layer], gdn_dt_bias[layer], gdn_out_norm[layer],
                   w_branch_a[layer], w_branch_b[layer], w_out[layer], ffn2_norm[layer], ffn2_w_gate[layer],
                   ffn2_w_up[layer], ffn2_w_down[layer], fin_g, final_norm=(layer == depth - 1),
                   tm_ffn=512, tm_mix=512, gdn_tile=512)
    return x
```

```python
import functools

import jax
import jax.numpy as jnp
from jax import lax
from jax.experimental import pallas as pl
from jax.experimental.pallas import tpu as pltpu

F32 = jnp.float32
BF16 = jnp.bfloat16

D_MODEL = 1024
D_FF = 2816
EPS = 1e-6

ATT_GROUPS = ((128, 1), (512, 4), (2048, 16))
ATT_HEADS_PER_GROUP = 4
ATT_HEAD_DIM = 64
ATT_BLOCK = 128
ATT_GROUP_WIDTH = ATT_HEADS_PER_GROUP * ATT_HEAD_DIM
ATT_QKV_WIDTH = len(ATT_GROUPS) * ATT_GROUP_WIDTH
ROPE_THETA = 10000.0

GDN_HEADS = 8
GDN_HEAD_DIM = 128
GDN_WIDTH = GDN_HEADS * GDN_HEAD_DIM
GDN_CONV = 4
GDN_CHUNK = 64

LANES = 128
SUBLANES = 8
VMEM_LIMIT_BYTES = 56 * 1024 * 1024

W_IN_GDN = 3 * ATT_QKV_WIDTH
W_IN_BD = W_IN_GDN + 3 * GDN_WIDTH
W_IN_GATES = W_IN_BD + 2 * GDN_HEADS
W_ATT = 0
W_BD = 3 * ATT_QKV_WIDTH
W_GDN = 3 * GDN_WIDTH
W_GATES = 2 * W_GDN
FFN_CHUNKS = ((0, 768), (768, 1536), (1536, 2304), (2304, 2816))
NEG_BIG = -1e30


def _resident(shape):
    nd = len(shape)
    return pl.BlockSpec(shape, lambda *_: (0,) * nd, pipeline_mode=pl.Buffered(1))


def _rmsnorm(x, g):
    return x * lax.rsqrt(jnp.mean(x * x, axis=-1, keepdims=True) + EPS) * g


def _sigmoid(x):
    return 1.0 / (1.0 + jnp.exp(-x))


def _dot(a, b):
    return jnp.dot(a, b, preferred_element_type=F32)


def _dot_nt(a, b):
    return lax.dot_general(a, b, (((1,), (1,)), ((), ())), preferred_element_type=F32)


def _dot_tn(a, b):
    return lax.dot_general(a, b, (((0,), (0,)), ((), ())), preferred_element_type=F32)


def _swiglu_residual(x, g, wg_ref, wu_ref, wd_ref):
    h = _rmsnorm(x, g).astype(BF16)
    acc = x
    for lo, hi in FFN_CHUNKS:
        gate = _dot(h, wg_ref[:, lo:hi])
        up = _dot(h, wu_ref[:, lo:hi])
        act = (0.5 * gate * _sigmoid(gate) * up).astype(BF16)
        acc = acc + _dot(act, wd_ref[lo:hi, :])
    return acc


def _ffn_kernel(x_ref, g_ref, wg_ref, wu_ref, wd_ref, fin_ref, o_ref, *, final_norm):
    y = _swiglu_residual(x_ref[...], g_ref[...], wg_ref, wu_ref, wd_ref)
    if final_norm:
        y = _rmsnorm(y, fin_ref[...])
    o_ref[...] = y


def _ffn(x, norm_g, wg, wu, wd, fin_g, *, final_norm, tm):
    n = x.shape[0]
    row = pl.BlockSpec((tm, D_MODEL), lambda i: (i, 0))
    return pl.pallas_call(
        functools.partial(_ffn_kernel, final_norm=final_norm),
        grid=(n // tm,),
        in_specs=[row, _resident((1, D_MODEL)), _resident((D_MODEL, D_FF)), _resident((D_MODEL, D_FF)),
                  _resident((D_FF, D_MODEL)), _resident((1, D_MODEL))],
        out_specs=row,
        out_shape=jax.ShapeDtypeStruct((n, D_MODEL), F32),
        compiler_params=pltpu.CompilerParams(dimension_semantics=("arbitrary",),
                                             vmem_limit_bytes=VMEM_LIMIT_BYTES),
        name="ffn_final" if final_norm else "ffn",
    )(x, norm_g, wg, wu, wd, fin_g)


MIX_BLOCK = 256
HEAD_BLOCK = 256


def _interleave(primary, n_primary, secondary, n_secondary):
    done = 0
    for i in range(n_primary):
        next(primary)
        while done * n_primary < (i + 1) * n_secondary:
            next(secondary)
            done += 1
    for gen in (primary, secondary):
        for _ in gen:
            pass


def _ffn_stages(x, g, wg_ref, wu_ref, wd_ref, finish):
    h = _rmsnorm(x, g).astype(BF16)
    acc = x
    pending = None
    for lo in range(0, D_FF, HEAD_BLOCK):
        hi = min(lo + HEAD_BLOCK, D_FF)
        gate = _dot(h, wg_ref[:, lo:hi])
        yield
        up = _dot(h, wu_ref[:, lo:hi])
        yield
        if pending is not None:
            acc = acc + _dot(pending[0], wd_ref[pending[1]:pending[2], :])
            yield
        pending = ((0.5 * gate * _sigmoid(gate) * up).astype(BF16), lo, hi)
    acc = acc + _dot(pending[0], wd_ref[pending[1]:pending[2], :])
    yield
    finish(acc)


FFN_GRANULES = 3 * -(-D_FF // HEAD_BLOCK)


def _mixer_in_stages(x_ref, g_ref, wm_ref, convw_ref, alog_ref, dtb_ref, cos_ref, sin_ref,
                     att_refs, gdn_refs, bd_ref, conv_ref, stage_ref, *, tm):
    h = _rmsnorm(x_ref[...], g_ref[...]).astype(BF16)

    heads_per_block = HEAD_BLOCK // GDN_HEAD_DIM
    for part in range(3):
        for blk in range(GDN_WIDTH // HEAD_BLOCK):
            base = part * GDN_WIDTH + blk * HEAD_BLOCK
            conv_ref[SUBLANES:SUBLANES + tm, base:base + HEAD_BLOCK] = _dot(h, wm_ref[:, W_GDN + base:W_GDN + base + HEAD_BLOCK])
            yield
            for hh in range(blk * heads_per_block, (blk + 1) * heads_per_block):
                col = part * GDN_WIDTH + hh * GDN_HEAD_DIM
                acc = None
                for t in range(GDN_CONV):
                    start = SUBLANES - (GDN_CONV - 1) + t
                    term = conv_ref[start:start + tm, col:col + GDN_HEAD_DIM] * convw_ref[t:t + 1, col:col + GDN_HEAD_DIM]
                    acc = term if acc is None else acc + term
                act = acc * _sigmoid(acc)
                if part < 2:
                    act = act * lax.rsqrt(jnp.sum(act * act, axis=-1, keepdims=True) + EPS)
                if part == 0:
                    act = act * (GDN_HEAD_DIM ** -0.5)
                gdn_refs[part][:, hh * GDN_HEAD_DIM:(hh + 1) * GDN_HEAD_DIM] = act
                yield
    conv_ref[0:SUBLANES, :] = conv_ref[tm:tm + SUBLANES, :]

    cos = cos_ref[...]
    sin = sin_ref[...]
    lane = lax.broadcasted_iota(jnp.int32, (1, LANES), 1)
    first_half = (lane % ATT_HEAD_DIM) < (ATT_HEAD_DIM // 2)
    slot = 0
    for part in range(3):
        for gi, (_, dil) in enumerate(ATT_GROUPS):
            col = W_ATT + part * ATT_QKV_WIDTH + gi * ATT_GROUP_WIDTH
            y = _dot(h, wm_ref[:, col:col + ATT_GROUP_WIDTH])
            yield
            for j in range(ATT_GROUP_WIDTH // LANES):
                blk = y[:, j * LANES:(j + 1) * LANES]
                if part < 2:
                    swapped = jnp.where(first_half, pltpu.roll(blk, LANES - ATT_HEAD_DIM // 2, 1),
                                        pltpu.roll(blk, ATT_HEAD_DIM // 2, 1))
                    blk = blk * cos + swapped * sin
                if part == 0:
                    blk = blk * (ATT_HEAD_DIM ** -0.5)
                dst = part * ATT_GROUP_WIDTH + j * LANES
                if dil == 1:
                    att_refs[gi][:, dst:dst + LANES] = blk.astype(BF16)
                else:
                    stage_ref[slot] = blk
                    for r in range(dil):
                        rows = stage_ref[slot, pl.ds(r, tm // dil, stride=dil), :]
                        lo = r * 3 * ATT_GROUP_WIDTH + dst
                        att_refs[gi][:, lo:lo + LANES] = rows.astype(BF16)
                    slot += 1
                yield

    raw = _dot(h, wm_ref[:, W_BD:W_BD + LANES])
    yield
    z = raw + dtb_ref[...]
    softplus = jnp.maximum(z, 0.0) + jnp.log1p(jnp.exp(-jnp.abs(z)))
    g = -jnp.exp(alog_ref[...]) * softplus
    bd_ref[...] = jnp.where(lane < GDN_HEADS, _sigmoid(raw), jnp.where(lane < 2 * GDN_HEADS, g, 0.0))


MIXER_IN_GRANULES = (3 * (GDN_WIDTH // HEAD_BLOCK) + 3 * GDN_HEADS
                     + 3 * len(ATT_GROUPS) * (1 + ATT_GROUP_WIDTH // LANES) + 1)
MIXER_IN_STAGED = 3 * (len(ATT_GROUPS) - 1) * (ATT_GROUP_WIDTH // LANES)


def _head_kernel(x_ref, fg_ref, wg_ref, wu_ref, wd_ref, mg_ref, wm_ref, convw_ref, alog_ref, dtb_ref, cos_ref, sin_ref,
                 x1_ref, a0_ref, a1_ref, a2_ref, qb_ref, kb_ref, vb_ref, bd_ref, x1_scr, conv_ref, stage_ref,
                 *, tm, tiles_per_seq):
    step = pl.program_id(0)

    @pl.when(step == 0)
    def _():
        x1_scr[...] = jnp.zeros(x1_scr.shape, F32)

    @pl.when(jnp.maximum(step - 1, 0) % tiles_per_seq == 0)
    def _():
        conv_ref[0:SUBLANES, :] = jnp.zeros((SUBLANES, 3 * GDN_WIDTH), F32)

    def finish(x1):
        x1_ref[...] = x1
        x1_scr[...] = x1

    ffn = _ffn_stages(x_ref[...], fg_ref[...], wg_ref, wu_ref, wd_ref, finish)
    mix = _mixer_in_stages(x1_scr, mg_ref, wm_ref, convw_ref, alog_ref, dtb_ref, cos_ref, sin_ref,
                           (a0_ref, a1_ref, a2_ref), (qb_ref, kb_ref, vb_ref), bd_ref, conv_ref, stage_ref, tm=tm)
    _interleave(ffn, FFN_GRANULES, mix, MIXER_IN_GRANULES)


def _head(x, ffn_g, wg, wu, wd, mix_g, w_all, conv_w, a_log, dt_bias, cos_t, sin_t, *, tm, seq):
    n = x.shape[0]
    n_tiles = n // tm
    tiles_per_seq = seq // tm
    cur_idx = lambda s: jnp.minimum(s, n_tiles - 1)
    prev_idx = lambda s: jnp.maximum(s - 1, 0)
    cur = pl.BlockSpec((tm, D_MODEL), lambda s: (cur_idx(s), 0))
    prev = lambda rows, w: pl.BlockSpec((rows, w), lambda s: (prev_idx(s), 0))
    table = pl.BlockSpec((tm, LANES), lambda s: (prev_idx(s) % tiles_per_seq, 0))
    wq = 3 * ATT_GROUP_WIDTH
    att_specs = [prev(tm // dil, dil * wq) for _, dil in ATT_GROUPS]
    att_shapes = [jax.ShapeDtypeStruct((n // dil, dil * wq), BF16) for _, dil in ATT_GROUPS]
    gdn = jax.ShapeDtypeStruct((n, GDN_WIDTH), F32)
    return pl.pallas_call(
        functools.partial(_head_kernel, tm=tm, tiles_per_seq=tiles_per_seq),
        grid=(n_tiles + 1,),
        in_specs=[cur, _resident((1, D_MODEL)), _resident((D_MODEL, D_FF)), _resident((D_MODEL, D_FF)),
                  _resident((D_FF, D_MODEL)), _resident((1, D_MODEL)),
                  pl.BlockSpec((D_MODEL, W_GATES), lambda s: (0, 0), pipeline_mode=pl.Buffered(1)),
                  _resident((GDN_CONV, 3 * GDN_WIDTH)), _resident((1, LANES)), _resident((1, LANES)), table, table],
        out_specs=[cur] + att_specs + [prev(tm, GDN_WIDTH)] * 3 + [prev(tm, LANES)],
        out_shape=[jax.ShapeDtypeStruct((n, D_MODEL), F32)] + att_shapes + [gdn] * 3
                  + [jax.ShapeDtypeStruct((n, LANES), F32)],
        scratch_shapes=[pltpu.VMEM((tm, D_MODEL), F32), pltpu.VMEM((tm + SUBLANES, 3 * GDN_WIDTH), F32),
                        pltpu.VMEM((MIXER_IN_STAGED, tm, LANES), F32)],
        compiler_params=pltpu.CompilerParams(dimension_semantics=("arbitrary",),
                                             vmem_limit_bytes=VMEM_LIMIT_BYTES),
        name="ffn_mixer_in",
    )(x, ffn_g, wg, wu, wd, mix_g, w_all, conv_w, a_log, dt_bias, cos_t, sin_t)


def _attention_block(q, k, v, valid):
    lane = lax.broadcasted_iota(jnp.int32, (1, ATT_GROUP_WIDTH), 1)
    o = jnp.zeros((ATT_BLOCK, ATT_GROUP_WIDTH), F32)
    inv_l = jnp.zeros((ATT_BLOCK, ATT_GROUP_WIDTH), F32)
    lse = jnp.zeros((ATT_BLOCK, ATT_GROUP_WIDTH), F32)
    for hh in range(ATT_HEADS_PER_GROUP):
        in_head = (lane // ATT_HEAD_DIM) == hh
        keep = jnp.where(in_head, 1.0, 0.0).astype(BF16)
        s = _dot_nt(q * keep, k)
        s = jnp.where(valid, s, NEG_BIG)
        m = jnp.max(s, axis=-1, keepdims=True)
        p = jnp.exp(s - m)
        l = jnp.sum(p, axis=-1, keepdims=True)
        o = o + _dot(p.astype(BF16), v * keep)
        inv_l = jnp.where(in_head, 1.0 / l, inv_l)
        lse = jnp.where(in_head, m + jnp.log(l), lse)
    return o * inv_l, lse


def _attention_kernel(a0_ref, a1_ref, a2_ref, ya_ref, o0, l0, o1, l1, o2, l2, *, seq):
    in_refs = (a0_ref, a1_ref, a2_ref)
    o_refs = (o0, o1, o2)
    l_refs = (l0, l1, l2)
    qi = lax.broadcasted_iota(jnp.int32, (ATT_BLOCK, ATT_BLOCK), 0)
    kj = lax.broadcasted_iota(jnp.int32, (ATT_BLOCK, ATT_BLOCK), 1)
    causal = kj <= qi
    qi2 = lax.broadcasted_iota(jnp.int32, (ATT_BLOCK, 2 * ATT_BLOCK), 0)
    kj2 = lax.broadcasted_iota(jnp.int32, (ATT_BLOCK, 2 * ATT_BLOCK), 1)
    band = (kj2 >= qi2) & (kj2 - ATT_BLOCK <= qi2)

    for gi, (window, dil) in enumerate(ATT_GROUPS):
        assert window // dil == ATT_BLOCK
        length = seq // dil
        nblk = length // ATT_BLOCK
        src, o_ref, l_ref = in_refs[gi], o_refs[gi], l_refs[gi]
        wq = 3 * ATT_GROUP_WIDTH

        def store(o_ref, l_ref, dil, r, n, o, lse):
            if dil == 1:
                rows = pl.ds(pl.multiple_of(n * ATT_BLOCK, ATT_BLOCK), ATT_BLOCK)
            else:
                rows = pl.ds(n * ATT_BLOCK * dil + r, ATT_BLOCK, stride=dil)
            for half in range(ATT_GROUP_WIDTH // LANES):
                o_ref[half, rows, :] = o[:, half * LANES:(half + 1) * LANES]
                l_ref[half, rows, :] = lse[:, half * LANES:(half + 1) * LANES]

        for r in range(dil):
            base = r * wq
            q = src[0, 0:ATT_BLOCK, base:base + ATT_GROUP_WIDTH]
            k = src[0, 0:ATT_BLOCK, base + ATT_GROUP_WIDTH:base + 2 * ATT_GROUP_WIDTH]
            v = src[0, 0:ATT_BLOCK, base + 2 * ATT_GROUP_WIDTH:base + 3 * ATT_GROUP_WIDTH]
            o, lse = _attention_block(q, k, v, causal)
            store(o_ref, l_ref, dil, r, 0, o, lse)

            if nblk > 1:
                def body(n, carry, src=src, base=base, o_ref=o_ref, l_ref=l_ref, dil=dil, r=r):
                    qrows = pl.ds(pl.multiple_of(n * ATT_BLOCK, ATT_BLOCK), ATT_BLOCK)
                    krows = pl.ds(pl.multiple_of((n - 1) * ATT_BLOCK, ATT_BLOCK), 2 * ATT_BLOCK)
                    q = src[0, qrows, base:base + ATT_GROUP_WIDTH]
                    k = src[0, krows, base + ATT_GROUP_WIDTH:base + 2 * ATT_GROUP_WIDTH]
                    v = src[0, krows, base + 2 * ATT_GROUP_WIDTH:base + 3 * ATT_GROUP_WIDTH]
                    o, lse = _attention_block(q, k, v, band)
                    store(o_ref, l_ref, dil, r, n, o, lse)
                    return carry
                lax.fori_loop(1, nblk, body, 0)

    rows_per_step = 256

    def merge(i, carry):
        rows = pl.ds(pl.multiple_of(i * rows_per_step, rows_per_step), rows_per_step)
        for half in range(ATT_GROUP_WIDTH // LANES):
            la, lb, lc = l0[half, rows, :], l1[half, rows, :], l2[half, rows, :]
            m = jnp.maximum(jnp.maximum(la, lb), lc)
            ea, eb, ec = jnp.exp(la - m), jnp.exp(lb - m), jnp.exp(lc - m)
            num = ea * o0[half, rows, :] + eb * o1[half, rows, :] + ec * o2[half, rows, :]
            ya_ref[0, rows, half * LANES:(half + 1) * LANES] = num / (ea + eb + ec)
        return carry
    lax.fori_loop(0, seq // rows_per_step, merge, 0)


def _attention(a0, a1, a2, *, batch):
    views = tuple(a.reshape(batch, a.shape[0] // batch, a.shape[1]) for a in (a0, a1, a2))
    b, s, _ = views[0].shape
    specs = [pl.BlockSpec((1,) + arr.shape[1:], lambda bi: (bi, 0, 0)) for arr in views]
    scratch = [pltpu.VMEM((ATT_GROUP_WIDTH // LANES, s, LANES), F32) for _ in range(6)]
    return pl.pallas_call(
        functools.partial(_attention_kernel, seq=s),
        grid=(b,),
        in_specs=specs,
        out_specs=pl.BlockSpec((1, s, ATT_GROUP_WIDTH), lambda bi: (bi, 0, 0)),
        out_shape=jax.ShapeDtypeStruct((b, s, ATT_GROUP_WIDTH), F32),
        scratch_shapes=scratch,
        compiler_params=pltpu.CompilerParams(dimension_semantics=("arbitrary",),
                                             vmem_limit_bytes=VMEM_LIMIT_BYTES),
        name="dilated_attention",
    )(*views)


def _deltanet_stages(q_ref, k_ref, v_ref, bd_ref, gnorm, state_ref, ob_ref, slot, *, tile):
    c = GDN_CHUNK
    d = GDN_HEAD_DIM
    heads = range(GDN_HEADS)
    pairs = [(2 * pp, 2 * pp + 1) for pp in range(GDN_HEADS // 2)]
    cols = [slice(hh * d, (hh + 1) * d) for hh in heads]
    glane = [GDN_HEADS + hh for hh in heads]
    ii = lax.broadcasted_iota(jnp.int32, (c, 2 * c), 0)
    ll = lax.broadcasted_iota(jnp.int32, (c, 2 * c), 1)
    jj = ll % c
    lower = ii >= jj
    strict = ii > jj
    left = ll < c
    left_row = lax.broadcasted_iota(jnp.int32, (1, 2 * c), 1) < c
    keep_left = jnp.where(left, 1.0, 0.0).astype(BF16)
    keep_right = jnp.where(left, 0.0, 1.0).astype(BF16)
    ti = lax.broadcasted_iota(jnp.int32, (c, c), 0)
    tj = lax.broadcasted_iota(jnp.int32, (c, c), 1)
    tri_ones = jnp.where(ti >= tj, 1.0, 0.0).astype(BF16)

    def blockdiag(x):
        return jnp.concatenate([x * keep_left, x * keep_right], axis=0)

    def stack_diag(xa, xb):
        zero = jnp.zeros_like(xa)
        return jnp.concatenate([jnp.concatenate([xa, zero], axis=1), jnp.concatenate([zero, xb], axis=1)], axis=0)

    def prepare(ci, out):
        rows = slice(ci * c, (ci + 1) * c)
        bd = bd_ref[rows, :]
        bd_hi = bd.astype(BF16)
        bd_rest = bd - bd_hi.astype(F32)
        bd_mid = bd_rest.astype(BF16)
        bd_lo = (bd_rest - bd_mid.astype(F32)).astype(BF16)
        gcum = _dot(tri_ones, bd_hi) + _dot(tri_ones, bd_mid) + _dot(tri_ones, bd_lo)
        yield
        gtot = jnp.broadcast_to(gcum[c - 1:c, :], (c, LANES))
        gcum_t = jnp.concatenate([gcum, gcum], axis=0).T
        e_cum_all = jnp.exp(gcum)
        e_rest_all = jnp.exp(gtot - gcum)
        e_tot_all = jnp.exp(gtot)
        q = [q_ref[rows, cols[hh]] for hh in heads]
        k = [k_ref[rows, cols[hh]] for hh in heads]
        v = [v_ref[rows, cols[hh]] for hh in heads]
        beta = [bd[:, hh:hh + 1] for hh in heads]
        e_cum = [e_cum_all[:, gl:gl + 1] for gl in glane]
        kbeta = [k[hh] * beta[hh] for hh in heads]
        kq = [_dot_nt(jnp.concatenate([jnp.concatenate([kbeta[a], kbeta[b]], axis=1),
                                       jnp.concatenate([q[a], q[b]], axis=1)], axis=0).astype(BF16),
                      stack_diag(k[a].astype(BF16), k[b].astype(BF16)))
              for a, b in pairs]
        yield
        decay = [jnp.exp(jnp.where(lower,
                                   jnp.where(left, gcum[:, glane[a]:glane[a] + 1], gcum[:, glane[b]:glane[b] + 1])
                                   - jnp.where(left_row, gcum_t[glane[a]:glane[a] + 1, :], gcum_t[glane[b]:glane[b] + 1, :]),
                                   NEG_BIG)) for a, b in pairs]
        m = [jnp.where(strict, kq[pp][0:c] * decay[pp], 0.0) for pp in range(len(pairs))]
        n = [-mm for mm in m]
        pb = [mm.astype(BF16) for mm in m]
        p = [_dot(x, blockdiag(x)) for x in pb]
        yield
        rounds = 5
        for r in range(rounds):
            pb = [x.astype(BF16) for x in p]
            upd = [_dot(x, blockdiag(y.astype(BF16))) for x, y in zip(pb, n)]
            p_next = [_dot(x, blockdiag(x)) for x in pb] if r + 1 < rounds else None
            yield
            n = [y + x + u for y, x, u in zip(n, p, upd)]
            p = p_next
        rhs = [jnp.concatenate([v[hh] * beta[hh], kbeta[hh] * e_cum[hh]], axis=1) for hh in heads]
        nr = [_dot(n[pp].astype(BF16), stack_diag(rhs[a].astype(BF16), rhs[b].astype(BF16)))
              for pp, (a, b) in enumerate(pairs)]
        yield
        sol = [rhs[hh] + nr[hh // 2][:, (hh % 2) * 2 * d:(hh % 2 + 1) * 2 * d] for hh in heads]
        out.update(
            rows=rows,
            u=[sol[hh][:, 0:d] for hh in heads],
            wq=[jnp.concatenate([sol[hh][:, d:2 * d], q[hh] * e_cum[hh]], axis=0).astype(BF16) for hh in heads],
            a_qk=[(kq[pp][c:2 * c] * decay[pp]).astype(BF16) for pp in range(len(pairs))],
            k_dec=[(k[hh] * e_rest_all[:, gl:gl + 1]).astype(BF16) for hh, gl in zip(heads, glane)],
            e_tot=[e_tot_all[0:1, gl:gl + 1] for gl in glane])

    for first in range(0, tile // c, GDN_GROUP):
        group = [dict() for _ in range(GDN_GROUP)]
        gens = [prepare(first + gi, group[gi]) for gi in range(GDN_GROUP)]
        for _ in range(GDN_PREP_LAYERS):
            for gen in gens:
                next(gen)
            yield
        for gen in gens:
            for _ in gen:
                pass
        for pre in group:
            state = [state_ref[hh] for hh in heads]
            ws = [_dot(pre["wq"][hh], state[hh].astype(BF16)) for hh in heads]
            yield
            v_new = [(pre["u"][hh] - ws[hh][0:c]).astype(BF16) for hh in heads]
            kv = [_dot_tn(pre["k_dec"][hh], v_new[hh]) for hh in heads]
            av = [_dot(pre["a_qk"][pp], stack_diag(v_new[a], v_new[b])) for pp, (a, b) in enumerate(pairs)]
            yield
            for hh in heads:
                state_ref[hh] = state[hh] * pre["e_tot"][hh] + kv[hh]
            for hh in heads:
                o = ws[hh][c:2 * c] + av[hh // 2][:, (hh % 2) * d:(hh % 2 + 1) * d]
                ob_ref[slot, pre["rows"], cols[hh]] = _rmsnorm(o, gnorm)


GDN_GROUP = 4
GDN_PREP_LAYERS = 9
GDN_LAYERS_PER_GROUP = GDN_PREP_LAYERS + 2 * GDN_GROUP


def _mixer_out_stages(x_ref, ya_ref, ob_ref, slot, g_ref, wgt_ref, wa_ref, wb_ref, wo_ref, o_ref):
    x = x_ref[...]
    h = _rmsnorm(x, g_ref[...]).astype(BF16)
    ya = ya_ref[...].astype(BF16)
    blocks = [slice(j * MIX_BLOCK, (j + 1) * MIX_BLOCK) for j in range(D_MODEL // MIX_BLOCK)]
    gate_cols = lambda which, blk: slice(which * D_MODEL + blk.start, which * D_MODEL + blk.stop)
    yb = []
    for blk in blocks:
        gdn_gate = _dot(h, wgt_ref[:, gate_cols(0, blk)])
        yield
        yb.append((ob_ref[slot, :, blk] * (gdn_gate * _sigmoid(gdn_gate))).astype(BF16))
    yb = jnp.concatenate(yb, axis=1)
    merged = []
    for blk in blocks:
        gate_a = _dot(h, wgt_ref[:, gate_cols(1, blk)])
        branch_a = _dot(ya, wa_ref[:, blk])
        yield
        gate_b = _dot(h, wgt_ref[:, gate_cols(2, blk)])
        yield
        branch_b = _dot(yb, wb_ref[:, blk])
        yield
        merged.append((_sigmoid(gate_a) * branch_a + _sigmoid(gate_b) * branch_b).astype(BF16))
    merged = jnp.concatenate(merged, axis=1)
    for blk in blocks:
        o_ref[:, blk] = x[:, blk] + _dot(merged, wo_ref[:, blk])
        yield


MIX_GRANULES = 5 * (D_MODEL // MIX_BLOCK)


def _mixer_tail_kernel(q_ref, k_ref, v_ref, bd_ref, gn_ref, x_ref, ya_ref, g_ref, wgt_ref, wa_ref, wb_ref, wo_ref,
                       o_ref, state_ref, ob_ref, *, tile, tiles_per_seq, n_tiles):
    step = pl.program_id(0)

    @pl.when(step == 0)
    def _():
        ob_ref[...] = jnp.zeros(ob_ref.shape, F32)

    @pl.when(jnp.minimum(step, n_tiles - 1) % tiles_per_seq == 0)
    def _():
        state_ref[...] = jnp.zeros(state_ref.shape, F32)

    slot = step % 2
    gdn = _deltanet_stages(q_ref, k_ref, v_ref, bd_ref, gn_ref[...], state_ref, ob_ref, slot, tile=tile)
    mix = _mixer_out_stages(x_ref, ya_ref, ob_ref, 1 - slot, g_ref, wgt_ref, wa_ref, wb_ref, wo_ref, o_ref)
    _interleave(gdn, GDN_LAYERS_PER_GROUP * (tile // (GDN_CHUNK * GDN_GROUP)), mix, MIX_GRANULES)


def _mixer_tail(qb, kb, vb, bd, out_norm, x1, ya, norm_g, w_all, w_a, w_b, w_o, *, tile, seq):
    n = x1.shape[0]
    n_tiles = n // tile
    cur = lambda w: pl.BlockSpec((tile, w), lambda s: (jnp.minimum(s, n_tiles - 1), 0))
    prev = lambda w: pl.BlockSpec((tile, w), lambda s: (jnp.maximum(s - 1, 0), 0))
    return pl.pallas_call(
        functools.partial(_mixer_tail_kernel, tile=tile, tiles_per_seq=seq // tile, n_tiles=n_tiles),
        grid=(n_tiles + 1,),
        in_specs=[cur(GDN_WIDTH), cur(GDN_WIDTH), cur(GDN_WIDTH), cur(LANES), _resident((1, GDN_HEAD_DIM)),
                  prev(D_MODEL), prev(ATT_GROUP_WIDTH), _resident((1, D_MODEL)),
                  pl.BlockSpec((D_MODEL, W_GDN), lambda s: (0, W_GATES // W_GDN), pipeline_mode=pl.Buffered(1)),
                  _resident(w_a.shape), _resident(w_b.shape), _resident(w_o.shape)],
        out_specs=prev(D_MODEL),
        out_shape=jax.ShapeDtypeStruct((n, D_MODEL), F32),
        scratch_shapes=[pltpu.VMEM((GDN_HEADS, GDN_HEAD_DIM, GDN_HEAD_DIM), F32),
                        pltpu.VMEM((2, tile, GDN_WIDTH), F32)],
        compiler_params=pltpu.CompilerParams(dimension_semantics=("arbitrary",),
                                             vmem_limit_bytes=VMEM_LIMIT_BYTES),
        name="deltanet_mixer_out",
    )(qb, kb, vb, bd, out_norm, x1, ya, norm_g, w_all, w_a, w_b, w_o)


def _rope_tables(seq):
    half = ATT_HEAD_DIM // 2
    inv_freq = ROPE_THETA ** (-jnp.arange(half, dtype=F32) / half)
    ang = jnp.arange(seq, dtype=F32)[:, None] * inv_freq[None, :]
    cos, sin = jnp.cos(ang), jnp.sin(ang)
    reps = LANES // ATT_HEAD_DIM
    return jnp.tile(jnp.concatenate([cos, cos], axis=-1), (1, reps)), jnp.tile(jnp.concatenate([-sin, sin], axis=-1), (1, reps))


def _pad_lanes(row, offset):
    return jnp.zeros((1, LANES), F32).at[0, offset:offset + row.shape[0]].set(row.astype(F32))


def _layer(x, ffn1_norm, ffn1_w_gate, ffn1_w_up, ffn1_w_down, mix_norm, w_in, gdn_conv_w, gdn_a_log, gdn_dt_bias,
           gdn_out_norm, w_branch_a, w_branch_b, w_out, ffn2_norm, ffn2_w_gate, ffn2_w_up, ffn2_w_down, fin_g,
           *, final_norm, tm_ffn, tm_mix, gdn_tile):
    b, s, _ = x.shape
    n = b * s
    row = lambda v: v.reshape(1, -1).astype(F32)
    w_all = jnp.concatenate(
        [w_in[:, :W_IN_GDN], w_in[:, W_IN_BD:W_IN_GATES], jnp.zeros((D_MODEL, W_GDN - W_BD - 2 * GDN_HEADS), w_in.dtype),
         w_in[:, W_IN_GDN:W_IN_BD], w_in[:, W_IN_GATES:]], axis=1).astype(BF16)
    cos_t, sin_t = _rope_tables(s)
    x1, a0, a1, a2, qb, kb, vb, bd = _head(
        x.reshape(n, D_MODEL), row(ffn1_norm), ffn1_w_gate.astype(BF16), ffn1_w_up.astype(BF16),
        ffn1_w_down.astype(BF16), row(mix_norm), w_all, gdn_conv_w.astype(F32),
        _pad_lanes(gdn_a_log, GDN_HEADS), _pad_lanes(gdn_dt_bias, GDN_HEADS), cos_t, sin_t, tm=tm_mix, seq=s)

    ya = _attention(a0, a1, a2, batch=b)
    x2 = _mixer_tail(qb, kb, vb, bd, row(gdn_out_norm), x1, ya.reshape(n, ATT_GROUP_WIDTH), row(mix_norm), w_all,
                     w_branch_a.astype(BF16), w_branch_b.astype(BF16), w_out.astype(BF16), tile=gdn_tile, seq=s)
    x3 = _ffn(x2, row(ffn2_norm), ffn2_w_gate.astype(BF16), ffn2_w_up.astype(BF16), ffn2_w_down.astype(BF16),
              fin_g, final_norm=final_norm, tm=tm_ffn)
    return x3.reshape(b, s, D_MODEL)


def kernel(x, ffn1_norm, ffn1_w_gate, ffn1_w_up, ffn1_w_down, mix_norm, w_in, gdn_conv_w, gdn_a_log, gdn_dt_bias,
           gdn_out_norm, w_branch_a, w_branch_b, w_out, ffn2_norm, ffn2_w_gate, ffn2_w_up, ffn2_w_down, final_norm):
    depth = ffn1_norm.shape[0]
    fin_g = final_norm.reshape(1, -1).astype(F32)
    for layer in range(depth):
        x = _layer(x, ffn1_norm[layer], ffn1_w_gate[layer], ffn1_w_up[layer], ffn1_w_down[layer], mix_norm[layer],
                   w_in[layer], gdn_conv_w[layer], gdn_a_log[layer], gdn_dt_bias[layer], gdn_out_norm[layer],
                   w_branch_a[layer], w_branch_b[layer], w_out[layer], ffn2_norm[layer], ffn2_w_gate[layer],
                   ffn2_w_up[layer], ffn2_w_down[layer], fin_g, final_norm=(layer == depth - 1),
                   tm_ffn=512, tm_mix=256, gdn_tile=512)
    return x
```

```python
import functools

import jax
import jax.numpy as jnp
from jax import lax
from jax.experimental import pallas as pl
from jax.experimental.pallas import tpu as pltpu

F32 = jnp.float32
BF16 = jnp.bfloat16

D_MODEL = 1024
D_FF = 2816
EPS = 1e-6

ATT_GROUPS = ((128, 1), (512, 4), (2048, 16))
ATT_HEADS_PER_GROUP = 4
ATT_HEAD_DIM = 64
ATT_BLOCK = 128
ATT_GROUP_WIDTH = ATT_HEADS_PER_GROUP * ATT_HEAD_DIM
ATT_QKV_WIDTH = len(ATT_GROUPS) * ATT_GROUP_WIDTH
ROPE_THETA = 10000.0

GDN_HEADS = 8
GDN_HEAD_DIM = 128
GDN_WIDTH = GDN_HEADS * GDN_HEAD_DIM
GDN_CONV = 4
GDN_CHUNK = 64

LANES = 128
SUBLANES = 8
VMEM_LIMIT_BYTES = 56 * 1024 * 1024

W_IN_GDN = 3 * ATT_QKV_WIDTH
W_IN_BD = W_IN_GDN + 3 * GDN_WIDTH
W_IN_GATES = W_IN_BD + 2 * GDN_HEADS
W_ATT = 0
W_BD = 3 * ATT_QKV_WIDTH
W_GDN = 3 * GDN_WIDTH
W_GATES = 2 * W_GDN
FFN_CHUNKS = ((0, 768), (768, 1536), (1536, 2304), (2304, 2816))
NEG_BIG = -1e30


def _resident(shape):
    nd = len(shape)
    return pl.BlockSpec(shape, lambda *_: (0,) * nd, pipeline_mode=pl.Buffered(1))


def _rmsnorm(x, g):
    return x * lax.rsqrt(jnp.mean(x * x, axis=-1, keepdims=True) + EPS) * g


def _sigmoid(x):
    return 1.0 / (1.0 + jnp.exp(-x))


def _dot(a, b):
    return jnp.dot(a, b, preferred_element_type=F32)


def _dot_nt(a, b):
    return lax.dot_general(a, b, (((1,), (1,)), ((), ())), preferred_element_type=F32)


def _dot_tn(a, b):
    return lax.dot_general(a, b, (((0,), (0,)), ((), ())), preferred_element_type=F32)


def _swiglu_residual(x, g, wg_ref, wu_ref, wd_ref):
    h = _rmsnorm(x, g).astype(BF16)
    acc = x
    for lo, hi in FFN_CHUNKS:
        gate = _dot(h, wg_ref[:, lo:hi])
        up = _dot(h, wu_ref[:, lo:hi])
        act = (0.5 * gate * _sigmoid(gate) * up).astype(BF16)
        acc = acc + _dot(act, wd_ref[lo:hi, :])
    return acc


def _ffn_kernel(x_ref, g_ref, wg_ref, wu_ref, wd_ref, fin_ref, o_ref, *, final_norm):
    y = _swiglu_residual(x_ref[...], g_ref[...], wg_ref, wu_ref, wd_ref)
    if final_norm:
        y = _rmsnorm(y, fin_ref[...])
    o_ref[...] = y


def _ffn(x, norm_g, wg, wu, wd, fin_g, *, final_norm, tm):
    n = x.shape[0]
    row = pl.BlockSpec((tm, D_MODEL), lambda i: (i, 0))
    return pl.pallas_call(
        functools.partial(_ffn_kernel, final_norm=final_norm),
        grid=(n // tm,),
        in_specs=[row, _resident((1, D_MODEL)), _resident((D_MODEL, D_FF)), _resident((D_MODEL, D_FF)),
                  _resident((D_FF, D_MODEL)), _resident((1, D_MODEL))],
        out_specs=row,
        out_shape=jax.ShapeDtypeStruct((n, D_MODEL), F32),
        compiler_params=pltpu.CompilerParams(dimension_semantics=("arbitrary",),
                                             vmem_limit_bytes=VMEM_LIMIT_BYTES),
        name="ffn_final" if final_norm else "ffn",
    )(x, norm_g, wg, wu, wd, fin_g)


MIX_BLOCK = 256


def _interleave(primary, n_primary, secondary, n_secondary):
    done = 0
    for i in range(n_primary):
        next(primary)
        while done * n_primary < (i + 1) * n_secondary:
            next(secondary)
            done += 1
    for gen in (primary, secondary):
        for _ in gen:
            pass


def _chunk_time(row):
    return SUBLANES * (row % SUBLANES) + row // SUBLANES


def _mixer_in_stages(x_ref, g_ref, wm_ref, convw_ref, alog_ref, dtb_ref, cos_ref, sin_ref,
                     att_refs, gdn_refs, bd_ref, carry_ref, perm_ref, stage_ref, *, tm):
    hf = _rmsnorm(x_ref[...], g_ref[...])
    h = hf.astype(BF16)
    vregs = GDN_CHUNK // SUBLANES
    for cb in range(D_MODEL // LANES):
        perm_ref[cb] = hf[:, cb * LANES:(cb + 1) * LANES]
    h_perm = jnp.concatenate(
        [jnp.concatenate([perm_ref[cb, pl.ds(c0 + v, SUBLANES, stride=vregs), :]
                          for c0 in range(0, tm, GDN_CHUNK) for v in range(vregs)], axis=0)
         for cb in range(D_MODEL // LANES)], axis=1).astype(BF16)

    cos = cos_ref[...]
    sin = sin_ref[...]
    lane = lax.broadcasted_iota(jnp.int32, (1, LANES), 1)
    first_half = (lane % ATT_HEAD_DIM) < (ATT_HEAD_DIM // 2)
    slot = 0
    for part in range(3):
        for gi, (_, dil) in enumerate(ATT_GROUPS):
            col = W_ATT + part * ATT_QKV_WIDTH + gi * ATT_GROUP_WIDTH
            y = _dot(h, wm_ref[:, col:col + ATT_GROUP_WIDTH])
            yield
            for j in range(ATT_GROUP_WIDTH // LANES):
                blk = y[:, j * LANES:(j + 1) * LANES]
                if part < 2:
                    swapped = jnp.where(first_half, pltpu.roll(blk, LANES - ATT_HEAD_DIM // 2, 1),
                                        pltpu.roll(blk, ATT_HEAD_DIM // 2, 1))
                    blk = blk * cos + swapped * sin
                if part == 0:
                    blk = blk * (ATT_HEAD_DIM ** -0.5)
                dst = part * ATT_GROUP_WIDTH + j * LANES
                if dil == 1:
                    att_refs[gi][:, dst:dst + LANES] = blk.astype(BF16)
                else:
                    stage_ref[slot] = blk
                    for r in range(dil):
                        rows = stage_ref[slot, pl.ds(r, tm // dil, stride=dil), :]
                        lo = r * 3 * ATT_GROUP_WIDTH + dst
                        att_refs[gi][:, lo:lo + LANES] = rows.astype(BF16)
                    slot += 1
                yield

    raw = _dot(h_perm, wm_ref[:, W_BD:W_BD + LANES])
    yield
    z = raw + dtb_ref[...]
    softplus = jnp.maximum(z, 0.0) + jnp.log1p(jnp.exp(-jnp.abs(z)))
    g = -jnp.exp(alog_ref[...]) * softplus
    bd_ref[...] = jnp.where(lane < GDN_HEADS, _sigmoid(raw), jnp.where(lane < 2 * GDN_HEADS, g, 0.0))

    halo = GDN_CONV - 1
    chunks = tm // GDN_CHUNK
    last_sublane = lax.broadcasted_iota(jnp.int32, (SUBLANES, GDN_HEAD_DIM), 0) == SUBLANES - 1
    heads_per_block = MIX_BLOCK // GDN_HEAD_DIM
    for part in range(3):
        for blk in range(GDN_WIDTH // MIX_BLOCK):
            base = part * GDN_WIDTH + blk * MIX_BLOCK
            y = _dot(h_perm, wm_ref[:, W_GDN + base:W_GDN + base + MIX_BLOCK])
            yield
            for hb in range(heads_per_block):
                hh = blk * heads_per_block + hb
                col = part * GDN_WIDTH + hh * GDN_HEAD_DIM
                cur = y[:, hb * GDN_HEAD_DIM:(hb + 1) * GDN_HEAD_DIM]
                vreg = lambda c, v: cur[c * GDN_CHUNK + v * SUBLANES:c * GDN_CHUNK + (v + 1) * SUBLANES]
                prev_tile = carry_ref[:, col:col + GDN_HEAD_DIM]
                carry_ref[:, col:col + GDN_HEAD_DIM] = cur[tm - halo * SUBLANES:, :]
                wrapped = []
                for c in range(chunks):
                    row = []
                    for i in range(halo):
                        before = (prev_tile[i * SUBLANES:(i + 1) * SUBLANES] if c == 0
                                  else vreg(c - 1, vregs - halo + i))
                        own = vreg(c, vregs - halo + i)
                        row.append(pltpu.roll(jnp.where(last_sublane, before, own), 1, 0))
                    wrapped.append(row)
                acc = cur * convw_ref[halo:halo + 1, col:col + GDN_HEAD_DIM]
                for shift in range(1, GDN_CONV):
                    pieces = []
                    for c in range(chunks):
                        pieces += wrapped[c][halo - shift:]
                        pieces.append(cur[c * GDN_CHUNK:(c + 1) * GDN_CHUNK - shift * SUBLANES])
                    shifted = jnp.concatenate(pieces, axis=0)
                    acc = acc + shifted * convw_ref[halo - shift:halo - shift + 1, col:col + GDN_HEAD_DIM]
                act = acc * _sigmoid(acc)
                if part < 2:
                    act = act * lax.rsqrt(jnp.sum(act * act, axis=-1, keepdims=True) + EPS)
                if part == 0:
                    act = act * (GDN_HEAD_DIM ** -0.5)
                gdn_refs[part][:, hh * GDN_HEAD_DIM:(hh + 1) * GDN_HEAD_DIM] = act
                yield


MIXER_IN_STAGED = 3 * (len(ATT_GROUPS) - 1) * (ATT_GROUP_WIDTH // LANES)


def _mixer_in_kernel(x_ref, g_ref, wm_ref, convw_ref, alog_ref, dtb_ref, cos_ref, sin_ref,
                     a0_ref, a1_ref, a2_ref, qb_ref, kb_ref, vb_ref, bd_ref, carry_ref, perm_ref, stage_ref, *, tm):
    @pl.when(pl.program_id(1) == 0)
    def _():
        carry_ref[...] = jnp.zeros(carry_ref.shape, F32)

    for _ in _mixer_in_stages(x_ref, g_ref, wm_ref, convw_ref, alog_ref, dtb_ref, cos_ref, sin_ref,
                              (a0_ref, a1_ref, a2_ref), (qb_ref, kb_ref, vb_ref), bd_ref, carry_ref, perm_ref,
                              stage_ref, tm=tm):
        pass


def _mixer_in(x1, norm_g, w_all, conv_w, a_log, dt_bias, cos_t, sin_t, *, tm, seq):
    n = x1.shape[0]
    tiles_per_seq = seq // tm
    tile = lambda rows, w: pl.BlockSpec((rows, w), lambda bi, i: (bi * tiles_per_seq + i, 0))
    table = pl.BlockSpec((tm, LANES), lambda bi, i: (i, 0))
    wq = 3 * ATT_GROUP_WIDTH
    att_specs = [tile(tm // dil, dil * wq) for _, dil in ATT_GROUPS]
    att_shapes = [jax.ShapeDtypeStruct((n // dil, dil * wq), BF16) for _, dil in ATT_GROUPS]
    gdn = jax.ShapeDtypeStruct((n, GDN_WIDTH), F32)
    return pl.pallas_call(
        functools.partial(_mixer_in_kernel, tm=tm),
        grid=(n // seq, tiles_per_seq),
        in_specs=[tile(tm, D_MODEL), _resident((1, D_MODEL)),
                  pl.BlockSpec((D_MODEL, W_GATES), lambda bi, i: (0, 0), pipeline_mode=pl.Buffered(1)),
                  _resident((GDN_CONV, 3 * GDN_WIDTH)), _resident((1, LANES)), _resident((1, LANES)), table, table],
        out_specs=att_specs + [tile(tm, GDN_WIDTH)] * 3 + [tile(tm, LANES)],
        out_shape=att_shapes + [gdn] * 3 + [jax.ShapeDtypeStruct((n, LANES), F32)],
        scratch_shapes=[pltpu.VMEM(((GDN_CONV - 1) * SUBLANES, 3 * GDN_WIDTH), F32),
                        pltpu.VMEM((D_MODEL // LANES, tm, LANES), F32), pltpu.VMEM((MIXER_IN_STAGED, tm, LANES), F32)],
        compiler_params=pltpu.CompilerParams(dimension_semantics=("arbitrary", "arbitrary"),
                                             vmem_limit_bytes=VMEM_LIMIT_BYTES),
        name="mixer_in",
    )(x1, norm_g, w_all, conv_w, a_log, dt_bias, cos_t, sin_t)


def _attention_block(q, k, v, valid):
    lane = lax.broadcasted_iota(jnp.int32, (1, ATT_GROUP_WIDTH), 1)
    o = jnp.zeros((ATT_BLOCK, ATT_GROUP_WIDTH), F32)
    inv_l = jnp.zeros((ATT_BLOCK, ATT_GROUP_WIDTH), F32)
    lse = jnp.zeros((ATT_BLOCK, ATT_GROUP_WIDTH), F32)
    for hh in range(ATT_HEADS_PER_GROUP):
        in_head = (lane // ATT_HEAD_DIM) == hh
        keep = jnp.where(in_head, 1.0, 0.0).astype(BF16)
        s = _dot_nt(q * keep, k)
        s = jnp.where(valid, s, NEG_BIG)
        m = jnp.max(s, axis=-1, keepdims=True)
        p = jnp.exp(s - m)
        l = jnp.sum(p, axis=-1, keepdims=True)
        o = o + _dot(p.astype(BF16), v * keep)
        inv_l = jnp.where(in_head, 1.0 / l, inv_l)
        lse = jnp.where(in_head, m + jnp.log(l), lse)
    return o * inv_l, lse


def _attention_kernel(a0_ref, a1_ref, a2_ref, ya_ref, o0, l0, o1, l1, o2, l2, *, seq):
    in_refs = (a0_ref, a1_ref, a2_ref)
    o_refs = (o0, o1, o2)
    l_refs = (l0, l1, l2)
    qi = lax.broadcasted_iota(jnp.int32, (ATT_BLOCK, ATT_BLOCK), 0)
    kj = lax.broadcasted_iota(jnp.int32, (ATT_BLOCK, ATT_BLOCK), 1)
    causal = kj <= qi
    qi2 = lax.broadcasted_iota(jnp.int32, (ATT_BLOCK, 2 * ATT_BLOCK), 0)
    kj2 = lax.broadcasted_iota(jnp.int32, (ATT_BLOCK, 2 * ATT_BLOCK), 1)
    band = (kj2 >= qi2) & (kj2 - ATT_BLOCK <= qi2)

    for gi, (window, dil) in enumerate(ATT_GROUPS):
        assert window // dil == ATT_BLOCK
        length = seq // dil
        nblk = length // ATT_BLOCK
        src, o_ref, l_ref = in_refs[gi], o_refs[gi], l_refs[gi]
        wq = 3 * ATT_GROUP_WIDTH

        def store(o_ref, l_ref, dil, r, n, o, lse):
            if dil == 1:
                rows = pl.ds(pl.multiple_of(n * ATT_BLOCK, ATT_BLOCK), ATT_BLOCK)
            else:
                rows = pl.ds(n * ATT_BLOCK * dil + r, ATT_BLOCK, stride=dil)
            for half in range(ATT_GROUP_WIDTH // LANES):
                o_ref[half, rows, :] = o[:, half * LANES:(half + 1) * LANES]
                l_ref[half, rows, :] = lse[:, half * LANES:(half + 1) * LANES]

        for r in range(dil):
            base = r * wq
            q = src[0, 0:ATT_BLOCK, base:base + ATT_GROUP_WIDTH]
            k = src[0, 0:ATT_BLOCK, base + ATT_GROUP_WIDTH:base + 2 * ATT_GROUP_WIDTH]
            v = src[0, 0:ATT_BLOCK, base + 2 * ATT_GROUP_WIDTH:base + 3 * ATT_GROUP_WIDTH]
            o, lse = _attention_block(q, k, v, causal)
            store(o_ref, l_ref, dil, r, 0, o, lse)

            if nblk > 1:
                def body(n, carry, src=src, base=base, o_ref=o_ref, l_ref=l_ref, dil=dil, r=r):
                    qrows = pl.ds(pl.multiple_of(n * ATT_BLOCK, ATT_BLOCK), ATT_BLOCK)
                    krows = pl.ds(pl.multiple_of((n - 1) * ATT_BLOCK, ATT_BLOCK), 2 * ATT_BLOCK)
                    q = src[0, qrows, base:base + ATT_GROUP_WIDTH]
                    k = src[0, krows, base + ATT_GROUP_WIDTH:base + 2 * ATT_GROUP_WIDTH]
                    v = src[0, krows, base + 2 * ATT_GROUP_WIDTH:base + 3 * ATT_GROUP_WIDTH]
                    o, lse = _attention_block(q, k, v, band)
                    store(o_ref, l_ref, dil, r, n, o, lse)
                    return carry
                lax.fori_loop(1, nblk, body, 0)

    rows_per_step = 256

    def merge(i, carry):
        rows = pl.ds(pl.multiple_of(i * rows_per_step, rows_per_step), rows_per_step)
        for half in range(ATT_GROUP_WIDTH // LANES):
            la, lb, lc = l0[half, rows, :], l1[half, rows, :], l2[half, rows, :]
            m = jnp.maximum(jnp.maximum(la, lb), lc)
            ea, eb, ec = jnp.exp(la - m), jnp.exp(lb - m), jnp.exp(lc - m)
            num = ea * o0[half, rows, :] + eb * o1[half, rows, :] + ec * o2[half, rows, :]
            ya_ref[0, rows, half * LANES:(half + 1) * LANES] = num / (ea + eb + ec)
        return carry
    lax.fori_loop(0, seq // rows_per_step, merge, 0)


def _attention(a0, a1, a2, *, batch):
    views = tuple(a.reshape(batch, a.shape[0] // batch, a.shape[1]) for a in (a0, a1, a2))
    b, s, _ = views[0].shape
    specs = [pl.BlockSpec((1,) + arr.shape[1:], lambda bi: (bi, 0, 0)) for arr in views]
    scratch = [pltpu.VMEM((ATT_GROUP_WIDTH // LANES, s, LANES), F32) for _ in range(6)]
    return pl.pallas_call(
        functools.partial(_attention_kernel, seq=s),
        grid=(b,),
        in_specs=specs,
        out_specs=pl.BlockSpec((1, s, ATT_GROUP_WIDTH), lambda bi: (bi, 0, 0)),
        out_shape=jax.ShapeDtypeStruct((b, s, ATT_GROUP_WIDTH), F32),
        scratch_shapes=scratch,
        compiler_params=pltpu.CompilerParams(dimension_semantics=("arbitrary",),
                                             vmem_limit_bytes=VMEM_LIMIT_BYTES),
        name="dilated_attention",
    )(*views)


def _deltanet_stages(q_ref, k_ref, v_ref, bd_ref, gnorm, state_ref, ob_ref, slot, *, tile):
    c = GDN_CHUNK
    d = GDN_HEAD_DIM
    heads = range(GDN_HEADS)
    pairs = [(2 * pp, 2 * pp + 1) for pp in range(GDN_HEADS // 2)]
    cols = [slice(hh * d, (hh + 1) * d) for hh in heads]
    glane = [GDN_HEADS + hh for hh in heads]
    ii = _chunk_time(lax.broadcasted_iota(jnp.int32, (c, 2 * c), 0))
    ll = lax.broadcasted_iota(jnp.int32, (c, 2 * c), 1)
    jj = _chunk_time(ll % c)
    lower = ii >= jj
    strict = ii > jj
    left = ll < c
    left_row = lax.broadcasted_iota(jnp.int32, (1, 2 * c), 1) < c
    keep_left = jnp.where(left, 1.0, 0.0).astype(BF16)
    keep_right = jnp.where(left, 0.0, 1.0).astype(BF16)
    ti = _chunk_time(lax.broadcasted_iota(jnp.int32, (c, c), 0))
    tj = _chunk_time(lax.broadcasted_iota(jnp.int32, (c, c), 1))
    tri_ones = jnp.where(ti >= tj, 1.0, 0.0).astype(BF16)

    def blockdiag(x):
        return jnp.concatenate([x * keep_left, x * keep_right], axis=0)

    def stack_diag(xa, xb):
        zero = jnp.zeros_like(xa)
        return jnp.concatenate([jnp.concatenate([xa, zero], axis=1), jnp.concatenate([zero, xb], axis=1)], axis=0)

    def prepare(ci, out):
        rows = slice(ci * c, (ci + 1) * c)
        bd = bd_ref[rows, :]
        bd_hi = bd.astype(BF16)
        bd_rest = bd - bd_hi.astype(F32)
        bd_mid = bd_rest.astype(BF16)
        bd_lo = (bd_rest - bd_mid.astype(F32)).astype(BF16)
        gcum = _dot(tri_ones, bd_hi) + _dot(tri_ones, bd_mid) + _dot(tri_ones, bd_lo)
        yield
        gtot = jnp.broadcast_to(gcum[c - 1:c, :], (c, LANES))
        gcum_t = jnp.concatenate([gcum, gcum], axis=0).T
        e_cum_all = jnp.exp(gcum)
        e_rest_all = jnp.exp(gtot - gcum)
        e_tot_all = jnp.exp(gtot)
        q = [q_ref[rows, cols[hh]] for hh in heads]
        k = [k_ref[rows, cols[hh]] for hh in heads]
        v = [v_ref[rows, cols[hh]] for hh in heads]
        beta = [bd[:, hh:hh + 1] for hh in heads]
        e_cum = [e_cum_all[:, gl:gl + 1] for gl in glane]
        kbeta = [k[hh] * beta[hh] for hh in heads]
        kq = [_dot_nt(jnp.concatenate([jnp.concatenate([kbeta[a], kbeta[b]], axis=1),
                                       jnp.concatenate([q[a], q[b]], axis=1)], axis=0).astype(BF16),
                      stack_diag(k[a].astype(BF16), k[b].astype(BF16)))
              for a, b in pairs]
        yield
        decay = [jnp.exp(jnp.where(lower,
                                   jnp.where(left, gcum[:, glane[a]:glane[a] + 1], gcum[:, glane[b]:glane[b] + 1])
                                   - jnp.where(left_row, gcum_t[glane[a]:glane[a] + 1, :], gcum_t[glane[b]:glane[b] + 1, :]),
                                   NEG_BIG)) for a, b in pairs]
        m = [jnp.where(strict, kq[pp][0:c] * decay[pp], 0.0) for pp in range(len(pairs))]
        n = [-mm for mm in m]
        pb = [mm.astype(BF16) for mm in m]
        p = [_dot(x, blockdiag(x)) for x in pb]
        yield
        rounds = 5
        for r in range(rounds):
            pb = [x.astype(BF16) for x in p]
            upd = [_dot(x, blockdiag(y.astype(BF16))) for x, y in zip(pb, n)]
            p_next = [_dot(x, blockdiag(x)) for x in pb] if r + 1 < rounds else None
            yield
            n = [y + x + u for y, x, u in zip(n, p, upd)]
            p = p_next
        rhs = [jnp.concatenate([v[hh] * beta[hh], kbeta[hh] * e_cum[hh]], axis=1) for hh in heads]
        nr = [_dot(n[pp].astype(BF16), stack_diag(rhs[a].astype(BF16), rhs[b].astype(BF16)))
              for pp, (a, b) in enumerate(pairs)]
        yield
        sol = [rhs[hh] + nr[hh // 2][:, (hh % 2) * 2 * d:(hh % 2 + 1) * 2 * d] for hh in heads]
        out.update(
            first=ci * c,
            u=[sol[hh][:, 0:d] for hh in heads],
            wq=[jnp.concatenate([sol[hh][:, d:2 * d], q[hh] * e_cum[hh]], axis=0).astype(BF16) for hh in heads],
            a_qk=[(kq[pp][c:2 * c] * decay[pp]).astype(BF16) for pp in range(len(pairs))],
            k_dec=[(k[hh] * e_rest_all[:, gl:gl + 1]).astype(BF16) for hh, gl in zip(heads, glane)],
            e_tot=[e_tot_all[0:1, gl:gl + 1] for gl in glane])

    for first in range(0, tile // c, GDN_GROUP):
        group = [dict() for _ in range(GDN_GROUP)]
        gens = [prepare(first + gi, group[gi]) for gi in range(GDN_GROUP)]
        for _ in range(GDN_PREP_LAYERS):
            for gen in gens:
                next(gen)
            yield
        for gen in gens:
            for _ in gen:
                pass
        for pre in group:
            state = [state_ref[hh] for hh in heads]
            ws = [_dot(pre["wq"][hh], state[hh].astype(BF16)) for hh in heads]
            yield
            v_new = [(pre["u"][hh] - ws[hh][0:c]).astype(BF16) for hh in heads]
            kv = [_dot_tn(pre["k_dec"][hh], v_new[hh]) for hh in heads]
            av = [_dot(pre["a_qk"][pp], stack_diag(v_new[a], v_new[b])) for pp, (a, b) in enumerate(pairs)]
            yield
            for hh in heads:
                state_ref[hh] = state[hh] * pre["e_tot"][hh] + kv[hh]
            for hh in heads:
                o = _rmsnorm(ws[hh][c:2 * c] + av[hh // 2][:, (hh % 2) * d:(hh % 2 + 1) * d], gnorm)
                for vv in range(c // SUBLANES):
                    ob_ref[slot, hh, pl.ds(pre["first"] + vv, SUBLANES, stride=c // SUBLANES), :] = (
                        o[vv * SUBLANES:(vv + 1) * SUBLANES])


GDN_GROUP = 4
GDN_PREP_LAYERS = 9
GDN_LAYERS_PER_GROUP = GDN_PREP_LAYERS + 2 * GDN_GROUP


def _mixer_out_stages(x_ref, ya_ref, ob_ref, slot, g_ref, wgt_ref, wa_ref, wb_ref, wo_ref, o_ref):
    x = x_ref[...]
    h = _rmsnorm(x, g_ref[...]).astype(BF16)
    ya = ya_ref[...].astype(BF16)
    blocks = [slice(j * MIX_BLOCK, (j + 1) * MIX_BLOCK) for j in range(D_MODEL // MIX_BLOCK)]
    gate_cols = lambda which, blk: slice(which * D_MODEL + blk.start, which * D_MODEL + blk.stop)
    yb = []
    for blk in blocks:
        gdn_gate = _dot(h, wgt_ref[:, gate_cols(0, blk)])
        yield
        ob = jnp.concatenate([ob_ref[slot, hh] for hh in range(blk.start // GDN_HEAD_DIM, blk.stop // GDN_HEAD_DIM)],
                             axis=1)
        yb.append((ob * (gdn_gate * _sigmoid(gdn_gate))).astype(BF16))
    yb = jnp.concatenate(yb, axis=1)
    merged = []
    for blk in blocks:
        gate_a = _dot(h, wgt_ref[:, gate_cols(1, blk)])
        branch_a = _dot(ya, wa_ref[:, blk])
        yield
        gate_b = _dot(h, wgt_ref[:, gate_cols(2, blk)])
        yield
        branch_b = _dot(yb, wb_ref[:, blk])
        yield
        merged.append((_sigmoid(gate_a) * branch_a + _sigmoid(gate_b) * branch_b).astype(BF16))
    merged = jnp.concatenate(merged, axis=1)
    for blk in blocks:
        o_ref[:, blk] = x[:, blk] + _dot(merged, wo_ref[:, blk])
        yield


MIX_GRANULES = 5 * (D_MODEL // MIX_BLOCK)


def _mixer_tail_kernel(q_ref, k_ref, v_ref, bd_ref, gn_ref, x_ref, ya_ref, g_ref, wgt_ref, wa_ref, wb_ref, wo_ref,
                       o_ref, state_ref, ob_ref, *, tile, tiles_per_seq, n_tiles):
    step = pl.program_id(0)

    @pl.when(step == 0)
    def _():
        ob_ref[...] = jnp.zeros(ob_ref.shape, F32)

    @pl.when(jnp.minimum(step, n_tiles - 1) % tiles_per_seq == 0)
    def _():
        state_ref[...] = jnp.zeros(state_ref.shape, F32)

    slot = step % 2
    gdn = _deltanet_stages(q_ref, k_ref, v_ref, bd_ref, gn_ref[...], state_ref, ob_ref, slot, tile=tile)
    mix = _mixer_out_stages(x_ref, ya_ref, ob_ref, 1 - slot, g_ref, wgt_ref, wa_ref, wb_ref, wo_ref, o_ref)
    _interleave(gdn, GDN_LAYERS_PER_GROUP * (tile // (GDN_CHUNK * GDN_GROUP)), mix, MIX_GRANULES)


def _mixer_tail(qb, kb, vb, bd, out_norm, x1, ya, norm_g, w_all, w_a, w_b, w_o, *, tile, seq):
    n = x1.shape[0]
    n_tiles = n // tile
    cur = lambda w: pl.BlockSpec((tile, w), lambda s: (jnp.minimum(s, n_tiles - 1), 0))
    prev = lambda w: pl.BlockSpec((tile, w), lambda s: (jnp.maximum(s - 1, 0), 0))
    return pl.pallas_call(
        functools.partial(_mixer_tail_kernel, tile=tile, tiles_per_seq=seq // tile, n_tiles=n_tiles),
        grid=(n_tiles + 1,),
        in_specs=[cur(GDN_WIDTH), cur(GDN_WIDTH), cur(GDN_WIDTH), cur(LANES), _resident((1, GDN_HEAD_DIM)),
                  prev(D_MODEL), prev(ATT_GROUP_WIDTH), _resident((1, D_MODEL)),
                  pl.BlockSpec((D_MODEL, W_GDN), lambda s: (0, W_GATES // W_GDN), pipeline_mode=pl.Buffered(1)),
                  _resident(w_a.shape), _resident(w_b.shape), _resident(w_o.shape)],
        out_specs=prev(D_MODEL),
        out_shape=jax.ShapeDtypeStruct((n, D_MODEL), F32),
        scratch_shapes=[pltpu.VMEM((GDN_HEADS, GDN_HEAD_DIM, GDN_HEAD_DIM), F32),
                        pltpu.VMEM((2, GDN_HEADS, tile, GDN_HEAD_DIM), F32)],
        compiler_params=pltpu.CompilerParams(dimension_semantics=("arbitrary",),
                                             vmem_limit_bytes=VMEM_LIMIT_BYTES),
        name="deltanet_mixer_out",
    )(qb, kb, vb, bd, out_norm, x1, ya, norm_g, w_all, w_a, w_b, w_o)


def _rope_tables(seq):
    half = ATT_HEAD_DIM // 2
    inv_freq = ROPE_THETA ** (-jnp.arange(half, dtype=F32) / half)
    ang = jnp.arange(seq, dtype=F32)[:, None] * inv_freq[None, :]
    cos, sin = jnp.cos(ang), jnp.sin(ang)
    reps = LANES // ATT_HEAD_DIM
    return jnp.tile(jnp.concatenate([cos, cos], axis=-1), (1, reps)), jnp.tile(jnp.concatenate([-sin, sin], axis=-1), (1, reps))


def _pad_lanes(row, offset):
    return jnp.zeros((1, LANES), F32).at[0, offset:offset + row.shape[0]].set(row.astype(F32))


def _layer(x, ffn1_norm, ffn1_w_gate, ffn1_w_up, ffn1_w_down, mix_norm, w_in, gdn_conv_w, gdn_a_log, gdn_dt_bias,
           gdn_out_norm, w_branch_a, w_branch_b, w_out, ffn2_norm, ffn2_w_gate, ffn2_w_up, ffn2_w_down, fin_g,
           *, final_norm, tm_ffn, tm_mix, gdn_tile):
    b, s, _ = x.shape
    n = b * s
    row = lambda v: v.reshape(1, -1).astype(F32)
    w_all = jnp.concatenate(
        [w_in[:, :W_IN_GDN], w_in[:, W_IN_BD:W_IN_GATES], jnp.zeros((D_MODEL, W_GDN - W_BD - 2 * GDN_HEADS), w_in.dtype),
         w_in[:, W_IN_GDN:W_IN_BD], w_in[:, W_IN_GATES:]], axis=1).astype(BF16)
    cos_t, sin_t = _rope_tables(s)
    x1 = _ffn(x.reshape(n, D_MODEL), row(ffn1_norm), ffn1_w_gate.astype(BF16), ffn1_w_up.astype(BF16),
              ffn1_w_down.astype(BF16), fin_g, final_norm=False, tm=tm_ffn)
    a0, a1, a2, qb, kb, vb, bd = _mixer_in(
        x1, row(mix_norm), w_all, gdn_conv_w.astype(F32), _pad_lanes(gdn_a_log, GDN_HEADS),
        _pad_lanes(gdn_dt_bias, GDN_HEADS), cos_t, sin_t, tm=tm_mix, seq=s)

    ya = _attention(a0, a1, a2, batch=b)
    x2 = _mixer_tail(qb, kb, vb, bd, row(gdn_out_norm), x1, ya.reshape(n, ATT_GROUP_WIDTH), row(mix_norm), w_all,
                     w_branch_a.astype(BF16), w_branch_b.astype(BF16), w_out.astype(BF16), tile=gdn_tile, seq=s)
    x3 = _ffn(x2, row(ffn2_norm), ffn2_w_gate.astype(BF16), ffn2_w_up.astype(BF16), ffn2_w_down.astype(BF16),
              fin_g, final_norm=final_norm, tm=tm_ffn)
    return x3.reshape(b, s, D_MODEL)


def kernel(x, ffn1_norm, ffn1_w_gate, ffn1_w_up, ffn1_w_down, mix_norm, w_in, gdn_conv_w, gdn_a_log, gdn_dt_bias,
           gdn_out_norm, w_branch_a, w_branch_b, w_out, ffn2_norm, ffn2_w_gate, ffn2_w_up, ffn2_w_down, final_norm):
    depth = ffn1_norm.shape[0]
    fin_g = final_norm.reshape(1, -1).astype(F32)
    for layer in range(depth):
        x = _layer(x, ffn1_norm[layer], ffn1_w_gate[layer], ffn1_w_up[layer], ffn1_w_down[layer], mix_norm[layer],
                   w_in[layer], gdn_conv_w[layer], gdn_a_log[layer], gdn_dt_bias[layer], gdn_out_norm[layer],
                   w_branch_a[layer], w_branch_b[layer], w_out[layer], ffn2_norm[layer], ffn2_w_gate[layer],
                   ffn2_w_up[layer], ffn2_w_down[layer], fin_g, final_norm=(layer == depth - 1),
                   tm_ffn=512, tm_mix=512, gdn_tile=512)
    return x
```

```python
import functools

import jax
import jax.numpy as jnp
from jax import lax
from jax.experimental import pallas as pl
from jax.experimental.pallas import tpu as pltpu

F32 = jnp.float32
BF16 = jnp.bfloat16

D_MODEL = 1024
D_FF = 2816
EPS = 1e-6

ATT_GROUPS = ((128, 1), (512, 4), (2048, 16))
ATT_HEADS_PER_GROUP = 4
ATT_HEAD_DIM = 64
ATT_BLOCK = 128
ATT_GROUP_WIDTH = ATT_HEADS_PER_GROUP * ATT_HEAD_DIM
ATT_QKV_WIDTH = len(ATT_GROUPS) * ATT_GROUP_WIDTH
ROPE_THETA = 10000.0

GDN_HEADS = 8
GDN_HEAD_DIM = 128
GDN_WIDTH = GDN_HEADS * GDN_HEAD_DIM
GDN_CONV = 4
GDN_CHUNK = 64

LANES = 128
SUBLANES = 8
VMEM_LIMIT_BYTES = 56 * 1024 * 1024

W_IN_GDN = 3 * ATT_QKV_WIDTH
W_IN_BD = W_IN_GDN + 3 * GDN_WIDTH
W_IN_GATES = W_IN_BD + 2 * GDN_HEADS
W_ATT = 0
W_BD = 3 * ATT_QKV_WIDTH
W_GDN = 3 * GDN_WIDTH
W_GATES = 2 * W_GDN
FFN_CHUNKS = ((0, 768), (768, 1536), (1536, 2304), (2304, 2816))
NEG_BIG = -1e30


def _resident(shape):
    nd = len(shape)
    return pl.BlockSpec(shape, lambda *_: (0,) * nd, pipeline_mode=pl.Buffered(1))


def _rmsnorm(x, g):
    return x * lax.rsqrt(jnp.mean(x * x, axis=-1, keepdims=True) + EPS) * g


def _sigmoid(x):
    return 1.0 / (1.0 + jnp.exp(-x))


def _dot(a, b):
    return jnp.dot(a, b, preferred_element_type=F32)


def _dot_nt(a, b):
    return lax.dot_general(a, b, (((1,), (1,)), ((), ())), preferred_element_type=F32)


def _dot_tn(a, b):
    return lax.dot_general(a, b, (((0,), (0,)), ((), ())), preferred_element_type=F32)


def _swiglu_residual(x, g, wg_ref, wu_ref, wd_ref):
    h = _rmsnorm(x, g).astype(BF16)
    acc = x
    for lo, hi in FFN_CHUNKS:
        gate = _dot(h, wg_ref[:, lo:hi])
        up = _dot(h, wu_ref[:, lo:hi])
        act = (0.5 * gate * _sigmoid(gate) * up).astype(BF16)
        acc = acc + _dot(act, wd_ref[lo:hi, :])
    return acc


def _ffn_kernel(x_ref, g_ref, wg_ref, wu_ref, wd_ref, fin_ref, o_ref, *, final_norm):
    y = _swiglu_residual(x_ref[...], g_ref[...], wg_ref, wu_ref, wd_ref)
    if final_norm:
        y = _rmsnorm(y, fin_ref[...])
    o_ref[...] = y


def _ffn(x, norm_g, wg, wu, wd, fin_g, *, final_norm, tm):
    n = x.shape[0]
    row = pl.BlockSpec((tm, D_MODEL), lambda i: (i, 0))
    return pl.pallas_call(
        functools.partial(_ffn_kernel, final_norm=final_norm),
        grid=(n // tm,),
        in_specs=[row, _resident((1, D_MODEL)), _resident((D_MODEL, D_FF)), _resident((D_MODEL, D_FF)),
                  _resident((D_FF, D_MODEL)), _resident((1, D_MODEL))],
        out_specs=row,
        out_shape=jax.ShapeDtypeStruct((n, D_MODEL), F32),
        compiler_params=pltpu.CompilerParams(dimension_semantics=("arbitrary",),
                                             vmem_limit_bytes=VMEM_LIMIT_BYTES),
        name="ffn_final" if final_norm else "ffn",
    )(x, norm_g, wg, wu, wd, fin_g)


MIX_BLOCK = 256


def _interleave(primary, n_primary, secondary, n_secondary):
    done = 0
    for i in range(n_primary):
        next(primary)
        while done * n_primary < (i + 1) * n_secondary:
            next(secondary)
            done += 1
    for gen in (primary, secondary):
        for _ in gen:
            pass


def _chunk_time(row):
    return SUBLANES * (row % SUBLANES) + row // SUBLANES


def _mixer_in_stages(x_ref, g_ref, wm_ref, convw_ref, alog_ref, dtb_ref, cos_ref, sin_ref,
                     att_refs, gdn_refs, bd_ref, carry_ref, perm_ref, stage_ref, *, tm):
    hf = _rmsnorm(x_ref[...], g_ref[...])
    h = hf.astype(BF16)
    vregs = GDN_CHUNK // SUBLANES
    for cb in range(D_MODEL // LANES):
        perm_ref[cb] = hf[:, cb * LANES:(cb + 1) * LANES]
    h_perm = jnp.concatenate(
        [jnp.concatenate([perm_ref[cb, pl.ds(c0 + v, SUBLANES, stride=vregs), :]
                          for c0 in range(0, tm, GDN_CHUNK) for v in range(vregs)], axis=0)
         for cb in range(D_MODEL // LANES)], axis=1).astype(BF16)

    cos = cos_ref[...]
    sin = sin_ref[...]
    lane = lax.broadcasted_iota(jnp.int32, (1, LANES), 1)
    first_half = (lane % ATT_HEAD_DIM) < (ATT_HEAD_DIM // 2)
    slot = 0
    for part in range(3):
        for gi, (_, dil) in enumerate(ATT_GROUPS):
            col = W_ATT + part * ATT_QKV_WIDTH + gi * ATT_GROUP_WIDTH
            y = _dot(h, wm_ref[:, col:col + ATT_GROUP_WIDTH])
            yield
            for j in range(ATT_GROUP_WIDTH // LANES):
                blk = y[:, j * LANES:(j + 1) * LANES]
                if part < 2:
                    swapped = jnp.where(first_half, pltpu.roll(blk, LANES - ATT_HEAD_DIM // 2, 1),
                                        pltpu.roll(blk, ATT_HEAD_DIM // 2, 1))
                    blk = blk * cos + swapped * sin
                if part == 0:
                    blk = blk * (ATT_HEAD_DIM ** -0.5)
                dst = part * ATT_GROUP_WIDTH + j * LANES
                if dil == 1:
                    att_refs[gi][:, dst:dst + LANES] = blk.astype(BF16)
                else:
                    stage_ref[slot] = blk
                    for r in range(dil):
                        rows = stage_ref[slot, pl.ds(r, tm // dil, stride=dil), :]
                        lo = r * 3 * ATT_GROUP_WIDTH + dst
                        att_refs[gi][:, lo:lo + LANES] = rows.astype(BF16)
                    slot += 1
                yield

    raw = _dot(h_perm, wm_ref[:, W_BD:W_BD + LANES])
    yield
    z = raw + dtb_ref[...]
    softplus = jnp.maximum(z, 0.0) + jnp.log1p(jnp.exp(-jnp.abs(z)))
    g = -jnp.exp(alog_ref[...]) * softplus
    bd_ref[...] = jnp.where(lane < GDN_HEADS, _sigmoid(raw), jnp.where(lane < 2 * GDN_HEADS, g, 0.0))

    halo = GDN_CONV - 1
    chunks = tm // GDN_CHUNK
    last_sublane = lax.broadcasted_iota(jnp.int32, (SUBLANES, GDN_HEAD_DIM), 0) == SUBLANES - 1
    heads_per_block = MIX_BLOCK // GDN_HEAD_DIM
    for part in range(3):
        for blk in range(GDN_WIDTH // MIX_BLOCK):
            base = part * GDN_WIDTH + blk * MIX_BLOCK
            y = _dot(h_perm, wm_ref[:, W_GDN + base:W_GDN + base + MIX_BLOCK])
            yield
            for hb in range(heads_per_block):
                hh = blk * heads_per_block + hb
                col = part * GDN_WIDTH + hh * GDN_HEAD_DIM
                cur = y[:, hb * GDN_HEAD_DIM:(hb + 1) * GDN_HEAD_DIM]
                vreg = lambda c, v: cur[c * GDN_CHUNK + v * SUBLANES:c * GDN_CHUNK + (v + 1) * SUBLANES]
                prev_tile = carry_ref[:, col:col + GDN_HEAD_DIM]
                carry_ref[:, col:col + GDN_HEAD_DIM] = cur[tm - halo * SUBLANES:, :]
                wrapped = []
                for c in range(chunks):
                    row = []
                    for i in range(halo):
                        before = (prev_tile[i * SUBLANES:(i + 1) * SUBLANES] if c == 0
                                  else vreg(c - 1, vregs - halo + i))
                        own = vreg(c, vregs - halo + i)
                        row.append(pltpu.roll(jnp.where(last_sublane, before, own), 1, 0))
                    wrapped.append(row)
                acc = cur * convw_ref[halo:halo + 1, col:col + GDN_HEAD_DIM]
                for shift in range(1, GDN_CONV):
                    pieces = []
                    for c in range(chunks):
                        pieces += wrapped[c][halo - shift:]
                        pieces.append(cur[c * GDN_CHUNK:(c + 1) * GDN_CHUNK - shift * SUBLANES])
                    shifted = jnp.concatenate(pieces, axis=0)
                    acc = acc + shifted * convw_ref[halo - shift:halo - shift + 1, col:col + GDN_HEAD_DIM]
                act = acc * _sigmoid(acc)
                if part < 2:
                    act = act * lax.rsqrt(jnp.sum(act * act, axis=-1, keepdims=True) + EPS)
                if part == 0:
                    act = act * (GDN_HEAD_DIM ** -0.5)
                gdn_refs[part][:, hh * GDN_HEAD_DIM:(hh + 1) * GDN_HEAD_DIM] = act
                yield


MIXER_IN_STAGED = 3 * (len(ATT_GROUPS) - 1) * (ATT_GROUP_WIDTH // LANES)


def _mixer_in_kernel(x_ref, g_ref, wm_ref, convw_ref, alog_ref, dtb_ref, cos_ref, sin_ref,
                     a0_ref, a1_ref, a2_ref, qb_ref, kb_ref, vb_ref, bd_ref, carry_ref, perm_ref, stage_ref, *, tm):
    @pl.when(pl.program_id(1) == 0)
    def _():
        carry_ref[...] = jnp.zeros(carry_ref.shape, F32)

    for _ in _mixer_in_stages(x_ref, g_ref, wm_ref, convw_ref, alog_ref, dtb_ref, cos_ref, sin_ref,
                              (a0_ref, a1_ref, a2_ref), (qb_ref, kb_ref, vb_ref), bd_ref, carry_ref, perm_ref,
                              stage_ref, tm=tm):
        pass


def _mixer_in(x1, norm_g, w_all, conv_w, a_log, dt_bias, cos_t, sin_t, *, tm, seq):
    n = x1.shape[0]
    tiles_per_seq = seq // tm
    tile = lambda rows, w: pl.BlockSpec((rows, w), lambda bi, i: (bi * tiles_per_seq + i, 0))
    table = pl.BlockSpec((tm, LANES), lambda bi, i: (i, 0))
    wq = 3 * ATT_GROUP_WIDTH
    att_specs = [tile(tm // dil, dil * wq) for _, dil in ATT_GROUPS]
    att_shapes = [jax.ShapeDtypeStruct((n // dil, dil * wq), BF16) for _, dil in ATT_GROUPS]
    gdn = jax.ShapeDtypeStruct((n, GDN_WIDTH), F32)
    return pl.pallas_call(
        functools.partial(_mixer_in_kernel, tm=tm),
        grid=(n // seq, tiles_per_seq),
        in_specs=[tile(tm, D_MODEL), _resident((1, D_MODEL)),
                  pl.BlockSpec((D_MODEL, W_GATES), lambda bi, i: (0, 0), pipeline_mode=pl.Buffered(1)),
                  _resident((GDN_CONV, 3 * GDN_WIDTH)), _resident((1, LANES)), _resident((1, LANES)), table, table],
        out_specs=att_specs + [tile(tm, GDN_WIDTH)] * 3 + [tile(tm, LANES)],
        out_shape=att_shapes + [gdn] * 3 + [jax.ShapeDtypeStruct((n, LANES), F32)],
        scratch_shapes=[pltpu.VMEM(((GDN_CONV - 1) * SUBLANES, 3 * GDN_WIDTH), F32),
                        pltpu.VMEM((D_MODEL // LANES, tm, LANES), F32), pltpu.VMEM((MIXER_IN_STAGED, tm, LANES), F32)],
        compiler_params=pltpu.CompilerParams(dimension_semantics=("arbitrary", "arbitrary"),
                                             vmem_limit_bytes=VMEM_LIMIT_BYTES),
        name="mixer_in",
    )(x1, norm_g, w_all, conv_w, a_log, dt_bias, cos_t, sin_t)


ATT_BATCH = 3


def _attention_blocks(items):
    lane = lax.broadcasted_iota(jnp.int32, (1, ATT_GROUP_WIDTH), 1)
    heads = range(ATT_HEADS_PER_GROUP)
    in_head = [(lane // ATT_HEAD_DIM) == hh for hh in heads]
    keep = [jnp.where(in_head[hh], 1.0, 0.0).astype(BF16) for hh in heads]
    nq = ATT_BLOCK
    s_all = [_dot_nt(jnp.concatenate([q * keep[hh] for hh in heads], axis=0), k) for q, k, _, _ in items]
    stats, p_all = [], []
    for (_, _, _, valid), sa in zip(items, s_all):
        s = [jnp.where(valid, sa[hh * nq:(hh + 1) * nq], NEG_BIG) for hh in heads]
        m = [jnp.max(s[hh], axis=-1, keepdims=True) for hh in heads]
        p = [jnp.exp(s[hh] - m[hh]) for hh in heads]
        l = [jnp.sum(p[hh], axis=-1, keepdims=True) for hh in heads]
        stats.append((m, l))
        p_all.append(jnp.concatenate([p[hh].astype(BF16) for hh in heads], axis=0))
    pv_all = [_dot(ps, v) for ps, (_, _, v, _) in zip(p_all, items)]
    outs = []
    for pv, (m, l) in zip(pv_all, stats):
        o = jnp.zeros((ATT_BLOCK, ATT_GROUP_WIDTH), F32)
        lse = jnp.zeros((ATT_BLOCK, ATT_GROUP_WIDTH), F32)
        for hh in heads:
            o = jnp.where(in_head[hh], pv[hh * nq:(hh + 1) * nq] * (1.0 / l[hh]), o)
            lse = jnp.where(in_head[hh], m[hh] + jnp.log(l[hh]), lse)
        outs.append((o, lse))
    return outs


def _attention_kernel(a0_ref, a1_ref, a2_ref, ya_ref, o0, l0, o1, l1, o2, l2, *, seq):
    in_refs = (a0_ref, a1_ref, a2_ref)
    o_refs = (o0, o1, o2)
    l_refs = (l0, l1, l2)
    qi = lax.broadcasted_iota(jnp.int32, (ATT_BLOCK, ATT_BLOCK), 0)
    kj = lax.broadcasted_iota(jnp.int32, (ATT_BLOCK, ATT_BLOCK), 1)
    causal = kj <= qi
    qi2 = lax.broadcasted_iota(jnp.int32, (ATT_BLOCK, 2 * ATT_BLOCK), 0)
    kj2 = lax.broadcasted_iota(jnp.int32, (ATT_BLOCK, 2 * ATT_BLOCK), 1)
    band = (kj2 >= qi2) & (kj2 - ATT_BLOCK <= qi2)
    wq = 3 * ATT_GROUP_WIDTH

    def load(gi, r, n):
        src, base = in_refs[gi], r * wq
        if isinstance(n, int) and n == 0:
            qrows = krows = slice(0, ATT_BLOCK)
            valid = causal
        else:
            start = lambda x: x if isinstance(x, int) else pl.multiple_of(x, ATT_BLOCK)
            qrows = pl.ds(start(n * ATT_BLOCK), ATT_BLOCK)
            krows = pl.ds(start((n - 1) * ATT_BLOCK), 2 * ATT_BLOCK)
            valid = band
        return (src[0, qrows, base:base + ATT_GROUP_WIDTH],
                src[0, krows, base + ATT_GROUP_WIDTH:base + 2 * ATT_GROUP_WIDTH],
                src[0, krows, base + 2 * ATT_GROUP_WIDTH:base + 3 * ATT_GROUP_WIDTH], valid)

    def store(gi, r, n, o, lse):
        dil = ATT_GROUPS[gi][1]
        if dil == 1:
            first = n * ATT_BLOCK
            rows = pl.ds(first if isinstance(first, int) else pl.multiple_of(first, ATT_BLOCK), ATT_BLOCK)
        else:
            rows = pl.ds(n * ATT_BLOCK * dil + r, ATT_BLOCK, stride=dil)
        for half in range(ATT_GROUP_WIDTH // LANES):
            o_refs[gi][half, rows, :] = o[:, half * LANES:(half + 1) * LANES]
            l_refs[gi][half, rows, :] = lse[:, half * LANES:(half + 1) * LANES]

    def run(blocks):
        for (gi, r, n), (o, lse) in zip(blocks, _attention_blocks([load(*blk) for blk in blocks])):
            store(gi, r, n, o, lse)

    static_blocks = []
    looped = None
    for gi, (window, dil) in enumerate(ATT_GROUPS):
        assert window // dil == ATT_BLOCK
        nblk = seq // dil // ATT_BLOCK
        if dil == 1 and (nblk - 1) % ATT_BATCH == 0:
            static_blocks.append((gi, 0, 0))
            looped = (gi, nblk)
        else:
            static_blocks += [(gi, r, n) for r in range(dil) for n in range(nblk)]
    for i in range(0, len(static_blocks), ATT_BATCH + 1):
        run(static_blocks[i:i + ATT_BATCH + 1])
    if looped is not None:
        gi, nblk = looped

        def body(i, carry):
            run([(gi, 0, 1 + i * ATT_BATCH + j) for j in range(ATT_BATCH)])
            return carry
        lax.fori_loop(0, (nblk - 1) // ATT_BATCH, body, 0)

    rows_per_step = 256

    def merge(i, carry):
        rows = pl.ds(pl.multiple_of(i * rows_per_step, rows_per_step), rows_per_step)
        for half in range(ATT_GROUP_WIDTH // LANES):
            la, lb, lc = l0[half, rows, :], l1[half, rows, :], l2[half, rows, :]
            m = jnp.maximum(jnp.maximum(la, lb), lc)
            ea, eb, ec = jnp.exp(la - m), jnp.exp(lb - m), jnp.exp(lc - m)
            num = ea * o0[half, rows, :] + eb * o1[half, rows, :] + ec * o2[half, rows, :]
            ya_ref[0, rows, half * LANES:(half + 1) * LANES] = num / (ea + eb + ec)
        return carry
    lax.fori_loop(0, seq // rows_per_step, merge, 0)


def _attention(a0, a1, a2, *, batch):
    views = tuple(a.reshape(batch, a.shape[0] // batch, a.shape[1]) for a in (a0, a1, a2))
    b, s, _ = views[0].shape
    specs = [pl.BlockSpec((1,) + arr.shape[1:], lambda bi: (bi, 0, 0)) for arr in views]
    scratch = [pltpu.VMEM((ATT_GROUP_WIDTH // LANES, s, LANES), F32) for _ in range(6)]
    return pl.pallas_call(
        functools.partial(_attention_kernel, seq=s),
        grid=(b,),
        in_specs=specs,
        out_specs=pl.BlockSpec((1, s, ATT_GROUP_WIDTH), lambda bi: (bi, 0, 0)),
        out_shape=jax.ShapeDtypeStruct((b, s, ATT_GROUP_WIDTH), F32),
        scratch_shapes=scratch,
        compiler_params=pltpu.CompilerParams(dimension_semantics=("arbitrary",),
                                             vmem_limit_bytes=VMEM_LIMIT_BYTES),
        name="dilated_attention",
    )(*views)


def _deltanet_stages(q_ref, k_ref, v_ref, bd_ref, gnorm, state_ref, ob_ref, slot, *, tile):
    c = GDN_CHUNK
    d = GDN_HEAD_DIM
    heads = range(GDN_HEADS)
    pairs = [(2 * pp, 2 * pp + 1) for pp in range(GDN_HEADS // 2)]
    cols = [slice(hh * d, (hh + 1) * d) for hh in heads]
    glane = [GDN_HEADS + hh for hh in heads]
    ii = _chunk_time(lax.broadcasted_iota(jnp.int32, (c, 2 * c), 0))
    ll = lax.broadcasted_iota(jnp.int32, (c, 2 * c), 1)
    jj = _chunk_time(ll % c)
    lower = ii >= jj
    strict = ii > jj
    left = ll < c
    left_row = lax.broadcasted_iota(jnp.int32, (1, 2 * c), 1) < c
    keep_left = jnp.where(left, 1.0, 0.0).astype(BF16)
    keep_right = jnp.where(left, 0.0, 1.0).astype(BF16)
    ti = _chunk_time(lax.broadcasted_iota(jnp.int32, (c, c), 0))
    tj = _chunk_time(lax.broadcasted_iota(jnp.int32, (c, c), 1))
    tri_ones = jnp.where(ti >= tj, 1.0, 0.0).astype(BF16)

    def blockdiag(x):
        return jnp.concatenate([x * keep_left, x * keep_right], axis=0)

    def stack_diag(xa, xb):
        zero = jnp.zeros_like(xa)
        return jnp.concatenate([jnp.concatenate([xa, zero], axis=1), jnp.concatenate([zero, xb], axis=1)], axis=0)

    def prepare(ci, out):
        rows = slice(ci * c, (ci + 1) * c)
        bd = bd_ref[rows, :]
        bd_hi = bd.astype(BF16)
        bd_rest = bd - bd_hi.astype(F32)
        bd_mid = bd_rest.astype(BF16)
        bd_lo = (bd_rest - bd_mid.astype(F32)).astype(BF16)
        gcum = _dot(tri_ones, bd_hi) + _dot(tri_ones, bd_mid) + _dot(tri_ones, bd_lo)
        yield
        gtot = jnp.broadcast_to(gcum[c - 1:c, :], (c, LANES))
        gcum_t = jnp.concatenate([gcum, gcum], axis=0).T
        e_cum_all = jnp.exp(gcum)
        e_rest_all = jnp.exp(gtot - gcum)
        e_tot_all = jnp.exp(gtot)
        q = [q_ref[rows, cols[hh]] for hh in heads]
        k = [k_ref[rows, cols[hh]] for hh in heads]
        v = [v_ref[rows, cols[hh]] for hh in heads]
        beta = [bd[:, hh:hh + 1] for hh in heads]
        e_cum = [e_cum_all[:, gl:gl + 1] for gl in glane]
        kbeta = [k[hh] * beta[hh] for hh in heads]
        kq = [_dot_nt(jnp.concatenate([jnp.concatenate([kbeta[a], kbeta[b]], axis=1),
                                       jnp.concatenate([q[a], q[b]], axis=1)], axis=0).astype(BF16),
                      stack_diag(k[a].astype(BF16), k[b].astype(BF16)))
              for a, b in pairs]
        yield
        decay = [jnp.exp(jnp.where(lower,
                                   jnp.where(left, gcum[:, glane[a]:glane[a] + 1], gcum[:, glane[b]:glane[b] + 1])
                                   - jnp.where(left_row, gcum_t[glane[a]:glane[a] + 1, :], gcum_t[glane[b]:glane[b] + 1, :]),
                                   NEG_BIG)) for a, b in pairs]
        m = [jnp.where(strict, kq[pp][0:c] * decay[pp], 0.0) for pp in range(len(pairs))]
        n = [-mm for mm in m]
        pb = [mm.astype(BF16) for mm in m]
        p = [_dot(x, blockdiag(x)) for x in pb]
        yield
        rounds = 5
        for r in range(rounds):
            pb = [x.astype(BF16) for x in p]
            upd = [_dot(x, blockdiag(y.astype(BF16))) for x, y in zip(pb, n)]
            p_next = [_dot(x, blockdiag(x)) for x in pb] if r + 1 < rounds else None
            yield
            n = [y + x + u for y, x, u in zip(n, p, upd)]
            p = p_next
        rhs = [jnp.concatenate([v[hh] * beta[hh], kbeta[hh] * e_cum[hh]], axis=1) for hh in heads]
        nr = [_dot(n[pp].astype(BF16), stack_diag(rhs[a].astype(BF16), rhs[b].astype(BF16)))
              for pp, (a, b) in enumerate(pairs)]
        yield
        sol = [rhs[hh] + nr[hh // 2][:, (hh % 2) * 2 * d:(hh % 2 + 1) * 2 * d] for hh in heads]
        out.update(
            first=ci * c,
            u=[sol[hh][:, 0:d] for hh in heads],
            wq=[jnp.concatenate([sol[hh][:, d:2 * d], q[hh] * e_cum[hh]], axis=0).astype(BF16) for hh in heads],
            a_qk=[(kq[pp][c:2 * c] * decay[pp]).astype(BF16) for pp in range(len(pairs))],
            k_dec=[(k[hh] * e_rest_all[:, gl:gl + 1]).astype(BF16) for hh, gl in zip(heads, glane)],
            e_tot=[e_tot_all[0:1, gl:gl + 1] for gl in glane])

    for first in range(0, tile // c, GDN_GROUP):
        group = [dict() for _ in range(GDN_GROUP)]
        gens = [prepare(first + gi, group[gi]) for gi in range(GDN_GROUP)]
        for _ in range(GDN_PREP_LAYERS):
            for gen in gens:
                next(gen)
            yield
        for gen in gens:
            for _ in gen:
                pass
        for pre in group:
            state = [state_ref[hh] for hh in heads]
            ws = [_dot(pre["wq"][hh], state[hh].astype(BF16)) for hh in heads]
            yield
            v_new = [(pre["u"][hh] - ws[hh][0:c]).astype(BF16) for hh in heads]
            kv = [_dot_tn(pre["k_dec"][hh], v_new[hh]) for hh in heads]
            av = [_dot(pre["a_qk"][pp], stack_diag(v_new[a], v_new[b])) for pp, (a, b) in enumerate(pairs)]
            yield
            for hh in heads:
                state_ref[hh] = state[hh] * pre["e_tot"][hh] + kv[hh]
            for hh in heads:
                o = _rmsnorm(ws[hh][c:2 * c] + av[hh // 2][:, (hh % 2) * d:(hh % 2 + 1) * d], gnorm)
                for vv in range(c // SUBLANES):
                    ob_ref[slot, hh, pl.ds(pre["first"] + vv, SUBLANES, stride=c // SUBLANES), :] = (
                        o[vv * SUBLANES:(vv + 1) * SUBLANES])


GDN_GROUP = 4
GDN_PREP_LAYERS = 9
GDN_LAYERS_PER_GROUP = GDN_PREP_LAYERS + 2 * GDN_GROUP


def _mixer_out_stages(x_ref, ya_ref, ob_ref, slot, g_ref, wgt_ref, wa_ref, wb_ref, wo_ref, o_ref):
    x = x_ref[...]
    h = _rmsnorm(x, g_ref[...]).astype(BF16)
    ya = ya_ref[...].astype(BF16)
    blocks = [slice(j * MIX_BLOCK, (j + 1) * MIX_BLOCK) for j in range(D_MODEL // MIX_BLOCK)]
    gate_cols = lambda which, blk: slice(which * D_MODEL + blk.start, which * D_MODEL + blk.stop)
    yb = []
    for blk in blocks:
        gdn_gate = _dot(h, wgt_ref[:, gate_cols(0, blk)])
        yield
        ob = jnp.concatenate([ob_ref[slot, hh] for hh in range(blk.start // GDN_HEAD_DIM, blk.stop // GDN_HEAD_DIM)],
                             axis=1)
        yb.append((ob * (gdn_gate * _sigmoid(gdn_gate))).astype(BF16))
    yb = jnp.concatenate(yb, axis=1)
    merged = []
    for blk in blocks:
        gate_a = _dot(h, wgt_ref[:, gate_cols(1, blk)])
        branch_a = _dot(ya, wa_ref[:, blk])
        yield
        gate_b = _dot(h, wgt_ref[:, gate_cols(2, blk)])
        yield
        branch_b = _dot(yb, wb_ref[:, blk])
        yield
        merged.append((_sigmoid(gate_a) * branch_a + _sigmoid(gate_b) * branch_b).astype(BF16))
    merged = jnp.concatenate(merged, axis=1)
    for blk in blocks:
        o_ref[:, blk] = x[:, blk] + _dot(merged, wo_ref[:, blk])
        yield


MIX_GRANULES = 5 * (D_MODEL // MIX_BLOCK)


def _mixer_tail_kernel(q_ref, k_ref, v_ref, bd_ref, gn_ref, x_ref, ya_ref, g_ref, wgt_ref, wa_ref, wb_ref, wo_ref,
                       o_ref, state_ref, ob_ref, *, tile, tiles_per_seq, n_tiles):
    step = pl.program_id(0)

    @pl.when(step == 0)
    def _():
        ob_ref[...] = jnp.zeros(ob_ref.shape, F32)

    @pl.when(jnp.minimum(step, n_tiles - 1) % tiles_per_seq == 0)
    def _():
        state_ref[...] = jnp.zeros(state_ref.shape, F32)

    slot = step % 2
    gdn = _deltanet_stages(q_ref, k_ref, v_ref, bd_ref, gn_ref[...], state_ref, ob_ref, slot, tile=tile)
    mix = _mixer_out_stages(x_ref, ya_ref, ob_ref, 1 - slot, g_ref, wgt_ref, wa_ref, wb_ref, wo_ref, o_ref)
    _interleave(gdn, GDN_LAYERS_PER_GROUP * (tile // (GDN_CHUNK * GDN_GROUP)), mix, MIX_GRANULES)


def _mixer_tail(qb, kb, vb, bd, out_norm, x1, ya, norm_g, w_all, w_a, w_b, w_o, *, tile, seq):
    n = x1.shape[0]
    n_tiles = n // tile
    cur = lambda w: pl.BlockSpec((tile, w), lambda s: (jnp.minimum(s, n_tiles - 1), 0))
    prev = lambda w: pl.BlockSpec((tile, w), lambda s: (jnp.maximum(s - 1, 0), 0))
    return pl.pallas_call(
        functools.partial(_mixer_tail_kernel, tile=tile, tiles_per_seq=seq // tile, n_tiles=n_tiles),
        grid=(n_tiles + 1,),
        in_specs=[cur(GDN_WIDTH), cur(GDN_WIDTH), cur(GDN_WIDTH), cur(LANES), _resident((1, GDN_HEAD_DIM)),
                  prev(D_MODEL), prev(ATT_GROUP_WIDTH), _resident((1, D_MODEL)),
                  pl.BlockSpec((D_MODEL, W_GDN), lambda s: (0, W_GATES // W_GDN), pipeline_mode=pl.Buffered(1)),
                  _resident(w_a.shape), _resident(w_b.shape), _resident(w_o.shape)],
        out_specs=prev(D_MODEL),
        out_shape=jax.ShapeDtypeStruct((n, D_MODEL), F32),
        scratch_shapes=[pltpu.VMEM((GDN_HEADS, GDN_HEAD_DIM, GDN_HEAD_DIM), F32),
                        pltpu.VMEM((2, GDN_HEADS, tile, GDN_HEAD_DIM), F32)],
        compiler_params=pltpu.CompilerParams(dimension_semantics=("arbitrary",),
                                             vmem_limit_bytes=VMEM_LIMIT_BYTES),
        name="deltanet_mixer_out",
    )(qb, kb, vb, bd, out_norm, x1, ya, norm_g, w_all, w_a, w_b, w_o)


def _rope_tables(seq):
    half = ATT_HEAD_DIM // 2
    inv_freq = ROPE_THETA ** (-jnp.arange(half, dtype=F32) / half)
    ang = jnp.arange(seq, dtype=F32)[:, None] * inv_freq[None, :]
    cos, sin = jnp.cos(ang), jnp.sin(ang)
    reps = LANES // ATT_HEAD_DIM
    return jnp.tile(jnp.concatenate([cos, cos], axis=-1), (1, reps)), jnp.tile(jnp.concatenate([-sin, sin], axis=-1), (1, reps))


def _pad_lanes(row, offset):
    return jnp.zeros((1, LANES), F32).at[0, offset:offset + row.shape[0]].set(row.astype(F32))


def _layer(x, ffn1_norm, ffn1_w_gate, ffn1_w_up, ffn1_w_down, mix_norm, w_in, gdn_conv_w, gdn_a_log, gdn_dt_bias,
           gdn_out_norm, w_branch_a, w_branch_b, w_out, ffn2_norm, ffn2_w_gate, ffn2_w_up, ffn2_w_down, fin_g,
           *, final_norm, tm_ffn, tm_mix, gdn_tile):
    b, s, _ = x.shape
    n = b * s
    row = lambda v: v.reshape(1, -1).astype(F32)
    w_all = jnp.concatenate(
        [w_in[:, :W_IN_GDN], w_in[:, W_IN_BD:W_IN_GATES], jnp.zeros((D_MODEL, W_GDN - W_BD - 2 * GDN_HEADS), w_in.dtype),
         w_in[:, W_IN_GDN:W_IN_BD], w_in[:, W_IN_GATES:]], axis=1).astype(BF16)
    cos_t, sin_t = _rope_tables(s)
    x1 = _ffn(x.reshape(n, D_MODEL), row(ffn1_norm), ffn1_w_gate.astype(BF16), ffn1_w_up.astype(BF16),
              ffn1_w_down.astype(BF16), fin_g, final_norm=False, tm=tm_ffn)
    a0, a1, a2, qb, kb, vb, bd = _mixer_in(
        x1, row(mix_norm), w_all, gdn_conv_w.astype(F32), _pad_lanes(gdn_a_log, GDN_HEADS),
        _pad_lanes(gdn_dt_bias, GDN_HEADS), cos_t, sin_t, tm=tm_mix, seq=s)

    ya = _attention(a0, a1, a2, batch=b)
    x2 = _mixer_tail(qb, kb, vb, bd, row(gdn_out_norm), x1, ya.reshape(n, ATT_GROUP_WIDTH), row(mix_norm), w_all,
                     w_branch_a.astype(BF16), w_branch_b.astype(BF16), w_out.astype(BF16), tile=gdn_tile, seq=s)
    x3 = _ffn(x2, row(ffn2_norm), ffn2_w_gate.astype(BF16), ffn2_w_up.astype(BF16), ffn2_w_down.astype(BF16),
              fin_g, final_norm=final_norm, tm=tm_ffn)
    return x3.reshape(b, s, D_MODEL)


def kernel(x, ffn1_norm, ffn1_w_gate, ffn1_w_up, ffn1_w_down, mix_norm, w_in, gdn_conv_w, gdn_a_log, gdn_dt_bias,
           gdn_out_norm, w_branch_a, w_branch_b, w_out, ffn2_norm, ffn2_w_gate, ffn2_w_up, ffn2_w_down, final_norm):
    depth = ffn1_norm.shape[0]
    fin_g = final_norm.reshape(1, -1).astype(F32)
    for layer in range(depth):
        x = _layer(x, ffn1_norm[layer], ffn1_w_gate[layer], ffn1_w_up[layer], ffn1_w_down[layer], mix_norm[layer],
                   w_in[layer], gdn_conv_w[layer], gdn_a_log[layer], gdn_dt_bias[layer], gdn_out_norm[layer],
                   w_branch_a[layer], w_branch_b[layer], w_out[layer], ffn2_norm[layer], ffn2_w_gate[layer],
                   ffn2_w_up[layer], ffn2_w_down[layer], fin_g, final_norm=(layer == depth - 1),
                   tm_ffn=512, tm_mix=512, gdn_tile=512)
    return x
```

```python
import functools

import jax
import jax.numpy as jnp
from jax import lax
from jax.experimental import pallas as pl
from jax.experimental.pallas import tpu as pltpu

F32 = jnp.float32
BF16 = jnp.bfloat16

D_MODEL = 1024
D_FF = 2816
EPS = 1e-6

ATT_GROUPS = ((128, 1), (512, 4), (2048, 16))
ATT_HEADS_PER_GROUP = 4
ATT_HEAD_DIM = 64
ATT_BLOCK = 128
ATT_GROUP_WIDTH = ATT_HEADS_PER_GROUP * ATT_HEAD_DIM
ATT_QKV_WIDTH = len(ATT_GROUPS) * ATT_GROUP_WIDTH
ROPE_THETA = 10000.0

GDN_HEADS = 8
GDN_HEAD_DIM = 128
GDN_WIDTH = GDN_HEADS * GDN_HEAD_DIM
GDN_CONV = 4
GDN_CHUNK = 64

LANES = 128
SUBLANES = 8
VMEM_LIMIT_BYTES = 56 * 1024 * 1024

W_IN_GDN = 3 * ATT_QKV_WIDTH
W_IN_BD = W_IN_GDN + 3 * GDN_WIDTH
W_IN_GATES = W_IN_BD + 2 * GDN_HEADS
FFN_CHUNKS = ((0, 768), (768, 1536), (1536, 2304), (2304, 2816))
NEG_BIG = -1e30


def _resident(shape):
    nd = len(shape)
    return pl.BlockSpec(shape, lambda *_: (0,) * nd, pipeline_mode=pl.Buffered(1))


def _rmsnorm(x, g):
    return x * lax.rsqrt(jnp.mean(x * x, axis=-1, keepdims=True) + EPS) * g


def _sigmoid(x):
    return 1.0 / (1.0 + jnp.exp(-x))


def _dot(a, b):
    return jnp.dot(a, b, preferred_element_type=F32)


def _dot_nt(a, b):
    return lax.dot_general(a, b, (((1,), (1,)), ((), ())), preferred_element_type=F32)


def _dot_tn(a, b):
    return lax.dot_general(a, b, (((0,), (0,)), ((), ())), preferred_element_type=F32)


def _swiglu_residual(x, g, wg_ref, wu_ref, wd_ref):
    h = _rmsnorm(x, g).astype(BF16)
    acc = x
    for lo, hi in FFN_CHUNKS:
        gate = _dot(h, wg_ref[:, lo:hi])
        up = _dot(h, wu_ref[:, lo:hi])
        act = (0.5 * gate * _sigmoid(gate) * up).astype(BF16)
        acc = acc + _dot(act, wd_ref[lo:hi, :])
    return acc


def _ffn_kernel(x_ref, g_ref, wg_ref, wu_ref, wd_ref, fin_ref, o_ref, *, final_norm):
    y = _swiglu_residual(x_ref[...], g_ref[...], wg_ref, wu_ref, wd_ref)
    if final_norm:
        y = _rmsnorm(y, fin_ref[...])
    o_ref[...] = y


def _ffn(x, norm_g, wg, wu, wd, fin_g, *, final_norm, tm):
    n = x.shape[0]
    row = pl.BlockSpec((tm, D_MODEL), lambda i: (i, 0))
    return pl.pallas_call(
        functools.partial(_ffn_kernel, final_norm=final_norm),
        grid=(n // tm,),
        in_specs=[row, _resident((1, D_MODEL)), _resident((D_MODEL, D_FF)), _resident((D_MODEL, D_FF)),
                  _resident((D_FF, D_MODEL)), _resident((1, D_MODEL))],
        out_specs=row,
        out_shape=jax.ShapeDtypeStruct((n, D_MODEL), F32),
        compiler_params=pltpu.CompilerParams(dimension_semantics=("arbitrary",),
                                             vmem_limit_bytes=VMEM_LIMIT_BYTES),
        name="ffn_final" if final_norm else "ffn",
    )(x, norm_g, wg, wu, wd, fin_g)


MIX_BLOCK = 256


def _interleave(primary, n_primary, secondary, n_secondary):
    done = 0
    for i in range(n_primary):
        next(primary)
        while done * n_primary < (i + 1) * n_secondary:
            next(secondary)
            done += 1
    for gen in (primary, secondary):
        for _ in gen:
            pass


def _chunk_time(row):
    return SUBLANES * (row % SUBLANES) + row // SUBLANES


def _attention_qkv_stages(h, wm_ref, cos_ref, sin_ref, att_refs, stage_ref, *, tm):
    cos = cos_ref[...]
    sin = sin_ref[...]
    lane = lax.broadcasted_iota(jnp.int32, (1, LANES), 1)
    first_half = (lane % ATT_HEAD_DIM) < (ATT_HEAD_DIM // 2)
    slot = 0
    for part in range(3):
        for gi, (_, dil) in enumerate(ATT_GROUPS):
            col = part * ATT_QKV_WIDTH + gi * ATT_GROUP_WIDTH
            y = _dot(h, wm_ref[:, col:col + ATT_GROUP_WIDTH])
            yield
            for j in range(ATT_GROUP_WIDTH // LANES):
                blk = y[:, j * LANES:(j + 1) * LANES]
                if part < 2:
                    swapped = jnp.where(first_half, pltpu.roll(blk, LANES - ATT_HEAD_DIM // 2, 1),
                                        pltpu.roll(blk, ATT_HEAD_DIM // 2, 1))
                    blk = blk * cos + swapped * sin
                if part == 0:
                    blk = blk * (ATT_HEAD_DIM ** -0.5)
                dst = part * ATT_GROUP_WIDTH + j * LANES
                if dil == 1:
                    att_refs[gi][:, dst:dst + LANES] = blk.astype(BF16)
                else:
                    stage_ref[slot] = blk
                    for r in range(dil):
                        rows = stage_ref[slot, pl.ds(r, tm // dil, stride=dil), :]
                        lo = r * 3 * ATT_GROUP_WIDTH + dst
                        att_refs[gi][:, lo:lo + LANES] = rows.astype(BF16)
                    slot += 1
                yield


ATT_QKV_STAGES = 3 * len(ATT_GROUPS) * (1 + ATT_GROUP_WIDTH // LANES)
MIXER_IN_STAGED = 3 * (len(ATT_GROUPS) - 1) * (ATT_GROUP_WIDTH // LANES)


def _deltanet_qkv_stages(h_perm, wm_ref, wbd_ref, convw_ref, alog_ref, dtb_ref, gdn_refs, bd_ref, carry_ref, *, tm):
    lane = lax.broadcasted_iota(jnp.int32, (1, LANES), 1)
    raw = _dot(h_perm, wbd_ref[...])
    yield
    z = raw + dtb_ref[...]
    softplus = jnp.maximum(z, 0.0) + jnp.log1p(jnp.exp(-jnp.abs(z)))
    g = -jnp.exp(alog_ref[...]) * softplus
    bd_ref[...] = jnp.where(lane < GDN_HEADS, _sigmoid(raw), jnp.where(lane < 2 * GDN_HEADS, g, 0.0))

    vregs = GDN_CHUNK // SUBLANES
    halo = GDN_CONV - 1
    chunks = tm // GDN_CHUNK
    last_sublane = lax.broadcasted_iota(jnp.int32, (SUBLANES, GDN_HEAD_DIM), 0) == SUBLANES - 1
    heads_per_block = MIX_BLOCK // GDN_HEAD_DIM
    for part in range(3):
        for blk in range(GDN_WIDTH // MIX_BLOCK):
            base = part * GDN_WIDTH + blk * MIX_BLOCK
            y = _dot(h_perm, wm_ref[:, W_IN_GDN + base:W_IN_GDN + base + MIX_BLOCK])
            yield
            for hb in range(heads_per_block):
                hh = blk * heads_per_block + hb
                col = part * GDN_WIDTH + hh * GDN_HEAD_DIM
                cur = y[:, hb * GDN_HEAD_DIM:(hb + 1) * GDN_HEAD_DIM]
                vreg = lambda c, v: cur[c * GDN_CHUNK + v * SUBLANES:c * GDN_CHUNK + (v + 1) * SUBLANES]
                prev_tile = carry_ref[:, col:col + GDN_HEAD_DIM]
                carry_ref[:, col:col + GDN_HEAD_DIM] = cur[tm - halo * SUBLANES:, :]
                wrapped = []
                for c in range(chunks):
                    row = []
                    for i in range(halo):
                        before = (prev_tile[i * SUBLANES:(i + 1) * SUBLANES] if c == 0
                                  else vreg(c - 1, vregs - halo + i))
                        own = vreg(c, vregs - halo + i)
                        row.append(pltpu.roll(jnp.where(last_sublane, before, own), 1, 0))
                    wrapped.append(row)
                acc = cur * convw_ref[halo:halo + 1, col:col + GDN_HEAD_DIM]
                for shift in range(1, GDN_CONV):
                    pieces = []
                    for c in range(chunks):
                        pieces += wrapped[c][halo - shift:]
                        pieces.append(cur[c * GDN_CHUNK:(c + 1) * GDN_CHUNK - shift * SUBLANES])
                    shifted = jnp.concatenate(pieces, axis=0)
                    acc = acc + shifted * convw_ref[halo - shift:halo - shift + 1, col:col + GDN_HEAD_DIM]
                act = acc * _sigmoid(acc)
                if part < 2:
                    act = act * lax.rsqrt(jnp.sum(act * act, axis=-1, keepdims=True) + EPS)
                if part == 0:
                    act = act * (GDN_HEAD_DIM ** -0.5)
                gdn_refs[part][:, hh * GDN_HEAD_DIM:(hh + 1) * GDN_HEAD_DIM] = act
                yield


DELTANET_QKV_STAGES = 1 + 3 * (GDN_WIDTH // MIX_BLOCK) + 3 * GDN_HEADS


def _mixer_in_kernel(x_ref, g_ref, wm_ref, wbd_ref, convw_ref, alog_ref, dtb_ref, cos_ref, sin_ref,
                     a0_ref, a1_ref, a2_ref, qb_ref, kb_ref, vb_ref, bd_ref, carry_ref, perm_ref, stage_ref, *, tm):
    @pl.when(pl.program_id(1) == 0)
    def _():
        carry_ref[...] = jnp.zeros(carry_ref.shape, F32)

    hf = _rmsnorm(x_ref[...], g_ref[...])
    h = hf.astype(BF16)
    vregs = GDN_CHUNK // SUBLANES
    for cb in range(D_MODEL // LANES):
        perm_ref[cb] = hf[:, cb * LANES:(cb + 1) * LANES]
    h_perm = jnp.concatenate(
        [jnp.concatenate([perm_ref[cb, pl.ds(c0 + v, SUBLANES, stride=vregs), :]
                          for c0 in range(0, tm, GDN_CHUNK) for v in range(vregs)], axis=0)
         for cb in range(D_MODEL // LANES)], axis=1).astype(BF16)
    gdn = _deltanet_qkv_stages(h_perm, wm_ref, wbd_ref, convw_ref, alog_ref, dtb_ref, (qb_ref, kb_ref, vb_ref), bd_ref,
                               carry_ref, tm=tm)
    att = _attention_qkv_stages(h, wm_ref, cos_ref, sin_ref, (a0_ref, a1_ref, a2_ref), stage_ref, tm=tm)
    _interleave(gdn, DELTANET_QKV_STAGES, att, ATT_QKV_STAGES)


def _mixer_in(x1, norm_g, w_in, conv_w, a_log, dt_bias, cos_t, sin_t, *, tm, seq):
    n = x1.shape[0]
    tiles_per_seq = seq // tm
    tile = lambda rows, w: pl.BlockSpec((rows, w), lambda bi, i: (bi * tiles_per_seq + i, 0))
    table = pl.BlockSpec((tm, LANES), lambda bi, i: (i, 0))
    wq = 3 * ATT_GROUP_WIDTH
    att_specs = [tile(tm // dil, dil * wq) for _, dil in ATT_GROUPS]
    att_shapes = [jax.ShapeDtypeStruct((n // dil, dil * wq), BF16) for _, dil in ATT_GROUPS]
    gdn = jax.ShapeDtypeStruct((n, GDN_WIDTH), F32)
    return pl.pallas_call(
        functools.partial(_mixer_in_kernel, tm=tm),
        grid=(n // seq, tiles_per_seq),
        in_specs=[tile(tm, D_MODEL), _resident((1, D_MODEL)),
                  pl.BlockSpec((D_MODEL, W_IN_BD), lambda bi, i: (0, 0), pipeline_mode=pl.Buffered(1)),
                  pl.BlockSpec((D_MODEL, LANES), lambda bi, i: (0, W_IN_BD // LANES), pipeline_mode=pl.Buffered(1)),
                  _resident((GDN_CONV, 3 * GDN_WIDTH)), _resident((1, LANES)), _resident((1, LANES)), table, table],
        out_specs=att_specs + [tile(tm, GDN_WIDTH)] * 3 + [tile(tm, LANES)],
        out_shape=att_shapes + [gdn] * 3 + [jax.ShapeDtypeStruct((n, LANES), F32)],
        scratch_shapes=[pltpu.VMEM(((GDN_CONV - 1) * SUBLANES, 3 * GDN_WIDTH), F32),
                        pltpu.VMEM((D_MODEL // LANES, tm, LANES), F32), pltpu.VMEM((MIXER_IN_STAGED, tm, LANES), F32)],
        compiler_params=pltpu.CompilerParams(dimension_semantics=("arbitrary", "arbitrary"),
                                             vmem_limit_bytes=VMEM_LIMIT_BYTES),
        name="mixer_in",
    )(x1, norm_g, w_in, w_in, conv_w, a_log, dt_bias, cos_t, sin_t)


ATT_BATCH = 3


def _attention_blocks(items):
    lane = lax.broadcasted_iota(jnp.int32, (1, ATT_GROUP_WIDTH), 1)
    heads = range(ATT_HEADS_PER_GROUP)
    in_head = [(lane // ATT_HEAD_DIM) == hh for hh in heads]
    keep = [jnp.where(in_head[hh], 1.0, 0.0).astype(BF16) for hh in heads]
    nq = ATT_BLOCK
    s_all = [_dot_nt(jnp.concatenate([q * keep[hh] for hh in heads], axis=0), k) for q, k, _, _ in items]
    stats, p_all = [], []
    for (_, _, _, valid), sa in zip(items, s_all):
        s = [jnp.where(valid, sa[hh * nq:(hh + 1) * nq], NEG_BIG) for hh in heads]
        m = [jnp.max(s[hh], axis=-1, keepdims=True) for hh in heads]
        p = [jnp.exp(s[hh] - m[hh]) for hh in heads]
        l = [jnp.sum(p[hh], axis=-1, keepdims=True) for hh in heads]
        stats.append((m, l))
        p_all.append(jnp.concatenate([p[hh].astype(BF16) for hh in heads], axis=0))
    pv_all = [_dot(ps, v) for ps, (_, _, v, _) in zip(p_all, items)]
    outs = []
    for pv, (m, l) in zip(pv_all, stats):
        o = jnp.zeros((ATT_BLOCK, ATT_GROUP_WIDTH), F32)
        lse = jnp.zeros((ATT_BLOCK, ATT_GROUP_WIDTH), F32)
        for hh in heads:
            o = jnp.where(in_head[hh], pv[hh * nq:(hh + 1) * nq] * (1.0 / l[hh]), o)
            lse = jnp.where(in_head[hh], m[hh] + jnp.log(l[hh]), lse)
        outs.append((o, lse))
    return outs


def _attention_kernel(a0_ref, a1_ref, a2_ref, ya_ref, o0, l0, o1, l1, o2, l2, *, seq):
    in_refs = (a0_ref, a1_ref, a2_ref)
    o_refs = (o0, o1, o2)
    l_refs = (l0, l1, l2)
    qi = lax.broadcasted_iota(jnp.int32, (ATT_BLOCK, ATT_BLOCK), 0)
    kj = lax.broadcasted_iota(jnp.int32, (ATT_BLOCK, ATT_BLOCK), 1)
    causal = kj <= qi
    qi2 = lax.broadcasted_iota(jnp.int32, (ATT_BLOCK, 2 * ATT_BLOCK), 0)
    kj2 = lax.broadcasted_iota(jnp.int32, (ATT_BLOCK, 2 * ATT_BLOCK), 1)
    band = (kj2 >= qi2) & (kj2 - ATT_BLOCK <= qi2)
    wq = 3 * ATT_GROUP_WIDTH

    def load(gi, r, n):
        src, base = in_refs[gi], r * wq
        if isinstance(n, int) and n == 0:
            qrows = krows = slice(0, ATT_BLOCK)
            valid = causal
        else:
            start = lambda x: x if isinstance(x, int) else pl.multiple_of(x, ATT_BLOCK)
            qrows = pl.ds(start(n * ATT_BLOCK), ATT_BLOCK)
            krows = pl.ds(start((n - 1) * ATT_BLOCK), 2 * ATT_BLOCK)
            valid = band
        return (src[0, qrows, base:base + ATT_GROUP_WIDTH],
                src[0, krows, base + ATT_GROUP_WIDTH:base + 2 * ATT_GROUP_WIDTH],
                src[0, krows, base + 2 * ATT_GROUP_WIDTH:base + 3 * ATT_GROUP_WIDTH], valid)

    def store(gi, r, n, o, lse):
        dil = ATT_GROUPS[gi][1]
        if dil == 1:
            first = n * ATT_BLOCK
            rows = pl.ds(first if isinstance(first, int) else pl.multiple_of(first, ATT_BLOCK), ATT_BLOCK)
        else:
            rows = pl.ds(n * ATT_BLOCK * dil + r, ATT_BLOCK, stride=dil)
        for half in range(ATT_GROUP_WIDTH // LANES):
            o_refs[gi][half, rows, :] = o[:, half * LANES:(half + 1) * LANES]
            l_refs[gi][half, rows, :] = lse[:, half * LANES:(half + 1) * LANES]

    def run(blocks):
        for (gi, r, n), (o, lse) in zip(blocks, _attention_blocks([load(*blk) for blk in blocks])):
            store(gi, r, n, o, lse)

    static_blocks = []
    looped = None
    for gi, (window, dil) in enumerate(ATT_GROUPS):
        assert window // dil == ATT_BLOCK
        nblk = seq // dil // ATT_BLOCK
        if dil == 1 and (nblk - 1) % ATT_BATCH == 0:
            static_blocks.append((gi, 0, 0))
            looped = (gi, nblk)
        else:
            static_blocks += [(gi, r, n) for r in range(dil) for n in range(nblk)]
    for i in range(0, len(static_blocks), ATT_BATCH + 1):
        run(static_blocks[i:i + ATT_BATCH + 1])
    if looped is not None:
        gi, nblk = looped

        def body(i, carry):
            run([(gi, 0, 1 + i * ATT_BATCH + j) for j in range(ATT_BATCH)])
            return carry
        lax.fori_loop(0, (nblk - 1) // ATT_BATCH, body, 0)

    rows_per_step = 256

    def merge(i, carry):
        rows = pl.ds(pl.multiple_of(i * rows_per_step, rows_per_step), rows_per_step)
        for half in range(ATT_GROUP_WIDTH // LANES):
            la, lb, lc = l0[half, rows, :], l1[half, rows, :], l2[half, rows, :]
            m = jnp.maximum(jnp.maximum(la, lb), lc)
            ea, eb, ec = jnp.exp(la - m), jnp.exp(lb - m), jnp.exp(lc - m)
            num = ea * o0[half, rows, :] + eb * o1[half, rows, :] + ec * o2[half, rows, :]
            ya_ref[0, rows, half * LANES:(half + 1) * LANES] = num / (ea + eb + ec)
        return carry
    lax.fori_loop(0, seq // rows_per_step, merge, 0)


def _attention(a0, a1, a2, *, batch):
    views = tuple(a.reshape(batch, a.shape[0] // batch, a.shape[1]) for a in (a0, a1, a2))
    b, s, _ = views[0].shape
    specs = [pl.BlockSpec((1,) + arr.shape[1:], lambda bi: (bi, 0, 0)) for arr in views]
    scratch = [pltpu.VMEM((ATT_GROUP_WIDTH // LANES, s, LANES), F32) for _ in range(6)]
    return pl.pallas_call(
        functools.partial(_attention_kernel, seq=s),
        grid=(b,),
        in_specs=specs,
        out_specs=pl.BlockSpec((1, s, ATT_GROUP_WIDTH), lambda bi: (bi, 0, 0)),
        out_shape=jax.ShapeDtypeStruct((b, s, ATT_GROUP_WIDTH), F32),
        scratch_shapes=scratch,
        compiler_params=pltpu.CompilerParams(dimension_semantics=("arbitrary",),
                                             vmem_limit_bytes=VMEM_LIMIT_BYTES),
        name="dilated_attention",
    )(*views)


def _deltanet_stages(q_ref, k_ref, v_ref, bd_ref, gnorm, state_ref, ob_ref, slot, *, tile):
    c = GDN_CHUNK
    d = GDN_HEAD_DIM
    heads = range(GDN_HEADS)
    pairs = [(2 * pp, 2 * pp + 1) for pp in range(GDN_HEADS // 2)]
    cols = [slice(hh * d, (hh + 1) * d) for hh in heads]
    glane = [GDN_HEADS + hh for hh in heads]
    ii = _chunk_time(lax.broadcasted_iota(jnp.int32, (c, 2 * c), 0))
    ll = lax.broadcasted_iota(jnp.int32, (c, 2 * c), 1)
    jj = _chunk_time(ll % c)
    lower = ii >= jj
    strict = ii > jj
    left = ll < c
    left_row = lax.broadcasted_iota(jnp.int32, (1, 2 * c), 1) < c
    keep_left = jnp.where(left, 1.0, 0.0).astype(BF16)
    keep_right = jnp.where(left, 0.0, 1.0).astype(BF16)
    ti = _chunk_time(lax.broadcasted_iota(jnp.int32, (c, c), 0))
    tj = _chunk_time(lax.broadcasted_iota(jnp.int32, (c, c), 1))
    tri_ones = jnp.where(ti >= tj, 1.0, 0.0).astype(BF16)

    def blockdiag(x):
        return jnp.concatenate([x * keep_left, x * keep_right], axis=0)

    def stack_diag(xa, xb):
        zero = jnp.zeros_like(xa)
        return jnp.concatenate([jnp.concatenate([xa, zero], axis=1), jnp.concatenate([zero, xb], axis=1)], axis=0)

    def prepare(ci, out):
        rows = slice(ci * c, (ci + 1) * c)
        bd = bd_ref[rows, :]
        bd_hi = bd.astype(BF16)
        bd_rest = bd - bd_hi.astype(F32)
        bd_mid = bd_rest.astype(BF16)
        bd_lo = (bd_rest - bd_mid.astype(F32)).astype(BF16)
        gcum = _dot(tri_ones, bd_hi) + _dot(tri_ones, bd_mid) + _dot(tri_ones, bd_lo)
        yield
        gtot = jnp.broadcast_to(gcum[c - 1:c, :], (c, LANES))
        gcum_t = jnp.concatenate([gcum, gcum], axis=0).T
        e_cum_all = jnp.exp(gcum)
        e_rest_all = jnp.exp(gtot - gcum)
        e_tot_all = jnp.exp(gtot)
        q = [q_ref[rows, cols[hh]] for hh in heads]
        k = [k_ref[rows, cols[hh]] for hh in heads]
        v = [v_ref[rows, cols[hh]] for hh in heads]
        beta = [bd[:, hh:hh + 1] for hh in heads]
        e_cum = [e_cum_all[:, gl:gl + 1] for gl in glane]
        kbeta = [k[hh] * beta[hh] for hh in heads]
        kq = [_dot_nt(jnp.concatenate([jnp.concatenate([kbeta[a], kbeta[b]], axis=1),
                                       jnp.concatenate([q[a], q[b]], axis=1)], axis=0).astype(BF16),
                      stack_diag(k[a].astype(BF16), k[b].astype(BF16)))
              for a, b in pairs]
        yield
        decay = [jnp.exp(jnp.where(lower,
                                   jnp.where(left, gcum[:, glane[a]:glane[a] + 1], gcum[:, glane[b]:glane[b] + 1])
                                   - jnp.where(left_row, gcum_t[glane[a]:glane[a] + 1, :], gcum_t[glane[b]:glane[b] + 1, :]),
                                   NEG_BIG)) for a, b in pairs]
        m = [jnp.where(strict, kq[pp][0:c] * decay[pp], 0.0) for pp in range(len(pairs))]
        n = [-mm for mm in m]
        pb = [mm.astype(BF16) for mm in m]
        p = [_dot(x, blockdiag(x)) for x in pb]
        yield
        rounds = 5
        for r in range(rounds):
            pb = [x.astype(BF16) for x in p]
            upd = [_dot(x, blockdiag(y.astype(BF16))) for x, y in zip(pb, n)]
            p_next = [_dot(x, blockdiag(x)) for x in pb] if r + 1 < rounds else None
            yield
            n = [y + x + u for y, x, u in zip(n, p, upd)]
            p = p_next
        rhs = [jnp.concatenate([v[hh] * beta[hh], kbeta[hh] * e_cum[hh]], axis=1) for hh in heads]
        nr = [_dot(n[pp].astype(BF16), stack_diag(rhs[a].astype(BF16), rhs[b].astype(BF16)))
              for pp, (a, b) in enumerate(pairs)]
        yield
        sol = [rhs[hh] + nr[hh // 2][:, (hh % 2) * 2 * d:(hh % 2 + 1) * 2 * d] for hh in heads]
        out.update(
            first=ci * c,
            u=[sol[hh][:, 0:d] for hh in heads],
            wq=[jnp.concatenate([sol[hh][:, d:2 * d], q[hh] * e_cum[hh]], axis=0).astype(BF16) for hh in heads],
            a_qk=[(kq[pp][c:2 * c] * decay[pp]).astype(BF16) for pp in range(len(pairs))],
            k_dec=[(k[hh] * e_rest_all[:, gl:gl + 1]).astype(BF16) for hh, gl in zip(heads, glane)],
            e_tot=[e_tot_all[0:1, gl:gl + 1] for gl in glane])

    for first in range(0, tile // c, GDN_GROUP):
        group = [dict() for _ in range(GDN_GROUP)]
        gens = [prepare(first + gi, group[gi]) for gi in range(GDN_GROUP)]
        for _ in range(GDN_PREP_LAYERS):
            for gen in gens:
                next(gen)
            yield
        for gen in gens:
            for _ in gen:
                pass
        for pre in group:
            state = [state_ref[hh] for hh in heads]
            ws = [_dot(pre["wq"][hh], state[hh].astype(BF16)) for hh in heads]
            yield
            v_new = [(pre["u"][hh] - ws[hh][0:c]).astype(BF16) for hh in heads]
            kv = [_dot_tn(pre["k_dec"][hh], v_new[hh]) for hh in heads]
            av = [_dot(pre["a_qk"][pp], stack_diag(v_new[a], v_new[b])) for pp, (a, b) in enumerate(pairs)]
            yield
            for hh in heads:
                state_ref[hh] = state[hh] * pre["e_tot"][hh] + kv[hh]
            for hh in heads:
                o = _rmsnorm(ws[hh][c:2 * c] + av[hh // 2][:, (hh % 2) * d:(hh % 2 + 1) * d], gnorm)
                for vv in range(c // SUBLANES):
                    ob_ref[slot, hh, pl.ds(pre["first"] + vv, SUBLANES, stride=c // SUBLANES), :] = (
                        o[vv * SUBLANES:(vv + 1) * SUBLANES])


GDN_GROUP = 4
GDN_PREP_LAYERS = 9
GDN_LAYERS_PER_GROUP = GDN_PREP_LAYERS + 2 * GDN_GROUP


def _mixer_out_stages(x_ref, ya_ref, ob_ref, slot, g_ref, wgt_ref, wa_ref, wb_ref, wo_ref, o_ref):
    x = x_ref[...]
    h = _rmsnorm(x, g_ref[...]).astype(BF16)
    ya = ya_ref[...].astype(BF16)
    blocks = [slice(j * MIX_BLOCK, (j + 1) * MIX_BLOCK) for j in range(D_MODEL // MIX_BLOCK)]
    gate_cols = lambda which, blk: slice(which * D_MODEL + blk.start, which * D_MODEL + blk.stop)
    yb = []
    for blk in blocks:
        gdn_gate = _dot(h, wgt_ref[:, gate_cols(0, blk)])
        yield
        ob = jnp.concatenate([ob_ref[slot, hh] for hh in range(blk.start // GDN_HEAD_DIM, blk.stop // GDN_HEAD_DIM)],
                             axis=1)
        yb.append((ob * (gdn_gate * _sigmoid(gdn_gate))).astype(BF16))
    yb = jnp.concatenate(yb, axis=1)
    merged = []
    for blk in blocks:
        gate_a = _dot(h, wgt_ref[:, gate_cols(1, blk)])
        branch_a = _dot(ya, wa_ref[:, blk])
        yield
        gate_b = _dot(h, wgt_ref[:, gate_cols(2, blk)])
        yield
        branch_b = _dot(yb, wb_ref[:, blk])
        yield
        merged.append((_sigmoid(gate_a) * branch_a + _sigmoid(gate_b) * branch_b).astype(BF16))
    merged = jnp.concatenate(merged, axis=1)
    for blk in blocks:
        o_ref[:, blk] = x[:, blk] + _dot(merged, wo_ref[:, blk])
        yield


MIX_GRANULES = 5 * (D_MODEL // MIX_BLOCK)


def _mixer_tail_kernel(q_ref, k_ref, v_ref, bd_ref, gn_ref, x_ref, ya_ref, g_ref, wgt_ref, wa_ref, wb_ref, wo_ref,
                       o_ref, state_ref, ob_ref, *, tile, tiles_per_seq, n_tiles):
    step = pl.program_id(0)

    @pl.when(step == 0)
    def _():
        ob_ref[...] = jnp.zeros(ob_ref.shape, F32)

    @pl.when(jnp.minimum(step, n_tiles - 1) % tiles_per_seq == 0)
    def _():
        state_ref[...] = jnp.zeros(state_ref.shape, F32)

    slot = step % 2
    gdn = _deltanet_stages(q_ref, k_ref, v_ref, bd_ref, gn_ref[...], state_ref, ob_ref, slot, tile=tile)
    mix = _mixer_out_stages(x_ref, ya_ref, ob_ref, 1 - slot, g_ref, wgt_ref, wa_ref, wb_ref, wo_ref, o_ref)
    _interleave(gdn, GDN_LAYERS_PER_GROUP * (tile // (GDN_CHUNK * GDN_GROUP)), mix, MIX_GRANULES)


def _mixer_tail(qb, kb, vb, bd, out_norm, x1, ya, norm_g, w_gates, w_a, w_b, w_o, *, tile, seq):
    n = x1.shape[0]
    n_tiles = n // tile
    cur = lambda w: pl.BlockSpec((tile, w), lambda s: (jnp.minimum(s, n_tiles - 1), 0))
    prev = lambda w: pl.BlockSpec((tile, w), lambda s: (jnp.maximum(s - 1, 0), 0))
    return pl.pallas_call(
        functools.partial(_mixer_tail_kernel, tile=tile, tiles_per_seq=seq // tile, n_tiles=n_tiles),
        grid=(n_tiles + 1,),
        in_specs=[cur(GDN_WIDTH), cur(GDN_WIDTH), cur(GDN_WIDTH), cur(LANES), _resident((1, GDN_HEAD_DIM)),
                  prev(D_MODEL), prev(ATT_GROUP_WIDTH), _resident((1, D_MODEL)),
                  _resident(w_gates.shape), _resident(w_a.shape), _resident(w_b.shape), _resident(w_o.shape)],
        out_specs=prev(D_MODEL),
        out_shape=jax.ShapeDtypeStruct((n, D_MODEL), F32),
        scratch_shapes=[pltpu.VMEM((GDN_HEADS, GDN_HEAD_DIM, GDN_HEAD_DIM), F32),
                        pltpu.VMEM((2, GDN_HEADS, tile, GDN_HEAD_DIM), F32)],
        compiler_params=pltpu.CompilerParams(dimension_semantics=("arbitrary",),
                                             vmem_limit_bytes=VMEM_LIMIT_BYTES),
        name="deltanet_mixer_out",
    )(qb, kb, vb, bd, out_norm, x1, ya, norm_g, w_gates, w_a, w_b, w_o)


def _rope_tables(seq):
    half = ATT_HEAD_DIM // 2
    inv_freq = ROPE_THETA ** (-jnp.arange(half, dtype=F32) / half)
    ang = jnp.arange(seq, dtype=F32)[:, None] * inv_freq[None, :]
    cos, sin = jnp.cos(ang), jnp.sin(ang)
    reps = LANES // ATT_HEAD_DIM
    return jnp.tile(jnp.concatenate([cos, cos], axis=-1), (1, reps)), jnp.tile(jnp.concatenate([-sin, sin], axis=-1), (1, reps))


def _pad_lanes(row, offset):
    return jnp.zeros((1, LANES), F32).at[0, offset:offset + row.shape[0]].set(row.astype(F32))


def _layer(x, ffn1_norm, ffn1_w_gate, ffn1_w_up, ffn1_w_down, mix_norm, w_in, gdn_conv_w, gdn_a_log, gdn_dt_bias,
           gdn_out_norm, w_branch_a, w_branch_b, w_out, ffn2_norm, ffn2_w_gate, ffn2_w_up, ffn2_w_down, fin_g,
           *, final_norm, tm_ffn, tm_mix, gdn_tile):
    b, s, _ = x.shape
    n = b * s
    row = lambda v: v.reshape(1, -1).astype(F32)
    w_in = w_in.astype(BF16)
    w_gates = w_in[:, W_IN_GATES:]
    cos_t, sin_t = _rope_tables(s)
    x1 = _ffn(x.reshape(n, D_MODEL), row(ffn1_norm), ffn1_w_gate.astype(BF16), ffn1_w_up.astype(BF16),
              ffn1_w_down.astype(BF16), fin_g, final_norm=False, tm=tm_ffn)
    a0, a1, a2, qb, kb, vb, bd = _mixer_in(
        x1, row(mix_norm), w_in, gdn_conv_w.astype(F32), _pad_lanes(gdn_a_log, GDN_HEADS),
        _pad_lanes(gdn_dt_bias, GDN_HEADS), cos_t, sin_t, tm=tm_mix, seq=s)

    ya = _attention(a0, a1, a2, batch=b)
    x2 = _mixer_tail(qb, kb, vb, bd, row(gdn_out_norm), x1, ya.reshape(n, ATT_GROUP_WIDTH), row(mix_norm), w_gates,
                     w_branch_a.astype(BF16), w_branch_b.astype(BF16), w_out.astype(BF16), tile=gdn_tile, seq=s)
    x3 = _ffn(x2, row(ffn2_norm), ffn2_w_gate.astype(BF16), ffn2_w_up.astype(BF16), ffn2_w_down.astype(BF16),
              fin_g, final_norm=final_norm, tm=tm_ffn)
    return x3.reshape(b, s, D_MODEL)


def kernel(x, ffn1_norm, ffn1_w_gate, ffn1_w_up, ffn1_w_down, mix_norm, w_in, gdn_conv_w, gdn_a_log, gdn_dt_bias,
           gdn_out_norm, w_branch_a, w_branch_b, w_out, ffn2_norm, ffn2_w_gate, ffn2_w_up, ffn2_w_down, final_norm):
    depth = ffn1_norm.shape[0]
    fin_g = final_norm.reshape(1, -1).astype(F32)
    for layer in range(depth):
        x = _layer(x, ffn1_norm[layer], ffn1_w_gate[layer], ffn1_w_up[layer], ffn1_w_down[layer], mix_norm[layer],
                   w_in[layer], gdn_conv_w[layer], gdn_a_log[layer], gdn_dt_bias[layer], gdn_out_norm[layer],
                   w_branch_a[layer], w_branch_b[layer], w_out[layer], ffn2_norm[layer], ffn2_w_gate[layer],
                   ffn2_w_up[layer], ffn2_w_down[layer], fin_g, final_norm=(layer == depth - 1),
                   tm_ffn=512, tm_mix=512, gdn_tile=512)
    return x
```

```python
import functools

import jax
import jax.numpy as jnp
from jax import lax
from jax.experimental import pallas as pl
from jax.experimental.pallas import tpu as pltpu

F32 = jnp.float32
BF16 = jnp.bfloat16

D_MODEL = 1024
D_FF = 2816
EPS = 1e-6

ATT_GROUPS = ((128, 1), (512, 4), (2048, 16))
ATT_HEADS_PER_GROUP = 4
ATT_HEAD_DIM = 64
ATT_BLOCK = 128
ATT_GROUP_WIDTH = ATT_HEADS_PER_GROUP * ATT_HEAD_DIM
ATT_QKV_WIDTH = len(ATT_GROUPS) * ATT_GROUP_WIDTH
ROPE_THETA = 10000.0

GDN_HEADS = 8
GDN_HEAD_DIM = 128
GDN_WIDTH = GDN_HEADS * GDN_HEAD_DIM
GDN_CONV = 4
GDN_CHUNK = 64

LANES = 128
SUBLANES = 8
VMEM_LIMIT_BYTES = 56 * 1024 * 1024

W_IN_GDN = 3 * ATT_QKV_WIDTH
W_IN_BD = W_IN_GDN + 3 * GDN_WIDTH
W_IN_GATES = W_IN_BD + 2 * GDN_HEADS
FFN_CHUNKS = ((0, 768), (768, 1536), (1536, 2304), (2304, 2816))
NEG_BIG = -1e30


def _resident(shape):
    nd = len(shape)
    return pl.BlockSpec(shape, lambda *_: (0,) * nd, pipeline_mode=pl.Buffered(1))


def _rmsnorm(x, g):
    return x * lax.rsqrt(jnp.mean(x * x, axis=-1, keepdims=True) + EPS) * g


def _sigmoid(x):
    return 1.0 / (1.0 + jnp.exp(-x))


def _dot(a, b):
    return jnp.dot(a, b, preferred_element_type=F32)


def _dot_nt(a, b):
    return lax.dot_general(a, b, (((1,), (1,)), ((), ())), preferred_element_type=F32)


def _dot_tn(a, b):
    return lax.dot_general(a, b, (((0,), (0,)), ((), ())), preferred_element_type=F32)


def _swiglu_residual(x, g, wg_ref, wu_ref, wd_ref):
    h = _rmsnorm(x, g).astype(BF16)
    acc = x
    for lo, hi in FFN_CHUNKS:
        gate = _dot(h, wg_ref[:, lo:hi])
        up = _dot(h, wu_ref[:, lo:hi])
        act = (0.5 * gate * _sigmoid(gate) * up).astype(BF16)
        acc = acc + _dot(act, wd_ref[lo:hi, :])
    return acc


def _ffn_kernel(x_ref, g_ref, wg_ref, wu_ref, wd_ref, fin_ref, o_ref, *, final_norm):
    y = _swiglu_residual(x_ref[...], g_ref[...], wg_ref, wu_ref, wd_ref)
    if final_norm:
        y = _rmsnorm(y, fin_ref[...])
    o_ref[...] = y


def _ffn(x, norm_g, wg, wu, wd, fin_g, *, final_norm, tm):
    n = x.shape[0]
    row = pl.BlockSpec((tm, D_MODEL), lambda i: (i, 0))
    return pl.pallas_call(
        functools.partial(_ffn_kernel, final_norm=final_norm),
        grid=(n // tm,),
        in_specs=[row, _resident((1, D_MODEL)), _resident((D_MODEL, D_FF)), _resident((D_MODEL, D_FF)),
                  _resident((D_FF, D_MODEL)), _resident((1, D_MODEL))],
        out_specs=row,
        out_shape=jax.ShapeDtypeStruct((n, D_MODEL), F32),
        compiler_params=pltpu.CompilerParams(dimension_semantics=("arbitrary",),
                                             vmem_limit_bytes=VMEM_LIMIT_BYTES),
        name="ffn_final" if final_norm else "ffn",
    )(x, norm_g, wg, wu, wd, fin_g)


MIX_BLOCK = 256


def _interleave(primary, n_primary, secondary, n_secondary):
    done = 0
    for i in range(n_primary):
        next(primary)
        while done * n_primary < (i + 1) * n_secondary:
            next(secondary)
            done += 1
    for gen in (primary, secondary):
        for _ in gen:
            pass


def _chunk_time(row):
    return SUBLANES * (row % SUBLANES) + row // SUBLANES


def _attention_qkv_stages(h, wm_ref, cos_ref, sin_ref, att_refs, stage_ref, *, tm):
    cos = cos_ref[...]
    sin = sin_ref[...]
    lane = lax.broadcasted_iota(jnp.int32, (1, LANES), 1)
    first_half = (lane % ATT_HEAD_DIM) < (ATT_HEAD_DIM // 2)
    slot = 0
    for part in range(3):
        for gi, (_, dil) in enumerate(ATT_GROUPS):
            col = part * ATT_QKV_WIDTH + gi * ATT_GROUP_WIDTH
            y = _dot(h, wm_ref[:, col:col + ATT_GROUP_WIDTH])
            yield
            for j in range(ATT_GROUP_WIDTH // LANES):
                blk = y[:, j * LANES:(j + 1) * LANES]
                if part < 2:
                    swapped = jnp.where(first_half, pltpu.roll(blk, LANES - ATT_HEAD_DIM // 2, 1),
                                        pltpu.roll(blk, ATT_HEAD_DIM // 2, 1))
                    blk = blk * cos + swapped * sin
                if part == 0:
                    blk = blk * (ATT_HEAD_DIM ** -0.5)
                dst = part * ATT_GROUP_WIDTH + j * LANES
                if dil == 1:
                    att_refs[gi][:, dst:dst + LANES] = blk.astype(BF16)
                else:
                    stage_ref[slot] = blk
                    for r in range(dil):
                        rows = stage_ref[slot, pl.ds(r, tm // dil, stride=dil), :]
                        lo = r * 3 * ATT_GROUP_WIDTH + dst
                        att_refs[gi][:, lo:lo + LANES] = rows.astype(BF16)
                    slot += 1
                yield


ATT_QKV_STAGES = 3 * len(ATT_GROUPS) * (1 + ATT_GROUP_WIDTH // LANES)
MIXER_IN_STAGED = 3 * (len(ATT_GROUPS) - 1) * (ATT_GROUP_WIDTH // LANES)


def _deltanet_qkv_stages(h_perm, wm_ref, wbd_ref, convw_ref, alog_ref, dtb_ref, gdn_refs, bd_ref, carry_ref, *, tm):
    lane = lax.broadcasted_iota(jnp.int32, (1, LANES), 1)
    raw = _dot(h_perm, wbd_ref[...])
    yield
    z = raw + dtb_ref[...]
    softplus = jnp.maximum(z, 0.0) + jnp.log1p(jnp.exp(-jnp.abs(z)))
    g = -jnp.exp(alog_ref[...]) * softplus
    bd_ref[...] = jnp.where(lane < GDN_HEADS, _sigmoid(raw), jnp.where(lane < 2 * GDN_HEADS, g, 0.0))

    vregs = GDN_CHUNK // SUBLANES
    halo = GDN_CONV - 1
    chunks = tm // GDN_CHUNK
    last_sublane = lax.broadcasted_iota(jnp.int32, (SUBLANES, GDN_HEAD_DIM), 0) == SUBLANES - 1
    heads_per_block = MIX_BLOCK // GDN_HEAD_DIM
    for part in range(3):
        for blk in range(GDN_WIDTH // MIX_BLOCK):
            base = part * GDN_WIDTH + blk * MIX_BLOCK
            y = _dot(h_perm, wm_ref[:, W_IN_GDN + base:W_IN_GDN + base + MIX_BLOCK])
            yield
            for hb in range(heads_per_block):
                hh = blk * heads_per_block + hb
                col = part * GDN_WIDTH + hh * GDN_HEAD_DIM
                cur = y[:, hb * GDN_HEAD_DIM:(hb + 1) * GDN_HEAD_DIM]
                vreg = lambda c, v: cur[c * GDN_CHUNK + v * SUBLANES:c * GDN_CHUNK + (v + 1) * SUBLANES]
                prev_tile = carry_ref[:, col:col + GDN_HEAD_DIM]
                carry_ref[:, col:col + GDN_HEAD_DIM] = cur[tm - halo * SUBLANES:, :]
                wrapped = []
                for c in range(chunks):
                    row = []
                    for i in range(halo):
                        before = (prev_tile[i * SUBLANES:(i + 1) * SUBLANES] if c == 0
                                  else vreg(c - 1, vregs - halo + i))
                        own = vreg(c, vregs - halo + i)
                        row.append(pltpu.roll(jnp.where(last_sublane, before, own), 1, 0))
                    wrapped.append(row)
                acc = cur * convw_ref[halo:halo + 1, col:col + GDN_HEAD_DIM]
                for shift in range(1, GDN_CONV):
                    pieces = []
                    for c in range(chunks):
                        pieces += wrapped[c][halo - shift:]
                        pieces.append(cur[c * GDN_CHUNK:(c + 1) * GDN_CHUNK - shift * SUBLANES])
                    shifted = jnp.concatenate(pieces, axis=0)
                    acc = acc + shifted * convw_ref[halo - shift:halo - shift + 1, col:col + GDN_HEAD_DIM]
                act = acc * _sigmoid(acc)
                if part < 2:
                    act = act * lax.rsqrt(jnp.sum(act * act, axis=-1, keepdims=True) + EPS)
                if part == 0:
                    act = act * (GDN_HEAD_DIM ** -0.5)
                gdn_refs[part][:, hh * GDN_HEAD_DIM:(hh + 1) * GDN_HEAD_DIM] = act
                yield


DELTANET_QKV_STAGES = 1 + 3 * (GDN_WIDTH // MIX_BLOCK) + 3 * GDN_HEADS


def _mixer_in_kernel(x_ref, g_ref, wm_ref, wbd_ref, convw_ref, alog_ref, dtb_ref, cos_ref, sin_ref,
                     a0_ref, a1_ref, a2_ref, qb_ref, kb_ref, vb_ref, bd_ref, carry_ref, perm_ref, stage_ref, *, tm):
    @pl.when(pl.program_id(1) == 0)
    def _():
        carry_ref[...] = jnp.zeros(carry_ref.shape, F32)

    hf = _rmsnorm(x_ref[...], g_ref[...])
    h = hf.astype(BF16)
    vregs = GDN_CHUNK // SUBLANES
    for cb in range(D_MODEL // LANES):
        perm_ref[cb] = hf[:, cb * LANES:(cb + 1) * LANES]
    h_perm = jnp.concatenate(
        [jnp.concatenate([perm_ref[cb, pl.ds(c0 + v, SUBLANES, stride=vregs), :]
                          for c0 in range(0, tm, GDN_CHUNK) for v in range(vregs)], axis=0)
         for cb in range(D_MODEL // LANES)], axis=1).astype(BF16)
    gdn = _deltanet_qkv_stages(h_perm, wm_ref, wbd_ref, convw_ref, alog_ref, dtb_ref, (qb_ref, kb_ref, vb_ref), bd_ref,
                               carry_ref, tm=tm)
    att = _attention_qkv_stages(h, wm_ref, cos_ref, sin_ref, (a0_ref, a1_ref, a2_ref), stage_ref, tm=tm)
    _interleave(gdn, DELTANET_QKV_STAGES, att, ATT_QKV_STAGES)


def _mixer_in(x1, norm_g, w_in, conv_w, a_log, dt_bias, cos_t, sin_t, *, tm, seq):
    n = x1.shape[0]
    tiles_per_seq = seq // tm
    tile = lambda rows, w: pl.BlockSpec((rows, w), lambda bi, i: (bi * tiles_per_seq + i, 0))
    table = pl.BlockSpec((tm, LANES), lambda bi, i: (i, 0))
    wq = 3 * ATT_GROUP_WIDTH
    att_specs = [tile(tm // dil, dil * wq) for _, dil in ATT_GROUPS]
    att_shapes = [jax.ShapeDtypeStruct((n // dil, dil * wq), BF16) for _, dil in ATT_GROUPS]
    gdn = jax.ShapeDtypeStruct((n, GDN_WIDTH), F32)
    return pl.pallas_call(
        functools.partial(_mixer_in_kernel, tm=tm),
        grid=(n // seq, tiles_per_seq),
        in_specs=[tile(tm, D_MODEL), _resident((1, D_MODEL)),
                  pl.BlockSpec((D_MODEL, W_IN_BD), lambda bi, i: (0, 0), pipeline_mode=pl.Buffered(1)),
                  pl.BlockSpec((D_MODEL, LANES), lambda bi, i: (0, W_IN_BD // LANES), pipeline_mode=pl.Buffered(1)),
                  _resident((GDN_CONV, 3 * GDN_WIDTH)), _resident((1, LANES)), _resident((1, LANES)), table, table],
        out_specs=att_specs + [tile(tm, GDN_WIDTH)] * 3 + [tile(tm, LANES)],
        out_shape=att_shapes + [gdn] * 3 + [jax.ShapeDtypeStruct((n, LANES), F32)],
        scratch_shapes=[pltpu.VMEM(((GDN_CONV - 1) * SUBLANES, 3 * GDN_WIDTH), F32),
                        pltpu.VMEM((D_MODEL // LANES, tm, LANES), F32), pltpu.VMEM((MIXER_IN_STAGED, tm, LANES), F32)],
        compiler_params=pltpu.CompilerParams(dimension_semantics=("arbitrary", "arbitrary"),
                                             vmem_limit_bytes=VMEM_LIMIT_BYTES),
        name="mixer_in",
    )(x1, norm_g, w_in, w_in, conv_w, a_log, dt_bias, cos_t, sin_t)


ATT_BATCH = 3


def _attention_blocks(items):
    lane = lax.broadcasted_iota(jnp.int32, (1, ATT_GROUP_WIDTH), 1)
    heads = range(ATT_HEADS_PER_GROUP)
    in_head = [(lane // ATT_HEAD_DIM) == hh for hh in heads]
    keep = [jnp.where(in_head[hh], 1.0, 0.0).astype(BF16) for hh in heads]
    nq = ATT_BLOCK
    s_all = [_dot_nt(jnp.concatenate([q * keep[hh] for hh in heads], axis=0), k) for q, k, _, _ in items]
    stats, p_all = [], []
    for (_, _, _, valid), sa in zip(items, s_all):
        s = [jnp.where(valid, sa[hh * nq:(hh + 1) * nq], NEG_BIG) for hh in heads]
        m = [jnp.max(s[hh], axis=-1, keepdims=True) for hh in heads]
        p = [jnp.exp(s[hh] - m[hh]) for hh in heads]
        l = [jnp.sum(p[hh], axis=-1, keepdims=True) for hh in heads]
        stats.append((m, l))
        p_all.append(jnp.concatenate([p[hh].astype(BF16) for hh in heads], axis=0))
    pv_all = [_dot(ps, v) for ps, (_, _, v, _) in zip(p_all, items)]
    outs = []
    for pv, (m, l) in zip(pv_all, stats):
        o = jnp.zeros((ATT_BLOCK, ATT_GROUP_WIDTH), F32)
        lse = jnp.zeros((ATT_BLOCK, ATT_GROUP_WIDTH), F32)
        for hh in heads:
            o = jnp.where(in_head[hh], pv[hh * nq:(hh + 1) * nq] * (1.0 / l[hh]), o)
            lse = jnp.where(in_head[hh], m[hh] + jnp.log(l[hh]), lse)
        outs.append((o, lse))
    return outs


def _attention_kernel(a0_ref, a1_ref, a2_ref, ya_ref, o0, l0, o1, l1, o2, l2, *, seq):
    in_refs = (a0_ref, a1_ref, a2_ref)
    o_refs = (o0, o1, o2)
    l_refs = (l0, l1, l2)
    qi = lax.broadcasted_iota(jnp.int32, (ATT_BLOCK, ATT_BLOCK), 0)
    kj = lax.broadcasted_iota(jnp.int32, (ATT_BLOCK, ATT_BLOCK), 1)
    causal = kj <= qi
    qi2 = lax.broadcasted_iota(jnp.int32, (ATT_BLOCK, 2 * ATT_BLOCK), 0)
    kj2 = lax.broadcasted_iota(jnp.int32, (ATT_BLOCK, 2 * ATT_BLOCK), 1)
    band = (kj2 >= qi2) & (kj2 - ATT_BLOCK <= qi2)
    wq = 3 * ATT_GROUP_WIDTH

    def load(gi, r, n):
        src, base = in_refs[gi], r * wq
        if isinstance(n, int) and n == 0:
            qrows = krows = slice(0, ATT_BLOCK)
            valid = causal
        else:
            start = lambda x: x if isinstance(x, int) else pl.multiple_of(x, ATT_BLOCK)
            qrows = pl.ds(start(n * ATT_BLOCK), ATT_BLOCK)
            krows = pl.ds(start((n - 1) * ATT_BLOCK), 2 * ATT_BLOCK)
            valid = band
        return (src[0, qrows, base:base + ATT_GROUP_WIDTH],
                src[0, krows, base + ATT_GROUP_WIDTH:base + 2 * ATT_GROUP_WIDTH],
                src[0, krows, base + 2 * ATT_GROUP_WIDTH:base + 3 * ATT_GROUP_WIDTH], valid)

    def store(gi, r, n, o, lse):
        dil = ATT_GROUPS[gi][1]
        if dil == 1:
            first = n * ATT_BLOCK
            rows = pl.ds(first if isinstance(first, int) else pl.multiple_of(first, ATT_BLOCK), ATT_BLOCK)
        else:
            rows = pl.ds(n * ATT_BLOCK * dil + r, ATT_BLOCK, stride=dil)
        for half in range(ATT_GROUP_WIDTH // LANES):
            o_refs[gi][half, rows, :] = o[:, half * LANES:(half + 1) * LANES]
            l_refs[gi][half, rows, :] = lse[:, half * LANES:(half + 1) * LANES]

    def run(blocks):
        for (gi, r, n), (o, lse) in zip(blocks, _attention_blocks([load(*blk) for blk in blocks])):
            store(gi, r, n, o, lse)

    static_blocks = []
    looped = None
    for gi, (window, dil) in enumerate(ATT_GROUPS):
        assert window // dil == ATT_BLOCK
        nblk = seq // dil // ATT_BLOCK
        if dil == 1 and (nblk - 1) % ATT_BATCH == 0:
            static_blocks.append((gi, 0, 0))
            looped = (gi, nblk)
        else:
            static_blocks += [(gi, r, n) for r in range(dil) for n in range(nblk)]
    for i in range(0, len(static_blocks), ATT_BATCH + 1):
        run(static_blocks[i:i + ATT_BATCH + 1])
    if looped is not None:
        gi, nblk = looped

        def body(i, carry):
            run([(gi, 0, 1 + i * ATT_BATCH + j) for j in range(ATT_BATCH)])
            return carry
        lax.fori_loop(0, (nblk - 1) // ATT_BATCH, body, 0)

    rows_per_step = 256

    def merge(i, carry):
        rows = pl.ds(pl.multiple_of(i * rows_per_step, rows_per_step), rows_per_step)
        for half in range(ATT_GROUP_WIDTH // LANES):
            la, lb, lc = l0[half, rows, :], l1[half, rows, :], l2[half, rows, :]
            m = jnp.maximum(jnp.maximum(la, lb), lc)
            ea, eb, ec = jnp.exp(la - m), jnp.exp(lb - m), jnp.exp(lc - m)
            num = ea * o0[half, rows, :] + eb * o1[half, rows, :] + ec * o2[half, rows, :]
            ya_ref[0, rows, half * LANES:(half + 1) * LANES] = num / (ea + eb + ec)
        return carry
    lax.fori_loop(0, seq // rows_per_step, merge, 0)


def _attention(a0, a1, a2, *, batch):
    views = tuple(a.reshape(batch, a.shape[0] // batch, a.shape[1]) for a in (a0, a1, a2))
    b, s, _ = views[0].shape
    specs = [pl.BlockSpec((1,) + arr.shape[1:], lambda bi: (bi, 0, 0)) for arr in views]
    scratch = [pltpu.VMEM((ATT_GROUP_WIDTH // LANES, s, LANES), F32) for _ in range(6)]
    return pl.pallas_call(
        functools.partial(_attention_kernel, seq=s),
        grid=(b,),
        in_specs=specs,
        out_specs=pl.BlockSpec((1, s, ATT_GROUP_WIDTH), lambda bi: (bi, 0, 0)),
        out_shape=jax.ShapeDtypeStruct((b, s, ATT_GROUP_WIDTH), F32),
        scratch_shapes=scratch,
        compiler_params=pltpu.CompilerParams(dimension_semantics=("arbitrary",),
                                             vmem_limit_bytes=VMEM_LIMIT_BYTES),
        name="dilated_attention",
    )(*views)


def _deltanet_stages(q_ref, k_ref, v_ref, bd_ref, gnorm, state_ref, ob_ref, slot, *, tile):
    c = GDN_CHUNK
    d = GDN_HEAD_DIM
    heads = range(GDN_HEADS)
    pairs = [(2 * pp, 2 * pp + 1) for pp in range(GDN_HEADS // 2)]
    cols = [slice(hh * d, (hh + 1) * d) for hh in heads]
    glane = [GDN_HEADS + hh for hh in heads]
    ii = _chunk_time(lax.broadcasted_iota(jnp.int32, (c, 2 * c), 0))
    ll = lax.broadcasted_iota(jnp.int32, (c, 2 * c), 1)
    jj = _chunk_time(ll % c)
    lower = ii >= jj
    strict = ii > jj
    left = ll < c
    left_row = lax.broadcasted_iota(jnp.int32, (1, 2 * c), 1) < c
    keep_left = jnp.where(left, 1.0, 0.0).astype(BF16)
    keep_right = jnp.where(left, 0.0, 1.0).astype(BF16)
    ti = _chunk_time(lax.broadcasted_iota(jnp.int32, (c, c), 0))
    tj = _chunk_time(lax.broadcasted_iota(jnp.int32, (c, c), 1))
    tri_ones = jnp.where(ti >= tj, 1.0, 0.0).astype(BF16)

    def blockdiag(x):
        return jnp.concatenate([x * keep_left, x * keep_right], axis=0)

    def stack_diag(xa, xb):
        zero = jnp.zeros_like(xa)
        return jnp.concatenate([jnp.concatenate([xa, zero], axis=1), jnp.concatenate([zero, xb], axis=1)], axis=0)

    def prepare(ci, out):
        rows = slice(ci * c, (ci + 1) * c)
        bd = bd_ref[rows, :]
        bd_hi = bd.astype(BF16)
        bd_rest = bd - bd_hi.astype(F32)
        bd_mid = bd_rest.astype(BF16)
        bd_lo = (bd_rest - bd_mid.astype(F32)).astype(BF16)
        gcum = _dot(tri_ones, bd_hi) + _dot(tri_ones, bd_mid) + _dot(tri_ones, bd_lo)
        yield
        gtot = jnp.broadcast_to(gcum[c - 1:c, :], (c, LANES))
        gcum_t = jnp.concatenate([gcum, gcum], axis=0).T
        e_cum_all = jnp.exp(gcum)
        e_rest_all = jnp.exp(gtot - gcum)
        e_tot_all = jnp.exp(gtot)
        q = [q_ref[rows, cols[hh]] for hh in heads]
        k = [k_ref[rows, cols[hh]] for hh in heads]
        v = [v_ref[rows, cols[hh]] for hh in heads]
        beta = [bd[:, hh:hh + 1] for hh in heads]
        e_cum = [e_cum_all[:, gl:gl + 1] for gl in glane]
        kbeta = [k[hh] * beta[hh] for hh in heads]
        kq = [_dot_nt(jnp.concatenate([jnp.concatenate([kbeta[a], kbeta[b]], axis=1),
                                       jnp.concatenate([q[a], q[b]], axis=1)], axis=0).astype(BF16),
                      stack_diag(k[a].astype(BF16), k[b].astype(BF16)))
              for a, b in pairs]
        yield
        decay = [jnp.exp(jnp.where(lower,
                                   jnp.where(left, gcum[:, glane[a]:glane[a] + 1], gcum[:, glane[b]:glane[b] + 1])
                                   - jnp.where(left_row, gcum_t[glane[a]:glane[a] + 1, :], gcum_t[glane[b]:glane[b] + 1, :]),
                                   NEG_BIG)) for a, b in pairs]
        m = [jnp.where(strict, kq[pp][0:c] * decay[pp], 0.0) for pp in range(len(pairs))]
        n = [-mm for mm in m]
        pb = [mm.astype(BF16) for mm in m]
        p = [_dot(x, blockdiag(x)) for x in pb]
        yield
        rounds = 5
        for r in range(rounds):
            pb = [x.astype(BF16) for x in p]
            upd = [_dot(x, blockdiag(y.astype(BF16))) for x, y in zip(pb, n)]
            p_next = [_dot(x, blockdiag(x)) for x in pb] if r + 1 < rounds else None
            yield
            n = [y + x + u for y, x, u in zip(n, p, upd)]
            p = p_next
        rhs = [jnp.concatenate([v[hh] * beta[hh], kbeta[hh] * e_cum[hh]], axis=1) for hh in heads]
        nr = [_dot(n[pp].astype(BF16), stack_diag(rhs[a].astype(BF16), rhs[b].astype(BF16)))
              for pp, (a, b) in enumerate(pairs)]
        yield
        sol = [rhs[hh] + nr[hh // 2][:, (hh % 2) * 2 * d:(hh % 2 + 1) * 2 * d] for hh in heads]
        out.update(
            first=ci * c,
            u=[sol[hh][:, 0:d] for hh in heads],
            wq=[jnp.concatenate([sol[hh][:, d:2 * d], q[hh] * e_cum[hh]], axis=0).astype(BF16) for hh in heads],
            a_qk=[(kq[pp][c:2 * c] * decay[pp]).astype(BF16) for pp in range(len(pairs))],
            k_dec=[(k[hh] * e_rest_all[:, gl:gl + 1]).astype(BF16) for hh, gl in zip(heads, glane)],
            e_tot=[e_tot_all[0:1, gl:gl + 1] for gl in glane])

    for first in range(0, tile // c, GDN_GROUP):
        group = [dict() for _ in range(GDN_GROUP)]
        gens = [prepare(first + gi, group[gi]) for gi in range(GDN_GROUP)]
        for _ in range(GDN_PREP_LAYERS):
            for gen in gens:
                next(gen)
            yield
        for gen in gens:
            for _ in gen:
                pass
        for pre in group:
            state = [state_ref[hh] for hh in heads]
            ws = [_dot(pre["wq"][hh], state[hh].astype(BF16)) for hh in heads]
            yield
            v_new = [(pre["u"][hh] - ws[hh][0:c]).astype(BF16) for hh in heads]
            kv = [_dot_tn(pre["k_dec"][hh], v_new[hh]) for hh in heads]
            av = [_dot(pre["a_qk"][pp], stack_diag(v_new[a], v_new[b])) for pp, (a, b) in enumerate(pairs)]
            yield
            for hh in heads:
                state_ref[hh] = state[hh] * pre["e_tot"][hh] + kv[hh]
            for hh in heads:
                o = _rmsnorm(ws[hh][c:2 * c] + av[hh // 2][:, (hh % 2) * d:(hh % 2 + 1) * d], gnorm)
                for vv in range(c // SUBLANES):
                    ob_ref[slot, hh, pl.ds(pre["first"] + vv, SUBLANES, stride=c // SUBLANES), :] = (
                        o[vv * SUBLANES:(vv + 1) * SUBLANES])


GDN_GROUP = 4
GDN_PREP_LAYERS = 9
GDN_LAYERS_PER_GROUP = GDN_PREP_LAYERS + 2 * GDN_GROUP


def _mixer_out_stages(x_ref, ya_ref, ob_ref, slot, g_ref, wgt_ref, wa_ref, wb_ref, wo_ref, o_ref):
    x = x_ref[...]
    h = _rmsnorm(x, g_ref[...]).astype(BF16)
    ya = ya_ref[...].astype(BF16)
    blocks = [slice(j * MIX_BLOCK, (j + 1) * MIX_BLOCK) for j in range(D_MODEL // MIX_BLOCK)]
    gate_cols = lambda which, blk: slice(which * D_MODEL + blk.start, which * D_MODEL + blk.stop)
    yb = []
    for blk in blocks:
        gdn_gate = _dot(h, wgt_ref[:, gate_cols(0, blk)])
        yield
        ob = jnp.concatenate([ob_ref[slot, hh] for hh in range(blk.start // GDN_HEAD_DIM, blk.stop // GDN_HEAD_DIM)],
                             axis=1)
        yb.append((ob * (gdn_gate * _sigmoid(gdn_gate))).astype(BF16))
    yb = jnp.concatenate(yb, axis=1)
    merged = []
    for blk in blocks:
        gate_a = _dot(h, wgt_ref[:, gate_cols(1, blk)])
        branch_a = _dot(ya, wa_ref[:, blk])
        yield
        gate_b = _dot(h, wgt_ref[:, gate_cols(2, blk)])
        yield
        branch_b = _dot(yb, wb_ref[:, blk])
        yield
        merged.append((_sigmoid(gate_a) * branch_a + _sigmoid(gate_b) * branch_b).astype(BF16))
    merged = jnp.concatenate(merged, axis=1)
    for blk in blocks:
        o_ref[:, blk] = x[:, blk] + _dot(merged, wo_ref[:, blk])
        yield


MIX_GRANULES = 5 * (D_MODEL // MIX_BLOCK)


def _mixer_tail_kernel(q_ref, k_ref, v_ref, bd_ref, gn_ref, x_ref, ya_ref, g_ref, wgt_ref, wa_ref, wb_ref, wo_ref,
                       o_ref, state_ref, ob_ref, *, tile, tiles_per_seq, n_tiles):
    step = pl.program_id(0)

    @pl.when(step == 0)
    def _():
        ob_ref[...] = jnp.zeros(ob_ref.shape, F32)

    @pl.when(jnp.minimum(step, n_tiles - 1) % tiles_per_seq == 0)
    def _():
        state_ref[...] = jnp.zeros(state_ref.shape, F32)

    slot = step % 2
    gdn = _deltanet_stages(q_ref, k_ref, v_ref, bd_ref, gn_ref[...], state_ref, ob_ref, slot, tile=tile)
    mix = _mixer_out_stages(x_ref, ya_ref, ob_ref, 1 - slot, g_ref, wgt_ref, wa_ref, wb_ref, wo_ref, o_ref)
    _interleave(gdn, GDN_LAYERS_PER_GROUP * (tile // (GDN_CHUNK * GDN_GROUP)), mix, MIX_GRANULES)


def _mixer_tail(qb, kb, vb, bd, out_norm, x1, ya, norm_g, w_gates, w_a, w_b, w_o, *, tile, seq):
    n = x1.shape[0]
    n_tiles = n // tile
    cur = lambda w: pl.BlockSpec((tile, w), lambda s: (jnp.minimum(s, n_tiles - 1), 0))
    prev = lambda w: pl.BlockSpec((tile, w), lambda s: (jnp.maximum(s - 1, 0), 0))
    return pl.pallas_call(
        functools.partial(_mixer_tail_kernel, tile=tile, tiles_per_seq=seq // tile, n_tiles=n_tiles),
        grid=(n_tiles + 1,),
        in_specs=[cur(GDN_WIDTH), cur(GDN_WIDTH), cur(GDN_WIDTH), cur(LANES), _resident((1, GDN_HEAD_DIM)),
                  prev(D_MODEL), prev(ATT_GROUP_WIDTH), _resident((1, D_MODEL)),
                  _resident(w_gates.shape), _resident(w_a.shape), _resident(w_b.shape), _resident(w_o.shape)],
        out_specs=prev(D_MODEL),
        out_shape=jax.ShapeDtypeStruct((n, D_MODEL), F32),
        scratch_shapes=[pltpu.VMEM((GDN_HEADS, GDN_HEAD_DIM, GDN_HEAD_DIM), F32),
                        pltpu.VMEM((2, GDN_HEADS, tile, GDN_HEAD_DIM), F32)],
        compiler_params=pltpu.CompilerParams(dimension_semantics=("arbitrary",),
                                             vmem_limit_bytes=VMEM_LIMIT_BYTES),
        name="deltanet_mixer_out",
    )(qb, kb, vb, bd, out_norm, x1, ya, norm_g, w_gates, w_a, w_b, w_o)


def _rope_tables(seq):
    half = ATT_HEAD_DIM // 2
    inv_freq = ROPE_THETA ** (-jnp.arange(half, dtype=F32) / half)
    ang = jnp.arange(seq, dtype=F32)[:, None] * inv_freq[None, :]
    cos, sin = jnp.cos(ang), jnp.sin(ang)
    reps = LANES // ATT_HEAD_DIM
    return jnp.tile(jnp.concatenate([cos, cos], axis=-1), (1, reps)), jnp.tile(jnp.concatenate([-sin, sin], axis=-1), (1, reps))


def _pad_lanes(row, offset):
    return jnp.zeros((1, LANES), F32).at[0, offset:offset + row.shape[0]].set(row.astype(F32))


def _layer(x, ffn1_norm, ffn1_w_gate, ffn1_w_up, ffn1_w_down, mix_norm, w_in, gdn_conv_w, gdn_a_log, gdn_dt_bias,
           gdn_out_norm, w_branch_a, w_branch_b, w_out, ffn2_norm, ffn2_w_gate, ffn2_w_up, ffn2_w_down, fin_g,
           *, final_norm, tm_ffn, tm_mix, gdn_tile):
    b, s, _ = x.shape
    n = b * s
    row = lambda v: v.reshape(1, -1).astype(F32)
    w_in = w_in.astype(BF16)
    w_gates = w_in[:, W_IN_GATES:]
    cos_t, sin_t = _rope_tables(s)
    x1 = _ffn(x.reshape(n, D_MODEL), row(ffn1_norm), ffn1_w_gate.astype(BF16), ffn1_w_up.astype(BF16),
              ffn1_w_down.astype(BF16), fin_g, final_norm=False, tm=tm_ffn)
    a0, a1, a2, qb, kb, vb, bd = _mixer_in(
        x1, row(mix_norm), w_in, gdn_conv_w.astype(F32), _pad_lanes(gdn_a_log, GDN_HEADS),
        _pad_lanes(gdn_dt_bias, GDN_HEADS), cos_t, sin_t, tm=tm_mix, seq=s)

    ya = _attention(a0, a1, a2, batch=b)
    x2 = _mixer_tail(qb, kb, vb, bd, row(gdn_out_norm), x1, ya.reshape(n, ATT_GROUP_WIDTH), row(mix_norm), w_gates,
                     w_branch_a.astype(BF16), w_branch_b.astype(BF16), w_out.astype(BF16), tile=gdn_tile, seq=s)
    x3 = _ffn(x2, row(ffn2_norm), ffn2_w_gate.astype(BF16), ffn2_w_up.astype(BF16), ffn2_w_down.astype(BF16),
              fin_g, final_norm=final_norm, tm=tm_ffn)
    return x3.reshape(b, s, D_MODEL)


def kernel(x, ffn1_norm, ffn1_w_gate, ffn1_w_up, ffn1_w_down, mix_norm, w_in, gdn_conv_w, gdn_a_log, gdn_dt_bias,
           gdn_out_norm, w_branch_a, w_branch_b, w_out, ffn2_norm, ffn2_w_gate, ffn2_w_up, ffn2_w_down, final_norm):
    depth = ffn1_norm.shape[0]
    fin_g = final_norm.reshape(1, -1).astype(F32)
    for layer in range(depth):
        x = _layer(x, ffn1_norm[layer], ffn1_w_gate[layer], ffn1_w_up[layer], ffn1_w_down[layer], mix_norm[layer],
                   w_in[layer], gdn_conv_w[layer], gdn_a_log[layer], gdn_dt_bias[layer], gdn_out_norm[layer],
                   w_branch_a[layer], w_branch_b[layer], w_out[layer], ffn2_norm[layer], ffn2_w_gate[layer],
                   ffn2_w_up[layer], ffn2_w_down[layer], fin_g, final_norm=(layer == depth - 1),
                   tm_ffn=1024, tm_mix=512, gdn_tile=512)
    return x
```

```python
import functools

import jax
import jax.numpy as jnp
from jax import lax
from jax.experimental import pallas as pl
from jax.experimental.pallas import tpu as pltpu

F32 = jnp.float32
BF16 = jnp.bfloat16

D_MODEL = 1024
D_FF = 2816
EPS = 1e-6

ATT_GROUPS = ((128, 1), (512, 4), (2048, 16))
ATT_HEADS_PER_GROUP = 4
ATT_HEAD_DIM = 64
ATT_BLOCK = 128
ATT_GROUP_WIDTH = ATT_HEADS_PER_GROUP * ATT_HEAD_DIM
ATT_QKV_WIDTH = len(ATT_GROUPS) * ATT_GROUP_WIDTH
ROPE_THETA = 10000.0

GDN_HEADS = 8
GDN_HEAD_DIM = 128
GDN_WIDTH = GDN_HEADS * GDN_HEAD_DIM
GDN_CONV = 4
GDN_CHUNK = 64

LANES = 128
SUBLANES = 8
VMEM_LIMIT_BYTES = 56 * 1024 * 1024

W_IN_GDN = 3 * ATT_QKV_WIDTH
W_IN_BD = W_IN_GDN + 3 * GDN_WIDTH
W_IN_GATES = W_IN_BD + 2 * GDN_HEADS
FFN_CHUNKS = ((0, 768), (768, 1536), (1536, 2304), (2304, 2816))
NEG_BIG = -1e30


def _resident(shape):
    nd = len(shape)
    return pl.BlockSpec(shape, lambda *_: (0,) * nd, pipeline_mode=pl.Buffered(1))


def _rmsnorm(x, g):
    return x * lax.rsqrt(jnp.mean(x * x, axis=-1, keepdims=True) + EPS) * g


def _sigmoid(x):
    return 1.0 / (1.0 + jnp.exp(-x))


def _dot(a, b):
    return jnp.dot(a, b, preferred_element_type=F32)


def _dot_nt(a, b):
    return lax.dot_general(a, b, (((1,), (1,)), ((), ())), preferred_element_type=F32)


def _dot_tn(a, b):
    return lax.dot_general(a, b, (((0,), (0,)), ((), ())), preferred_element_type=F32)


def _swiglu_residual(x, g, wg_ref, wu_ref, wd_ref):
    h = _rmsnorm(x, g).astype(BF16)
    acc = x
    for lo, hi in FFN_CHUNKS:
        gate = _dot(h, wg_ref[:, lo:hi])
        up = _dot(h, wu_ref[:, lo:hi])
        act = (0.5 * gate * _sigmoid(gate) * up).astype(BF16)
        acc = acc + _dot(act, wd_ref[lo:hi, :])
    return acc


def _cast_rows(src_refs, dst_refs):
    for src, dst in zip(src_refs, dst_refs):
        dst[...] = src[...].astype(BF16)


def _cast_specs(arrays, steps):
    specs, shapes = [], []
    for arr in arrays:
        packed_rows = 2 * SUBLANES
        parts = next(p for p in range(steps, 0, -1)
                     if arr.shape[0] % p == 0 and (arr.shape[0] // p) % packed_rows == 0)
        specs.append(pl.BlockSpec((arr.shape[0] // parts, arr.shape[1]),
                                  lambda s, parts=parts: (jnp.minimum(s, parts - 1), 0)))
        shapes.append(jax.ShapeDtypeStruct(arr.shape, BF16))
    return specs, shapes


def _ffn_kernel(x_ref, g_ref, wg_ref, wu_ref, wd_ref, fin_ref, *refs, final_norm):
    n_cast = (len(refs) - 1) // 2
    o_ref = refs[n_cast]
    y = _swiglu_residual(x_ref[...], g_ref[...], wg_ref, wu_ref, wd_ref)
    if final_norm:
        y = _rmsnorm(y, fin_ref[...])
    o_ref[...] = y
    _cast_rows(refs[:n_cast], refs[n_cast + 1:])


def _ffn(x, norm_g, wg, wu, wd, fin_g, *, final_norm, tm, cast=()):
    n = x.shape[0]
    steps = n // tm
    row = pl.BlockSpec((tm, D_MODEL), lambda i: (i, 0))
    cast_specs, cast_shapes = _cast_specs(cast, steps)
    out = pl.pallas_call(
        functools.partial(_ffn_kernel, final_norm=final_norm),
        grid=(steps,),
        in_specs=[row, _resident((1, D_MODEL)), _resident((D_MODEL, D_FF)), _resident((D_MODEL, D_FF)),
                  _resident((D_FF, D_MODEL)), _resident((1, D_MODEL))] + cast_specs,
        out_specs=[row] + cast_specs,
        out_shape=[jax.ShapeDtypeStruct((n, D_MODEL), F32)] + cast_shapes,
        compiler_params=pltpu.CompilerParams(dimension_semantics=("arbitrary",),
                                             vmem_limit_bytes=VMEM_LIMIT_BYTES),
        name="ffn_final" if final_norm else "ffn",
    )(x, norm_g, wg, wu, wd, fin_g, *cast)
    return out[0], out[1:]


MIX_BLOCK = 256


def _interleave(primary, n_primary, secondary, n_secondary):
    done = 0
    for i in range(n_primary):
        next(primary)
        while done * n_primary < (i + 1) * n_secondary:
            next(secondary)
            done += 1
    for gen in (primary, secondary):
        for _ in gen:
            pass


def _chunk_time(row):
    return SUBLANES * (row % SUBLANES) + row // SUBLANES


def _attention_qkv_stages(h, wm_ref, cos_ref, sin_ref, att_refs, stage_ref, *, tm):
    cos = cos_ref[...]
    sin = sin_ref[...]
    lane = lax.broadcasted_iota(jnp.int32, (1, LANES), 1)
    first_half = (lane % ATT_HEAD_DIM) < (ATT_HEAD_DIM // 2)
    slot = 0
    for part in range(3):
        for gi, (_, dil) in enumerate(ATT_GROUPS):
            col = part * ATT_QKV_WIDTH + gi * ATT_GROUP_WIDTH
            y = _dot(h, wm_ref[:, col:col + ATT_GROUP_WIDTH])
            yield
            for j in range(ATT_GROUP_WIDTH // LANES):
                blk = y[:, j * LANES:(j + 1) * LANES]
                if part < 2:
                    swapped = jnp.where(first_half, pltpu.roll(blk, LANES - ATT_HEAD_DIM // 2, 1),
                                        pltpu.roll(blk, ATT_HEAD_DIM // 2, 1))
                    blk = blk * cos + swapped * sin
                if part == 0:
                    blk = blk * (ATT_HEAD_DIM ** -0.5)
                dst = part * ATT_GROUP_WIDTH + j * LANES
                if dil == 1:
                    att_refs[gi][:, dst:dst + LANES] = blk.astype(BF16)
                else:
                    stage_ref[slot] = blk
                    for r in range(dil):
                        rows = stage_ref[slot, pl.ds(r, tm // dil, stride=dil), :]
                        lo = r * 3 * ATT_GROUP_WIDTH + dst
                        att_refs[gi][:, lo:lo + LANES] = rows.astype(BF16)
                    slot += 1
                yield


ATT_QKV_STAGES = 3 * len(ATT_GROUPS) * (1 + ATT_GROUP_WIDTH // LANES)
MIXER_IN_STAGED = 3 * (len(ATT_GROUPS) - 1) * (ATT_GROUP_WIDTH // LANES)


def _deltanet_qkv_stages(h_perm, wm_ref, wbd_ref, convw_ref, alog_ref, dtb_ref, gdn_refs, bd_ref, carry_ref, *, tm):
    lane = lax.broadcasted_iota(jnp.int32, (1, LANES), 1)
    raw = _dot(h_perm, wbd_ref[...])
    yield
    z = raw + dtb_ref[...]
    softplus = jnp.maximum(z, 0.0) + jnp.log1p(jnp.exp(-jnp.abs(z)))
    g = -jnp.exp(alog_ref[...]) * softplus
    bd_ref[...] = jnp.where(lane < GDN_HEADS, _sigmoid(raw), jnp.where(lane < 2 * GDN_HEADS, g, 0.0))

    vregs = GDN_CHUNK // SUBLANES
    halo = GDN_CONV - 1
    chunks = tm // GDN_CHUNK
    last_sublane = lax.broadcasted_iota(jnp.int32, (SUBLANES, GDN_HEAD_DIM), 0) == SUBLANES - 1
    heads_per_block = MIX_BLOCK // GDN_HEAD_DIM
    for part in range(3):
        for blk in range(GDN_WIDTH // MIX_BLOCK):
            base = part * GDN_WIDTH + blk * MIX_BLOCK
            y = _dot(h_perm, wm_ref[:, W_IN_GDN + base:W_IN_GDN + base + MIX_BLOCK])
            yield
            for hb in range(heads_per_block):
                hh = blk * heads_per_block + hb
                col = part * GDN_WIDTH + hh * GDN_HEAD_DIM
                cur = y[:, hb * GDN_HEAD_DIM:(hb + 1) * GDN_HEAD_DIM]
                vreg = lambda c, v: cur[c * GDN_CHUNK + v * SUBLANES:c * GDN_CHUNK + (v + 1) * SUBLANES]
                prev_tile = carry_ref[:, col:col + GDN_HEAD_DIM]
                carry_ref[:, col:col + GDN_HEAD_DIM] = cur[tm - halo * SUBLANES:, :]
                wrapped = []
                for c in range(chunks):
                    row = []
                    for i in range(halo):
                        before = (prev_tile[i * SUBLANES:(i + 1) * SUBLANES] if c == 0
                                  else vreg(c - 1, vregs - halo + i))
                        own = vreg(c, vregs - halo + i)
                        row.append(pltpu.roll(jnp.where(last_sublane, before, own), 1, 0))
                    wrapped.append(row)
                acc = cur * convw_ref[halo:halo + 1, col:col + GDN_HEAD_DIM]
                for shift in range(1, GDN_CONV):
                    pieces = []
                    for c in range(chunks):
                        pieces += wrapped[c][halo - shift:]
                        pieces.append(cur[c * GDN_CHUNK:(c + 1) * GDN_CHUNK - shift * SUBLANES])
                    shifted = jnp.concatenate(pieces, axis=0)
                    acc = acc + shifted * convw_ref[halo - shift:halo - shift + 1, col:col + GDN_HEAD_DIM]
                act = acc * _sigmoid(acc)
                if part < 2:
                    act = act * lax.rsqrt(jnp.sum(act * act, axis=-1, keepdims=True) + EPS)
                if part == 0:
                    act = act * (GDN_HEAD_DIM ** -0.5)
                gdn_refs[part][:, hh * GDN_HEAD_DIM:(hh + 1) * GDN_HEAD_DIM] = act
                yield


DELTANET_QKV_STAGES = 1 + 3 * (GDN_WIDTH // MIX_BLOCK) + 3 * GDN_HEADS


def _mixer_in_kernel(x_ref, g_ref, wm_ref, wbd_ref, convw_ref, alog_ref, dtb_ref, cos_ref, sin_ref,
                     a0_ref, a1_ref, a2_ref, qb_ref, kb_ref, vb_ref, bd_ref, carry_ref, perm_ref, stage_ref, *, tm):
    @pl.when(pl.program_id(1) == 0)
    def _():
        carry_ref[...] = jnp.zeros(carry_ref.shape, F32)

    hf = _rmsnorm(x_ref[...], g_ref[...])
    h = hf.astype(BF16)
    vregs = GDN_CHUNK // SUBLANES
    for cb in range(D_MODEL // LANES):
        perm_ref[cb] = hf[:, cb * LANES:(cb + 1) * LANES]
    h_perm = jnp.concatenate(
        [jnp.concatenate([perm_ref[cb, pl.ds(c0 + v, SUBLANES, stride=vregs), :]
                          for c0 in range(0, tm, GDN_CHUNK) for v in range(vregs)], axis=0)
         for cb in range(D_MODEL // LANES)], axis=1).astype(BF16)
    gdn = _deltanet_qkv_stages(h_perm, wm_ref, wbd_ref, convw_ref, alog_ref, dtb_ref, (qb_ref, kb_ref, vb_ref), bd_ref,
                               carry_ref, tm=tm)
    att = _attention_qkv_stages(h, wm_ref, cos_ref, sin_ref, (a0_ref, a1_ref, a2_ref), stage_ref, tm=tm)
    _interleave(gdn, DELTANET_QKV_STAGES, att, ATT_QKV_STAGES)


def _mixer_in(x1, norm_g, w_in, conv_w, a_log, dt_bias, cos_t, sin_t, *, tm, seq):
    n = x1.shape[0]
    tiles_per_seq = seq // tm
    tile = lambda rows, w: pl.BlockSpec((rows, w), lambda bi, i: (bi * tiles_per_seq + i, 0))
    table = pl.BlockSpec((tm, LANES), lambda bi, i: (i, 0))
    wq = 3 * ATT_GROUP_WIDTH
    att_specs = [tile(tm // dil, dil * wq) for _, dil in ATT_GROUPS]
    att_shapes = [jax.ShapeDtypeStruct((n // dil, dil * wq), BF16) for _, dil in ATT_GROUPS]
    gdn = jax.ShapeDtypeStruct((n, GDN_WIDTH), F32)
    return pl.pallas_call(
        functools.partial(_mixer_in_kernel, tm=tm),
        grid=(n // seq, tiles_per_seq),
        in_specs=[tile(tm, D_MODEL), _resident((1, D_MODEL)),
                  pl.BlockSpec((D_MODEL, W_IN_BD), lambda bi, i: (0, 0), pipeline_mode=pl.Buffered(1)),
                  pl.BlockSpec((D_MODEL, LANES), lambda bi, i: (0, W_IN_BD // LANES), pipeline_mode=pl.Buffered(1)),
                  _resident((GDN_CONV, 3 * GDN_WIDTH)), _resident((1, LANES)), _resident((1, LANES)), table, table],
        out_specs=att_specs + [tile(tm, GDN_WIDTH)] * 3 + [tile(tm, LANES)],
        out_shape=att_shapes + [gdn] * 3 + [jax.ShapeDtypeStruct((n, LANES), F32)],
        scratch_shapes=[pltpu.VMEM(((GDN_CONV - 1) * SUBLANES, 3 * GDN_WIDTH), F32),
                        pltpu.VMEM((D_MODEL // LANES, tm, LANES), F32), pltpu.VMEM((MIXER_IN_STAGED, tm, LANES), F32)],
        compiler_params=pltpu.CompilerParams(dimension_semantics=("arbitrary", "arbitrary"),
                                             vmem_limit_bytes=VMEM_LIMIT_BYTES),
        name="mixer_in",
    )(x1, norm_g, w_in, w_in, conv_w, a_log, dt_bias, cos_t, sin_t)


ATT_BATCH = 3


def _attention_blocks(items):
    lane = lax.broadcasted_iota(jnp.int32, (1, ATT_GROUP_WIDTH), 1)
    heads = range(ATT_HEADS_PER_GROUP)
    in_head = [(lane // ATT_HEAD_DIM) == hh for hh in heads]
    keep = [jnp.where(in_head[hh], 1.0, 0.0).astype(BF16) for hh in heads]
    nq = ATT_BLOCK
    s_all = [_dot_nt(jnp.concatenate([q * keep[hh] for hh in heads], axis=0), k) for q, k, _, _ in items]
    stats, p_all = [], []
    for (_, _, _, valid), sa in zip(items, s_all):
        s = [jnp.where(valid, sa[hh * nq:(hh + 1) * nq], NEG_BIG) for hh in heads]
        m = [jnp.max(s[hh], axis=-1, keepdims=True) for hh in heads]
        p = [jnp.exp(s[hh] - m[hh]) for hh in heads]
        l = [jnp.sum(p[hh], axis=-1, keepdims=True) for hh in heads]
        stats.append((m, l))
        p_all.append(jnp.concatenate([p[hh].astype(BF16) for hh in heads], axis=0))
    pv_all = [_dot(ps, v) for ps, (_, _, v, _) in zip(p_all, items)]
    outs = []
    for pv, (m, l) in zip(pv_all, stats):
        o = jnp.zeros((ATT_BLOCK, ATT_GROUP_WIDTH), F32)
        lse = jnp.zeros((ATT_BLOCK, ATT_GROUP_WIDTH), F32)
        for hh in heads:
            o = jnp.where(in_head[hh], pv[hh * nq:(hh + 1) * nq] * (1.0 / l[hh]), o)
            lse = jnp.where(in_head[hh], m[hh] + jnp.log(l[hh]), lse)
        outs.append((o, lse))
    return outs


def _attention_kernel(a0_ref, a1_ref, a2_ref, ya_ref, o0, l0, o1, l1, o2, l2, *, seq):
    in_refs = (a0_ref, a1_ref, a2_ref)
    o_refs = (o0, o1, o2)
    l_refs = (l0, l1, l2)
    qi = lax.broadcasted_iota(jnp.int32, (ATT_BLOCK, ATT_BLOCK), 0)
    kj = lax.broadcasted_iota(jnp.int32, (ATT_BLOCK, ATT_BLOCK), 1)
    causal = kj <= qi
    qi2 = lax.broadcasted_iota(jnp.int32, (ATT_BLOCK, 2 * ATT_BLOCK), 0)
    kj2 = lax.broadcasted_iota(jnp.int32, (ATT_BLOCK, 2 * ATT_BLOCK), 1)
    band = (kj2 >= qi2) & (kj2 - ATT_BLOCK <= qi2)
    wq = 3 * ATT_GROUP_WIDTH

    def load(gi, r, n):
        src, base = in_refs[gi], r * wq
        if isinstance(n, int) and n == 0:
            qrows = krows = slice(0, ATT_BLOCK)
            valid = causal
        else:
            start = lambda x: x if isinstance(x, int) else pl.multiple_of(x, ATT_BLOCK)
            qrows = pl.ds(start(n * ATT_BLOCK), ATT_BLOCK)
            krows = pl.ds(start((n - 1) * ATT_BLOCK), 2 * ATT_BLOCK)
            valid = band
        return (src[0, qrows, base:base + ATT_GROUP_WIDTH],
                src[0, krows, base + ATT_GROUP_WIDTH:base + 2 * ATT_GROUP_WIDTH],
                src[0, krows, base + 2 * ATT_GROUP_WIDTH:base + 3 * ATT_GROUP_WIDTH], valid)

    def store(gi, r, n, o, lse):
        dil = ATT_GROUPS[gi][1]
        if dil == 1:
            first = n * ATT_BLOCK
            rows = pl.ds(first if isinstance(first, int) else pl.multiple_of(first, ATT_BLOCK), ATT_BLOCK)
        else:
            rows = pl.ds(n * ATT_BLOCK * dil + r, ATT_BLOCK, stride=dil)
        for half in range(ATT_GROUP_WIDTH // LANES):
            o_refs[gi][half, rows, :] = o[:, half * LANES:(half + 1) * LANES]
            l_refs[gi][half, rows, :] = lse[:, half * LANES:(half + 1) * LANES]

    def run(blocks):
        for (gi, r, n), (o, lse) in zip(blocks, _attention_blocks([load(*blk) for blk in blocks])):
            store(gi, r, n, o, lse)

    static_blocks = []
    looped = None
    for gi, (window, dil) in enumerate(ATT_GROUPS):
        assert window // dil == ATT_BLOCK
        nblk = seq // dil // ATT_BLOCK
        if dil == 1 and (nblk - 1) % ATT_BATCH == 0:
            static_blocks.append((gi, 0, 0))
            looped = (gi, nblk)
        else:
            static_blocks += [(gi, r, n) for r in range(dil) for n in range(nblk)]
    for i in range(0, len(static_blocks), ATT_BATCH + 1):
        run(static_blocks[i:i + ATT_BATCH + 1])
    if looped is not None:
        gi, nblk = looped

        def body(i, carry):
            run([(gi, 0, 1 + i * ATT_BATCH + j) for j in range(ATT_BATCH)])
            return carry
        lax.fori_loop(0, (nblk - 1) // ATT_BATCH, body, 0)

    rows_per_step = 256

    def merge(i, carry):
        rows = pl.ds(pl.multiple_of(i * rows_per_step, rows_per_step), rows_per_step)
        for half in range(ATT_GROUP_WIDTH // LANES):
            la, lb, lc = l0[half, rows, :], l1[half, rows, :], l2[half, rows, :]
            m = jnp.maximum(jnp.maximum(la, lb), lc)
            ea, eb, ec = jnp.exp(la - m), jnp.exp(lb - m), jnp.exp(lc - m)
            num = ea * o0[half, rows, :] + eb * o1[half, rows, :] + ec * o2[half, rows, :]
            ya_ref[0, rows, half * LANES:(half + 1) * LANES] = num / (ea + eb + ec)
        return carry
    lax.fori_loop(0, seq // rows_per_step, merge, 0)


def _attention(a0, a1, a2, *, batch):
    views = tuple(a.reshape(batch, a.shape[0] // batch, a.shape[1]) for a in (a0, a1, a2))
    b, s, _ = views[0].shape
    specs = [pl.BlockSpec((1,) + arr.shape[1:], lambda bi: (bi, 0, 0)) for arr in views]
    scratch = [pltpu.VMEM((ATT_GROUP_WIDTH // LANES, s, LANES), F32) for _ in range(6)]
    return pl.pallas_call(
        functools.partial(_attention_kernel, seq=s),
        grid=(b,),
        in_specs=specs,
        out_specs=pl.BlockSpec((1, s, ATT_GROUP_WIDTH), lambda bi: (bi, 0, 0)),
        out_shape=jax.ShapeDtypeStruct((b, s, ATT_GROUP_WIDTH), F32),
        scratch_shapes=scratch,
        compiler_params=pltpu.CompilerParams(dimension_semantics=("arbitrary",),
                                             vmem_limit_bytes=VMEM_LIMIT_BYTES),
        name="dilated_attention",
    )(*views)


def _deltanet_stages(q_ref, k_ref, v_ref, bd_ref, gnorm, state_ref, ob_ref, slot, *, tile):
    c = GDN_CHUNK
    d = GDN_HEAD_DIM
    heads = range(GDN_HEADS)
    pairs = [(2 * pp, 2 * pp + 1) for pp in range(GDN_HEADS // 2)]
    cols = [slice(hh * d, (hh + 1) * d) for hh in heads]
    glane = [GDN_HEADS + hh for hh in heads]
    ii = _chunk_time(lax.broadcasted_iota(jnp.int32, (c, 2 * c), 0))
    ll = lax.broadcasted_iota(jnp.int32, (c, 2 * c), 1)
    jj = _chunk_time(ll % c)
    lower = ii >= jj
    strict = ii > jj
    left = ll < c
    left_row = lax.broadcasted_iota(jnp.int32, (1, 2 * c), 1) < c
    keep_left = jnp.where(left, 1.0, 0.0).astype(BF16)
    keep_right = jnp.where(left, 0.0, 1.0).astype(BF16)
    ti = _chunk_time(lax.broadcasted_iota(jnp.int32, (c, c), 0))
    tj = _chunk_time(lax.broadcasted_iota(jnp.int32, (c, c), 1))
    tri_ones = jnp.where(ti >= tj, 1.0, 0.0).astype(BF16)

    def blockdiag(x):
        return jnp.concatenate([x * keep_left, x * keep_right], axis=0)

    def stack_diag(xa, xb):
        zero = jnp.zeros_like(xa)
        return jnp.concatenate([jnp.concatenate([xa, zero], axis=1), jnp.concatenate([zero, xb], axis=1)], axis=0)

    def prepare(ci, out):
        rows = slice(ci * c, (ci + 1) * c)
        bd = bd_ref[rows, :]
        bd_hi = bd.astype(BF16)
        bd_rest = bd - bd_hi.astype(F32)
        bd_mid = bd_rest.astype(BF16)
        bd_lo = (bd_rest - bd_mid.astype(F32)).astype(BF16)
        gcum = _dot(tri_ones, bd_hi) + _dot(tri_ones, bd_mid) + _dot(tri_ones, bd_lo)
        yield
        gtot = jnp.broadcast_to(gcum[c - 1:c, :], (c, LANES))
        gcum_t = jnp.concatenate([gcum, gcum], axis=0).T
        e_cum_all = jnp.exp(gcum)
        e_rest_all = jnp.exp(gtot - gcum)
        e_tot_all = jnp.exp(gtot)
        q = [q_ref[rows, cols[hh]] for hh in heads]
        k = [k_ref[rows, cols[hh]] for hh in heads]
        v = [v_ref[rows, cols[hh]] for hh in heads]
        beta = [bd[:, hh:hh + 1] for hh in heads]
        e_cum = [e_cum_all[:, gl:gl + 1] for gl in glane]
        kbeta = [k[hh] * beta[hh] for hh in heads]
        kq = [_dot_nt(jnp.concatenate([jnp.concatenate([kbeta[a], kbeta[b]], axis=1),
                                       jnp.concatenate([q[a], q[b]], axis=1)], axis=0).astype(BF16),
                      stack_diag(k[a].astype(BF16), k[b].astype(BF16)))
              for a, b in pairs]
        yield
        decay = [jnp.exp(jnp.where(lower,
                                   jnp.where(left, gcum[:, glane[a]:glane[a] + 1], gcum[:, glane[b]:glane[b] + 1])
                                   - jnp.where(left_row, gcum_t[glane[a]:glane[a] + 1, :], gcum_t[glane[b]:glane[b] + 1, :]),
                                   NEG_BIG)) for a, b in pairs]
        m = [jnp.where(strict, kq[pp][0:c] * decay[pp], 0.0) for pp in range(len(pairs))]
        n = [-mm for mm in m]
        pb = [mm.astype(BF16) for mm in m]
        p = [_dot(x, blockdiag(x)) for x in pb]
        yield
        rounds = 5
        for r in range(rounds):
            pb = [x.astype(BF16) for x in p]
            upd = [_dot(x, blockdiag(y.astype(BF16))) for x, y in zip(pb, n)]
            p_next = [_dot(x, blockdiag(x)) for x in pb] if r + 1 < rounds else None
            yield
            n = [y + x + u for y, x, u in zip(n, p, upd)]
            p = p_next
        rhs = [jnp.concatenate([v[hh] * beta[hh], kbeta[hh] * e_cum[hh]], axis=1) for hh in heads]
        nr = [_dot(n[pp].astype(BF16), stack_diag(rhs[a].astype(BF16), rhs[b].astype(BF16)))
              for pp, (a, b) in enumerate(pairs)]
        yield
        sol = [rhs[hh] + nr[hh // 2][:, (hh % 2) * 2 * d:(hh % 2 + 1) * 2 * d] for hh in heads]
        out.update(
            first=ci * c,
            u=[sol[hh][:, 0:d] for hh in heads],
            wq=[jnp.concatenate([sol[hh][:, d:2 * d], q[hh] * e_cum[hh]], axis=0).astype(BF16) for hh in heads],
            a_qk=[(kq[pp][c:2 * c] * decay[pp]).astype(BF16) for pp in range(len(pairs))],
            k_dec=[(k[hh] * e_rest_all[:, gl:gl + 1]).astype(BF16) for hh, gl in zip(heads, glane)],
            e_tot=[e_tot_all[0:1, gl:gl + 1] for gl in glane])

    for first in range(0, tile // c, GDN_GROUP):
        group = [dict() for _ in range(GDN_GROUP)]
        gens = [prepare(first + gi, group[gi]) for gi in range(GDN_GROUP)]
        for _ in range(GDN_PREP_LAYERS):
            for gen in gens:
                next(gen)
            yield
        for gen in gens:
            for _ in gen:
                pass
        for pre in group:
            state = [state_ref[hh] for hh in heads]
            ws = [_dot(pre["wq"][hh], state[hh].astype(BF16)) for hh in heads]
            yield
            v_new = [(pre["u"][hh] - ws[hh][0:c]).astype(BF16) for hh in heads]
            kv = [_dot_tn(pre["k_dec"][hh], v_new[hh]) for hh in heads]
            av = [_dot(pre["a_qk"][pp], stack_diag(v_new[a], v_new[b])) for pp, (a, b) in enumerate(pairs)]
            yield
            for hh in heads:
                state_ref[hh] = state[hh] * pre["e_tot"][hh] + kv[hh]
            for hh in heads:
                o = _rmsnorm(ws[hh][c:2 * c] + av[hh // 2][:, (hh % 2) * d:(hh % 2 + 1) * d], gnorm)
                for vv in range(c // SUBLANES):
                    ob_ref[slot, hh, pl.ds(pre["first"] + vv, SUBLANES, stride=c // SUBLANES), :] = (
                        o[vv * SUBLANES:(vv + 1) * SUBLANES])


GDN_GROUP = 4
GDN_PREP_LAYERS = 9
GDN_LAYERS_PER_GROUP = GDN_PREP_LAYERS + 2 * GDN_GROUP


def _mixer_out_stages(x_ref, ya_ref, ob_ref, slot, g_ref, wgt_ref, wa_ref, wb_ref, wo_ref, o_ref):
    x = x_ref[...]
    h = _rmsnorm(x, g_ref[...]).astype(BF16)
    ya = ya_ref[...].astype(BF16)
    blocks = [slice(j * MIX_BLOCK, (j + 1) * MIX_BLOCK) for j in range(D_MODEL // MIX_BLOCK)]
    gate_cols = lambda which, blk: slice(which * D_MODEL + blk.start, which * D_MODEL + blk.stop)
    yb = []
    for blk in blocks:
        gdn_gate = _dot(h, wgt_ref[:, gate_cols(0, blk)])
        yield
        ob = jnp.concatenate([ob_ref[slot, hh] for hh in range(blk.start // GDN_HEAD_DIM, blk.stop // GDN_HEAD_DIM)],
                             axis=1)
        yb.append((ob * (gdn_gate * _sigmoid(gdn_gate))).astype(BF16))
    yb = jnp.concatenate(yb, axis=1)
    merged = []
    for blk in blocks:
        gate_a = _dot(h, wgt_ref[:, gate_cols(1, blk)])
        branch_a = _dot(ya, wa_ref[:, blk])
        yield
        gate_b = _dot(h, wgt_ref[:, gate_cols(2, blk)])
        yield
        branch_b = _dot(yb, wb_ref[:, blk])
        yield
        merged.append((_sigmoid(gate_a) * branch_a + _sigmoid(gate_b) * branch_b).astype(BF16))
    merged = jnp.concatenate(merged, axis=1)
    for blk in blocks:
        o_ref[:, blk] = x[:, blk] + _dot(merged, wo_ref[:, blk])
        yield


MIX_GRANULES = 5 * (D_MODEL // MIX_BLOCK)


def _mixer_tail_kernel(q_ref, k_ref, v_ref, bd_ref, gn_ref, x_ref, ya_ref, g_ref, wgt_ref, wa_ref, wb_ref, wo_ref,
                       *refs, tile, tiles_per_seq, n_tiles):
    n_cast = (len(refs) - 3) // 2
    o_ref, state_ref, ob_ref = refs[n_cast], refs[2 * n_cast + 1], refs[2 * n_cast + 2]
    step = pl.program_id(0)

    @pl.when(step == 0)
    def _():
        ob_ref[...] = jnp.zeros(ob_ref.shape, F32)

    @pl.when(jnp.minimum(step, n_tiles - 1) % tiles_per_seq == 0)
    def _():
        state_ref[...] = jnp.zeros(state_ref.shape, F32)

    slot = step % 2
    gdn = _deltanet_stages(q_ref, k_ref, v_ref, bd_ref, gn_ref[...], state_ref, ob_ref, slot, tile=tile)
    mix = _mixer_out_stages(x_ref, ya_ref, ob_ref, 1 - slot, g_ref, wgt_ref, wa_ref, wb_ref, wo_ref, o_ref)
    _interleave(gdn, GDN_LAYERS_PER_GROUP * (tile // (GDN_CHUNK * GDN_GROUP)), mix, MIX_GRANULES)
    _cast_rows(refs[:n_cast], refs[n_cast + 1:2 * n_cast + 1])


def _mixer_tail(qb, kb, vb, bd, out_norm, x1, ya, norm_g, w_gates, w_a, w_b, w_o, *, tile, seq, cast=()):
    n = x1.shape[0]
    n_tiles = n // tile
    cast_specs, cast_shapes = _cast_specs(cast, n_tiles)
    cur = lambda w: pl.BlockSpec((tile, w), lambda s: (jnp.minimum(s, n_tiles - 1), 0))
    prev = lambda w: pl.BlockSpec((tile, w), lambda s: (jnp.maximum(s - 1, 0), 0))
    out = pl.pallas_call(
        functools.partial(_mixer_tail_kernel, tile=tile, tiles_per_seq=seq // tile, n_tiles=n_tiles),
        grid=(n_tiles + 1,),
        in_specs=[cur(GDN_WIDTH), cur(GDN_WIDTH), cur(GDN_WIDTH), cur(LANES), _resident((1, GDN_HEAD_DIM)),
                  prev(D_MODEL), prev(ATT_GROUP_WIDTH), _resident((1, D_MODEL)),
                  _resident(w_gates.shape), _resident(w_a.shape), _resident(w_b.shape), _resident(w_o.shape)]
                 + cast_specs,
        out_specs=[prev(D_MODEL)] + cast_specs,
        out_shape=[jax.ShapeDtypeStruct((n, D_MODEL), F32)] + cast_shapes,
        scratch_shapes=[pltpu.VMEM((GDN_HEADS, GDN_HEAD_DIM, GDN_HEAD_DIM), F32),
                        pltpu.VMEM((2, GDN_HEADS, tile, GDN_HEAD_DIM), F32)],
        compiler_params=pltpu.CompilerParams(dimension_semantics=("arbitrary",),
                                             vmem_limit_bytes=VMEM_LIMIT_BYTES),
        name="deltanet_mixer_out",
    )(qb, kb, vb, bd, out_norm, x1, ya, norm_g, w_gates, w_a, w_b, w_o, *cast)
    return out[0], out[1:]


def _rope_tables(seq):
    half = ATT_HEAD_DIM // 2
    inv_freq = ROPE_THETA ** (-jnp.arange(half, dtype=F32) / half)
    ang = jnp.arange(seq, dtype=F32)[:, None] * inv_freq[None, :]
    cos, sin = jnp.cos(ang), jnp.sin(ang)
    reps = LANES // ATT_HEAD_DIM
    return jnp.tile(jnp.concatenate([cos, cos], axis=-1), (1, reps)), jnp.tile(jnp.concatenate([-sin, sin], axis=-1), (1, reps))


def _pad_lanes(row, offset):
    return jnp.zeros((1, LANES), F32).at[0, offset:offset + row.shape[0]].set(row.astype(F32))


def _layer(x, ffn1_norm, ffn1_w_gate, ffn1_w_up, ffn1_w_down, mix_norm, w_in, gdn_conv_w, gdn_a_log, gdn_dt_bias,
           gdn_out_norm, w_branch_a, w_branch_b, w_out, ffn2_norm, ffn2_w_gate, ffn2_w_up, ffn2_w_down, fin_g,
           *, final_norm, tm_ffn, tm_mix, gdn_tile):
    b, s, _ = x.shape
    n = b * s
    row = lambda v: v.reshape(1, -1).astype(F32)
    x1, (w_in,) = _ffn(x.reshape(n, D_MODEL), row(ffn1_norm), ffn1_w_gate.astype(BF16), ffn1_w_up.astype(BF16),
                       ffn1_w_down.astype(BF16), fin_g, final_norm=False, tm=tm_ffn, cast=(w_in,))
    w_gates = w_in[:, W_IN_GATES:]
    cos_t, sin_t = _rope_tables(s)
    a0, a1, a2, qb, kb, vb, bd = _mixer_in(
        x1, row(mix_norm), w_in, gdn_conv_w.astype(F32), _pad_lanes(gdn_a_log, GDN_HEADS),
        _pad_lanes(gdn_dt_bias, GDN_HEADS), cos_t, sin_t, tm=tm_mix, seq=s)

    ya = _attention(a0, a1, a2, batch=b)
    x2, ffn2_w = _mixer_tail(qb, kb, vb, bd, row(gdn_out_norm), x1, ya.reshape(n, ATT_GROUP_WIDTH), row(mix_norm),
                             w_gates, w_branch_a.astype(BF16), w_branch_b.astype(BF16), w_out.astype(BF16),
                             tile=gdn_tile, seq=s, cast=(ffn2_w_gate, ffn2_w_up, ffn2_w_down))
    x3, _ = _ffn(x2, row(ffn2_norm), *ffn2_w, fin_g, final_norm=final_norm, tm=tm_ffn)
    return x3.reshape(b, s, D_MODEL)


def kernel(x, ffn1_norm, ffn1_w_gate, ffn1_w_up, ffn1_w_down, mix_norm, w_in, gdn_conv_w, gdn_a_log, gdn_dt_bias,
           gdn_out_norm, w_branch_a, w_branch_b, w_out, ffn2_norm, ffn2_w_gate, ffn2_w_up, ffn2_w_down, final_norm):
    depth = ffn1_norm.shape[0]
    fin_g = final_norm.reshape(1, -1).astype(F32)
    for layer in range(depth):
        x = _layer(x, ffn1_norm[layer], ffn1_w_gate[layer], ffn1_w_up[layer], ffn1_w_down[layer], mix_norm[layer],
                   w_in[layer], gdn_conv_w[layer], gdn_a_log[layer], gdn_dt_bias[layer], gdn_out_norm[layer],
                   w_branch_a[layer], w_branch_b[layer], w_out[layer], ffn2_norm[layer], ffn2_w_gate[layer],
                   ffn2_w_up[layer], ffn2_w_down[layer], fin_g, final_norm=(layer == depth - 1),
                   tm_ffn=512, tm_mix=512, gdn_tile=512)
    return x
```

```python
import functools

import jax
import jax.numpy as jnp
from jax import lax
from jax.experimental import pallas as pl
from jax.experimental.pallas import tpu as pltpu

F32 = jnp.float32
BF16 = jnp.bfloat16

D_MODEL = 1024
D_FF = 2816
EPS = 1e-6

ATT_GROUPS = ((128, 1), (512, 4), (2048, 16))
ATT_HEADS_PER_GROUP = 4
ATT_HEAD_DIM = 64
ATT_BLOCK = 128
ATT_GROUP_WIDTH = ATT_HEADS_PER_GROUP * ATT_HEAD_DIM
ATT_QKV_WIDTH = len(ATT_GROUPS) * ATT_GROUP_WIDTH
ROPE_THETA = 10000.0

GDN_HEADS = 8
GDN_HEAD_DIM = 128
GDN_WIDTH = GDN_HEADS * GDN_HEAD_DIM
GDN_CONV = 4
GDN_CHUNK = 64

LANES = 128
SUBLANES = 8
VMEM_LIMIT_BYTES = 56 * 1024 * 1024

W_IN_GDN = 3 * ATT_QKV_WIDTH
W_IN_BD = W_IN_GDN + 3 * GDN_WIDTH
W_IN_GATES = W_IN_BD + 2 * GDN_HEADS
FFN_CHUNKS = ((0, 768), (768, 1536), (1536, 2304), (2304, 2816))
NEG_BIG = -1e30


def _resident(shape):
    nd = len(shape)
    return pl.BlockSpec(shape, lambda *_: (0,) * nd, pipeline_mode=pl.Buffered(1))


def _rmsnorm(x, g):
    return x * lax.rsqrt(jnp.mean(x * x, axis=-1, keepdims=True) + EPS) * g


def _sigmoid(x):
    return 1.0 / (1.0 + jnp.exp(-x))


def _dot(a, b):
    return jnp.dot(a, b, preferred_element_type=F32)


def _dot_nt(a, b):
    return lax.dot_general(a, b, (((1,), (1,)), ((), ())), preferred_element_type=F32)


def _dot_tn(a, b):
    return lax.dot_general(a, b, (((0,), (0,)), ((), ())), preferred_element_type=F32)


def _swiglu_residual(x, g, wg_ref, wu_ref, wd_ref):
    h = _rmsnorm(x, g).astype(BF16)
    acc = x
    for lo, hi in FFN_CHUNKS:
        gate = _dot(h, wg_ref[:, lo:hi])
        up = _dot(h, wu_ref[:, lo:hi])
        act = (0.5 * gate * _sigmoid(gate) * up).astype(BF16)
        acc = acc + _dot(act, wd_ref[lo:hi, :])
    return acc


def _cast_rows(src_refs, dst_refs):
    for src, dst in zip(src_refs, dst_refs):
        dst[...] = src[...].astype(BF16)


def _cast_specs(arrays, layer, steps):
    in_specs, out_specs, shapes = [], [], []
    for arr in arrays:
        _, n_rows, n_cols = arr.shape
        packed_rows = 2 * SUBLANES
        parts = next(p for p in range(steps, 0, -1) if n_rows % p == 0 and (n_rows // p) % packed_rows == 0)
        in_specs.append(pl.BlockSpec((None, n_rows // parts, n_cols),
                                     lambda s, parts=parts: (layer, jnp.minimum(s, parts - 1), 0)))
        out_specs.append(pl.BlockSpec((n_rows // parts, n_cols), lambda s, parts=parts: (jnp.minimum(s, parts - 1), 0)))
        shapes.append(jax.ShapeDtypeStruct((n_rows, n_cols), BF16))
    return in_specs, out_specs, shapes


def _ffn_kernel(x_ref, g_ref, wg_ref, wu_ref, wd_ref, fin_ref, *refs, final_norm):
    n_cast = (len(refs) - 1) // 2
    o_ref = refs[n_cast]
    y = _swiglu_residual(x_ref[...], g_ref[...], wg_ref, wu_ref, wd_ref)
    if final_norm:
        y = _rmsnorm(y, fin_ref[...])
    o_ref[...] = y
    _cast_rows(refs[:n_cast], refs[n_cast + 1:])


def _ffn(x, norm_g, wg, wu, wd, fin_g, *, final_norm, tm, cast=(), layer=0):
    n = x.shape[0]
    steps = n // tm
    row = pl.BlockSpec((tm, D_MODEL), lambda i: (i, 0))
    cast_in, cast_out, cast_shapes = _cast_specs(cast, layer, steps)
    out = pl.pallas_call(
        functools.partial(_ffn_kernel, final_norm=final_norm),
        grid=(steps,),
        in_specs=[row, _resident((1, D_MODEL)), _resident((D_MODEL, D_FF)), _resident((D_MODEL, D_FF)),
                  _resident((D_FF, D_MODEL)), _resident((1, D_MODEL))] + cast_in,
        out_specs=[row] + cast_out,
        out_shape=[jax.ShapeDtypeStruct((n, D_MODEL), F32)] + cast_shapes,
        compiler_params=pltpu.CompilerParams(dimension_semantics=("arbitrary",),
                                             vmem_limit_bytes=VMEM_LIMIT_BYTES),
        name="ffn_final" if final_norm else "ffn",
    )(x, norm_g, wg, wu, wd, fin_g, *cast)
    return out[0], out[1:]


MIX_BLOCK = 256


def _interleave(primary, n_primary, secondary, n_secondary):
    done = 0
    for i in range(n_primary):
        next(primary)
        while done * n_primary < (i + 1) * n_secondary:
            next(secondary)
            done += 1
    for gen in (primary, secondary):
        for _ in gen:
            pass


def _chunk_time(row):
    return SUBLANES * (row % SUBLANES) + row // SUBLANES


def _attention_qkv_stages(h, wm_ref, cos_ref, sin_ref, att_refs, stage_ref, *, tm):
    cos = cos_ref[...]
    sin = sin_ref[...]
    lane = lax.broadcasted_iota(jnp.int32, (1, LANES), 1)
    first_half = (lane % ATT_HEAD_DIM) < (ATT_HEAD_DIM // 2)
    slot = 0
    for part in range(3):
        for gi, (_, dil) in enumerate(ATT_GROUPS):
            col = part * ATT_QKV_WIDTH + gi * ATT_GROUP_WIDTH
            y = _dot(h, wm_ref[:, col:col + ATT_GROUP_WIDTH])
            yield
            for j in range(ATT_GROUP_WIDTH // LANES):
                blk = y[:, j * LANES:(j + 1) * LANES]
                if part < 2:
                    swapped = jnp.where(first_half, pltpu.roll(blk, LANES - ATT_HEAD_DIM // 2, 1),
                                        pltpu.roll(blk, ATT_HEAD_DIM // 2, 1))
                    blk = blk * cos + swapped * sin
                if part == 0:
                    blk = blk * (ATT_HEAD_DIM ** -0.5)
                dst = part * ATT_GROUP_WIDTH + j * LANES
                if dil == 1:
                    att_refs[gi][:, dst:dst + LANES] = blk.astype(BF16)
                else:
                    stage_ref[slot] = blk
                    for r in range(dil):
                        rows = stage_ref[slot, pl.ds(r, tm // dil, stride=dil), :]
                        lo = r * 3 * ATT_GROUP_WIDTH + dst
                        att_refs[gi][:, lo:lo + LANES] = rows.astype(BF16)
                    slot += 1
                yield


ATT_QKV_STAGES = 3 * len(ATT_GROUPS) * (1 + ATT_GROUP_WIDTH // LANES)
MIXER_IN_STAGED = 3 * (len(ATT_GROUPS) - 1) * (ATT_GROUP_WIDTH // LANES)


def _deltanet_qkv_stages(h_perm, wm_ref, wbd_ref, convw_ref, alog_ref, dtb_ref, gdn_refs, bd_ref, carry_ref, *, tm):
    lane = lax.broadcasted_iota(jnp.int32, (1, LANES), 1)
    raw = _dot(h_perm, wbd_ref[...])
    yield
    z = raw + dtb_ref[...]
    softplus = jnp.maximum(z, 0.0) + jnp.log1p(jnp.exp(-jnp.abs(z)))
    g = -jnp.exp(alog_ref[...]) * softplus
    bd_ref[...] = jnp.where(lane < GDN_HEADS, _sigmoid(raw), jnp.where(lane < 2 * GDN_HEADS, g, 0.0))

    vregs = GDN_CHUNK // SUBLANES
    halo = GDN_CONV - 1
    chunks = tm // GDN_CHUNK
    last_sublane = lax.broadcasted_iota(jnp.int32, (SUBLANES, GDN_HEAD_DIM), 0) == SUBLANES - 1
    heads_per_block = MIX_BLOCK // GDN_HEAD_DIM
    for part in range(3):
        for blk in range(GDN_WIDTH // MIX_BLOCK):
            base = part * GDN_WIDTH + blk * MIX_BLOCK
            y = _dot(h_perm, wm_ref[:, W_IN_GDN + base:W_IN_GDN + base + MIX_BLOCK])
            yield
            for hb in range(heads_per_block):
                hh = blk * heads_per_block + hb
                col = part * GDN_WIDTH + hh * GDN_HEAD_DIM
                cur = y[:, hb * GDN_HEAD_DIM:(hb + 1) * GDN_HEAD_DIM]
                vreg = lambda c, v: cur[c * GDN_CHUNK + v * SUBLANES:c * GDN_CHUNK + (v + 1) * SUBLANES]
                prev_tile = carry_ref[:, col:col + GDN_HEAD_DIM]
                carry_ref[:, col:col + GDN_HEAD_DIM] = cur[tm - halo * SUBLANES:, :]
                wrapped = []
                for c in range(chunks):
                    row = []
                    for i in range(halo):
                        before = (prev_tile[i * SUBLANES:(i + 1) * SUBLANES] if c == 0
                                  else vreg(c - 1, vregs - halo + i))
                        own = vreg(c, vregs - halo + i)
                        row.append(pltpu.roll(jnp.where(last_sublane, before, own), 1, 0))
                    wrapped.append(row)
                acc = cur * convw_ref[halo:halo + 1, col:col + GDN_HEAD_DIM]
                for shift in range(1, GDN_CONV):
                    pieces = []
                    for c in range(chunks):
                        pieces += wrapped[c][halo - shift:]
                        pieces.append(cur[c * GDN_CHUNK:(c + 1) * GDN_CHUNK - shift * SUBLANES])
                    shifted = jnp.concatenate(pieces, axis=0)
                    acc = acc + shifted * convw_ref[halo - shift:halo - shift + 1, col:col + GDN_HEAD_DIM]
                act = acc * _sigmoid(acc)
                if part < 2:
                    act = act * lax.rsqrt(jnp.sum(act * act, axis=-1, keepdims=True) + EPS)
                if part == 0:
                    act = act * (GDN_HEAD_DIM ** -0.5)
                gdn_refs[part][:, hh * GDN_HEAD_DIM:(hh + 1) * GDN_HEAD_DIM] = act
                yield


DELTANET_QKV_STAGES = 1 + 3 * (GDN_WIDTH // MIX_BLOCK) + 3 * GDN_HEADS


def _mixer_in_kernel(x_ref, g_ref, wm_ref, wbd_ref, convw_ref, alog_ref, dtb_ref, cos_ref, sin_ref,
                     a0_ref, a1_ref, a2_ref, qb_ref, kb_ref, vb_ref, bd_ref, carry_ref, perm_ref, stage_ref, *, tm):
    @pl.when(pl.program_id(1) == 0)
    def _():
        carry_ref[...] = jnp.zeros(carry_ref.shape, F32)

    hf = _rmsnorm(x_ref[...], g_ref[...])
    h = hf.astype(BF16)
    vregs = GDN_CHUNK // SUBLANES
    for cb in range(D_MODEL // LANES):
        perm_ref[cb] = hf[:, cb * LANES:(cb + 1) * LANES]
    h_perm = jnp.concatenate(
        [jnp.concatenate([perm_ref[cb, pl.ds(c0 + v, SUBLANES, stride=vregs), :]
                          for c0 in range(0, tm, GDN_CHUNK) for v in range(vregs)], axis=0)
         for cb in range(D_MODEL // LANES)], axis=1).astype(BF16)
    gdn = _deltanet_qkv_stages(h_perm, wm_ref, wbd_ref, convw_ref, alog_ref, dtb_ref, (qb_ref, kb_ref, vb_ref), bd_ref,
                               carry_ref, tm=tm)
    att = _attention_qkv_stages(h, wm_ref, cos_ref, sin_ref, (a0_ref, a1_ref, a2_ref), stage_ref, tm=tm)
    _interleave(gdn, DELTANET_QKV_STAGES, att, ATT_QKV_STAGES)


def _mixer_in(x1, norm_g, w_in, conv_w, a_log, dt_bias, cos_t, sin_t, *, tm, seq):
    n = x1.shape[0]
    tiles_per_seq = seq // tm
    tile = lambda rows, w: pl.BlockSpec((rows, w), lambda bi, i: (bi * tiles_per_seq + i, 0))
    table = pl.BlockSpec((tm, LANES), lambda bi, i: (i, 0))
    wq = 3 * ATT_GROUP_WIDTH
    att_specs = [tile(tm // dil, dil * wq) for _, dil in ATT_GROUPS]
    att_shapes = [jax.ShapeDtypeStruct((n // dil, dil * wq), BF16) for _, dil in ATT_GROUPS]
    gdn = jax.ShapeDtypeStruct((n, GDN_WIDTH), F32)
    return pl.pallas_call(
        functools.partial(_mixer_in_kernel, tm=tm),
        grid=(n // seq, tiles_per_seq),
        in_specs=[tile(tm, D_MODEL), _resident((1, D_MODEL)),
                  pl.BlockSpec((D_MODEL, W_IN_BD), lambda bi, i: (0, 0), pipeline_mode=pl.Buffered(1)),
                  pl.BlockSpec((D_MODEL, LANES), lambda bi, i: (0, W_IN_BD // LANES), pipeline_mode=pl.Buffered(1)),
                  _resident((GDN_CONV, 3 * GDN_WIDTH)), _resident((1, LANES)), _resident((1, LANES)), table, table],
        out_specs=att_specs + [tile(tm, GDN_WIDTH)] * 3 + [tile(tm, LANES)],
        out_shape=att_shapes + [gdn] * 3 + [jax.ShapeDtypeStruct((n, LANES), F32)],
        scratch_shapes=[pltpu.VMEM(((GDN_CONV - 1) * SUBLANES, 3 * GDN_WIDTH), F32),
                        pltpu.VMEM((D_MODEL // LANES, tm, LANES), F32), pltpu.VMEM((MIXER_IN_STAGED, tm, LANES), F32)],
        compiler_params=pltpu.CompilerParams(dimension_semantics=("arbitrary", "arbitrary"),
                                             vmem_limit_bytes=VMEM_LIMIT_BYTES),
        name="mixer_in",
    )(x1, norm_g, w_in, w_in, conv_w, a_log, dt_bias, cos_t, sin_t)


ATT_BATCH = 3


def _attention_blocks(items):
    lane = lax.broadcasted_iota(jnp.int32, (1, ATT_GROUP_WIDTH), 1)
    heads = range(ATT_HEADS_PER_GROUP)
    in_head = [(lane // ATT_HEAD_DIM) == hh for hh in heads]
    keep = [jnp.where(in_head[hh], 1.0, 0.0).astype(BF16) for hh in heads]
    nq = ATT_BLOCK
    s_all = [_dot_nt(jnp.concatenate([q * keep[hh] for hh in heads], axis=0), k) for q, k, _, _ in items]
    stats, p_all = [], []
    for (_, _, _, valid), sa in zip(items, s_all):
        s = [jnp.where(valid, sa[hh * nq:(hh + 1) * nq], NEG_BIG) for hh in heads]
        m = [jnp.max(s[hh], axis=-1, keepdims=True) for hh in heads]
        p = [jnp.exp(s[hh] - m[hh]) for hh in heads]
        l = [jnp.sum(p[hh], axis=-1, keepdims=True) for hh in heads]
        stats.append((m, l))
        p_all.append(jnp.concatenate([p[hh].astype(BF16) for hh in heads], axis=0))
    pv_all = [_dot(ps, v) for ps, (_, _, v, _) in zip(p_all, items)]
    outs = []
    for pv, (m, l) in zip(pv_all, stats):
        o = jnp.zeros((ATT_BLOCK, ATT_GROUP_WIDTH), F32)
        lse = jnp.zeros((ATT_BLOCK, ATT_GROUP_WIDTH), F32)
        for hh in heads:
            o = jnp.where(in_head[hh], pv[hh * nq:(hh + 1) * nq] * (1.0 / l[hh]), o)
            lse = jnp.where(in_head[hh], m[hh] + jnp.log(l[hh]), lse)
        outs.append((o, lse))
    return outs


def _attention_kernel(a0_ref, a1_ref, a2_ref, ya_ref, o0, l0, o1, l1, o2, l2, *, seq):
    in_refs = (a0_ref, a1_ref, a2_ref)
    o_refs = (o0, o1, o2)
    l_refs = (l0, l1, l2)
    qi = lax.broadcasted_iota(jnp.int32, (ATT_BLOCK, ATT_BLOCK), 0)
    kj = lax.broadcasted_iota(jnp.int32, (ATT_BLOCK, ATT_BLOCK), 1)
    causal = kj <= qi
    qi2 = lax.broadcasted_iota(jnp.int32, (ATT_BLOCK, 2 * ATT_BLOCK), 0)
    kj2 = lax.broadcasted_iota(jnp.int32, (ATT_BLOCK, 2 * ATT_BLOCK), 1)
    band = (kj2 >= qi2) & (kj2 - ATT_BLOCK <= qi2)
    wq = 3 * ATT_GROUP_WIDTH

    def load(gi, r, n):
        src, base = in_refs[gi], r * wq
        if isinstance(n, int) and n == 0:
            qrows = krows = slice(0, ATT_BLOCK)
            valid = causal
        else:
            start = lambda x: x if isinstance(x, int) else pl.multiple_of(x, ATT_BLOCK)
            qrows = pl.ds(start(n * ATT_BLOCK), ATT_BLOCK)
            krows = pl.ds(start((n - 1) * ATT_BLOCK), 2 * ATT_BLOCK)
            valid = band
        return (src[0, qrows, base:base + ATT_GROUP_WIDTH],
                src[0, krows, base + ATT_GROUP_WIDTH:base + 2 * ATT_GROUP_WIDTH],
                src[0, krows, base + 2 * ATT_GROUP_WIDTH:base + 3 * ATT_GROUP_WIDTH], valid)

    def store(gi, r, n, o, lse):
        dil = ATT_GROUPS[gi][1]
        if dil == 1:
            first = n * ATT_BLOCK
            rows = pl.ds(first if isinstance(first, int) else pl.multiple_of(first, ATT_BLOCK), ATT_BLOCK)
        else:
            rows = pl.ds(n * ATT_BLOCK * dil + r, ATT_BLOCK, stride=dil)
        for half in range(ATT_GROUP_WIDTH // LANES):
            o_refs[gi][half, rows, :] = o[:, half * LANES:(half + 1) * LANES]
            l_refs[gi][half, rows, :] = lse[:, half * LANES:(half + 1) * LANES]

    def run(blocks):
        for (gi, r, n), (o, lse) in zip(blocks, _attention_blocks([load(*blk) for blk in blocks])):
            store(gi, r, n, o, lse)

    static_blocks = []
    looped = None
    for gi, (window, dil) in enumerate(ATT_GROUPS):
        assert window // dil == ATT_BLOCK
        nblk = seq // dil // ATT_BLOCK
        if dil == 1 and (nblk - 1) % ATT_BATCH == 0:
            static_blocks.append((gi, 0, 0))
            looped = (gi, nblk)
        else:
            static_blocks += [(gi, r, n) for r in range(dil) for n in range(nblk)]
    for i in range(0, len(static_blocks), ATT_BATCH + 1):
        run(static_blocks[i:i + ATT_BATCH + 1])
    if looped is not None:
        gi, nblk = looped

        def body(i, carry):
            run([(gi, 0, 1 + i * ATT_BATCH + j) for j in range(ATT_BATCH)])
            return carry
        lax.fori_loop(0, (nblk - 1) // ATT_BATCH, body, 0)

    rows_per_step = 256

    def merge(i, carry):
        rows = pl.ds(pl.multiple_of(i * rows_per_step, rows_per_step), rows_per_step)
        for half in range(ATT_GROUP_WIDTH // LANES):
            la, lb, lc = l0[half, rows, :], l1[half, rows, :], l2[half, rows, :]
            m = jnp.maximum(jnp.maximum(la, lb), lc)
            ea, eb, ec = jnp.exp(la - m), jnp.exp(lb - m), jnp.exp(lc - m)
            num = ea * o0[half, rows, :] + eb * o1[half, rows, :] + ec * o2[half, rows, :]
            ya_ref[0, rows, half * LANES:(half + 1) * LANES] = num / (ea + eb + ec)
        return carry
    lax.fori_loop(0, seq // rows_per_step, merge, 0)


def _attention(a0, a1, a2, *, batch):
    views = tuple(a.reshape(batch, a.shape[0] // batch, a.shape[1]) for a in (a0, a1, a2))
    b, s, _ = views[0].shape
    specs = [pl.BlockSpec((1,) + arr.shape[1:], lambda bi: (bi, 0, 0)) for arr in views]
    scratch = [pltpu.VMEM((ATT_GROUP_WIDTH // LANES, s, LANES), F32) for _ in range(6)]
    return pl.pallas_call(
        functools.partial(_attention_kernel, seq=s),
        grid=(b,),
        in_specs=specs,
        out_specs=pl.BlockSpec((1, s, ATT_GROUP_WIDTH), lambda bi: (bi, 0, 0)),
        out_shape=jax.ShapeDtypeStruct((b, s, ATT_GROUP_WIDTH), F32),
        scratch_shapes=scratch,
        compiler_params=pltpu.CompilerParams(dimension_semantics=("arbitrary",),
                                             vmem_limit_bytes=VMEM_LIMIT_BYTES),
        name="dilated_attention",
    )(*views)


def _deltanet_stages(q_ref, k_ref, v_ref, bd_ref, gnorm, state_ref, ob_ref, slot, *, tile):
    c = GDN_CHUNK
    d = GDN_HEAD_DIM
    heads = range(GDN_HEADS)
    pairs = [(2 * pp, 2 * pp + 1) for pp in range(GDN_HEADS // 2)]
    cols = [slice(hh * d, (hh + 1) * d) for hh in heads]
    glane = [GDN_HEADS + hh for hh in heads]
    ii = _chunk_time(lax.broadcasted_iota(jnp.int32, (c, 2 * c), 0))
    ll = lax.broadcasted_iota(jnp.int32, (c, 2 * c), 1)
    jj = _chunk_time(ll % c)
    lower = ii >= jj
    strict = ii > jj
    left = ll < c
    left_row = lax.broadcasted_iota(jnp.int32, (1, 2 * c), 1) < c
    keep_left = jnp.where(left, 1.0, 0.0).astype(BF16)
    keep_right = jnp.where(left, 0.0, 1.0).astype(BF16)
    ti = _chunk_time(lax.broadcasted_iota(jnp.int32, (c, c), 0))
    tj = _chunk_time(lax.broadcasted_iota(jnp.int32, (c, c), 1))
    tri_ones = jnp.where(ti >= tj, 1.0, 0.0).astype(BF16)

    def blockdiag(x):
        return jnp.concatenate([x * keep_left, x * keep_right], axis=0)

    def stack_diag(xa, xb):
        zero = jnp.zeros_like(xa)
        return jnp.concatenate([jnp.concatenate([xa, zero], axis=1), jnp.concatenate([zero, xb], axis=1)], axis=0)

    def prepare(ci, out):
        rows = slice(ci * c, (ci + 1) * c)
        bd = bd_ref[rows, :]
        bd_hi = bd.astype(BF16)
        bd_rest = bd - bd_hi.astype(F32)
        bd_mid = bd_rest.astype(BF16)
        bd_lo = (bd_rest - bd_mid.astype(F32)).astype(BF16)
        gcum = _dot(tri_ones, bd_hi) + _dot(tri_ones, bd_mid) + _dot(tri_ones, bd_lo)
        yield
        gtot = jnp.broadcast_to(gcum[c - 1:c, :], (c, LANES))
        gcum_t = jnp.concatenate([gcum, gcum], axis=0).T
        e_cum_all = jnp.exp(gcum)
        e_rest_all = jnp.exp(gtot - gcum)
        e_tot_all = jnp.exp(gtot)
        q = [q_ref[rows, cols[hh]] for hh in heads]
        k = [k_ref[rows, cols[hh]] for hh in heads]
        v = [v_ref[rows, cols[hh]] for hh in heads]
        beta = [bd[:, hh:hh + 1] for hh in heads]
        e_cum = [e_cum_all[:, gl:gl + 1] for gl in glane]
        kbeta = [k[hh] * beta[hh] for hh in heads]
        kq = [_dot_nt(jnp.concatenate([jnp.concatenate([kbeta[a], kbeta[b]], axis=1),
                                       jnp.concatenate([q[a], q[b]], axis=1)], axis=0).astype(BF16),
                      stack_diag(k[a].astype(BF16), k[b].astype(BF16)))
              for a, b in pairs]
        yield
        decay = [jnp.exp(jnp.where(lower,
                                   jnp.where(left, gcum[:, glane[a]:glane[a] + 1], gcum[:, glane[b]:glane[b] + 1])
                                   - jnp.where(left_row, gcum_t[glane[a]:glane[a] + 1, :], gcum_t[glane[b]:glane[b] + 1, :]),
                                   NEG_BIG)) for a, b in pairs]
        m = [jnp.where(strict, kq[pp][0:c] * decay[pp], 0.0) for pp in range(len(pairs))]
        n = [-mm for mm in m]
        pb = [mm.astype(BF16) for mm in m]
        p = [_dot(x, blockdiag(x)) for x in pb]
        yield
        rounds = 5
        for r in range(rounds):
            pb = [x.astype(BF16) for x in p]
            upd = [_dot(x, blockdiag(y.astype(BF16))) for x, y in zip(pb, n)]
            p_next = [_dot(x, blockdiag(x)) for x in pb] if r + 1 < rounds else None
            yield
            n = [y + x + u for y, x, u in zip(n, p, upd)]
            p = p_next
        rhs = [jnp.concatenate([v[hh] * beta[hh], kbeta[hh] * e_cum[hh]], axis=1) for hh in heads]
        nr = [_dot(n[pp].astype(BF16), stack_diag(rhs[a].astype(BF16), rhs[b].astype(BF16)))
              for pp, (a, b) in enumerate(pairs)]
        yield
        sol = [rhs[hh] + nr[hh // 2][:, (hh % 2) * 2 * d:(hh % 2 + 1) * 2 * d] for hh in heads]
        out.update(
            first=ci * c,
            u=[sol[hh][:, 0:d] for hh in heads],
            wq=[jnp.concatenate([sol[hh][:, d:2 * d], q[hh] * e_cum[hh]], axis=0).astype(BF16) for hh in heads],
            a_qk=[(kq[pp][c:2 * c] * decay[pp]).astype(BF16) for pp in range(len(pairs))],
            k_dec=[(k[hh] * e_rest_all[:, gl:gl + 1]).astype(BF16) for hh, gl in zip(heads, glane)],
            e_tot=[e_tot_all[0:1, gl:gl + 1] for gl in glane])

    for first in range(0, tile // c, GDN_GROUP):
        group = [dict() for _ in range(GDN_GROUP)]
        gens = [prepare(first + gi, group[gi]) for gi in range(GDN_GROUP)]
        for _ in range(GDN_PREP_LAYERS):
            for gen in gens:
                next(gen)
            yield
        for gen in gens:
            for _ in gen:
                pass
        for pre in group:
            state = [state_ref[hh] for hh in heads]
            ws = [_dot(pre["wq"][hh], state[hh].astype(BF16)) for hh in heads]
            yield
            v_new = [(pre["u"][hh] - ws[hh][0:c]).astype(BF16) for hh in heads]
            kv = [_dot_tn(pre["k_dec"][hh], v_new[hh]) for hh in heads]
            av = [_dot(pre["a_qk"][pp], stack_diag(v_new[a], v_new[b])) for pp, (a, b) in enumerate(pairs)]
            yield
            for hh in heads:
                state_ref[hh] = state[hh] * pre["e_tot"][hh] + kv[hh]
            for hh in heads:
                o = _rmsnorm(ws[hh][c:2 * c] + av[hh // 2][:, (hh % 2) * d:(hh % 2 + 1) * d], gnorm)
                for vv in range(c // SUBLANES):
                    ob_ref[slot, hh, pl.ds(pre["first"] + vv, SUBLANES, stride=c // SUBLANES), :] = (
                        o[vv * SUBLANES:(vv + 1) * SUBLANES])


GDN_GROUP = 4
GDN_PREP_LAYERS = 9
GDN_LAYERS_PER_GROUP = GDN_PREP_LAYERS + 2 * GDN_GROUP


def _mixer_out_stages(x_ref, ya_ref, ob_ref, slot, g_ref, wgt_ref, wa_ref, wb_ref, wo_ref, o_ref):
    x = x_ref[...]
    h = _rmsnorm(x, g_ref[...]).astype(BF16)
    ya = ya_ref[...].astype(BF16)
    blocks = [slice(j * MIX_BLOCK, (j + 1) * MIX_BLOCK) for j in range(D_MODEL // MIX_BLOCK)]
    gate_cols = lambda which, blk: slice(which * D_MODEL + blk.start, which * D_MODEL + blk.stop)
    yb = []
    for blk in blocks:
        gdn_gate = _dot(h, wgt_ref[:, gate_cols(0, blk)])
        yield
        ob = jnp.concatenate([ob_ref[slot, hh] for hh in range(blk.start // GDN_HEAD_DIM, blk.stop // GDN_HEAD_DIM)],
                             axis=1)
        yb.append((ob * (gdn_gate * _sigmoid(gdn_gate))).astype(BF16))
    yb = jnp.concatenate(yb, axis=1)
    merged = []
    for blk in blocks:
        gate_a = _dot(h, wgt_ref[:, gate_cols(1, blk)])
        branch_a = _dot(ya, wa_ref[:, blk])
        yield
        gate_b = _dot(h, wgt_ref[:, gate_cols(2, blk)])
        yield
        branch_b = _dot(yb, wb_ref[:, blk])
        yield
        merged.append((_sigmoid(gate_a) * branch_a + _sigmoid(gate_b) * branch_b).astype(BF16))
    merged = jnp.concatenate(merged, axis=1)
    for blk in blocks:
        o_ref[:, blk] = x[:, blk] + _dot(merged, wo_ref[:, blk])
        yield


MIX_GRANULES = 5 * (D_MODEL // MIX_BLOCK)


def _mixer_tail_kernel(q_ref, k_ref, v_ref, bd_ref, gn_ref, x_ref, ya_ref, g_ref, wgt_ref, wa_ref, wb_ref, wo_ref,
                       *refs, tile, tiles_per_seq, n_tiles):
    n_cast = (len(refs) - 3) // 2
    o_ref, state_ref, ob_ref = refs[n_cast], refs[2 * n_cast + 1], refs[2 * n_cast + 2]
    step = pl.program_id(0)

    @pl.when(step == 0)
    def _():
        ob_ref[...] = jnp.zeros(ob_ref.shape, F32)

    @pl.when(jnp.minimum(step, n_tiles - 1) % tiles_per_seq == 0)
    def _():
        state_ref[...] = jnp.zeros(state_ref.shape, F32)

    slot = step % 2
    gdn = _deltanet_stages(q_ref, k_ref, v_ref, bd_ref, gn_ref[...], state_ref, ob_ref, slot, tile=tile)
    mix = _mixer_out_stages(x_ref, ya_ref, ob_ref, 1 - slot, g_ref, wgt_ref, wa_ref, wb_ref, wo_ref, o_ref)
    _interleave(gdn, GDN_LAYERS_PER_GROUP * (tile // (GDN_CHUNK * GDN_GROUP)), mix, MIX_GRANULES)
    _cast_rows(refs[:n_cast], refs[n_cast + 1:2 * n_cast + 1])


def _mixer_tail(qb, kb, vb, bd, out_norm, x1, ya, norm_g, w_gates, w_a, w_b, w_o, *, tile, seq, cast=(), layer=0):
    n = x1.shape[0]
    n_tiles = n // tile
    cast_in, cast_out, cast_shapes = _cast_specs(cast, layer, n_tiles)
    cur = lambda w: pl.BlockSpec((tile, w), lambda s: (jnp.minimum(s, n_tiles - 1), 0))
    prev = lambda w: pl.BlockSpec((tile, w), lambda s: (jnp.maximum(s - 1, 0), 0))
    out = pl.pallas_call(
        functools.partial(_mixer_tail_kernel, tile=tile, tiles_per_seq=seq // tile, n_tiles=n_tiles),
        grid=(n_tiles + 1,),
        in_specs=[cur(GDN_WIDTH), cur(GDN_WIDTH), cur(GDN_WIDTH), cur(LANES), _resident((1, GDN_HEAD_DIM)),
                  prev(D_MODEL), prev(ATT_GROUP_WIDTH), _resident((1, D_MODEL)),
                  _resident(w_gates.shape), _resident(w_a.shape), _resident(w_b.shape), _resident(w_o.shape)]
                 + cast_in,
        out_specs=[prev(D_MODEL)] + cast_out,
        out_shape=[jax.ShapeDtypeStruct((n, D_MODEL), F32)] + cast_shapes,
        scratch_shapes=[pltpu.VMEM((GDN_HEADS, GDN_HEAD_DIM, GDN_HEAD_DIM), F32),
                        pltpu.VMEM((2, GDN_HEADS, tile, GDN_HEAD_DIM), F32)],
        compiler_params=pltpu.CompilerParams(dimension_semantics=("arbitrary",),
                                             vmem_limit_bytes=VMEM_LIMIT_BYTES),
        name="deltanet_mixer_out",
    )(qb, kb, vb, bd, out_norm, x1, ya, norm_g, w_gates, w_a, w_b, w_o, *cast)
    return out[0], out[1:]


def _rope_tables(seq):
    half = ATT_HEAD_DIM // 2
    inv_freq = ROPE_THETA ** (-jnp.arange(half, dtype=F32) / half)
    ang = jnp.arange(seq, dtype=F32)[:, None] * inv_freq[None, :]
    cos, sin = jnp.cos(ang), jnp.sin(ang)
    reps = LANES // ATT_HEAD_DIM
    return jnp.tile(jnp.concatenate([cos, cos], axis=-1), (1, reps)), jnp.tile(jnp.concatenate([-sin, sin], axis=-1), (1, reps))


def _pad_lanes(row, offset):
    return jnp.zeros((1, LANES), F32).at[0, offset:offset + row.shape[0]].set(row.astype(F32))


def _layer(x, ffn1_norm, ffn1_w_gate, ffn1_w_up, ffn1_w_down, mix_norm, w_in_all, gdn_conv_w, gdn_a_log, gdn_dt_bias,
           gdn_out_norm, w_branch_a, w_branch_b, w_out, ffn2_norm, ffn2_all, fin_g,
           *, layer, final_norm, tm_ffn, tm_mix, gdn_tile):
    b, s, _ = x.shape
    n = b * s
    row = lambda v: v.reshape(1, -1).astype(F32)
    x1, (w_in,) = _ffn(x.reshape(n, D_MODEL), row(ffn1_norm), ffn1_w_gate.astype(BF16), ffn1_w_up.astype(BF16),
                       ffn1_w_down.astype(BF16), fin_g, final_norm=False, tm=tm_ffn, cast=(w_in_all,), layer=layer)
    w_gates = w_in[:, W_IN_GATES:]
    cos_t, sin_t = _rope_tables(s)
    a0, a1, a2, qb, kb, vb, bd = _mixer_in(
        x1, row(mix_norm), w_in, gdn_conv_w.astype(F32), _pad_lanes(gdn_a_log, GDN_HEADS),
        _pad_lanes(gdn_dt_bias, GDN_HEADS), cos_t, sin_t, tm=tm_mix, seq=s)

    ya = _attention(a0, a1, a2, batch=b)
    x2, ffn2_w = _mixer_tail(qb, kb, vb, bd, row(gdn_out_norm), x1, ya.reshape(n, ATT_GROUP_WIDTH), row(mix_norm),
                             w_gates, w_branch_a.astype(BF16), w_branch_b.astype(BF16), w_out.astype(BF16),
                             tile=gdn_tile, seq=s, cast=ffn2_all, layer=layer)
    x3, _ = _ffn(x2, row(ffn2_norm), *ffn2_w, fin_g, final_norm=final_norm, tm=tm_ffn)
    return x3.reshape(b, s, D_MODEL)


def kernel(x, ffn1_norm, ffn1_w_gate, ffn1_w_up, ffn1_w_down, mix_norm, w_in, gdn_conv_w, gdn_a_log, gdn_dt_bias,
           gdn_out_norm, w_branch_a, w_branch_b, w_out, ffn2_norm, ffn2_w_gate, ffn2_w_up, ffn2_w_down, final_norm):
    depth = ffn1_norm.shape[0]
    fin_g = final_norm.reshape(1, -1).astype(F32)
    for layer in range(depth):
        x = _layer(x, ffn1_norm[layer], ffn1_w_gate[layer], ffn1_w_up[layer], ffn1_w_down[layer], mix_norm[layer],
                   w_in, gdn_conv_w[layer], gdn_a_log[layer], gdn_dt_bias[layer], gdn_out_norm[layer],
                   w_branch_a[layer], w_branch_b[layer], w_out[layer], ffn2_norm[layer],
                   (ffn2_w_gate, ffn2_w_up, ffn2_w_down), fin_g, layer=layer, final_norm=(layer == depth - 1),
                   tm_ffn=512, tm_mix=512, gdn_tile=512)
    return x
```

```python
import functools

import jax
import jax.numpy as jnp
from jax import lax
from jax.experimental import pallas as pl
from jax.experimental.pallas import tpu as pltpu

F32 = jnp.float32
BF16 = jnp.bfloat16

D_MODEL = 1024
D_FF = 2816
EPS = 1e-6

ATT_GROUPS = ((128, 1), (512, 4), (2048, 16))
ATT_HEADS_PER_GROUP = 4
ATT_HEAD_DIM = 64
ATT_BLOCK = 128
ATT_GROUP_WIDTH = ATT_HEADS_PER_GROUP * ATT_HEAD_DIM
ATT_QKV_WIDTH = len(ATT_GROUPS) * ATT_GROUP_WIDTH
ROPE_THETA = 10000.0

GDN_HEADS = 8
GDN_HEAD_DIM = 128
GDN_WIDTH = GDN_HEADS * GDN_HEAD_DIM
GDN_CONV = 4
GDN_CHUNK = 64

LANES = 128
SUBLANES = 8
VMEM_LIMIT_BYTES = 56 * 1024 * 1024

W_IN_GDN = 3 * ATT_QKV_WIDTH
W_IN_BD = W_IN_GDN + 3 * GDN_WIDTH
W_IN_GATES = W_IN_BD + 2 * GDN_HEADS
FFN_CHUNKS = ((0, 768), (768, 1536), (1536, 2304), (2304, 2816))
NEG_BIG = -1e30


def _resident(shape):
    nd = len(shape)
    return pl.BlockSpec(shape, lambda *_: (0,) * nd, pipeline_mode=pl.Buffered(1))


def _rmsnorm(x, g):
    return x * lax.rsqrt(jnp.mean(x * x, axis=-1, keepdims=True) + EPS) * g


def _sigmoid(x):
    return 1.0 / (1.0 + jnp.exp(-x))


def _dot(a, b):
    return jnp.dot(a, b, preferred_element_type=F32)


def _dot_nt(a, b):
    return lax.dot_general(a, b, (((1,), (1,)), ((), ())), preferred_element_type=F32)


def _dot_tn(a, b):
    return lax.dot_general(a, b, (((0,), (0,)), ((), ())), preferred_element_type=F32)


def _swiglu_residual(x, g, wg_ref, wu_ref, wd_ref):
    h = _rmsnorm(x, g).astype(BF16)
    acc = x
    for lo, hi in FFN_CHUNKS:
        gate = _dot(h, wg_ref[:, lo:hi])
        up = _dot(h, wu_ref[:, lo:hi])
        act = (0.5 * gate * _sigmoid(gate) * up).astype(BF16)
        acc = acc + _dot(act, wd_ref[lo:hi, :])
    return acc


def _cast_rows(src_refs, dst_refs):
    for src, dst in zip(src_refs, dst_refs):
        dst[...] = src[...].astype(BF16)


def _cast_specs(arrays, layer, steps):
    in_specs, out_specs, shapes = [], [], []
    for arr in arrays:
        _, n_rows, n_cols = arr.shape
        packed_rows = 2 * SUBLANES
        parts = next(p for p in range(steps, 0, -1) if n_rows % p == 0 and (n_rows // p) % packed_rows == 0)
        in_specs.append(pl.BlockSpec((None, n_rows // parts, n_cols),
                                     lambda s, parts=parts: (layer, jnp.minimum(s, parts - 1), 0)))
        out_specs.append(pl.BlockSpec((n_rows // parts, n_cols), lambda s, parts=parts: (jnp.minimum(s, parts - 1), 0)))
        shapes.append(jax.ShapeDtypeStruct((n_rows, n_cols), BF16))
    return in_specs, out_specs, shapes


def _ffn_kernel(x_ref, g_ref, wg_ref, wu_ref, wd_ref, fin_ref, o_ref, *, final_norm):
    y = _swiglu_residual(x_ref[...], g_ref[...], wg_ref, wu_ref, wd_ref)
    if final_norm:
        y = _rmsnorm(y, fin_ref[...])
    o_ref[...] = y


def _ffn(x, norm_g, wg, wu, wd, fin_g, *, final_norm, tm):
    n = x.shape[0]
    row = pl.BlockSpec((tm, D_MODEL), lambda i: (i, 0))
    return pl.pallas_call(
        functools.partial(_ffn_kernel, final_norm=final_norm),
        grid=(n // tm,),
        in_specs=[row, _resident((1, D_MODEL)), _resident((D_MODEL, D_FF)), _resident((D_MODEL, D_FF)),
                  _resident((D_FF, D_MODEL)), _resident((1, D_MODEL))],
        out_specs=row,
        out_shape=jax.ShapeDtypeStruct((n, D_MODEL), F32),
        compiler_params=pltpu.CompilerParams(dimension_semantics=("arbitrary",),
                                             vmem_limit_bytes=VMEM_LIMIT_BYTES),
        name="ffn_final" if final_norm else "ffn",
    )(x, norm_g, wg, wu, wd, fin_g)


MIX_BLOCK = 256


def _interleave(primary, n_primary, secondary, n_secondary):
    done = 0
    for i in range(n_primary):
        next(primary)
        while done * n_primary < (i + 1) * n_secondary:
            next(secondary)
            done += 1
    for gen in (primary, secondary):
        for _ in gen:
            pass


def _chunk_time(row):
    return SUBLANES * (row % SUBLANES) + row // SUBLANES


def _attention_qkv_stages(h, wm_ref, cos_ref, sin_ref, att_refs, stage_ref, *, tm):
    cos = cos_ref[...]
    sin = sin_ref[...]
    lane = lax.broadcasted_iota(jnp.int32, (1, LANES), 1)
    first_half = (lane % ATT_HEAD_DIM) < (ATT_HEAD_DIM // 2)
    slot = 0
    for part in range(3):
        for gi, (_, dil) in enumerate(ATT_GROUPS):
            col = part * ATT_QKV_WIDTH + gi * ATT_GROUP_WIDTH
            y = _dot(h, wm_ref[:, col:col + ATT_GROUP_WIDTH])
            yield
            for j in range(ATT_GROUP_WIDTH // LANES):
                blk = y[:, j * LANES:(j + 1) * LANES]
                if part < 2:
                    swapped = jnp.where(first_half, pltpu.roll(blk, LANES - ATT_HEAD_DIM // 2, 1),
                                        pltpu.roll(blk, ATT_HEAD_DIM // 2, 1))
                    blk = blk * cos + swapped * sin
                if part == 0:
                    blk = blk * (ATT_HEAD_DIM ** -0.5)
                dst = part * ATT_GROUP_WIDTH + j * LANES
                if dil == 1:
                    att_refs[gi][:, dst:dst + LANES] = blk.astype(BF16)
                else:
                    stage_ref[slot] = blk
                    for r in range(dil):
                        rows = stage_ref[slot, pl.ds(r, tm // dil, stride=dil), :]
                        lo = r * 3 * ATT_GROUP_WIDTH + dst
                        att_refs[gi][:, lo:lo + LANES] = rows.astype(BF16)
                    slot += 1
                yield


ATT_QKV_STAGES = 3 * len(ATT_GROUPS) * (1 + ATT_GROUP_WIDTH // LANES)
MIXER_IN_STAGED = 3 * (len(ATT_GROUPS) - 1) * (ATT_GROUP_WIDTH // LANES)


def _deltanet_qkv_stages(h_perm, wm_ref, wbd_ref, convw_ref, alog_ref, dtb_ref, gdn_refs, bd_ref, carry_ref, *, tm):
    lane = lax.broadcasted_iota(jnp.int32, (1, LANES), 1)
    raw = _dot(h_perm, wbd_ref[...])
    yield
    z = raw + dtb_ref[...]
    softplus = jnp.maximum(z, 0.0) + jnp.log1p(jnp.exp(-jnp.abs(z)))
    g = -jnp.exp(alog_ref[...]) * softplus
    bd_ref[...] = jnp.where(lane < GDN_HEADS, _sigmoid(raw), jnp.where(lane < 2 * GDN_HEADS, g, 0.0))

    vregs = GDN_CHUNK // SUBLANES
    halo = GDN_CONV - 1
    chunks = tm // GDN_CHUNK
    last_sublane = lax.broadcasted_iota(jnp.int32, (SUBLANES, GDN_HEAD_DIM), 0) == SUBLANES - 1
    heads_per_block = MIX_BLOCK // GDN_HEAD_DIM
    for part in range(3):
        for blk in range(GDN_WIDTH // MIX_BLOCK):
            base = part * GDN_WIDTH + blk * MIX_BLOCK
            y = _dot(h_perm, wm_ref[:, W_IN_GDN + base:W_IN_GDN + base + MIX_BLOCK])
            yield
            for hb in range(heads_per_block):
                hh = blk * heads_per_block + hb
                col = part * GDN_WIDTH + hh * GDN_HEAD_DIM
                cur = y[:, hb * GDN_HEAD_DIM:(hb + 1) * GDN_HEAD_DIM]
                vreg = lambda c, v: cur[c * GDN_CHUNK + v * SUBLANES:c * GDN_CHUNK + (v + 1) * SUBLANES]
                prev_tile = carry_ref[:, col:col + GDN_HEAD_DIM]
                carry_ref[:, col:col + GDN_HEAD_DIM] = cur[tm - halo * SUBLANES:, :]
                wrapped = []
                for c in range(chunks):
                    row = []
                    for i in range(halo):
                        before = (prev_tile[i * SUBLANES:(i + 1) * SUBLANES] if c == 0
                                  else vreg(c - 1, vregs - halo + i))
                        own = vreg(c, vregs - halo + i)
                        row.append(pltpu.roll(jnp.where(last_sublane, before, own), 1, 0))
                    wrapped.append(row)
                acc = cur * convw_ref[halo:halo + 1, col:col + GDN_HEAD_DIM]
                for shift in range(1, GDN_CONV):
                    pieces = []
                    for c in range(chunks):
                        pieces += wrapped[c][halo - shift:]
                        pieces.append(cur[c * GDN_CHUNK:(c + 1) * GDN_CHUNK - shift * SUBLANES])
                    shifted = jnp.concatenate(pieces, axis=0)
                    acc = acc + shifted * convw_ref[halo - shift:halo - shift + 1, col:col + GDN_HEAD_DIM]
                act = acc * _sigmoid(acc)
                if part < 2:
                    act = act * lax.rsqrt(jnp.sum(act * act, axis=-1, keepdims=True) + EPS)
                if part == 0:
                    act = act * (GDN_HEAD_DIM ** -0.5)
                gdn_refs[part][:, hh * GDN_HEAD_DIM:(hh + 1) * GDN_HEAD_DIM] = act
                yield


DELTANET_QKV_STAGES = 1 + 3 * (GDN_WIDTH // MIX_BLOCK) + 3 * GDN_HEADS


def _mixer_in_kernel(x_ref, g_ref, wm_ref, wbd_ref, convw_ref, alog_ref, dtb_ref, cos_ref, sin_ref,
                     a0_ref, a1_ref, a2_ref, qb_ref, kb_ref, vb_ref, bd_ref, carry_ref, perm_ref, stage_ref, *, tm):
    @pl.when(pl.program_id(1) == 0)
    def _():
        carry_ref[...] = jnp.zeros(carry_ref.shape, F32)

    hf = _rmsnorm(x_ref[...], g_ref[...])
    h = hf.astype(BF16)
    vregs = GDN_CHUNK // SUBLANES
    for cb in range(D_MODEL // LANES):
        perm_ref[cb] = hf[:, cb * LANES:(cb + 1) * LANES]
    h_perm = jnp.concatenate(
        [jnp.concatenate([perm_ref[cb, pl.ds(c0 + v, SUBLANES, stride=vregs), :]
                          for c0 in range(0, tm, GDN_CHUNK) for v in range(vregs)], axis=0)
         for cb in range(D_MODEL // LANES)], axis=1).astype(BF16)
    gdn = _deltanet_qkv_stages(h_perm, wm_ref, wbd_ref, convw_ref, alog_ref, dtb_ref, (qb_ref, kb_ref, vb_ref), bd_ref,
                               carry_ref, tm=tm)
    att = _attention_qkv_stages(h, wm_ref, cos_ref, sin_ref, (a0_ref, a1_ref, a2_ref), stage_ref, tm=tm)
    _interleave(gdn, DELTANET_QKV_STAGES, att, ATT_QKV_STAGES)


def _mixer_in(x1, norm_g, w_in, conv_w, a_log, dt_bias, cos_t, sin_t, *, tm, seq):
    n = x1.shape[0]
    tiles_per_seq = seq // tm
    tile = lambda rows, w: pl.BlockSpec((rows, w), lambda bi, i: (bi * tiles_per_seq + i, 0))
    table = pl.BlockSpec((tm, LANES), lambda bi, i: (i, 0))
    wq = 3 * ATT_GROUP_WIDTH
    att_specs = [tile(tm // dil, dil * wq) for _, dil in ATT_GROUPS]
    att_shapes = [jax.ShapeDtypeStruct((n // dil, dil * wq), BF16) for _, dil in ATT_GROUPS]
    gdn = jax.ShapeDtypeStruct((n, GDN_WIDTH), F32)
    return pl.pallas_call(
        functools.partial(_mixer_in_kernel, tm=tm),
        grid=(n // seq, tiles_per_seq),
        in_specs=[tile(tm, D_MODEL), _resident((1, D_MODEL)),
                  pl.BlockSpec((D_MODEL, W_IN_BD), lambda bi, i: (0, 0), pipeline_mode=pl.Buffered(1)),
                  pl.BlockSpec((D_MODEL, LANES), lambda bi, i: (0, W_IN_BD // LANES), pipeline_mode=pl.Buffered(1)),
                  _resident((GDN_CONV, 3 * GDN_WIDTH)), _resident((1, LANES)), _resident((1, LANES)), table, table],
        out_specs=att_specs + [tile(tm, GDN_WIDTH)] * 3 + [tile(tm, LANES)],
        out_shape=att_shapes + [gdn] * 3 + [jax.ShapeDtypeStruct((n, LANES), F32)],
        scratch_shapes=[pltpu.VMEM(((GDN_CONV - 1) * SUBLANES, 3 * GDN_WIDTH), F32),
                        pltpu.VMEM((D_MODEL // LANES, tm, LANES), F32), pltpu.VMEM((MIXER_IN_STAGED, tm, LANES), F32)],
        compiler_params=pltpu.CompilerParams(dimension_semantics=("arbitrary", "arbitrary"),
                                             vmem_limit_bytes=VMEM_LIMIT_BYTES),
        name="mixer_in",
    )(x1, norm_g, w_in, w_in, conv_w, a_log, dt_bias, cos_t, sin_t)


ATT_BATCH = 3


def _attention_blocks(items):
    lane = lax.broadcasted_iota(jnp.int32, (1, ATT_GROUP_WIDTH), 1)
    heads = range(ATT_HEADS_PER_GROUP)
    in_head = [(lane // ATT_HEAD_DIM) == hh for hh in heads]
    keep = [jnp.where(in_head[hh], 1.0, 0.0).astype(BF16) for hh in heads]
    nq = ATT_BLOCK
    s_all = [_dot_nt(jnp.concatenate([q * keep[hh] for hh in heads], axis=0), k) for q, k, _, _ in items]
    stats, p_all = [], []
    for (_, _, _, valid), sa in zip(items, s_all):
        s = [jnp.where(valid, sa[hh * nq:(hh + 1) * nq], NEG_BIG) for hh in heads]
        m = [jnp.max(s[hh], axis=-1, keepdims=True) for hh in heads]
        p = [jnp.exp(s[hh] - m[hh]) for hh in heads]
        l = [jnp.sum(p[hh], axis=-1, keepdims=True) for hh in heads]
        stats.append((m, l))
        p_all.append(jnp.concatenate([p[hh].astype(BF16) for hh in heads], axis=0))
    pv_all = [_dot(ps, v) for ps, (_, _, v, _) in zip(p_all, items)]
    outs = []
    for pv, (m, l) in zip(pv_all, stats):
        o = jnp.zeros((ATT_BLOCK, ATT_GROUP_WIDTH), F32)
        lse = jnp.zeros((ATT_BLOCK, ATT_GROUP_WIDTH), F32)
        for hh in heads:
            o = jnp.where(in_head[hh], pv[hh * nq:(hh + 1) * nq] * (1.0 / l[hh]), o)
            lse = jnp.where(in_head[hh], m[hh] + jnp.log(l[hh]), lse)
        outs.append((o, lse))
    return outs


def _attention_kernel(a0_ref, a1_ref, a2_ref, ya_ref, o0, l0, o1, l1, o2, l2, *, seq):
    in_refs = (a0_ref, a1_ref, a2_ref)
    o_refs = (o0, o1, o2)
    l_refs = (l0, l1, l2)
    qi = lax.broadcasted_iota(jnp.int32, (ATT_BLOCK, ATT_BLOCK), 0)
    kj = lax.broadcasted_iota(jnp.int32, (ATT_BLOCK, ATT_BLOCK), 1)
    causal = kj <= qi
    qi2 = lax.broadcasted_iota(jnp.int32, (ATT_BLOCK, 2 * ATT_BLOCK), 0)
    kj2 = lax.broadcasted_iota(jnp.int32, (ATT_BLOCK, 2 * ATT_BLOCK), 1)
    band = (kj2 >= qi2) & (kj2 - ATT_BLOCK <= qi2)
    wq = 3 * ATT_GROUP_WIDTH

    def load(gi, r, n):
        src, base = in_refs[gi], r * wq
        if isinstance(n, int) and n == 0:
            qrows = krows = slice(0, ATT_BLOCK)
            valid = causal
        else:
            start = lambda x: x if isinstance(x, int) else pl.multiple_of(x, ATT_BLOCK)
            qrows = pl.ds(start(n * ATT_BLOCK), ATT_BLOCK)
            krows = pl.ds(start((n - 1) * ATT_BLOCK), 2 * ATT_BLOCK)
            valid = band
        return (src[0, qrows, base:base + ATT_GROUP_WIDTH],
                src[0, krows, base + ATT_GROUP_WIDTH:base + 2 * ATT_GROUP_WIDTH],
                src[0, krows, base + 2 * ATT_GROUP_WIDTH:base + 3 * ATT_GROUP_WIDTH], valid)

    def store(gi, r, n, o, lse):
        dil = ATT_GROUPS[gi][1]
        if dil == 1:
            first = n * ATT_BLOCK
            rows = pl.ds(first if isinstance(first, int) else pl.multiple_of(first, ATT_BLOCK), ATT_BLOCK)
        else:
            rows = pl.ds(n * ATT_BLOCK * dil + r, ATT_BLOCK, stride=dil)
        for half in range(ATT_GROUP_WIDTH // LANES):
            o_refs[gi][half, rows, :] = o[:, half * LANES:(half + 1) * LANES]
            l_refs[gi][half, rows, :] = lse[:, half * LANES:(half + 1) * LANES]

    def run(blocks):
        for (gi, r, n), (o, lse) in zip(blocks, _attention_blocks([load(*blk) for blk in blocks])):
            store(gi, r, n, o, lse)

    static_blocks = []
    looped = None
    for gi, (window, dil) in enumerate(ATT_GROUPS):
        assert window // dil == ATT_BLOCK
        nblk = seq // dil // ATT_BLOCK
        if dil == 1 and (nblk - 1) % ATT_BATCH == 0:
            static_blocks.append((gi, 0, 0))
            looped = (gi, nblk)
        else:
            static_blocks += [(gi, r, n) for r in range(dil) for n in range(nblk)]
    for i in range(0, len(static_blocks), ATT_BATCH + 1):
        run(static_blocks[i:i + ATT_BATCH + 1])
    if looped is not None:
        gi, nblk = looped

        def body(i, carry):
            run([(gi, 0, 1 + i * ATT_BATCH + j) for j in range(ATT_BATCH)])
            return carry
        lax.fori_loop(0, (nblk - 1) // ATT_BATCH, body, 0)

    rows_per_step = 256

    def merge(i, carry):
        rows = pl.ds(pl.multiple_of(i * rows_per_step, rows_per_step), rows_per_step)
        for half in range(ATT_GROUP_WIDTH // LANES):
            la, lb, lc = l0[half, rows, :], l1[half, rows, :], l2[half, rows, :]
            m = jnp.maximum(jnp.maximum(la, lb), lc)
            ea, eb, ec = jnp.exp(la - m), jnp.exp(lb - m), jnp.exp(lc - m)
            num = ea * o0[half, rows, :] + eb * o1[half, rows, :] + ec * o2[half, rows, :]
            ya_ref[0, rows, half * LANES:(half + 1) * LANES] = num / (ea + eb + ec)
        return carry
    lax.fori_loop(0, seq // rows_per_step, merge, 0)


def _attention(a0, a1, a2, *, batch):
    views = tuple(a.reshape(batch, a.shape[0] // batch, a.shape[1]) for a in (a0, a1, a2))
    b, s, _ = views[0].shape
    specs = [pl.BlockSpec((1,) + arr.shape[1:], lambda bi: (bi, 0, 0)) for arr in views]
    scratch = [pltpu.VMEM((ATT_GROUP_WIDTH // LANES, s, LANES), F32) for _ in range(6)]
    return pl.pallas_call(
        functools.partial(_attention_kernel, seq=s),
        grid=(b,),
        in_specs=specs,
        out_specs=pl.BlockSpec((1, s, ATT_GROUP_WIDTH), lambda bi: (bi, 0, 0)),
        out_shape=jax.ShapeDtypeStruct((b, s, ATT_GROUP_WIDTH), F32),
        scratch_shapes=scratch,
        compiler_params=pltpu.CompilerParams(dimension_semantics=("arbitrary",),
                                             vmem_limit_bytes=VMEM_LIMIT_BYTES),
        name="dilated_attention",
    )(*views)


def _deltanet_stages(q_ref, k_ref, v_ref, bd_ref, gnorm, state_ref, ob_ref, slot, *, tile):
    c = GDN_CHUNK
    d = GDN_HEAD_DIM
    heads = range(GDN_HEADS)
    pairs = [(2 * pp, 2 * pp + 1) for pp in range(GDN_HEADS // 2)]
    cols = [slice(hh * d, (hh + 1) * d) for hh in heads]
    glane = [GDN_HEADS + hh for hh in heads]
    ii = _chunk_time(lax.broadcasted_iota(jnp.int32, (c, 2 * c), 0))
    ll = lax.broadcasted_iota(jnp.int32, (c, 2 * c), 1)
    jj = _chunk_time(ll % c)
    lower = ii >= jj
    strict = ii > jj
    left = ll < c
    left_row = lax.broadcasted_iota(jnp.int32, (1, 2 * c), 1) < c
    keep_left = jnp.where(left, 1.0, 0.0).astype(BF16)
    keep_right = jnp.where(left, 0.0, 1.0).astype(BF16)
    ti = _chunk_time(lax.broadcasted_iota(jnp.int32, (c, c), 0))
    tj = _chunk_time(lax.broadcasted_iota(jnp.int32, (c, c), 1))
    tri_ones = jnp.where(ti >= tj, 1.0, 0.0).astype(BF16)

    def blockdiag(x):
        return jnp.concatenate([x * keep_left, x * keep_right], axis=0)

    def stack_diag(xa, xb):
        zero = jnp.zeros_like(xa)
        return jnp.concatenate([jnp.concatenate([xa, zero], axis=1), jnp.concatenate([zero, xb], axis=1)], axis=0)

    def prepare(ci, out):
        rows = slice(ci * c, (ci + 1) * c)
        bd = bd_ref[rows, :]
        bd_hi = bd.astype(BF16)
        bd_rest = bd - bd_hi.astype(F32)
        bd_mid = bd_rest.astype(BF16)
        bd_lo = (bd_rest - bd_mid.astype(F32)).astype(BF16)
        gcum = _dot(tri_ones, bd_hi) + _dot(tri_ones, bd_mid) + _dot(tri_ones, bd_lo)
        yield
        gtot = jnp.broadcast_to(gcum[c - 1:c, :], (c, LANES))
        gcum_t = jnp.concatenate([gcum, gcum], axis=0).T
        e_cum_all = jnp.exp(gcum)
        e_rest_all = jnp.exp(gtot - gcum)
        e_tot_all = jnp.exp(gtot)
        q = [q_ref[rows, cols[hh]] for hh in heads]
        k = [k_ref[rows, cols[hh]] for hh in heads]
        v = [v_ref[rows, cols[hh]] for hh in heads]
        beta = [bd[:, hh:hh + 1] for hh in heads]
        e_cum = [e_cum_all[:, gl:gl + 1] for gl in glane]
        kbeta = [k[hh] * beta[hh] for hh in heads]
        kq = [_dot_nt(jnp.concatenate([jnp.concatenate([kbeta[a], kbeta[b]], axis=1),
                                       jnp.concatenate([q[a], q[b]], axis=1)], axis=0).astype(BF16),
                      stack_diag(k[a].astype(BF16), k[b].astype(BF16)))
              for a, b in pairs]
        yield
        decay = [jnp.exp(jnp.where(lower,
                                   jnp.where(left, gcum[:, glane[a]:glane[a] + 1], gcum[:, glane[b]:glane[b] + 1])
                                   - jnp.where(left_row, gcum_t[glane[a]:glane[a] + 1, :], gcum_t[glane[b]:glane[b] + 1, :]),
                                   NEG_BIG)) for a, b in pairs]
        m = [jnp.where(strict, kq[pp][0:c] * decay[pp], 0.0) for pp in range(len(pairs))]
        n = [-mm for mm in m]
        pb = [mm.astype(BF16) for mm in m]
        p = [_dot(x, blockdiag(x)) for x in pb]
        yield
        rounds = 5
        for r in range(rounds):
            pb = [x.astype(BF16) for x in p]
            upd = [_dot(x, blockdiag(y.astype(BF16))) for x, y in zip(pb, n)]
            p_next = [_dot(x, blockdiag(x)) for x in pb] if r + 1 < rounds else None
            yield
            n = [y + x + u for y, x, u in zip(n, p, upd)]
            p = p_next
        rhs = [jnp.concatenate([v[hh] * beta[hh], kbeta[hh] * e_cum[hh]], axis=1) for hh in heads]
        nr = [_dot(n[pp].astype(BF16), stack_diag(rhs[a].astype(BF16), rhs[b].astype(BF16)))
              for pp, (a, b) in enumerate(pairs)]
        yield
        sol = [rhs[hh] + nr[hh // 2][:, (hh % 2) * 2 * d:(hh % 2 + 1) * 2 * d] for hh in heads]
        out.update(
            first=ci * c,
            u=[sol[hh][:, 0:d] for hh in heads],
            wq=[jnp.concatenate([sol[hh][:, d:2 * d], q[hh] * e_cum[hh]], axis=0).astype(BF16) for hh in heads],
            a_qk=[(kq[pp][c:2 * c] * decay[pp]).astype(BF16) for pp in range(len(pairs))],
            k_dec=[(k[hh] * e_rest_all[:, gl:gl + 1]).astype(BF16) for hh, gl in zip(heads, glane)],
            e_tot=[e_tot_all[0:1, gl:gl + 1] for gl in glane])

    for first in range(0, tile // c, GDN_GROUP):
        group = [dict() for _ in range(GDN_GROUP)]
        gens = [prepare(first + gi, group[gi]) for gi in range(GDN_GROUP)]
        for _ in range(GDN_PREP_LAYERS):
            for gen in gens:
                next(gen)
            yield
        for gen in gens:
            for _ in gen:
                pass
        for pre in group:
            state = [state_ref[hh] for hh in heads]
            ws = [_dot(pre["wq"][hh], state[hh].astype(BF16)) for hh in heads]
            yield
            v_new = [(pre["u"][hh] - ws[hh][0:c]).astype(BF16) for hh in heads]
            kv = [_dot_tn(pre["k_dec"][hh], v_new[hh]) for hh in heads]
            av = [_dot(pre["a_qk"][pp], stack_diag(v_new[a], v_new[b])) for pp, (a, b) in enumerate(pairs)]
            yield
            for hh in heads:
                state_ref[hh] = state[hh] * pre["e_tot"][hh] + kv[hh]
            for hh in heads:
                o = _rmsnorm(ws[hh][c:2 * c] + av[hh // 2][:, (hh % 2) * d:(hh % 2 + 1) * d], gnorm)
                for vv in range(c // SUBLANES):
                    ob_ref[slot, hh, pl.ds(pre["first"] + vv, SUBLANES, stride=c // SUBLANES), :] = (
                        o[vv * SUBLANES:(vv + 1) * SUBLANES])


GDN_GROUP = 4
GDN_PREP_LAYERS = 9
GDN_LAYERS_PER_GROUP = GDN_PREP_LAYERS + 2 * GDN_GROUP


def _mixer_out_stages(x_ref, ya_ref, ob_ref, slot, g_ref, wgt_ref, wa_ref, wb_ref, wo_ref, o_ref):
    x = x_ref[...]
    h = _rmsnorm(x, g_ref[...]).astype(BF16)
    ya = ya_ref[...].astype(BF16)
    blocks = [slice(j * MIX_BLOCK, (j + 1) * MIX_BLOCK) for j in range(D_MODEL // MIX_BLOCK)]
    gate_cols = lambda which, blk: slice(which * D_MODEL + blk.start, which * D_MODEL + blk.stop)
    yb = []
    for blk in blocks:
        gdn_gate = _dot(h, wgt_ref[:, gate_cols(0, blk)])
        yield
        ob = jnp.concatenate([ob_ref[slot, hh] for hh in range(blk.start // GDN_HEAD_DIM, blk.stop // GDN_HEAD_DIM)],
                             axis=1)
        yb.append((ob * (gdn_gate * _sigmoid(gdn_gate))).astype(BF16))
    yb = jnp.concatenate(yb, axis=1)
    merged = []
    for blk in blocks:
        gate_a = _dot(h, wgt_ref[:, gate_cols(1, blk)])
        branch_a = _dot(ya, wa_ref[:, blk])
        yield
        gate_b = _dot(h, wgt_ref[:, gate_cols(2, blk)])
        yield
        branch_b = _dot(yb, wb_ref[:, blk])
        yield
        merged.append((_sigmoid(gate_a) * branch_a + _sigmoid(gate_b) * branch_b).astype(BF16))
    merged = jnp.concatenate(merged, axis=1)
    for blk in blocks:
        o_ref[:, blk] = x[:, blk] + _dot(merged, wo_ref[:, blk])
        yield


MIX_GRANULES = 5 * (D_MODEL // MIX_BLOCK)


def _mixer_tail_kernel(q_ref, k_ref, v_ref, bd_ref, gn_ref, x_ref, ya_ref, g_ref, wgt_ref, wa_ref, wb_ref, wo_ref,
                       *refs, tile, tiles_per_seq, n_tiles):
    n_cast = (len(refs) - 3) // 2
    o_ref, state_ref, ob_ref = refs[n_cast], refs[2 * n_cast + 1], refs[2 * n_cast + 2]
    step = pl.program_id(0)

    @pl.when(step == 0)
    def _():
        ob_ref[...] = jnp.zeros(ob_ref.shape, F32)

    @pl.when(jnp.minimum(step, n_tiles - 1) % tiles_per_seq == 0)
    def _():
        state_ref[...] = jnp.zeros(state_ref.shape, F32)

    slot = step % 2
    gdn = _deltanet_stages(q_ref, k_ref, v_ref, bd_ref, gn_ref[...], state_ref, ob_ref, slot, tile=tile)
    mix = _mixer_out_stages(x_ref, ya_ref, ob_ref, 1 - slot, g_ref, wgt_ref, wa_ref, wb_ref, wo_ref, o_ref)
    _interleave(gdn, GDN_LAYERS_PER_GROUP * (tile // (GDN_CHUNK * GDN_GROUP)), mix, MIX_GRANULES)
    _cast_rows(refs[:n_cast], refs[n_cast + 1:2 * n_cast + 1])


def _mixer_tail(qb, kb, vb, bd, out_norm, x1, ya, norm_g, w_gates, w_a, w_b, w_o, *, tile, seq, cast=(), layer=0):
    n = x1.shape[0]
    n_tiles = n // tile
    cast_in, cast_out, cast_shapes = _cast_specs(cast, layer, n_tiles)
    cur = lambda w: pl.BlockSpec((tile, w), lambda s: (jnp.minimum(s, n_tiles - 1), 0))
    prev = lambda w: pl.BlockSpec((tile, w), lambda s: (jnp.maximum(s - 1, 0), 0))
    out = pl.pallas_call(
        functools.partial(_mixer_tail_kernel, tile=tile, tiles_per_seq=seq // tile, n_tiles=n_tiles),
        grid=(n_tiles + 1,),
        in_specs=[cur(GDN_WIDTH), cur(GDN_WIDTH), cur(GDN_WIDTH), cur(LANES), _resident((1, GDN_HEAD_DIM)),
                  prev(D_MODEL), prev(ATT_GROUP_WIDTH), _resident((1, D_MODEL)),
                  _resident(w_gates.shape), _resident(w_a.shape), _resident(w_b.shape), _resident(w_o.shape)]
                 + cast_in,
        out_specs=[prev(D_MODEL)] + cast_out,
        out_shape=[jax.ShapeDtypeStruct((n, D_MODEL), F32)] + cast_shapes,
        scratch_shapes=[pltpu.VMEM((GDN_HEADS, GDN_HEAD_DIM, GDN_HEAD_DIM), F32),
                        pltpu.VMEM((2, GDN_HEADS, tile, GDN_HEAD_DIM), F32)],
        compiler_params=pltpu.CompilerParams(dimension_semantics=("arbitrary",),
                                             vmem_limit_bytes=VMEM_LIMIT_BYTES),
        name="deltanet_mixer_out",
    )(qb, kb, vb, bd, out_norm, x1, ya, norm_g, w_gates, w_a, w_b, w_o, *cast)
    return out[0], out[1:]


def _rope_tables(seq):
    half = ATT_HEAD_DIM // 2
    inv_freq = ROPE_THETA ** (-jnp.arange(half, dtype=F32) / half)
    ang = jnp.arange(seq, dtype=F32)[:, None] * inv_freq[None, :]
    cos, sin = jnp.cos(ang), jnp.sin(ang)
    reps = LANES // ATT_HEAD_DIM
    return jnp.tile(jnp.concatenate([cos, cos], axis=-1), (1, reps)), jnp.tile(jnp.concatenate([-sin, sin], axis=-1), (1, reps))


def _pad_lanes(row, offset):
    return jnp.zeros((1, LANES), F32).at[0, offset:offset + row.shape[0]].set(row.astype(F32))


def _layer(x, ffn1_norm, ffn1_w_gate, ffn1_w_up, ffn1_w_down, mix_norm, w_in_all, gdn_conv_w, gdn_a_log, gdn_dt_bias,
           gdn_out_norm, w_branch_a, w_branch_b, w_out, ffn2_norm, ffn2_all, fin_g,
           *, layer, final_norm, tm_ffn, tm_mix, gdn_tile):
    b, s, _ = x.shape
    n = b * s
    row = lambda v: v.reshape(1, -1).astype(F32)
    x1 = _ffn(x.reshape(n, D_MODEL), row(ffn1_norm), ffn1_w_gate.astype(BF16), ffn1_w_up.astype(BF16),
              ffn1_w_down.astype(BF16), fin_g, final_norm=False, tm=tm_ffn)
    w_in = w_in_all[layer].astype(BF16)
    w_gates = w_in[:, W_IN_GATES:]
    cos_t, sin_t = _rope_tables(s)
    a0, a1, a2, qb, kb, vb, bd = _mixer_in(
        x1, row(mix_norm), w_in, gdn_conv_w.astype(F32), _pad_lanes(gdn_a_log, GDN_HEADS),
        _pad_lanes(gdn_dt_bias, GDN_HEADS), cos_t, sin_t, tm=tm_mix, seq=s)

    ya = _attention(a0, a1, a2, batch=b)
    x2, ffn2_w = _mixer_tail(qb, kb, vb, bd, row(gdn_out_norm), x1, ya.reshape(n, ATT_GROUP_WIDTH), row(mix_norm),
                             w_gates, w_branch_a.astype(BF16), w_branch_b.astype(BF16), w_out.astype(BF16),
                             tile=gdn_tile, seq=s, cast=ffn2_all, layer=layer)
    x3 = _ffn(x2, row(ffn2_norm), *ffn2_w, fin_g, final_norm=final_norm, tm=tm_ffn)
    return x3.reshape(b, s, D_MODEL)


def kernel(x, ffn1_norm, ffn1_w_gate, ffn1_w_up, ffn1_w_down, mix_norm, w_in, gdn_conv_w, gdn_a_log, gdn_dt_bias,
           gdn_out_norm, w_branch_a, w_branch_b, w_out, ffn2_norm, ffn2_w_gate, ffn2_w_up, ffn2_w_down, final_norm):
    depth = ffn1_norm.shape[0]
    fin_g = final_norm.reshape(1, -1).astype(F32)
    for layer in range(depth):
        x = _layer(x, ffn1_norm[layer], ffn1_w_gate[layer], ffn1_w_up[layer], ffn1_w_down[layer], mix_norm[layer],
                   w_in, gdn_conv_w[layer], gdn_a_log[layer], gdn_dt_bias[layer], gdn_out_norm[layer],
                   w_branch_a[layer], w_branch_b[layer], w_out[layer], ffn2_norm[layer],
                   (ffn2_w_gate, ffn2_w_up, ffn2_w_down), fin_g, layer=layer, final_norm=(layer == depth - 1),
                   tm_ffn=512, tm_mix=512, gdn_tile=512)
    return x
```

```python
import functools

import jax
import jax.numpy as jnp
from jax import lax
from jax.experimental import pallas as pl
from jax.experimental.pallas import tpu as pltpu

F32 = jnp.float32
BF16 = jnp.bfloat16

D_MODEL = 1024
D_FF = 2816
EPS = 1e-6

ATT_GROUPS = ((128, 1), (512, 4), (2048, 16))
ATT_HEADS_PER_GROUP = 4
ATT_HEAD_DIM = 64
ATT_BLOCK = 128
ATT_GROUP_WIDTH = ATT_HEADS_PER_GROUP * ATT_HEAD_DIM
ATT_QKV_WIDTH = len(ATT_GROUPS) * ATT_GROUP_WIDTH
ROPE_THETA = 10000.0
LOG2_E = 1.4426950408889634
LN_2 = 0.6931471805599453
ATT_Q_SCALE = ATT_HEAD_DIM ** -0.5 * LOG2_E

GDN_HEADS = 8
GDN_HEAD_DIM = 128
GDN_WIDTH = GDN_HEADS * GDN_HEAD_DIM
GDN_CONV = 4
GDN_CHUNK = 64

LANES = 128
SUBLANES = 8
VMEM_LIMIT_BYTES = 56 * 1024 * 1024

W_IN_GDN = 3 * ATT_QKV_WIDTH
W_IN_BD = W_IN_GDN + 3 * GDN_WIDTH
W_IN_GATES = W_IN_BD + 2 * GDN_HEADS
FFN_CHUNKS = ((0, 768), (768, 1536), (1536, 2304), (2304, 2816))
NEG_BIG = -1e30


def _resident(shape):
    nd = len(shape)
    return pl.BlockSpec(shape, lambda *_: (0,) * nd, pipeline_mode=pl.Buffered(1))


def _rmsnorm(x, g):
    return x * lax.rsqrt(jnp.mean(x * x, axis=-1, keepdims=True) + EPS) * g


def _sigmoid(x):
    return 1.0 / (1.0 + jnp.exp(-x))


def _dot(a, b):
    return jnp.dot(a, b, preferred_element_type=F32)


def _dot_nt(a, b):
    return lax.dot_general(a, b, (((1,), (1,)), ((), ())), preferred_element_type=F32)


def _dot_tn(a, b):
    return lax.dot_general(a, b, (((0,), (0,)), ((), ())), preferred_element_type=F32)


def _swiglu_residual(x, g, wg_ref, wu_ref, wd_ref):
    h = _rmsnorm(x, g).astype(BF16)
    acc = x
    for lo, hi in FFN_CHUNKS:
        gate = _dot(h, wg_ref[:, lo:hi])
        up = _dot(h, wu_ref[:, lo:hi])
        act = (0.5 * gate * _sigmoid(gate) * up).astype(BF16)
        acc = acc + _dot(act, wd_ref[lo:hi, :])
    return acc


def _cast_rows(src_refs, dst_refs):
    for src, dst in zip(src_refs, dst_refs):
        dst[...] = src[...].astype(BF16)


def _cast_specs(arrays, layer, steps):
    in_specs, out_specs, shapes = [], [], []
    for arr in arrays:
        _, n_rows, n_cols = arr.shape
        packed_rows = 2 * SUBLANES
        parts = next(p for p in range(steps, 0, -1) if n_rows % p == 0 and (n_rows // p) % packed_rows == 0)
        in_specs.append(pl.BlockSpec((None, n_rows // parts, n_cols),
                                     lambda s, parts=parts: (layer, jnp.minimum(s, parts - 1), 0)))
        out_specs.append(pl.BlockSpec((n_rows // parts, n_cols), lambda s, parts=parts: (jnp.minimum(s, parts - 1), 0)))
        shapes.append(jax.ShapeDtypeStruct((n_rows, n_cols), BF16))
    return in_specs, out_specs, shapes


def _ffn_kernel(x_ref, g_ref, wg_ref, wu_ref, wd_ref, fin_ref, o_ref, *, final_norm):
    y = _swiglu_residual(x_ref[...], g_ref[...], wg_ref, wu_ref, wd_ref)
    if final_norm:
        y = _rmsnorm(y, fin_ref[...])
    o_ref[...] = y


def _ffn(x, norm_g, wg, wu, wd, fin_g, *, final_norm, tm):
    n = x.shape[0]
    row = pl.BlockSpec((tm, D_MODEL), lambda i: (i, 0))
    return pl.pallas_call(
        functools.partial(_ffn_kernel, final_norm=final_norm),
        grid=(n // tm,),
        in_specs=[row, _resident((1, D_MODEL)), _resident((D_MODEL, D_FF)), _resident((D_MODEL, D_FF)),
                  _resident((D_FF, D_MODEL)), _resident((1, D_MODEL))],
        out_specs=row,
        out_shape=jax.ShapeDtypeStruct((n, D_MODEL), F32),
        compiler_params=pltpu.CompilerParams(dimension_semantics=("arbitrary",),
                                             vmem_limit_bytes=VMEM_LIMIT_BYTES),
        name="ffn_final" if final_norm else "ffn",
    )(x, norm_g, wg, wu, wd, fin_g)


MIX_BLOCK = 256


def _interleave(primary, n_primary, secondary, n_secondary):
    done = 0
    for i in range(n_primary):
        next(primary)
        while done * n_primary < (i + 1) * n_secondary:
            next(secondary)
            done += 1
    for gen in (primary, secondary):
        for _ in gen:
            pass


def _chunk_time(row):
    return SUBLANES * (row % SUBLANES) + row // SUBLANES


def _attention_qkv_stages(h, wm_ref, cos_ref, sin_ref, att_refs, stage_ref, *, tm):
    cos = cos_ref[...]
    sin = sin_ref[...]
    lane = lax.broadcasted_iota(jnp.int32, (1, LANES), 1)
    first_half = (lane % ATT_HEAD_DIM) < (ATT_HEAD_DIM // 2)
    slot = 0
    for part in range(3):
        for gi, (_, dil) in enumerate(ATT_GROUPS):
            col = part * ATT_QKV_WIDTH + gi * ATT_GROUP_WIDTH
            y = _dot(h, wm_ref[:, col:col + ATT_GROUP_WIDTH])
            yield
            for j in range(ATT_GROUP_WIDTH // LANES):
                blk = y[:, j * LANES:(j + 1) * LANES]
                if part < 2:
                    swapped = jnp.where(first_half, pltpu.roll(blk, LANES - ATT_HEAD_DIM // 2, 1),
                                        pltpu.roll(blk, ATT_HEAD_DIM // 2, 1))
                    blk = blk * cos + swapped * sin
                if part == 0:
                    blk = blk * ATT_Q_SCALE
                dst = part * ATT_GROUP_WIDTH + j * LANES
                if dil == 1:
                    att_refs[gi][:, dst:dst + LANES] = blk.astype(BF16)
                else:
                    stage_ref[slot] = blk
                    for r in range(dil):
                        rows = stage_ref[slot, pl.ds(r, tm // dil, stride=dil), :]
                        lo = r * 3 * ATT_GROUP_WIDTH + dst
                        att_refs[gi][:, lo:lo + LANES] = rows.astype(BF16)
                    slot += 1
                yield


ATT_QKV_STAGES = 3 * len(ATT_GROUPS) * (1 + ATT_GROUP_WIDTH // LANES)
MIXER_IN_STAGED = 3 * (len(ATT_GROUPS) - 1) * (ATT_GROUP_WIDTH // LANES)


def _deltanet_qkv_stages(h_perm, wm_ref, wbd_ref, convw_ref, alog_ref, dtb_ref, gdn_refs, bd_ref, carry_ref, *, tm):
    lane = lax.broadcasted_iota(jnp.int32, (1, LANES), 1)
    raw = _dot(h_perm, wbd_ref[...])
    yield
    z = raw + dtb_ref[...]
    softplus = jnp.maximum(z, 0.0) + jnp.log1p(jnp.exp(-jnp.abs(z)))
    g = -jnp.exp(alog_ref[...]) * softplus
    bd_ref[...] = jnp.where(lane < GDN_HEADS, _sigmoid(raw), jnp.where(lane < 2 * GDN_HEADS, g, 0.0))

    vregs = GDN_CHUNK // SUBLANES
    halo = GDN_CONV - 1
    chunks = tm // GDN_CHUNK
    last_sublane = lax.broadcasted_iota(jnp.int32, (SUBLANES, GDN_HEAD_DIM), 0) == SUBLANES - 1
    heads_per_block = MIX_BLOCK // GDN_HEAD_DIM
    for part in range(3):
        for blk in range(GDN_WIDTH // MIX_BLOCK):
            base = part * GDN_WIDTH + blk * MIX_BLOCK
            y = _dot(h_perm, wm_ref[:, W_IN_GDN + base:W_IN_GDN + base + MIX_BLOCK])
            yield
            for hb in range(heads_per_block):
                hh = blk * heads_per_block + hb
                col = part * GDN_WIDTH + hh * GDN_HEAD_DIM
                cur = y[:, hb * GDN_HEAD_DIM:(hb + 1) * GDN_HEAD_DIM]
                vreg = lambda c, v: cur[c * GDN_CHUNK + v * SUBLANES:c * GDN_CHUNK + (v + 1) * SUBLANES]
                prev_tile = carry_ref[:, col:col + GDN_HEAD_DIM]
                carry_ref[:, col:col + GDN_HEAD_DIM] = cur[tm - halo * SUBLANES:, :]
                wrapped = []
                for c in range(chunks):
                    row = []
                    for i in range(halo):
                        before = (prev_tile[i * SUBLANES:(i + 1) * SUBLANES] if c == 0
                                  else vreg(c - 1, vregs - halo + i))
                        own = vreg(c, vregs - halo + i)
                        row.append(pltpu.roll(jnp.where(last_sublane, before, own), 1, 0))
                    wrapped.append(row)
                acc = cur * convw_ref[halo:halo + 1, col:col + GDN_HEAD_DIM]
                for shift in range(1, GDN_CONV):
                    pieces = []
                    for c in range(chunks):
                        pieces += wrapped[c][halo - shift:]
                        pieces.append(cur[c * GDN_CHUNK:(c + 1) * GDN_CHUNK - shift * SUBLANES])
                    shifted = jnp.concatenate(pieces, axis=0)
                    acc = acc + shifted * convw_ref[halo - shift:halo - shift + 1, col:col + GDN_HEAD_DIM]
                act = acc * _sigmoid(acc)
                if part < 2:
                    act = act * lax.rsqrt(jnp.sum(act * act, axis=-1, keepdims=True) + EPS)
                if part == 0:
                    act = act * (GDN_HEAD_DIM ** -0.5)
                gdn_refs[part][:, hh * GDN_HEAD_DIM:(hh + 1) * GDN_HEAD_DIM] = act
                yield


DELTANET_QKV_STAGES = 1 + 3 * (GDN_WIDTH // MIX_BLOCK) + 3 * GDN_HEADS


def _mixer_in_kernel(x_ref, g_ref, wm_ref, wbd_ref, convw_ref, alog_ref, dtb_ref, cos_ref, sin_ref,
                     a0_ref, a1_ref, a2_ref, qb_ref, kb_ref, vb_ref, bd_ref, carry_ref, perm_ref, stage_ref, *, tm):
    @pl.when(pl.program_id(1) == 0)
    def _():
        carry_ref[...] = jnp.zeros(carry_ref.shape, F32)

    hf = _rmsnorm(x_ref[...], g_ref[...])
    h = hf.astype(BF16)
    vregs = GDN_CHUNK // SUBLANES
    for cb in range(D_MODEL // LANES):
        perm_ref[cb] = hf[:, cb * LANES:(cb + 1) * LANES]
    h_perm = jnp.concatenate(
        [jnp.concatenate([perm_ref[cb, pl.ds(c0 + v, SUBLANES, stride=vregs), :]
                          for c0 in range(0, tm, GDN_CHUNK) for v in range(vregs)], axis=0)
         for cb in range(D_MODEL // LANES)], axis=1).astype(BF16)
    gdn = _deltanet_qkv_stages(h_perm, wm_ref, wbd_ref, convw_ref, alog_ref, dtb_ref, (qb_ref, kb_ref, vb_ref), bd_ref,
                               carry_ref, tm=tm)
    att = _attention_qkv_stages(h, wm_ref, cos_ref, sin_ref, (a0_ref, a1_ref, a2_ref), stage_ref, tm=tm)
    _interleave(gdn, DELTANET_QKV_STAGES, att, ATT_QKV_STAGES)


def _mixer_in(x1, norm_g, w_in, conv_w, a_log, dt_bias, cos_t, sin_t, *, tm, seq):
    n = x1.shape[0]
    tiles_per_seq = seq // tm
    tile = lambda rows, w: pl.BlockSpec((rows, w), lambda bi, i: (bi * tiles_per_seq + i, 0))
    table = pl.BlockSpec((tm, LANES), lambda bi, i: (i, 0))
    wq = 3 * ATT_GROUP_WIDTH
    att_specs = [tile(tm // dil, dil * wq) for _, dil in ATT_GROUPS]
    att_shapes = [jax.ShapeDtypeStruct((n // dil, dil * wq), BF16) for _, dil in ATT_GROUPS]
    gdn = jax.ShapeDtypeStruct((n, GDN_WIDTH), F32)
    return pl.pallas_call(
        functools.partial(_mixer_in_kernel, tm=tm),
        grid=(n // seq, tiles_per_seq),
        in_specs=[tile(tm, D_MODEL), _resident((1, D_MODEL)),
                  pl.BlockSpec((D_MODEL, W_IN_BD), lambda bi, i: (0, 0), pipeline_mode=pl.Buffered(1)),
                  pl.BlockSpec((D_MODEL, LANES), lambda bi, i: (0, W_IN_BD // LANES), pipeline_mode=pl.Buffered(1)),
                  _resident((GDN_CONV, 3 * GDN_WIDTH)), _resident((1, LANES)), _resident((1, LANES)), table, table],
        out_specs=att_specs + [tile(tm, GDN_WIDTH)] * 3 + [tile(tm, LANES)],
        out_shape=att_shapes + [gdn] * 3 + [jax.ShapeDtypeStruct((n, LANES), F32)],
        scratch_shapes=[pltpu.VMEM(((GDN_CONV - 1) * SUBLANES, 3 * GDN_WIDTH), F32),
                        pltpu.VMEM((D_MODEL // LANES, tm, LANES), F32), pltpu.VMEM((MIXER_IN_STAGED, tm, LANES), F32)],
        compiler_params=pltpu.CompilerParams(dimension_semantics=("arbitrary", "arbitrary"),
                                             vmem_limit_bytes=VMEM_LIMIT_BYTES),
        name="mixer_in",
    )(x1, norm_g, w_in, w_in, conv_w, a_log, dt_bias, cos_t, sin_t)


ATT_BATCH = 3


def _attention_blocks(items):
    lane = lax.broadcasted_iota(jnp.int32, (1, ATT_GROUP_WIDTH), 1)
    heads = range(ATT_HEADS_PER_GROUP)
    in_head = [(lane // ATT_HEAD_DIM) == hh for hh in heads]
    keep = [jnp.where(in_head[hh], 1.0, 0.0).astype(BF16) for hh in heads]
    nq = ATT_BLOCK
    s_all = [_dot_nt(jnp.concatenate([q * keep[hh] for hh in heads], axis=0), k) for q, k, _, _ in items]
    stats, p_all = [], []
    for (_, _, _, valid), sa in zip(items, s_all):
        s = [jnp.where(valid, sa[hh * nq:(hh + 1) * nq], NEG_BIG) for hh in heads]
        m = [jnp.max(s[hh], axis=-1, keepdims=True) for hh in heads]
        p = [jnp.exp2(s[hh] - m[hh]) for hh in heads]
        l = [jnp.sum(p[hh], axis=-1, keepdims=True) for hh in heads]
        stats.append((m, l))
        p_all.append(jnp.concatenate([p[hh].astype(BF16) for hh in heads], axis=0))
    pv_all = [_dot(ps, v) for ps, (_, _, v, _) in zip(p_all, items)]
    outs = []
    for pv, (m, l) in zip(pv_all, stats):
        o = jnp.zeros((ATT_BLOCK, ATT_GROUP_WIDTH), F32)
        lse = jnp.zeros((ATT_BLOCK, ATT_GROUP_WIDTH), F32)
        for hh in heads:
            o = jnp.where(in_head[hh], pv[hh * nq:(hh + 1) * nq] * (1.0 / l[hh]), o)
            lse = jnp.where(in_head[hh], m[hh] * LN_2 + jnp.log(l[hh]), lse)
        outs.append((o, lse))
    return outs


def _attention_kernel(a0_ref, a1_ref, a2_ref, ya_ref, o0, l0, o1, l1, o2, l2, *, seq):
    in_refs = (a0_ref, a1_ref, a2_ref)
    o_refs = (o0, o1, o2)
    l_refs = (l0, l1, l2)
    qi = lax.broadcasted_iota(jnp.int32, (ATT_BLOCK, ATT_BLOCK), 0)
    kj = lax.broadcasted_iota(jnp.int32, (ATT_BLOCK, ATT_BLOCK), 1)
    causal = kj <= qi
    qi2 = lax.broadcasted_iota(jnp.int32, (ATT_BLOCK, 2 * ATT_BLOCK), 0)
    kj2 = lax.broadcasted_iota(jnp.int32, (ATT_BLOCK, 2 * ATT_BLOCK), 1)
    band = (kj2 >= qi2) & (kj2 - ATT_BLOCK <= qi2)
    wq = 3 * ATT_GROUP_WIDTH

    def load(gi, r, n):
        src, base = in_refs[gi], r * wq
        if isinstance(n, int) and n == 0:
            qrows = krows = slice(0, ATT_BLOCK)
            valid = causal
        else:
            start = lambda x: x if isinstance(x, int) else pl.multiple_of(x, ATT_BLOCK)
            qrows = pl.ds(start(n * ATT_BLOCK), ATT_BLOCK)
            krows = pl.ds(start((n - 1) * ATT_BLOCK), 2 * ATT_BLOCK)
            valid = band
        return (src[0, qrows, base:base + ATT_GROUP_WIDTH],
                src[0, krows, base + ATT_GROUP_WIDTH:base + 2 * ATT_GROUP_WIDTH],
                src[0, krows, base + 2 * ATT_GROUP_WIDTH:base + 3 * ATT_GROUP_WIDTH], valid)

    def store(gi, r, n, o, lse):
        dil = ATT_GROUPS[gi][1]
        if dil == 1:
            first = n * ATT_BLOCK
            rows = pl.ds(first if isinstance(first, int) else pl.multiple_of(first, ATT_BLOCK), ATT_BLOCK)
        else:
            rows = pl.ds(n * ATT_BLOCK * dil + r, ATT_BLOCK, stride=dil)
        for half in range(ATT_GROUP_WIDTH // LANES):
            o_refs[gi][half, rows, :] = o[:, half * LANES:(half + 1) * LANES]
            l_refs[gi][half, rows, :] = lse[:, half * LANES:(half + 1) * LANES]

    def run(blocks):
        for (gi, r, n), (o, lse) in zip(blocks, _attention_blocks([load(*blk) for blk in blocks])):
            store(gi, r, n, o, lse)

    static_blocks = []
    looped = None
    for gi, (window, dil) in enumerate(ATT_GROUPS):
        assert window // dil == ATT_BLOCK
        nblk = seq // dil // ATT_BLOCK
        if dil == 1 and (nblk - 1) % ATT_BATCH == 0:
            static_blocks.append((gi, 0, 0))
            looped = (gi, nblk)
        else:
            static_blocks += [(gi, r, n) for r in range(dil) for n in range(nblk)]
    for i in range(0, len(static_blocks), ATT_BATCH + 1):
        run(static_blocks[i:i + ATT_BATCH + 1])
    if looped is not None:
        gi, nblk = looped

        def body(i, carry):
            run([(gi, 0, 1 + i * ATT_BATCH + j) for j in range(ATT_BATCH)])
            return carry
        lax.fori_loop(0, (nblk - 1) // ATT_BATCH, body, 0)

    rows_per_step = 256

    def merge(i, carry):
        rows = pl.ds(pl.multiple_of(i * rows_per_step, rows_per_step), rows_per_step)
        for half in range(ATT_GROUP_WIDTH // LANES):
            la, lb, lc = l0[half, rows, :], l1[half, rows, :], l2[half, rows, :]
            m = jnp.maximum(jnp.maximum(la, lb), lc)
            ea, eb, ec = jnp.exp(la - m), jnp.exp(lb - m), jnp.exp(lc - m)
            num = ea * o0[half, rows, :] + eb * o1[half, rows, :] + ec * o2[half, rows, :]
            ya_ref[0, rows, half * LANES:(half + 1) * LANES] = num / (ea + eb + ec)
        return carry
    lax.fori_loop(0, seq // rows_per_step, merge, 0)


def _attention(a0, a1, a2, *, batch):
    views = tuple(a.reshape(batch, a.shape[0] // batch, a.shape[1]) for a in (a0, a1, a2))
    b, s, _ = views[0].shape
    specs = [pl.BlockSpec((1,) + arr.shape[1:], lambda bi: (bi, 0, 0)) for arr in views]
    scratch = [pltpu.VMEM((ATT_GROUP_WIDTH // LANES, s, LANES), F32) for _ in range(6)]
    return pl.pallas_call(
        functools.partial(_attention_kernel, seq=s),
        grid=(b,),
        in_specs=specs,
        out_specs=pl.BlockSpec((1, s, ATT_GROUP_WIDTH), lambda bi: (bi, 0, 0)),
        out_shape=jax.ShapeDtypeStruct((b, s, ATT_GROUP_WIDTH), F32),
        scratch_shapes=scratch,
        compiler_params=pltpu.CompilerParams(dimension_semantics=("arbitrary",),
                                             vmem_limit_bytes=VMEM_LIMIT_BYTES),
        name="dilated_attention",
    )(*views)


def _deltanet_stages(q_ref, k_ref, v_ref, bd_ref, gnorm, state_ref, ob_ref, slot, *, tile):
    c = GDN_CHUNK
    d = GDN_HEAD_DIM
    heads = range(GDN_HEADS)
    pairs = [(2 * pp, 2 * pp + 1) for pp in range(GDN_HEADS // 2)]
    cols = [slice(hh * d, (hh + 1) * d) for hh in heads]
    glane = [GDN_HEADS + hh for hh in heads]
    ii = _chunk_time(lax.broadcasted_iota(jnp.int32, (c, 2 * c), 0))
    ll = lax.broadcasted_iota(jnp.int32, (c, 2 * c), 1)
    jj = _chunk_time(ll % c)
    lower = ii >= jj
    strict = ii > jj
    left = ll < c
    left_row = lax.broadcasted_iota(jnp.int32, (1, 2 * c), 1) < c
    keep_left = jnp.where(left, 1.0, 0.0).astype(BF16)
    keep_right = jnp.where(left, 0.0, 1.0).astype(BF16)
    ti = _chunk_time(lax.broadcasted_iota(jnp.int32, (c, c), 0))
    tj = _chunk_time(lax.broadcasted_iota(jnp.int32, (c, c), 1))
    tri_ones = jnp.where(ti >= tj, 1.0, 0.0).astype(BF16)

    def blockdiag(x):
        return jnp.concatenate([x * keep_left, x * keep_right], axis=0)

    def stack_diag(xa, xb):
        zero = jnp.zeros_like(xa)
        return jnp.concatenate([jnp.concatenate([xa, zero], axis=1), jnp.concatenate([zero, xb], axis=1)], axis=0)

    def prepare(ci, out):
        rows = slice(ci * c, (ci + 1) * c)
        bd = bd_ref[rows, :]
        bd_hi = bd.astype(BF16)
        bd_rest = bd - bd_hi.astype(F32)
        bd_mid = bd_rest.astype(BF16)
        bd_lo = (bd_rest - bd_mid.astype(F32)).astype(BF16)
        gcum = _dot(tri_ones, bd_hi) + _dot(tri_ones, bd_mid) + _dot(tri_ones, bd_lo)
        yield
        gtot = jnp.broadcast_to(gcum[c - 1:c, :], (c, LANES))
        gcum_t = jnp.concatenate([gcum, gcum], axis=0).T
        e_cum_all = jnp.exp(gcum)
        e_rest_all = jnp.exp(gtot - gcum)
        e_tot_all = jnp.exp(gtot)
        q = [q_ref[rows, cols[hh]] for hh in heads]
        k = [k_ref[rows, cols[hh]] for hh in heads]
        v = [v_ref[rows, cols[hh]] for hh in heads]
        beta = [bd[:, hh:hh + 1] for hh in heads]
        e_cum = [e_cum_all[:, gl:gl + 1] for gl in glane]
        kbeta = [k[hh] * beta[hh] for hh in heads]
        kq = [_dot_nt(jnp.concatenate([jnp.concatenate([kbeta[a], kbeta[b]], axis=1),
                                       jnp.concatenate([q[a], q[b]], axis=1)], axis=0).astype(BF16),
                      stack_diag(k[a].astype(BF16), k[b].astype(BF16)))
              for a, b in pairs]
        yield
        decay = [jnp.exp(jnp.where(lower,
                                   jnp.where(left, gcum[:, glane[a]:glane[a] + 1], gcum[:, glane[b]:glane[b] + 1])
                                   - jnp.where(left_row, gcum_t[glane[a]:glane[a] + 1, :], gcum_t[glane[b]:glane[b] + 1, :]),
                                   NEG_BIG)) for a, b in pairs]
        m = [jnp.where(strict, kq[pp][0:c] * decay[pp], 0.0) for pp in range(len(pairs))]
        n = [-mm for mm in m]
        pb = [mm.astype(BF16) for mm in m]
        p = [_dot(x, blockdiag(x)) for x in pb]
        yield
        rounds = 5
        for r in range(rounds):
            pb = [x.astype(BF16) for x in p]
            upd = [_dot(x, blockdiag(y.astype(BF16))) for x, y in zip(pb, n)]
            p_next = [_dot(x, blockdiag(x)) for x in pb] if r + 1 < rounds else None
            yield
            n = [y + x + u for y, x, u in zip(n, p, upd)]
            p = p_next
        rhs = [jnp.concatenate([v[hh] * beta[hh], kbeta[hh] * e_cum[hh]], axis=1) for hh in heads]
        nr = [_dot(n[pp].astype(BF16), stack_diag(rhs[a].astype(BF16), rhs[b].astype(BF16)))
              for pp, (a, b) in enumerate(pairs)]
        yield
        sol = [rhs[hh] + nr[hh // 2][:, (hh % 2) * 2 * d:(hh % 2 + 1) * 2 * d] for hh in heads]
        out.update(
            first=ci * c,
            u=[sol[hh][:, 0:d] for hh in heads],
            wq=[jnp.concatenate([sol[hh][:, d:2 * d], q[hh] * e_cum[hh]], axis=0).astype(BF16) for hh in heads],
            a_qk=[(kq[pp][c:2 * c] * decay[pp]).astype(BF16) for pp in range(len(pairs))],
            k_dec=[(k[hh] * e_rest_all[:, gl:gl + 1]).astype(BF16) for hh, gl in zip(heads, glane)],
            e_tot=[e_tot_all[0:1, gl:gl + 1] for gl in glane])

    for first in range(0, tile // c, GDN_GROUP):
        group = [dict() for _ in range(GDN_GROUP)]
        gens = [prepare(first + gi, group[gi]) for gi in range(GDN_GROUP)]
        for _ in range(GDN_PREP_LAYERS):
            for gen in gens:
                next(gen)
            yield
        for gen in gens:
            for _ in gen:
                pass
        for pre in group:
            state = [state_ref[hh] for hh in heads]
            ws = [_dot(pre["wq"][hh], state[hh].astype(BF16)) for hh in heads]
            yield
            v_new = [(pre["u"][hh] - ws[hh][0:c]).astype(BF16) for hh in heads]
            kv = [_dot_tn(pre["k_dec"][hh], v_new[hh]) for hh in heads]
            av = [_dot(pre["a_qk"][pp], stack_diag(v_new[a], v_new[b])) for pp, (a, b) in enumerate(pairs)]
            yield
            for hh in heads:
                state_ref[hh] = state[hh] * pre["e_tot"][hh] + kv[hh]
            for hh in heads:
                o = _rmsnorm(ws[hh][c:2 * c] + av[hh // 2][:, (hh % 2) * d:(hh % 2 + 1) * d], gnorm)
                for vv in range(c // SUBLANES):
                    ob_ref[slot, hh, pl.ds(pre["first"] + vv, SUBLANES, stride=c // SUBLANES), :] = (
                        o[vv * SUBLANES:(vv + 1) * SUBLANES])


GDN_GROUP = 4
GDN_PREP_LAYERS = 9
GDN_LAYERS_PER_GROUP = GDN_PREP_LAYERS + 2 * GDN_GROUP


def _mixer_out_stages(x_ref, ya_ref, ob_ref, slot, g_ref, wgt_ref, wa_ref, wb_ref, wo_ref, o_ref):
    x = x_ref[...]
    h = _rmsnorm(x, g_ref[...]).astype(BF16)
    ya = ya_ref[...].astype(BF16)
    blocks = [slice(j * MIX_BLOCK, (j + 1) * MIX_BLOCK) for j in range(D_MODEL // MIX_BLOCK)]
    gate_cols = lambda which, blk: slice(which * D_MODEL + blk.start, which * D_MODEL + blk.stop)
    yb = []
    for blk in blocks:
        gdn_gate = _dot(h, wgt_ref[:, gate_cols(0, blk)])
        yield
        ob = jnp.concatenate([ob_ref[slot, hh] for hh in range(blk.start // GDN_HEAD_DIM, blk.stop // GDN_HEAD_DIM)],
                             axis=1)
        yb.append((ob * (gdn_gate * _sigmoid(gdn_gate))).astype(BF16))
    yb = jnp.concatenate(yb, axis=1)
    merged = []
    for blk in blocks:
        gate_a = _dot(h, wgt_ref[:, gate_cols(1, blk)])
        branch_a = _dot(ya, wa_ref[:, blk])
        yield
        gate_b = _dot(h, wgt_ref[:, gate_cols(2, blk)])
        yield
        branch_b = _dot(yb, wb_ref[:, blk])
        yield
        merged.append((_sigmoid(gate_a) * branch_a + _sigmoid(gate_b) * branch_b).astype(BF16))
    merged = jnp.concatenate(merged, axis=1)
    for blk in blocks:
        o_ref[:, blk] = x[:, blk] + _dot(merged, wo_ref[:, blk])
        yield


MIX_GRANULES = 5 * (D_MODEL // MIX_BLOCK)


def _mixer_tail_kernel(q_ref, k_ref, v_ref, bd_ref, gn_ref, x_ref, ya_ref, g_ref, wgt_ref, wa_ref, wb_ref, wo_ref,
                       *refs, tile, tiles_per_seq, n_tiles):
    n_cast = (len(refs) - 3) // 2
    o_ref, state_ref, ob_ref = refs[n_cast], refs[2 * n_cast + 1], refs[2 * n_cast + 2]
    step = pl.program_id(0)

    @pl.when(step == 0)
    def _():
        ob_ref[...] = jnp.zeros(ob_ref.shape, F32)

    @pl.when(jnp.minimum(step, n_tiles - 1) % tiles_per_seq == 0)
    def _():
        state_ref[...] = jnp.zeros(state_ref.shape, F32)

    slot = step % 2
    gdn = _deltanet_stages(q_ref, k_ref, v_ref, bd_ref, gn_ref[...], state_ref, ob_ref, slot, tile=tile)
    mix = _mixer_out_stages(x_ref, ya_ref, ob_ref, 1 - slot, g_ref, wgt_ref, wa_ref, wb_ref, wo_ref, o_ref)
    _interleave(gdn, GDN_LAYERS_PER_GROUP * (tile // (GDN_CHUNK * GDN_GROUP)), mix, MIX_GRANULES)
    _cast_rows(refs[:n_cast], refs[n_cast + 1:2 * n_cast + 1])


def _mixer_tail(qb, kb, vb, bd, out_norm, x1, ya, norm_g, w_gates, w_a, w_b, w_o, *, tile, seq, cast=(), layer=0):
    n = x1.shape[0]
    n_tiles = n // tile
    cast_in, cast_out, cast_shapes = _cast_specs(cast, layer, n_tiles)
    cur = lambda w: pl.BlockSpec((tile, w), lambda s: (jnp.minimum(s, n_tiles - 1), 0))
    prev = lambda w: pl.BlockSpec((tile, w), lambda s: (jnp.maximum(s - 1, 0), 0))
    out = pl.pallas_call(
        functools.partial(_mixer_tail_kernel, tile=tile, tiles_per_seq=seq // tile, n_tiles=n_tiles),
        grid=(n_tiles + 1,),
        in_specs=[cur(GDN_WIDTH), cur(GDN_WIDTH), cur(GDN_WIDTH), cur(LANES), _resident((1, GDN_HEAD_DIM)),
                  prev(D_MODEL), prev(ATT_GROUP_WIDTH), _resident((1, D_MODEL)),
                  _resident(w_gates.shape), _resident(w_a.shape), _resident(w_b.shape), _resident(w_o.shape)]
                 + cast_in,
        out_specs=[prev(D_MODEL)] + cast_out,
        out_shape=[jax.ShapeDtypeStruct((n, D_MODEL), F32)] + cast_shapes,
        scratch_shapes=[pltpu.VMEM((GDN_HEADS, GDN_HEAD_DIM, GDN_HEAD_DIM), F32),
                        pltpu.VMEM((2, GDN_HEADS, tile, GDN_HEAD_DIM), F32)],
        compiler_params=pltpu.CompilerParams(dimension_semantics=("arbitrary",),
                                             vmem_limit_bytes=VMEM_LIMIT_BYTES),
        name="deltanet_mixer_out",
    )(qb, kb, vb, bd, out_norm, x1, ya, norm_g, w_gates, w_a, w_b, w_o, *cast)
    return out[0], out[1:]


def _rope_tables(seq):
    half = ATT_HEAD_DIM // 2
    inv_freq = ROPE_THETA ** (-jnp.arange(half, dtype=F32) / half)
    ang = jnp.arange(seq, dtype=F32)[:, None] * inv_freq[None, :]
    cos, sin = jnp.cos(ang), jnp.sin(ang)
    reps = LANES // ATT_HEAD_DIM
    return jnp.tile(jnp.concatenate([cos, cos], axis=-1), (1, reps)), jnp.tile(jnp.concatenate([-sin, sin], axis=-1), (1, reps))


def _pad_lanes(row, offset):
    return jnp.zeros((1, LANES), F32).at[0, offset:offset + row.shape[0]].set(row.astype(F32))


def _layer(x, ffn1_norm, ffn1_w_gate, ffn1_w_up, ffn1_w_down, mix_norm, w_in_all, gdn_conv_w, gdn_a_log, gdn_dt_bias,
           gdn_out_norm, w_branch_a, w_branch_b, w_out, ffn2_norm, ffn2_all, fin_g,
           *, layer, final_norm, tm_ffn, tm_mix, gdn_tile):
    b, s, _ = x.shape
    n = b * s
    row = lambda v: v.reshape(1, -1).astype(F32)
    x1 = _ffn(x.reshape(n, D_MODEL), row(ffn1_norm), ffn1_w_gate.astype(BF16), ffn1_w_up.astype(BF16),
              ffn1_w_down.astype(BF16), fin_g, final_norm=False, tm=tm_ffn)
    w_in = w_in_all[layer].astype(BF16)
    w_gates = w_in[:, W_IN_GATES:]
    cos_t, sin_t = _rope_tables(s)
    a0, a1, a2, qb, kb, vb, bd = _mixer_in(
        x1, row(mix_norm), w_in, gdn_conv_w.astype(F32), _pad_lanes(gdn_a_log, GDN_HEADS),
        _pad_lanes(gdn_dt_bias, GDN_HEADS), cos_t, sin_t, tm=tm_mix, seq=s)

    ya = _attention(a0, a1, a2, batch=b)
    x2, ffn2_w = _mixer_tail(qb, kb, vb, bd, row(gdn_out_norm), x1, ya.reshape(n, ATT_GROUP_WIDTH), row(mix_norm),
                             w_gates, w_branch_a.astype(BF16), w_branch_b.astype(BF16), w_out.astype(BF16),
                             tile=gdn_tile, seq=s, cast=ffn2_all, layer=layer)
    x3 = _ffn(x2, row(ffn2_norm), *ffn2_w, fin_g, final_norm=final_norm, tm=tm_ffn)
    return x3.reshape(b, s, D_MODEL)


def kernel(x, ffn1_norm, ffn1_w_gate, ffn1_w_up, ffn1_w_down, mix_norm, w_in, gdn_conv_w, gdn_a_log, gdn_dt_bias,
           gdn_out_norm, w_branch_a, w_branch_b, w_out, ffn2_norm, ffn2_w_gate, ffn2_w_up, ffn2_w_down, final_norm):
    depth = ffn1_norm.shape[0]
    fin_g = final_norm.reshape(1, -1).astype(F32)
    for layer in range(depth):
        x = _layer(x, ffn1_norm[layer], ffn1_w_gate[layer], ffn1_w_up[layer], ffn1_w_down[layer], mix_norm[layer],
                   w_in, gdn_conv_w[layer], gdn_a_log[layer], gdn_dt_bias[layer], gdn_out_norm[layer],
                   w_branch_a[layer], w_branch_b[layer], w_out[layer], ffn2_norm[layer],
                   (ffn2_w_gate, ffn2_w_up, ffn2_w_down), fin_g, layer=layer, final_norm=(layer == depth - 1),
                   tm_ffn=512, tm_mix=512, gdn_tile=512)
    return x
```

```python
import functools

import jax
import jax.numpy as jnp
from jax import lax
from jax.experimental import pallas as pl
from jax.experimental.pallas import tpu as pltpu

F32 = jnp.float32
BF16 = jnp.bfloat16

D_MODEL = 1024
D_FF = 2816
EPS = 1e-6

ATT_GROUPS = ((128, 1), (512, 4), (2048, 16))
ATT_HEADS_PER_GROUP = 4
ATT_HEAD_DIM = 64
ATT_BLOCK = 128
ATT_GROUP_WIDTH = ATT_HEADS_PER_GROUP * ATT_HEAD_DIM
ATT_QKV_WIDTH = len(ATT_GROUPS) * ATT_GROUP_WIDTH
ROPE_THETA = 10000.0

GDN_HEADS = 8
GDN_HEAD_DIM = 128
GDN_WIDTH = GDN_HEADS * GDN_HEAD_DIM
GDN_CONV = 4
GDN_CHUNK = 64

LANES = 128
SUBLANES = 8
VMEM_LIMIT_BYTES = 56 * 1024 * 1024

W_IN_GDN = 3 * ATT_QKV_WIDTH
W_IN_BD = W_IN_GDN + 3 * GDN_WIDTH
W_IN_GATES = W_IN_BD + 2 * GDN_HEADS
FFN_CHUNKS = ((0, 768), (768, 1536), (1536, 2304), (2304, 2816))
NEG_BIG = -1e30


def _resident(shape):
    nd = len(shape)
    return pl.BlockSpec(shape, lambda *_: (0,) * nd, pipeline_mode=pl.Buffered(1))


def _rmsnorm(x, g):
    return x * lax.rsqrt(jnp.mean(x * x, axis=-1, keepdims=True) + EPS) * g


def _sigmoid(x):
    return 1.0 / (1.0 + jnp.exp(-x))


def _dot(a, b):
    return jnp.dot(a, b, preferred_element_type=F32)


def _dot_nt(a, b):
    return lax.dot_general(a, b, (((1,), (1,)), ((), ())), preferred_element_type=F32)


def _dot_tn(a, b):
    return lax.dot_general(a, b, (((0,), (0,)), ((), ())), preferred_element_type=F32)


def _swiglu_residual(x, g, wg_ref, wu_ref, wd_ref):
    h = _rmsnorm(x, g).astype(BF16)
    acc = x
    for lo, hi in FFN_CHUNKS:
        gate = _dot(h, wg_ref[:, lo:hi])
        up = _dot(h, wu_ref[:, lo:hi])
        act = (0.5 * gate * _sigmoid(gate) * up).astype(BF16)
        acc = acc + _dot(act, wd_ref[lo:hi, :])
    return acc


def _cast_rows(src_refs, dst_refs):
    for src, dst in zip(src_refs, dst_refs):
        dst[...] = src[...].astype(BF16)


def _cast_specs(arrays, layer, steps):
    in_specs, out_specs, shapes = [], [], []
    for arr in arrays:
        _, n_rows, n_cols = arr.shape
        packed_rows = 2 * SUBLANES
        parts = next(p for p in range(steps, 0, -1) if n_rows % p == 0 and (n_rows // p) % packed_rows == 0)
        in_specs.append(pl.BlockSpec((None, n_rows // parts, n_cols),
                                     lambda s, parts=parts: (layer, jnp.minimum(s, parts - 1), 0)))
        out_specs.append(pl.BlockSpec((n_rows // parts, n_cols), lambda s, parts=parts: (jnp.minimum(s, parts - 1), 0)))
        shapes.append(jax.ShapeDtypeStruct((n_rows, n_cols), BF16))
    return in_specs, out_specs, shapes


def _ffn_kernel(x_ref, g_ref, wg_ref, wu_ref, wd_ref, fin_ref, o_ref, *, final_norm):
    y = _swiglu_residual(x_ref[...], g_ref[...], wg_ref, wu_ref, wd_ref)
    if final_norm:
        y = _rmsnorm(y, fin_ref[...])
    o_ref[...] = y


def _ffn(x, norm_g, wg, wu, wd, fin_g, *, final_norm, tm):
    n = x.shape[0]
    row = pl.BlockSpec((tm, D_MODEL), lambda i: (i, 0))
    return pl.pallas_call(
        functools.partial(_ffn_kernel, final_norm=final_norm),
        grid=(n // tm,),
        in_specs=[row, _resident((1, D_MODEL)), _resident((D_MODEL, D_FF)), _resident((D_MODEL, D_FF)),
                  _resident((D_FF, D_MODEL)), _resident((1, D_MODEL))],
        out_specs=row,
        out_shape=jax.ShapeDtypeStruct((n, D_MODEL), F32),
        compiler_params=pltpu.CompilerParams(dimension_semantics=("arbitrary",),
                                             vmem_limit_bytes=VMEM_LIMIT_BYTES),
        name="ffn_final" if final_norm else "ffn",
    )(x, norm_g, wg, wu, wd, fin_g)


MIX_BLOCK = 256
QKV_BLOCK = 512


def _interleave(primary, n_primary, secondary, n_secondary):
    done = 0
    for i in range(n_primary):
        next(primary)
        while done * n_primary < (i + 1) * n_secondary:
            next(secondary)
            done += 1
    for gen in (primary, secondary):
        for _ in gen:
            pass


def _chunk_time(row):
    return SUBLANES * (row % SUBLANES) + row // SUBLANES


def _attention_qkv_stages(h, wm_ref, cos_ref, sin_ref, att_refs, stage_ref, *, tm):
    cos = cos_ref[...]
    sin = sin_ref[...]
    lane = lax.broadcasted_iota(jnp.int32, (1, LANES), 1)
    first_half = (lane % ATT_HEAD_DIM) < (ATT_HEAD_DIM // 2)
    slot = 0
    for part in range(3):
        for gi, (_, dil) in enumerate(ATT_GROUPS):
            col = part * ATT_QKV_WIDTH + gi * ATT_GROUP_WIDTH
            y = _dot(h, wm_ref[:, col:col + ATT_GROUP_WIDTH])
            yield
            for j in range(ATT_GROUP_WIDTH // LANES):
                blk = y[:, j * LANES:(j + 1) * LANES]
                if part < 2:
                    swapped = jnp.where(first_half, pltpu.roll(blk, LANES - ATT_HEAD_DIM // 2, 1),
                                        pltpu.roll(blk, ATT_HEAD_DIM // 2, 1))
                    blk = blk * cos + swapped * sin
                if part == 0:
                    blk = blk * (ATT_HEAD_DIM ** -0.5)
                dst = part * ATT_GROUP_WIDTH + j * LANES
                if dil == 1:
                    att_refs[gi][:, dst:dst + LANES] = blk.astype(BF16)
                else:
                    stage_ref[slot] = blk
                    for r in range(dil):
                        rows = stage_ref[slot, pl.ds(r, tm // dil, stride=dil), :]
                        lo = r * 3 * ATT_GROUP_WIDTH + dst
                        att_refs[gi][:, lo:lo + LANES] = rows.astype(BF16)
                    slot += 1
                yield


ATT_QKV_STAGES = 3 * len(ATT_GROUPS) * (1 + ATT_GROUP_WIDTH // LANES)
MIXER_IN_STAGED = 3 * (len(ATT_GROUPS) - 1) * (ATT_GROUP_WIDTH // LANES)


def _deltanet_qkv_stages(h_perm, wm_ref, wbd_ref, convw_ref, alog_ref, dtb_ref, gdn_refs, bd_ref, carry_ref, *, tm):
    lane = lax.broadcasted_iota(jnp.int32, (1, LANES), 1)
    raw = _dot(h_perm, wbd_ref[...])
    yield
    z = raw + dtb_ref[...]
    softplus = jnp.maximum(z, 0.0) + jnp.log1p(jnp.exp(-jnp.abs(z)))
    g = -jnp.exp(alog_ref[...]) * softplus
    bd_ref[...] = jnp.where(lane < GDN_HEADS, _sigmoid(raw), jnp.where(lane < 2 * GDN_HEADS, g, 0.0))

    vregs = GDN_CHUNK // SUBLANES
    halo = GDN_CONV - 1
    chunks = tm // GDN_CHUNK
    last_sublane = lax.broadcasted_iota(jnp.int32, (SUBLANES, GDN_HEAD_DIM), 0) == SUBLANES - 1
    heads_per_block = QKV_BLOCK // GDN_HEAD_DIM
    for part in range(3):
        for blk in range(GDN_WIDTH // QKV_BLOCK):
            base = part * GDN_WIDTH + blk * QKV_BLOCK
            y = _dot(h_perm, wm_ref[:, W_IN_GDN + base:W_IN_GDN + base + QKV_BLOCK])
            yield
            for hb in range(heads_per_block):
                hh = blk * heads_per_block + hb
                col = part * GDN_WIDTH + hh * GDN_HEAD_DIM
                cur = y[:, hb * GDN_HEAD_DIM:(hb + 1) * GDN_HEAD_DIM]
                vreg = lambda c, v: cur[c * GDN_CHUNK + v * SUBLANES:c * GDN_CHUNK + (v + 1) * SUBLANES]
                prev_tile = carry_ref[:, col:col + GDN_HEAD_DIM]
                carry_ref[:, col:col + GDN_HEAD_DIM] = cur[tm - halo * SUBLANES:, :]
                wrapped = []
                for c in range(chunks):
                    row = []
                    for i in range(halo):
                        before = (prev_tile[i * SUBLANES:(i + 1) * SUBLANES] if c == 0
                                  else vreg(c - 1, vregs - halo + i))
                        own = vreg(c, vregs - halo + i)
                        row.append(pltpu.roll(jnp.where(last_sublane, before, own), 1, 0))
                    wrapped.append(row)
                acc = cur * convw_ref[halo:halo + 1, col:col + GDN_HEAD_DIM]
                for shift in range(1, GDN_CONV):
                    pieces = []
                    for c in range(chunks):
                        pieces += wrapped[c][halo - shift:]
                        pieces.append(cur[c * GDN_CHUNK:(c + 1) * GDN_CHUNK - shift * SUBLANES])
                    shifted = jnp.concatenate(pieces, axis=0)
                    acc = acc + shifted * convw_ref[halo - shift:halo - shift + 1, col:col + GDN_HEAD_DIM]
                act = acc * _sigmoid(acc)
                if part < 2:
                    act = act * lax.rsqrt(jnp.sum(act * act, axis=-1, keepdims=True) + EPS)
                if part == 0:
                    act = act * (GDN_HEAD_DIM ** -0.5)
                gdn_refs[part][:, hh * GDN_HEAD_DIM:(hh + 1) * GDN_HEAD_DIM] = act
                yield


DELTANET_QKV_STAGES = 1 + 3 * (GDN_WIDTH // QKV_BLOCK) + 3 * GDN_HEADS


def _mixer_in_kernel(x_ref, g_ref, wm_ref, wbd_ref, convw_ref, alog_ref, dtb_ref, cos_ref, sin_ref,
                     a0_ref, a1_ref, a2_ref, qb_ref, kb_ref, vb_ref, bd_ref, carry_ref, perm_ref, stage_ref, *, tm):
    @pl.when(pl.program_id(1) == 0)
    def _():
        carry_ref[...] = jnp.zeros(carry_ref.shape, F32)

    hf = _rmsnorm(x_ref[...], g_ref[...])
    h = hf.astype(BF16)
    vregs = GDN_CHUNK // SUBLANES
    for cb in range(D_MODEL // LANES):
        perm_ref[cb] = hf[:, cb * LANES:(cb + 1) * LANES]
    h_perm = jnp.concatenate(
        [jnp.concatenate([perm_ref[cb, pl.ds(c0 + v, SUBLANES, stride=vregs), :]
                          for c0 in range(0, tm, GDN_CHUNK) for v in range(vregs)], axis=0)
         for cb in range(D_MODEL // LANES)], axis=1).astype(BF16)
    gdn = _deltanet_qkv_stages(h_perm, wm_ref, wbd_ref, convw_ref, alog_ref, dtb_ref, (qb_ref, kb_ref, vb_ref), bd_ref,
                               carry_ref, tm=tm)
    att = _attention_qkv_stages(h, wm_ref, cos_ref, sin_ref, (a0_ref, a1_ref, a2_ref), stage_ref, tm=tm)
    _interleave(gdn, DELTANET_QKV_STAGES, att, ATT_QKV_STAGES)


def _mixer_in(x1, norm_g, w_in, conv_w, a_log, dt_bias, cos_t, sin_t, *, tm, seq):
    n = x1.shape[0]
    tiles_per_seq = seq // tm
    tile = lambda rows, w: pl.BlockSpec((rows, w), lambda bi, i: (bi * tiles_per_seq + i, 0))
    table = pl.BlockSpec((tm, LANES), lambda bi, i: (i, 0))
    wq = 3 * ATT_GROUP_WIDTH
    att_specs = [tile(tm // dil, dil * wq) for _, dil in ATT_GROUPS]
    att_shapes = [jax.ShapeDtypeStruct((n // dil, dil * wq), BF16) for _, dil in ATT_GROUPS]
    gdn = jax.ShapeDtypeStruct((n, GDN_WIDTH), F32)
    return pl.pallas_call(
        functools.partial(_mixer_in_kernel, tm=tm),
        grid=(n // seq, tiles_per_seq),
        in_specs=[tile(tm, D_MODEL), _resident((1, D_MODEL)),
                  pl.BlockSpec((D_MODEL, W_IN_BD), lambda bi, i: (0, 0), pipeline_mode=pl.Buffered(1)),
                  pl.BlockSpec((D_MODEL, LANES), lambda bi, i: (0, W_IN_BD // LANES), pipeline_mode=pl.Buffered(1)),
                  _resident((GDN_CONV, 3 * GDN_WIDTH)), _resident((1, LANES)), _resident((1, LANES)), table, table],
        out_specs=att_specs + [tile(tm, GDN_WIDTH)] * 3 + [tile(tm, LANES)],
        out_shape=att_shapes + [gdn] * 3 + [jax.ShapeDtypeStruct((n, LANES), F32)],
        scratch_shapes=[pltpu.VMEM(((GDN_CONV - 1) * SUBLANES, 3 * GDN_WIDTH), F32),
                        pltpu.VMEM((D_MODEL // LANES, tm, LANES), F32), pltpu.VMEM((MIXER_IN_STAGED, tm, LANES), F32)],
        compiler_params=pltpu.CompilerParams(dimension_semantics=("arbitrary", "arbitrary"),
                                             vmem_limit_bytes=VMEM_LIMIT_BYTES),
        name="mixer_in",
    )(x1, norm_g, w_in, w_in, conv_w, a_log, dt_bias, cos_t, sin_t)


ATT_BATCH = 3


def _attention_blocks(items):
    lane = lax.broadcasted_iota(jnp.int32, (1, ATT_GROUP_WIDTH), 1)
    heads = range(ATT_HEADS_PER_GROUP)
    in_head = [(lane // ATT_HEAD_DIM) == hh for hh in heads]
    keep = [jnp.where(in_head[hh], 1.0, 0.0).astype(BF16) for hh in heads]
    nq = ATT_BLOCK
    s_all = [_dot_nt(jnp.concatenate([q * keep[hh] for hh in heads], axis=0), k) for q, k, _, _ in items]
    stats, p_all = [], []
    for (_, _, _, valid), sa in zip(items, s_all):
        s = [jnp.where(valid, sa[hh * nq:(hh + 1) * nq], NEG_BIG) for hh in heads]
        m = [jnp.max(s[hh], axis=-1, keepdims=True) for hh in heads]
        p = [jnp.exp(s[hh] - m[hh]) for hh in heads]
        l = [jnp.sum(p[hh], axis=-1, keepdims=True) for hh in heads]
        stats.append((m, l))
        p_all.append(jnp.concatenate([p[hh].astype(BF16) for hh in heads], axis=0))
    pv_all = [_dot(ps, v) for ps, (_, _, v, _) in zip(p_all, items)]
    outs = []
    for pv, (m, l) in zip(pv_all, stats):
        o = jnp.zeros((ATT_BLOCK, ATT_GROUP_WIDTH), F32)
        lse = jnp.zeros((ATT_BLOCK, ATT_GROUP_WIDTH), F32)
        for hh in heads:
            o = jnp.where(in_head[hh], pv[hh * nq:(hh + 1) * nq] * (1.0 / l[hh]), o)
            lse = jnp.where(in_head[hh], m[hh] + jnp.log(l[hh]), lse)
        outs.append((o, lse))
    return outs


def _attention_kernel(a0_ref, a1_ref, a2_ref, ya_ref, o0, l0, o1, l1, o2, l2, *, seq):
    in_refs = (a0_ref, a1_ref, a2_ref)
    o_refs = (o0, o1, o2)
    l_refs = (l0, l1, l2)
    qi = lax.broadcasted_iota(jnp.int32, (ATT_BLOCK, ATT_BLOCK), 0)
    kj = lax.broadcasted_iota(jnp.int32, (ATT_BLOCK, ATT_BLOCK), 1)
    causal = kj <= qi
    qi2 = lax.broadcasted_iota(jnp.int32, (ATT_BLOCK, 2 * ATT_BLOCK), 0)
    kj2 = lax.broadcasted_iota(jnp.int32, (ATT_BLOCK, 2 * ATT_BLOCK), 1)
    band = (kj2 >= qi2) & (kj2 - ATT_BLOCK <= qi2)
    wq = 3 * ATT_GROUP_WIDTH

    def load(gi, r, n):
        src, base = in_refs[gi], r * wq
        if isinstance(n, int) and n == 0:
            qrows = krows = slice(0, ATT_BLOCK)
            valid = causal
        else:
            start = lambda x: x if isinstance(x, int) else pl.multiple_of(x, ATT_BLOCK)
            qrows = pl.ds(start(n * ATT_BLOCK), ATT_BLOCK)
            krows = pl.ds(start((n - 1) * ATT_BLOCK), 2 * ATT_BLOCK)
            valid = band
        return (src[0, qrows, base:base + ATT_GROUP_WIDTH],
                src[0, krows, base + ATT_GROUP_WIDTH:base + 2 * ATT_GROUP_WIDTH],
                src[0, krows, base + 2 * ATT_GROUP_WIDTH:base + 3 * ATT_GROUP_WIDTH], valid)

    def store(gi, r, n, o, lse):
        dil = ATT_GROUPS[gi][1]
        if dil == 1:
            first = n * ATT_BLOCK
            rows = pl.ds(first if isinstance(first, int) else pl.multiple_of(first, ATT_BLOCK), ATT_BLOCK)
        else:
            rows = pl.ds(n * ATT_BLOCK * dil + r, ATT_BLOCK, stride=dil)
        for half in range(ATT_GROUP_WIDTH // LANES):
            o_refs[gi][half, rows, :] = o[:, half * LANES:(half + 1) * LANES]
            l_refs[gi][half, rows, :] = lse[:, half * LANES:(half + 1) * LANES]

    def run(blocks):
        for (gi, r, n), (o, lse) in zip(blocks, _attention_blocks([load(*blk) for blk in blocks])):
            store(gi, r, n, o, lse)

    static_blocks = []
    looped = None
    for gi, (window, dil) in enumerate(ATT_GROUPS):
        assert window // dil == ATT_BLOCK
        nblk = seq // dil // ATT_BLOCK
        if dil == 1 and (nblk - 1) % ATT_BATCH == 0:
            static_blocks.append((gi, 0, 0))
            looped = (gi, nblk)
        else:
            static_blocks += [(gi, r, n) for r in range(dil) for n in range(nblk)]
    for i in range(0, len(static_blocks), ATT_BATCH + 1):
        run(static_blocks[i:i + ATT_BATCH + 1])
    if looped is not None:
        gi, nblk = looped

        def body(i, carry):
            run([(gi, 0, 1 + i * ATT_BATCH + j) for j in range(ATT_BATCH)])
            return carry
        lax.fori_loop(0, (nblk - 1) // ATT_BATCH, body, 0)

    rows_per_step = 256

    def merge(i, carry):
        rows = pl.ds(pl.multiple_of(i * rows_per_step, rows_per_step), rows_per_step)
        for half in range(ATT_GROUP_WIDTH // LANES):
            la, lb, lc = l0[half, rows, :], l1[half, rows, :], l2[half, rows, :]
            m = jnp.maximum(jnp.maximum(la, lb), lc)
            ea, eb, ec = jnp.exp(la - m), jnp.exp(lb - m), jnp.exp(lc - m)
            num = ea * o0[half, rows, :] + eb * o1[half, rows, :] + ec * o2[half, rows, :]
            ya_ref[0, rows, half * LANES:(half + 1) * LANES] = num / (ea + eb + ec)
        return carry
    lax.fori_loop(0, seq // rows_per_step, merge, 0)


def _attention(a0, a1, a2, *, batch):
    views = tuple(a.reshape(batch, a.shape[0] // batch, a.shape[1]) for a in (a0, a1, a2))
    b, s, _ = views[0].shape
    specs = [pl.BlockSpec((1,) + arr.shape[1:], lambda bi: (bi, 0, 0)) for arr in views]
    scratch = [pltpu.VMEM((ATT_GROUP_WIDTH // LANES, s, LANES), F32) for _ in range(6)]
    return pl.pallas_call(
        functools.partial(_attention_kernel, seq=s),
        grid=(b,),
        in_specs=specs,
        out_specs=pl.BlockSpec((1, s, ATT_GROUP_WIDTH), lambda bi: (bi, 0, 0)),
        out_shape=jax.ShapeDtypeStruct((b, s, ATT_GROUP_WIDTH), F32),
        scratch_shapes=scratch,
        compiler_params=pltpu.CompilerParams(dimension_semantics=("arbitrary",),
                                             vmem_limit_bytes=VMEM_LIMIT_BYTES),
        name="dilated_attention",
    )(*views)


def _deltanet_stages(q_ref, k_ref, v_ref, bd_ref, gnorm, state_ref, ob_ref, slot, *, tile):
    c = GDN_CHUNK
    d = GDN_HEAD_DIM
    heads = range(GDN_HEADS)
    pairs = [(2 * pp, 2 * pp + 1) for pp in range(GDN_HEADS // 2)]
    cols = [slice(hh * d, (hh + 1) * d) for hh in heads]
    glane = [GDN_HEADS + hh for hh in heads]
    ii = _chunk_time(lax.broadcasted_iota(jnp.int32, (c, 2 * c), 0))
    ll = lax.broadcasted_iota(jnp.int32, (c, 2 * c), 1)
    jj = _chunk_time(ll % c)
    lower = ii >= jj
    strict = ii > jj
    left = ll < c
    left_row = lax.broadcasted_iota(jnp.int32, (1, 2 * c), 1) < c
    keep_left = jnp.where(left, 1.0, 0.0).astype(BF16)
    keep_right = jnp.where(left, 0.0, 1.0).astype(BF16)
    ti = _chunk_time(lax.broadcasted_iota(jnp.int32, (c, c), 0))
    tj = _chunk_time(lax.broadcasted_iota(jnp.int32, (c, c), 1))
    tri_ones = jnp.where(ti >= tj, 1.0, 0.0).astype(BF16)

    def blockdiag(x):
        return jnp.concatenate([x * keep_left, x * keep_right], axis=0)

    def stack_diag(xa, xb):
        zero = jnp.zeros_like(xa)
        return jnp.concatenate([jnp.concatenate([xa, zero], axis=1), jnp.concatenate([zero, xb], axis=1)], axis=0)

    def prepare(ci, out):
        rows = slice(ci * c, (ci + 1) * c)
        bd = bd_ref[rows, :]
        bd_hi = bd.astype(BF16)
        bd_rest = bd - bd_hi.astype(F32)
        bd_mid = bd_rest.astype(BF16)
        bd_lo = (bd_rest - bd_mid.astype(F32)).astype(BF16)
        gcum = _dot(tri_ones, bd_hi) + _dot(tri_ones, bd_mid) + _dot(tri_ones, bd_lo)
        yield
        gtot = jnp.broadcast_to(gcum[c - 1:c, :], (c, LANES))
        gcum_t = jnp.concatenate([gcum, gcum], axis=0).T
        e_cum_all = jnp.exp(gcum)
        e_rest_all = jnp.exp(gtot - gcum)
        e_tot_all = jnp.exp(gtot)
        q = [q_ref[rows, cols[hh]] for hh in heads]
        k = [k_ref[rows, cols[hh]] for hh in heads]
        v = [v_ref[rows, cols[hh]] for hh in heads]
        beta = [bd[:, hh:hh + 1] for hh in heads]
        e_cum = [e_cum_all[:, gl:gl + 1] for gl in glane]
        kbeta = [k[hh] * beta[hh] for hh in heads]
        kq = [_dot_nt(jnp.concatenate([jnp.concatenate([kbeta[a], kbeta[b]], axis=1),
                                       jnp.concatenate([q[a], q[b]], axis=1)], axis=0).astype(BF16),
                      stack_diag(k[a].astype(BF16), k[b].astype(BF16)))
              for a, b in pairs]
        yield
        decay = [jnp.exp(jnp.where(lower,
                                   jnp.where(left, gcum[:, glane[a]:glane[a] + 1], gcum[:, glane[b]:glane[b] + 1])
                                   - jnp.where(left_row, gcum_t[glane[a]:glane[a] + 1, :], gcum_t[glane[b]:glane[b] + 1, :]),
                                   NEG_BIG)) for a, b in pairs]
        m = [jnp.where(strict, kq[pp][0:c] * decay[pp], 0.0) for pp in range(len(pairs))]
        n = [-mm for mm in m]
        pb = [mm.astype(BF16) for mm in m]
        p = [_dot(x, blockdiag(x)) for x in pb]
        yield
        rounds = 5
        for r in range(rounds):
            pb = [x.astype(BF16) for x in p]
            upd = [_dot(x, blockdiag(y.astype(BF16))) for x, y in zip(pb, n)]
            p_next = [_dot(x, blockdiag(x)) for x in pb] if r + 1 < rounds else None
            yield
            n = [y + x + u for y, x, u in zip(n, p, upd)]
            p = p_next
        rhs = [jnp.concatenate([v[hh] * beta[hh], kbeta[hh] * e_cum[hh]], axis=1) for hh in heads]
        nr = [_dot(n[pp].astype(BF16), stack_diag(rhs[a].astype(BF16), rhs[b].astype(BF16)))
              for pp, (a, b) in enumerate(pairs)]
        yield
        sol = [rhs[hh] + nr[hh // 2][:, (hh % 2) * 2 * d:(hh % 2 + 1) * 2 * d] for hh in heads]
        out.update(
            first=ci * c,
            u=[sol[hh][:, 0:d] for hh in heads],
            wq=[jnp.concatenate([sol[hh][:, d:2 * d], q[hh] * e_cum[hh]], axis=0).astype(BF16) for hh in heads],
            a_qk=[(kq[pp][c:2 * c] * decay[pp]).astype(BF16) for pp in range(len(pairs))],
            k_dec=[(k[hh] * e_rest_all[:, gl:gl + 1]).astype(BF16) for hh, gl in zip(heads, glane)],
            e_tot=[e_tot_all[0:1, gl:gl + 1] for gl in glane])

    for first in range(0, tile // c, GDN_GROUP):
        group = [dict() for _ in range(GDN_GROUP)]
        gens = [prepare(first + gi, group[gi]) for gi in range(GDN_GROUP)]
        for _ in range(GDN_PREP_LAYERS):
            for gen in gens:
                next(gen)
            yield
        for gen in gens:
            for _ in gen:
                pass
        for pre in group:
            state = [state_ref[hh] for hh in heads]
            ws = [_dot(pre["wq"][hh], state[hh].astype(BF16)) for hh in heads]
            yield
            v_new = [(pre["u"][hh] - ws[hh][0:c]).astype(BF16) for hh in heads]
            kv = [_dot_tn(pre["k_dec"][hh], v_new[hh]) for hh in heads]
            av = [_dot(pre["a_qk"][pp], stack_diag(v_new[a], v_new[b])) for pp, (a, b) in enumerate(pairs)]
            yield
            for hh in heads:
                state_ref[hh] = state[hh] * pre["e_tot"][hh] + kv[hh]
            for hh in heads:
                o = _rmsnorm(ws[hh][c:2 * c] + av[hh // 2][:, (hh % 2) * d:(hh % 2 + 1) * d], gnorm)
                for vv in range(c // SUBLANES):
                    ob_ref[slot, hh, pl.ds(pre["first"] + vv, SUBLANES, stride=c // SUBLANES), :] = (
                        o[vv * SUBLANES:(vv + 1) * SUBLANES])


GDN_GROUP = 4
GDN_PREP_LAYERS = 9
GDN_LAYERS_PER_GROUP = GDN_PREP_LAYERS + 2 * GDN_GROUP


def _mixer_out_stages(x_ref, ya_ref, ob_ref, slot, g_ref, wgt_ref, wa_ref, wb_ref, wo_ref, o_ref):
    x = x_ref[...]
    h = _rmsnorm(x, g_ref[...]).astype(BF16)
    ya = ya_ref[...].astype(BF16)
    blocks = [slice(j * MIX_BLOCK, (j + 1) * MIX_BLOCK) for j in range(D_MODEL // MIX_BLOCK)]
    gate_cols = lambda which, blk: slice(which * D_MODEL + blk.start, which * D_MODEL + blk.stop)
    yb = []
    for blk in blocks:
        gdn_gate = _dot(h, wgt_ref[:, gate_cols(0, blk)])
        yield
        ob = jnp.concatenate([ob_ref[slot, hh] for hh in range(blk.start // GDN_HEAD_DIM, blk.stop // GDN_HEAD_DIM)],
                             axis=1)
        yb.append((ob * (gdn_gate * _sigmoid(gdn_gate))).astype(BF16))
    yb = jnp.concatenate(yb, axis=1)
    merged = []
    for blk in blocks:
        gate_a = _dot(h, wgt_ref[:, gate_cols(1, blk)])
        branch_a = _dot(ya, wa_ref[:, blk])
        yield
        gate_b = _dot(h, wgt_ref[:, gate_cols(2, blk)])
        yield
        branch_b = _dot(yb, wb_ref[:, blk])
        yield
        merged.append((_sigmoid(gate_a) * branch_a + _sigmoid(gate_b) * branch_b).astype(BF16))
    merged = jnp.concatenate(merged, axis=1)
    for blk in blocks:
        o_ref[:, blk] = x[:, blk] + _dot(merged, wo_ref[:, blk])
        yield


MIX_GRANULES = 5 * (D_MODEL // MIX_BLOCK)


def _mixer_tail_kernel(q_ref, k_ref, v_ref, bd_ref, gn_ref, x_ref, ya_ref, g_ref, wgt_ref, wa_ref, wb_ref, wo_ref,
                       *refs, tile, tiles_per_seq, n_tiles):
    n_cast = (len(refs) - 3) // 2
    o_ref, state_ref, ob_ref = refs[n_cast], refs[2 * n_cast + 1], refs[2 * n_cast + 2]
    step = pl.program_id(0)

    @pl.when(step == 0)
    def _():
        ob_ref[...] = jnp.zeros(ob_ref.shape, F32)

    @pl.when(jnp.minimum(step, n_tiles - 1) % tiles_per_seq == 0)
    def _():
        state_ref[...] = jnp.zeros(state_ref.shape, F32)

    slot = step % 2
    gdn = _deltanet_stages(q_ref, k_ref, v_ref, bd_ref, gn_ref[...], state_ref, ob_ref, slot, tile=tile)
    mix = _mixer_out_stages(x_ref, ya_ref, ob_ref, 1 - slot, g_ref, wgt_ref, wa_ref, wb_ref, wo_ref, o_ref)
    _interleave(gdn, GDN_LAYERS_PER_GROUP * (tile // (GDN_CHUNK * GDN_GROUP)), mix, MIX_GRANULES)
    _cast_rows(refs[:n_cast], refs[n_cast + 1:2 * n_cast + 1])


def _mixer_tail(qb, kb, vb, bd, out_norm, x1, ya, norm_g, w_gates, w_a, w_b, w_o, *, tile, seq, cast=(), layer=0):
    n = x1.shape[0]
    n_tiles = n // tile
    cast_in, cast_out, cast_shapes = _cast_specs(cast, layer, n_tiles)
    cur = lambda w: pl.BlockSpec((tile, w), lambda s: (jnp.minimum(s, n_tiles - 1), 0))
    prev = lambda w: pl.BlockSpec((tile, w), lambda s: (jnp.maximum(s - 1, 0), 0))
    out = pl.pallas_call(
        functools.partial(_mixer_tail_kernel, tile=tile, tiles_per_seq=seq // tile, n_tiles=n_tiles),
        grid=(n_tiles + 1,),
        in_specs=[cur(GDN_WIDTH), cur(GDN_WIDTH), cur(GDN_WIDTH), cur(LANES), _resident((1, GDN_HEAD_DIM)),
                  prev(D_MODEL), prev(ATT_GROUP_WIDTH), _resident((1, D_MODEL)),
                  _resident(w_gates.shape), _resident(w_a.shape), _resident(w_b.shape), _resident(w_o.shape)]
                 + cast_in,
        out_specs=[prev(D_MODEL)] + cast_out,
        out_shape=[jax.ShapeDtypeStruct((n, D_MODEL), F32)] + cast_shapes,
        scratch_shapes=[pltpu.VMEM((GDN_HEADS, GDN_HEAD_DIM, GDN_HEAD_DIM), F32),
                        pltpu.VMEM((2, GDN_HEADS, tile, GDN_HEAD_DIM), F32)],
        compiler_params=pltpu.CompilerParams(dimension_semantics=("arbitrary",),
                                             vmem_limit_bytes=VMEM_LIMIT_BYTES),
        name="deltanet_mixer_out",
    )(qb, kb, vb, bd, out_norm, x1, ya, norm_g, w_gates, w_a, w_b, w_o, *cast)
    return out[0], out[1:]


def _rope_tables(seq):
    half = ATT_HEAD_DIM // 2
    inv_freq = ROPE_THETA ** (-jnp.arange(half, dtype=F32) / half)
    ang = jnp.arange(seq, dtype=F32)[:, None] * inv_freq[None, :]
    cos, sin = jnp.cos(ang), jnp.sin(ang)
    reps = LANES // ATT_HEAD_DIM
    return jnp.tile(jnp.concatenate([cos, cos], axis=-1), (1, reps)), jnp.tile(jnp.concatenate([-sin, sin], axis=-1), (1, reps))


def _pad_lanes(row, offset):
    return jnp.zeros((1, LANES), F32).at[0, offset:offset + row.shape[0]].set(row.astype(F32))


def _layer(x, ffn1_norm, ffn1_w_gate, ffn1_w_up, ffn1_w_down, mix_norm, w_in_all, gdn_conv_w, gdn_a_log, gdn_dt_bias,
           gdn_out_norm, w_branch_a, w_branch_b, w_out, ffn2_norm, ffn2_all, fin_g,
           *, layer, final_norm, tm_ffn, tm_mix, gdn_tile):
    b, s, _ = x.shape
    n = b * s
    row = lambda v: v.reshape(1, -1).astype(F32)
    x1 = _ffn(x.reshape(n, D_MODEL), row(ffn1_norm), ffn1_w_gate.astype(BF16), ffn1_w_up.astype(BF16),
              ffn1_w_down.astype(BF16), fin_g, final_norm=False, tm=tm_ffn)
    w_in = w_in_all[layer].astype(BF16)
    w_gates = w_in[:, W_IN_GATES:]
    cos_t, sin_t = _rope_tables(s)
    a0, a1, a2, qb, kb, vb, bd = _mixer_in(
        x1, row(mix_norm), w_in, gdn_conv_w.astype(F32), _pad_lanes(gdn_a_log, GDN_HEADS),
        _pad_lanes(gdn_dt_bias, GDN_HEADS), cos_t, sin_t, tm=tm_mix, seq=s)

    ya = _attention(a0, a1, a2, batch=b)
    x2, ffn2_w = _mixer_tail(qb, kb, vb, bd, row(gdn_out_norm), x1, ya.reshape(n, ATT_GROUP_WIDTH), row(mix_norm),
                             w_gates, w_branch_a.astype(BF16), w_branch_b.astype(BF16), w_out.astype(BF16),
                             tile=gdn_tile, seq=s, cast=ffn2_all, layer=layer)
    x3 = _ffn(x2, row(ffn2_norm), *ffn2_w, fin_g, final_norm=final_norm, tm=tm_ffn)
    return x3.reshape(b, s, D_MODEL)


def kernel(x, ffn1_norm, ffn1_w_gate, ffn1_w_up, ffn1_w_down, mix_norm, w_in, gdn_conv_w, gdn_a_log, gdn_dt_bias,
           gdn_out_norm, w_branch_a, w_branch_b, w_out, ffn2_norm, ffn2_w_gate, ffn2_w_up, ffn2_w_down, final_norm):
    depth = ffn1_norm.shape[0]
    fin_g = final_norm.reshape(1, -1).astype(F32)
    for layer in range(depth):
        x = _layer(x, ffn1_norm[layer], ffn1_w_gate[layer], ffn1_w_up[layer], ffn1_w_down[layer], mix_norm[layer],
                   w_in, gdn_conv_w[layer], gdn_a_log[layer], gdn_dt_bias[layer], gdn_out_norm[layer],
                   w_branch_a[layer], w_branch_b[layer], w_out[layer], ffn2_norm[layer],
                   (ffn2_w_gate, ffn2_w_up, ffn2_w_down), fin_g, layer=layer, final_norm=(layer == depth - 1),
                   tm_ffn=512, tm_mix=512, gdn_tile=512)
    return x
```

```python
import functools

import jax
import jax.numpy as jnp
from jax import lax
from jax.experimental import pallas as pl
from jax.experimental.pallas import tpu as pltpu

F32 = jnp.float32
BF16 = jnp.bfloat16

D_MODEL = 1024
D_FF = 2816
EPS = 1e-6

ATT_GROUPS = ((128, 1), (512, 4), (2048, 16))
ATT_HEADS_PER_GROUP = 4
ATT_HEAD_DIM = 64
ATT_BLOCK = 128
ATT_GROUP_WIDTH = ATT_HEADS_PER_GROUP * ATT_HEAD_DIM
ATT_QKV_WIDTH = len(ATT_GROUPS) * ATT_GROUP_WIDTH
ROPE_THETA = 10000.0

GDN_HEADS = 8
GDN_HEAD_DIM = 128
GDN_WIDTH = GDN_HEADS * GDN_HEAD_DIM
GDN_CONV = 4
GDN_CHUNK = 64

LANES = 128
SUBLANES = 8
VMEM_LIMIT_BYTES = 56 * 1024 * 1024

W_IN_GDN = 3 * ATT_QKV_WIDTH
W_IN_BD = W_IN_GDN + 3 * GDN_WIDTH
W_IN_GATES = W_IN_BD + 2 * GDN_HEADS
FFN_CHUNKS = ((0, 768), (768, 1536), (1536, 2304), (2304, 2816))
NEG_BIG = -1e30


def _resident(shape):
    nd = len(shape)
    return pl.BlockSpec(shape, lambda *_: (0,) * nd, pipeline_mode=pl.Buffered(1))


def _rmsnorm(x, g):
    return x * lax.rsqrt(jnp.mean(x * x, axis=-1, keepdims=True) + EPS) * g


def _sigmoid(x):
    return 1.0 / (1.0 + jnp.exp(-x))


def _dot(a, b):
    return jnp.dot(a, b, preferred_element_type=F32)


def _dot_nt(a, b):
    return lax.dot_general(a, b, (((1,), (1,)), ((), ())), preferred_element_type=F32)


def _dot_tn(a, b):
    return lax.dot_general(a, b, (((0,), (0,)), ((), ())), preferred_element_type=F32)


def _swiglu_residual(x, g, wg_ref, wu_ref, wd_ref):
    h = _rmsnorm(x, g).astype(BF16)
    acc = x
    for lo, hi in FFN_CHUNKS:
        gate = _dot(h, wg_ref[:, lo:hi])
        up = _dot(h, wu_ref[:, lo:hi])
        act = (0.5 * gate * _sigmoid(gate) * up).astype(BF16)
        acc = acc + _dot(act, wd_ref[lo:hi, :])
    return acc


def _cast_rows(src_refs, dst_refs):
    for src, dst in zip(src_refs, dst_refs):
        dst[...] = src[...].astype(BF16)


def _cast_specs(arrays, layer, steps):
    in_specs, out_specs, shapes = [], [], []
    for arr in arrays:
        _, n_rows, n_cols = arr.shape
        packed_rows = 2 * SUBLANES
        parts = next(p for p in range(steps, 0, -1) if n_rows % p == 0 and (n_rows // p) % packed_rows == 0)
        in_specs.append(pl.BlockSpec((None, n_rows // parts, n_cols),
                                     lambda s, parts=parts: (layer, jnp.minimum(s, parts - 1), 0)))
        out_specs.append(pl.BlockSpec((n_rows // parts, n_cols), lambda s, parts=parts: (jnp.minimum(s, parts - 1), 0)))
        shapes.append(jax.ShapeDtypeStruct((n_rows, n_cols), BF16))
    return in_specs, out_specs, shapes


def _ffn_kernel(x_ref, g_ref, wg_ref, wu_ref, wd_ref, fin_ref, o_ref, *, final_norm):
    y = _swiglu_residual(x_ref[...], g_ref[...], wg_ref, wu_ref, wd_ref)
    if final_norm:
        y = _rmsnorm(y, fin_ref[...])
    o_ref[...] = y


def _ffn(x, norm_g, wg, wu, wd, fin_g, *, final_norm, tm):
    n = x.shape[0]
    row = pl.BlockSpec((tm, D_MODEL), lambda i: (i, 0))
    return pl.pallas_call(
        functools.partial(_ffn_kernel, final_norm=final_norm),
        grid=(n // tm,),
        in_specs=[row, _resident((1, D_MODEL)), _resident((D_MODEL, D_FF)), _resident((D_MODEL, D_FF)),
                  _resident((D_FF, D_MODEL)), _resident((1, D_MODEL))],
        out_specs=row,
        out_shape=jax.ShapeDtypeStruct((n, D_MODEL), F32),
        compiler_params=pltpu.CompilerParams(dimension_semantics=("arbitrary",),
                                             vmem_limit_bytes=VMEM_LIMIT_BYTES),
        name="ffn_final" if final_norm else "ffn",
    )(x, norm_g, wg, wu, wd, fin_g)


MIX_BLOCK = 256
QKV_BLOCK = 512


def _interleave(primary, n_primary, secondary, n_secondary):
    done = 0
    for i in range(n_primary):
        next(primary)
        while done * n_primary < (i + 1) * n_secondary:
            next(secondary)
            done += 1
    for gen in (primary, secondary):
        for _ in gen:
            pass


def _chunk_time(row):
    return SUBLANES * (row % SUBLANES) + row // SUBLANES


def _attention_qkv_stages(h, wm_ref, cos_ref, sin_ref, att_refs, stage_ref, *, tm, parts):
    cos = cos_ref[...]
    sin = sin_ref[...]
    lane = lax.broadcasted_iota(jnp.int32, (1, LANES), 1)
    first_half = (lane % ATT_HEAD_DIM) < (ATT_HEAD_DIM // 2)
    staged_per_part = (len(ATT_GROUPS) - 1) * (ATT_GROUP_WIDTH // LANES)
    for part in parts:
        slot = part * staged_per_part
        for gi, (_, dil) in enumerate(ATT_GROUPS):
            col = part * ATT_QKV_WIDTH + gi * ATT_GROUP_WIDTH
            y = _dot(h, wm_ref[:, col:col + ATT_GROUP_WIDTH])
            yield
            for j in range(ATT_GROUP_WIDTH // LANES):
                blk = y[:, j * LANES:(j + 1) * LANES]
                if part < 2:
                    swapped = jnp.where(first_half, pltpu.roll(blk, LANES - ATT_HEAD_DIM // 2, 1),
                                        pltpu.roll(blk, ATT_HEAD_DIM // 2, 1))
                    blk = blk * cos + swapped * sin
                if part == 0:
                    blk = blk * (ATT_HEAD_DIM ** -0.5)
                dst = part * ATT_GROUP_WIDTH + j * LANES
                if dil == 1:
                    att_refs[gi][:, dst:dst + LANES] = blk.astype(BF16)
                else:
                    stage_ref[slot] = blk
                    for r in range(dil):
                        rows = stage_ref[slot, pl.ds(r, tm // dil, stride=dil), :]
                        lo = r * 3 * ATT_GROUP_WIDTH + dst
                        att_refs[gi][:, lo:lo + LANES] = rows.astype(BF16)
                    slot += 1
                yield


ATT_PART_STAGES = len(ATT_GROUPS) * (1 + ATT_GROUP_WIDTH // LANES)
MIXER_IN_STAGED = 3 * (len(ATT_GROUPS) - 1) * (ATT_GROUP_WIDTH // LANES)


def _deltanet_qkv_stages(h_perm, wm_ref, wbd_ref, convw_ref, alog_ref, dtb_ref, gdn_refs, bd_ref, carry_ref, *, tm):
    lane = lax.broadcasted_iota(jnp.int32, (1, LANES), 1)
    raw = _dot(h_perm, wbd_ref[...])
    yield
    z = raw + dtb_ref[...]
    softplus = jnp.maximum(z, 0.0) + jnp.log1p(jnp.exp(-jnp.abs(z)))
    g = -jnp.exp(alog_ref[...]) * softplus
    bd_ref[...] = jnp.where(lane < GDN_HEADS, _sigmoid(raw), jnp.where(lane < 2 * GDN_HEADS, g, 0.0))

    vregs = GDN_CHUNK // SUBLANES
    halo = GDN_CONV - 1
    chunks = tm // GDN_CHUNK
    last_sublane = lax.broadcasted_iota(jnp.int32, (SUBLANES, GDN_HEAD_DIM), 0) == SUBLANES - 1
    heads_per_block = QKV_BLOCK // GDN_HEAD_DIM
    for part in range(3):
        for blk in range(GDN_WIDTH // QKV_BLOCK):
            base = part * GDN_WIDTH + blk * QKV_BLOCK
            y = _dot(h_perm, wm_ref[:, W_IN_GDN + base:W_IN_GDN + base + QKV_BLOCK])
            yield
            for hb in range(heads_per_block):
                hh = blk * heads_per_block + hb
                col = part * GDN_WIDTH + hh * GDN_HEAD_DIM
                cur = y[:, hb * GDN_HEAD_DIM:(hb + 1) * GDN_HEAD_DIM]
                vreg = lambda c, v: cur[c * GDN_CHUNK + v * SUBLANES:c * GDN_CHUNK + (v + 1) * SUBLANES]
                prev_tile = carry_ref[:, col:col + GDN_HEAD_DIM]
                carry_ref[:, col:col + GDN_HEAD_DIM] = cur[tm - halo * SUBLANES:, :]
                wrapped = []
                for c in range(chunks):
                    row = []
                    for i in range(halo):
                        before = (prev_tile[i * SUBLANES:(i + 1) * SUBLANES] if c == 0
                                  else vreg(c - 1, vregs - halo + i))
                        own = vreg(c, vregs - halo + i)
                        row.append(pltpu.roll(jnp.where(last_sublane, before, own), 1, 0))
                    wrapped.append(row)
                acc = cur * convw_ref[halo:halo + 1, col:col + GDN_HEAD_DIM]
                for shift in range(1, GDN_CONV):
                    pieces = []
                    for c in range(chunks):
                        pieces += wrapped[c][halo - shift:]
                        pieces.append(cur[c * GDN_CHUNK:(c + 1) * GDN_CHUNK - shift * SUBLANES])
                    shifted = jnp.concatenate(pieces, axis=0)
                    acc = acc + shifted * convw_ref[halo - shift:halo - shift + 1, col:col + GDN_HEAD_DIM]
                act = acc * _sigmoid(acc)
                if part < 2:
                    act = act * lax.rsqrt(jnp.sum(act * act, axis=-1, keepdims=True) + EPS)
                if part == 0:
                    act = act * (GDN_HEAD_DIM ** -0.5)
                gdn_refs[part][:, hh * GDN_HEAD_DIM:(hh + 1) * GDN_HEAD_DIM] = act
                yield


DELTANET_QKV_STAGES = 1 + 3 * (GDN_WIDTH // QKV_BLOCK) + 3 * GDN_HEADS


def _mixer_in_kernel(x_ref, g_ref, wm_ref, wbd_ref, convw_ref, alog_ref, dtb_ref, cos_ref, sin_ref,
                     a0_ref, a1_ref, a2_ref, qb_ref, kb_ref, vb_ref, bd_ref, carry_ref, perm_ref, stage_ref, *, tm):
    @pl.when(pl.program_id(1) == 0)
    def _():
        carry_ref[...] = jnp.zeros(carry_ref.shape, F32)

    hf = _rmsnorm(x_ref[...], g_ref[...])
    h = hf.astype(BF16)
    vregs = GDN_CHUNK // SUBLANES
    for cb in range(D_MODEL // LANES):
        perm_ref[cb] = hf[:, cb * LANES:(cb + 1) * LANES]
    h_perm = jnp.concatenate(
        [jnp.concatenate([perm_ref[cb, pl.ds(c0 + v, SUBLANES, stride=vregs), :]
                          for c0 in range(0, tm, GDN_CHUNK) for v in range(vregs)], axis=0)
         for cb in range(D_MODEL // LANES)], axis=1).astype(BF16)
    gdn = _deltanet_qkv_stages(h_perm, wm_ref, wbd_ref, convw_ref, alog_ref, dtb_ref, (qb_ref, kb_ref, vb_ref), bd_ref,
                               carry_ref, tm=tm)
    att = functools.partial(_attention_qkv_stages, h, wm_ref, cos_ref, sin_ref, (a0_ref, a1_ref, a2_ref), stage_ref,
                            tm=tm)
    _interleave(gdn, DELTANET_QKV_STAGES, att(parts=(0, 1)), 2 * ATT_PART_STAGES)
    for _ in att(parts=(2,)):
        pass


def _mixer_in(x1, norm_g, w_in, conv_w, a_log, dt_bias, cos_t, sin_t, *, tm, seq):
    n = x1.shape[0]
    tiles_per_seq = seq // tm
    tile = lambda rows, w: pl.BlockSpec((rows, w), lambda bi, i: (bi * tiles_per_seq + i, 0))
    table = pl.BlockSpec((tm, LANES), lambda bi, i: (i, 0))
    wq = 3 * ATT_GROUP_WIDTH
    att_specs = [tile(tm // dil, dil * wq) for _, dil in ATT_GROUPS]
    att_shapes = [jax.ShapeDtypeStruct((n // dil, dil * wq), BF16) for _, dil in ATT_GROUPS]
    gdn = jax.ShapeDtypeStruct((n, GDN_WIDTH), F32)
    return pl.pallas_call(
        functools.partial(_mixer_in_kernel, tm=tm),
        grid=(n // seq, tiles_per_seq),
        in_specs=[tile(tm, D_MODEL), _resident((1, D_MODEL)),
                  pl.BlockSpec((D_MODEL, W_IN_BD), lambda bi, i: (0, 0), pipeline_mode=pl.Buffered(1)),
                  pl.BlockSpec((D_MODEL, LANES), lambda bi, i: (0, W_IN_BD // LANES), pipeline_mode=pl.Buffered(1)),
                  _resident((GDN_CONV, 3 * GDN_WIDTH)), _resident((1, LANES)), _resident((1, LANES)), table, table],
        out_specs=att_specs + [tile(tm, GDN_WIDTH)] * 3 + [tile(tm, LANES)],
        out_shape=att_shapes + [gdn] * 3 + [jax.ShapeDtypeStruct((n, LANES), F32)],
        scratch_shapes=[pltpu.VMEM(((GDN_CONV - 1) * SUBLANES, 3 * GDN_WIDTH), F32),
                        pltpu.VMEM((D_MODEL // LANES, tm, LANES), F32), pltpu.VMEM((MIXER_IN_STAGED, tm, LANES), F32)],
        compiler_params=pltpu.CompilerParams(dimension_semantics=("arbitrary", "arbitrary"),
                                             vmem_limit_bytes=VMEM_LIMIT_BYTES),
        name="mixer_in",
    )(x1, norm_g, w_in, w_in, conv_w, a_log, dt_bias, cos_t, sin_t)


ATT_BATCH = 3


def _attention_blocks(items):
    lane = lax.broadcasted_iota(jnp.int32, (1, ATT_GROUP_WIDTH), 1)
    heads = range(ATT_HEADS_PER_GROUP)
    in_head = [(lane // ATT_HEAD_DIM) == hh for hh in heads]
    keep = [jnp.where(in_head[hh], 1.0, 0.0).astype(BF16) for hh in heads]
    nq = ATT_BLOCK
    s_all = [_dot_nt(jnp.concatenate([q * keep[hh] for hh in heads], axis=0), k) for q, k, _, _ in items]
    stats, p_all = [], []
    for (_, _, _, valid), sa in zip(items, s_all):
        s = [jnp.where(valid, sa[hh * nq:(hh + 1) * nq], NEG_BIG) for hh in heads]
        m = [jnp.max(s[hh], axis=-1, keepdims=True) for hh in heads]
        p = [jnp.exp(s[hh] - m[hh]) for hh in heads]
        l = [jnp.sum(p[hh], axis=-1, keepdims=True) for hh in heads]
        stats.append((m, l))
        p_all.append(jnp.concatenate([p[hh].astype(BF16) for hh in heads], axis=0))
    pv_all = [_dot(ps, v) for ps, (_, _, v, _) in zip(p_all, items)]
    outs = []
    for pv, (m, l) in zip(pv_all, stats):
        o = jnp.zeros((ATT_BLOCK, ATT_GROUP_WIDTH), F32)
        lse = jnp.zeros((ATT_BLOCK, ATT_GROUP_WIDTH), F32)
        for hh in heads:
            o = jnp.where(in_head[hh], pv[hh * nq:(hh + 1) * nq] * (1.0 / l[hh]), o)
            lse = jnp.where(in_head[hh], m[hh] + jnp.log(l[hh]), lse)
        outs.append((o, lse))
    return outs


def _attention_kernel(a0_ref, a1_ref, a2_ref, ya_ref, o0, l0, o1, l1, o2, l2, *, seq):
    in_refs = (a0_ref, a1_ref, a2_ref)
    o_refs = (o0, o1, o2)
    l_refs = (l0, l1, l2)
    qi = lax.broadcasted_iota(jnp.int32, (ATT_BLOCK, ATT_BLOCK), 0)
    kj = lax.broadcasted_iota(jnp.int32, (ATT_BLOCK, ATT_BLOCK), 1)
    causal = kj <= qi
    qi2 = lax.broadcasted_iota(jnp.int32, (ATT_BLOCK, 2 * ATT_BLOCK), 0)
    kj2 = lax.broadcasted_iota(jnp.int32, (ATT_BLOCK, 2 * ATT_BLOCK), 1)
    band = (kj2 >= qi2) & (kj2 - ATT_BLOCK <= qi2)
    wq = 3 * ATT_GROUP_WIDTH

    def load(gi, r, n):
        src, base = in_refs[gi], r * wq
        if isinstance(n, int) and n == 0:
            qrows = krows = slice(0, ATT_BLOCK)
            valid = causal
        else:
            start = lambda x: x if isinstance(x, int) else pl.multiple_of(x, ATT_BLOCK)
            qrows = pl.ds(start(n * ATT_BLOCK), ATT_BLOCK)
            krows = pl.ds(start((n - 1) * ATT_BLOCK), 2 * ATT_BLOCK)
            valid = band
        return (src[0, qrows, base:base + ATT_GROUP_WIDTH],
                src[0, krows, base + ATT_GROUP_WIDTH:base + 2 * ATT_GROUP_WIDTH],
                src[0, krows, base + 2 * ATT_GROUP_WIDTH:base + 3 * ATT_GROUP_WIDTH], valid)

    def store(gi, r, n, o, lse):
        dil = ATT_GROUPS[gi][1]
        if dil == 1:
            first = n * ATT_BLOCK
            rows = pl.ds(first if isinstance(first, int) else pl.multiple_of(first, ATT_BLOCK), ATT_BLOCK)
        else:
            rows = pl.ds(n * ATT_BLOCK * dil + r, ATT_BLOCK, stride=dil)
        for half in range(ATT_GROUP_WIDTH // LANES):
            o_refs[gi][half, rows, :] = o[:, half * LANES:(half + 1) * LANES]
            l_refs[gi][half, rows, :] = lse[:, half * LANES:(half + 1) * LANES]

    def run(blocks):
        for (gi, r, n), (o, lse) in zip(blocks, _attention_blocks([load(*blk) for blk in blocks])):
            store(gi, r, n, o, lse)

    static_blocks = []
    looped = None
    for gi, (window, dil) in enumerate(ATT_GROUPS):
        assert window // dil == ATT_BLOCK
        nblk = seq // dil // ATT_BLOCK
        if dil == 1 and (nblk - 1) % ATT_BATCH == 0:
            static_blocks.append((gi, 0, 0))
            looped = (gi, nblk)
        else:
            static_blocks += [(gi, r, n) for r in range(dil) for n in range(nblk)]
    for i in range(0, len(static_blocks), ATT_BATCH + 1):
        run(static_blocks[i:i + ATT_BATCH + 1])
    if looped is not None:
        gi, nblk = looped

        def body(i, carry):
            run([(gi, 0, 1 + i * ATT_BATCH + j) for j in range(ATT_BATCH)])
            return carry
        lax.fori_loop(0, (nblk - 1) // ATT_BATCH, body, 0)

    rows_per_step = 256

    def merge(i, carry):
        rows = pl.ds(pl.multiple_of(i * rows_per_step, rows_per_step), rows_per_step)
        for half in range(ATT_GROUP_WIDTH // LANES):
            la, lb, lc = l0[half, rows, :], l1[half, rows, :], l2[half, rows, :]
            m = jnp.maximum(jnp.maximum(la, lb), lc)
            ea, eb, ec = jnp.exp(la - m), jnp.exp(lb - m), jnp.exp(lc - m)
            num = ea * o0[half, rows, :] + eb * o1[half, rows, :] + ec * o2[half, rows, :]
            ya_ref[0, rows, half * LANES:(half + 1) * LANES] = num / (ea + eb + ec)
        return carry
    lax.fori_loop(0, seq // rows_per_step, merge, 0)


def _attention(a0, a1, a2, *, batch):
    views = tuple(a.reshape(batch, a.shape[0] // batch, a.shape[1]) for a in (a0, a1, a2))
    b, s, _ = views[0].shape
    specs = [pl.BlockSpec((1,) + arr.shape[1:], lambda bi: (bi, 0, 0)) for arr in views]
    scratch = [pltpu.VMEM((ATT_GROUP_WIDTH // LANES, s, LANES), F32) for _ in range(6)]
    return pl.pallas_call(
        functools.partial(_attention_kernel, seq=s),
        grid=(b,),
        in_specs=specs,
        out_specs=pl.BlockSpec((1, s, ATT_GROUP_WIDTH), lambda bi: (bi, 0, 0)),
        out_shape=jax.ShapeDtypeStruct((b, s, ATT_GROUP_WIDTH), F32),
        scratch_shapes=scratch,
        compiler_params=pltpu.CompilerParams(dimension_semantics=("arbitrary",),
                                             vmem_limit_bytes=VMEM_LIMIT_BYTES),
        name="dilated_attention",
    )(*views)


def _deltanet_stages(q_ref, k_ref, v_ref, bd_ref, gnorm, state_ref, ob_ref, slot, *, tile):
    c = GDN_CHUNK
    d = GDN_HEAD_DIM
    heads = range(GDN_HEADS)
    pairs = [(2 * pp, 2 * pp + 1) for pp in range(GDN_HEADS // 2)]
    cols = [slice(hh * d, (hh + 1) * d) for hh in heads]
    glane = [GDN_HEADS + hh for hh in heads]
    ii = _chunk_time(lax.broadcasted_iota(jnp.int32, (c, 2 * c), 0))
    ll = lax.broadcasted_iota(jnp.int32, (c, 2 * c), 1)
    jj = _chunk_time(ll % c)
    lower = ii >= jj
    strict = ii > jj
    left = ll < c
    left_row = lax.broadcasted_iota(jnp.int32, (1, 2 * c), 1) < c
    keep_left = jnp.where(left, 1.0, 0.0).astype(BF16)
    keep_right = jnp.where(left, 0.0, 1.0).astype(BF16)
    ti = _chunk_time(lax.broadcasted_iota(jnp.int32, (c, c), 0))
    tj = _chunk_time(lax.broadcasted_iota(jnp.int32, (c, c), 1))
    tri_ones = jnp.where(ti >= tj, 1.0, 0.0).astype(BF16)

    def blockdiag(x):
        return jnp.concatenate([x * keep_left, x * keep_right], axis=0)

    def stack_diag(xa, xb):
        zero = jnp.zeros_like(xa)
        return jnp.concatenate([jnp.concatenate([xa, zero], axis=1), jnp.concatenate([zero, xb], axis=1)], axis=0)

    def prepare(ci, out):
        rows = slice(ci * c, (ci + 1) * c)
        bd = bd_ref[rows, :]
        bd_hi = bd.astype(BF16)
        bd_rest = bd - bd_hi.astype(F32)
        bd_mid = bd_rest.astype(BF16)
        bd_lo = (bd_rest - bd_mid.astype(F32)).astype(BF16)
        gcum = _dot(tri_ones, bd_hi) + _dot(tri_ones, bd_mid) + _dot(tri_ones, bd_lo)
        yield
        gtot = jnp.broadcast_to(gcum[c - 1:c, :], (c, LANES))
        gcum_t = jnp.concatenate([gcum, gcum], axis=0).T
        e_cum_all = jnp.exp(gcum)
        e_rest_all = jnp.exp(gtot - gcum)
        e_tot_all = jnp.exp(gtot)
        q = [q_ref[rows, cols[hh]] for hh in heads]
        k = [k_ref[rows, cols[hh]] for hh in heads]
        v = [v_ref[rows, cols[hh]] for hh in heads]
        beta = [bd[:, hh:hh + 1] for hh in heads]
        e_cum = [e_cum_all[:, gl:gl + 1] for gl in glane]
        kbeta = [k[hh] * beta[hh] for hh in heads]
        kq = [_dot_nt(jnp.concatenate([jnp.concatenate([kbeta[a], kbeta[b]], axis=1),
                                       jnp.concatenate([q[a], q[b]], axis=1)], axis=0).astype(BF16),
                      stack_diag(k[a].astype(BF16), k[b].astype(BF16)))
              for a, b in pairs]
        yield
        decay = [jnp.exp(jnp.where(lower,
                                   jnp.where(left, gcum[:, glane[a]:glane[a] + 1], gcum[:, glane[b]:glane[b] + 1])
                                   - jnp.where(left_row, gcum_t[glane[a]:glane[a] + 1, :], gcum_t[glane[b]:glane[b] + 1, :]),
                                   NEG_BIG)) for a, b in pairs]
        m = [jnp.where(strict, kq[pp][0:c] * decay[pp], 0.0) for pp in range(len(pairs))]
        n = [-mm for mm in m]
        pb = [mm.astype(BF16) for mm in m]
        p = [_dot(x, blockdiag(x)) for x in pb]
        yield
        rounds = 5
        for r in range(rounds):
            pb = [x.astype(BF16) for x in p]
            upd = [_dot(x, blockdiag(y.astype(BF16))) for x, y in zip(pb, n)]
            p_next = [_dot(x, blockdiag(x)) for x in pb] if r + 1 < rounds else None
            yield
            n = [y + x + u for y, x, u in zip(n, p, upd)]
            p = p_next
        rhs = [jnp.concatenate([v[hh] * beta[hh], kbeta[hh] * e_cum[hh]], axis=1) for hh in heads]
        nr = [_dot(n[pp].astype(BF16), stack_diag(rhs[a].astype(BF16), rhs[b].astype(BF16)))
              for pp, (a, b) in enumerate(pairs)]
        yield
        sol = [rhs[hh] + nr[hh // 2][:, (hh % 2) * 2 * d:(hh % 2 + 1) * 2 * d] for hh in heads]
        out.update(
            first=ci * c,
            u=[sol[hh][:, 0:d] for hh in heads],
            wq=[jnp.concatenate([sol[hh][:, d:2 * d], q[hh] * e_cum[hh]], axis=0).astype(BF16) for hh in heads],
            a_qk=[(kq[pp][c:2 * c] * decay[pp]).astype(BF16) for pp in range(len(pairs))],
            k_dec=[(k[hh] * e_rest_all[:, gl:gl + 1]).astype(BF16) for hh, gl in zip(heads, glane)],
            e_tot=[e_tot_all[0:1, gl:gl + 1] for gl in glane])

    for first in range(0, tile // c, GDN_GROUP):
        group = [dict() for _ in range(GDN_GROUP)]
        gens = [prepare(first + gi, group[gi]) for gi in range(GDN_GROUP)]
        for _ in range(GDN_PREP_LAYERS):
            for gen in gens:
                next(gen)
            yield
        for gen in gens:
            for _ in gen:
                pass
        for pre in group:
            state = [state_ref[hh] for hh in heads]
            ws = [_dot(pre["wq"][hh], state[hh].astype(BF16)) for hh in heads]
            yield
            v_new = [(pre["u"][hh] - ws[hh][0:c]).astype(BF16) for hh in heads]
            kv = [_dot_tn(pre["k_dec"][hh], v_new[hh]) for hh in heads]
            av = [_dot(pre["a_qk"][pp], stack_diag(v_new[a], v_new[b])) for pp, (a, b) in enumerate(pairs)]
            yield
            for hh in heads:
                state_ref[hh] = state[hh] * pre["e_tot"][hh] + kv[hh]
            for hh in heads:
                o = _rmsnorm(ws[hh][c:2 * c] + av[hh // 2][:, (hh % 2) * d:(hh % 2 + 1) * d], gnorm)
                for vv in range(c // SUBLANES):
                    ob_ref[slot, hh, pl.ds(pre["first"] + vv, SUBLANES, stride=c // SUBLANES), :] = (
                        o[vv * SUBLANES:(vv + 1) * SUBLANES])


GDN_GROUP = 4
GDN_PREP_LAYERS = 9
GDN_LAYERS_PER_GROUP = GDN_PREP_LAYERS + 2 * GDN_GROUP


def _mixer_out_stages(x_ref, ya_ref, ob_ref, slot, g_ref, wgt_ref, wa_ref, wb_ref, wo_ref, o_ref):
    x = x_ref[...]
    h = _rmsnorm(x, g_ref[...]).astype(BF16)
    ya = ya_ref[...].astype(BF16)
    blocks = [slice(j * MIX_BLOCK, (j + 1) * MIX_BLOCK) for j in range(D_MODEL // MIX_BLOCK)]
    gate_cols = lambda which, blk: slice(which * D_MODEL + blk.start, which * D_MODEL + blk.stop)
    yb = []
    for blk in blocks:
        gdn_gate = _dot(h, wgt_ref[:, gate_cols(0, blk)])
        yield
        ob = jnp.concatenate([ob_ref[slot, hh] for hh in range(blk.start // GDN_HEAD_DIM, blk.stop // GDN_HEAD_DIM)],
                             axis=1)
        yb.append((ob * (gdn_gate * _sigmoid(gdn_gate))).astype(BF16))
    yb = jnp.concatenate(yb, axis=1)
    merged = []
    for blk in blocks:
        gate_a = _dot(h, wgt_ref[:, gate_cols(1, blk)])
        branch_a = _dot(ya, wa_ref[:, blk])
        yield
        gate_b = _dot(h, wgt_ref[:, gate_cols(2, blk)])
        yield
        branch_b = _dot(yb, wb_ref[:, blk])
        yield
        merged.append((_sigmoid(gate_a) * branch_a + _sigmoid(gate_b) * branch_b).astype(BF16))
    merged = jnp.concatenate(merged, axis=1)
    for blk in blocks:
        o_ref[:, blk] = x[:, blk] + _dot(merged, wo_ref[:, blk])
        yield


MIX_GRANULES = 5 * (D_MODEL // MIX_BLOCK)


def _mixer_tail_kernel(q_ref, k_ref, v_ref, bd_ref, gn_ref, x_ref, ya_ref, g_ref, wgt_ref, wa_ref, wb_ref, wo_ref,
                       *refs, tile, tiles_per_seq, n_tiles):
    n_cast = (len(refs) - 3) // 2
    o_ref, state_ref, ob_ref = refs[n_cast], refs[2 * n_cast + 1], refs[2 * n_cast + 2]
    step = pl.program_id(0)

    @pl.when(step == 0)
    def _():
        ob_ref[...] = jnp.zeros(ob_ref.shape, F32)

    @pl.when(jnp.minimum(step, n_tiles - 1) % tiles_per_seq == 0)
    def _():
        state_ref[...] = jnp.zeros(state_ref.shape, F32)

    slot = step % 2
    gdn = _deltanet_stages(q_ref, k_ref, v_ref, bd_ref, gn_ref[...], state_ref, ob_ref, slot, tile=tile)
    mix = _mixer_out_stages(x_ref, ya_ref, ob_ref, 1 - slot, g_ref, wgt_ref, wa_ref, wb_ref, wo_ref, o_ref)
    _interleave(gdn, GDN_LAYERS_PER_GROUP * (tile // (GDN_CHUNK * GDN_GROUP)), mix, MIX_GRANULES)
    _cast_rows(refs[:n_cast], refs[n_cast + 1:2 * n_cast + 1])


def _mixer_tail(qb, kb, vb, bd, out_norm, x1, ya, norm_g, w_gates, w_a, w_b, w_o, *, tile, seq, cast=(), layer=0):
    n = x1.shape[0]
    n_tiles = n // tile
    cast_in, cast_out, cast_shapes = _cast_specs(cast, layer, n_tiles)
    cur = lambda w: pl.BlockSpec((tile, w), lambda s: (jnp.minimum(s, n_tiles - 1), 0))
    prev = lambda w: pl.BlockSpec((tile, w), lambda s: (jnp.maximum(s - 1, 0), 0))
    out = pl.pallas_call(
        functools.partial(_mixer_tail_kernel, tile=tile, tiles_per_seq=seq // tile, n_tiles=n_tiles),
        grid=(n_tiles + 1,),
        in_specs=[cur(GDN_WIDTH), cur(GDN_WIDTH), cur(GDN_WIDTH), cur(LANES), _resident((1, GDN_HEAD_DIM)),
                  prev(D_MODEL), prev(ATT_GROUP_WIDTH), _resident((1, D_MODEL)),
                  _resident(w_gates.shape), _resident(w_a.shape), _resident(w_b.shape), _resident(w_o.shape)]
                 + cast_in,
        out_specs=[prev(D_MODEL)] + cast_out,
        out_shape=[jax.ShapeDtypeStruct((n, D_MODEL), F32)] + cast_shapes,
        scratch_shapes=[pltpu.VMEM((GDN_HEADS, GDN_HEAD_DIM, GDN_HEAD_DIM), F32),
                        pltpu.VMEM((2, GDN_HEADS, tile, GDN_HEAD_DIM), F32)],
        compiler_params=pltpu.CompilerParams(dimension_semantics=("arbitrary",),
                                             vmem_limit_bytes=VMEM_LIMIT_BYTES),
        name="deltanet_mixer_out",
    )(qb, kb, vb, bd, out_norm, x1, ya, norm_g, w_gates, w_a, w_b, w_o, *cast)
    return out[0], out[1:]


def _rope_tables(seq):
    half = ATT_HEAD_DIM // 2
    inv_freq = ROPE_THETA ** (-jnp.arange(half, dtype=F32) / half)
    ang = jnp.arange(seq, dtype=F32)[:, None] * inv_freq[None, :]
    cos, sin = jnp.cos(ang), jnp.sin(ang)
    reps = LANES // ATT_HEAD_DIM
    return jnp.tile(jnp.concatenate([cos, cos], axis=-1), (1, reps)), jnp.tile(jnp.concatenate([-sin, sin], axis=-1), (1, reps))


def _pad_lanes(row, offset):
    return jnp.zeros((1, LANES), F32).at[0, offset:offset + row.shape[0]].set(row.astype(F32))


def _layer(x, ffn1_norm, ffn1_w_gate, ffn1_w_up, ffn1_w_down, mix_norm, w_in_all, gdn_conv_w, gdn_a_log, gdn_dt_bias,
           gdn_out_norm, w_branch_a, w_branch_b, w_out, ffn2_norm, ffn2_all, fin_g,
           *, layer, final_norm, tm_ffn, tm_mix, gdn_tile):
    b, s, _ = x.shape
    n = b * s
    row = lambda v: v.reshape(1, -1).astype(F32)
    x1 = _ffn(x.reshape(n, D_MODEL), row(ffn1_norm), ffn1_w_gate.astype(BF16), ffn1_w_up.astype(BF16),
              ffn1_w_down.astype(BF16), fin_g, final_norm=False, tm=tm_ffn)
    w_in = w_in_all[layer].astype(BF16)
    w_gates = w_in[:, W_IN_GATES:]
    cos_t, sin_t = _rope_tables(s)
    a0, a1, a2, qb, kb, vb, bd = _mixer_in(
        x1, row(mix_norm), w_in, gdn_conv_w.astype(F32), _pad_lanes(gdn_a_log, GDN_HEADS),
        _pad_lanes(gdn_dt_bias, GDN_HEADS), cos_t, sin_t, tm=tm_mix, seq=s)

    ya = _attention(a0, a1, a2, batch=b)
    x2, ffn2_w = _mixer_tail(qb, kb, vb, bd, row(gdn_out_norm), x1, ya.reshape(n, ATT_GROUP_WIDTH), row(mix_norm),
                             w_gates, w_branch_a.astype(BF16), w_branch_b.astype(BF16), w_out.astype(BF16),
                             tile=gdn_tile, seq=s, cast=ffn2_all, layer=layer)
    x3 = _ffn(x2, row(ffn2_norm), *ffn2_w, fin_g, final_norm=final_norm, tm=tm_ffn)
    return x3.reshape(b, s, D_MODEL)


def kernel(x, ffn1_norm, ffn1_w_gate, ffn1_w_up, ffn1_w_down, mix_norm, w_in, gdn_conv_w, gdn_a_log, gdn_dt_bias,
           gdn_out_norm, w_branch_a, w_branch_b, w_out, ffn2_norm, ffn2_w_gate, ffn2_w_up, ffn2_w_down, final_norm):
    depth = ffn1_norm.shape[0]
    fin_g = final_norm.reshape(1, -1).astype(F32)
    for layer in range(depth):
        x = _layer(x, ffn1_norm[layer], ffn1_w_gate[layer], ffn1_w_up[layer], ffn1_w_down[layer], mix_norm[layer],
                   w_in, gdn_conv_w[layer], gdn_a_log[layer], gdn_dt_bias[layer], gdn_out_norm[layer],
                   w_branch_a[layer], w_branch_b[layer], w_out[layer], ffn2_norm[layer],
                   (ffn2_w_gate, ffn2_w_up, ffn2_w_down), fin_g, layer=layer, final_norm=(layer == depth - 1),
                   tm_ffn=512, tm_mix=512, gdn_tile=512)
    return x
```

```python
import functools

import jax
import jax.numpy as jnp
from jax import lax
from jax.experimental import pallas as pl
from jax.experimental.pallas import tpu as pltpu

F32 = jnp.float32
BF16 = jnp.bfloat16

D_MODEL = 1024
D_FF = 2816
EPS = 1e-6

ATT_GROUPS = ((128, 1), (512, 4), (2048, 16))
ATT_HEADS_PER_GROUP = 4
ATT_HEAD_DIM = 64
ATT_BLOCK = 128
ATT_GROUP_WIDTH = ATT_HEADS_PER_GROUP * ATT_HEAD_DIM
ATT_QKV_WIDTH = len(ATT_GROUPS) * ATT_GROUP_WIDTH
ROPE_THETA = 10000.0

GDN_HEADS = 8
GDN_HEAD_DIM = 128
GDN_WIDTH = GDN_HEADS * GDN_HEAD_DIM
GDN_CONV = 4
GDN_CHUNK = 64

LANES = 128
SUBLANES = 8
VMEM_LIMIT_BYTES = 56 * 1024 * 1024
TOKEN_TILE = 512

W_IN_GDN = 3 * ATT_QKV_WIDTH
W_IN_BD = W_IN_GDN + 3 * GDN_WIDTH
W_IN_GATES = W_IN_BD + 2 * GDN_HEADS
FFN_CHUNKS = ((0, 768), (768, 1536), (1536, 2304), (2304, 2816))
NEG_BIG = -1e30


def _resident(shape):
    nd = len(shape)
    return pl.BlockSpec(shape, lambda *_: (0,) * nd, pipeline_mode=pl.Buffered(1))


def _rmsnorm(x, g):
    return x * lax.rsqrt(jnp.mean(x * x, axis=-1, keepdims=True) + EPS) * g


def _sigmoid(x):
    return 1.0 / (1.0 + jnp.exp(-x))


def _dot(a, b):
    return jnp.dot(a, b, preferred_element_type=F32)


def _dot_nt(a, b):
    return lax.dot_general(a, b, (((1,), (1,)), ((), ())), preferred_element_type=F32)


def _dot_tn(a, b):
    return lax.dot_general(a, b, (((0,), (0,)), ((), ())), preferred_element_type=F32)


def _swiglu_residual(x, g, wg_ref, wu_ref, wd_ref):
    h = _rmsnorm(x, g).astype(BF16)
    acc = x
    for lo, hi in FFN_CHUNKS:
        gate = _dot(h, wg_ref[:, lo:hi])
        up = _dot(h, wu_ref[:, lo:hi])
        act = (0.5 * gate * _sigmoid(gate) * up).astype(BF16)
        acc = acc + _dot(act, wd_ref[lo:hi, :])
    return acc


def _cast_rows(src_refs, dst_refs):
    for src, dst in zip(src_refs, dst_refs):
        dst[...] = src[...].astype(BF16)


def _cast_specs(arrays, layer, steps):
    in_specs, out_specs, shapes = [], [], []
    for arr in arrays:
        _, n_rows, n_cols = arr.shape
        packed_rows = 2 * SUBLANES
        parts = next(p for p in range(steps, 0, -1) if n_rows % p == 0 and (n_rows // p) % packed_rows == 0)
        in_specs.append(pl.BlockSpec((None, n_rows // parts, n_cols),
                                     lambda s, parts=parts: (layer, jnp.minimum(s, parts - 1), 0)))
        out_specs.append(pl.BlockSpec((n_rows // parts, n_cols), lambda s, parts=parts: (jnp.minimum(s, parts - 1), 0)))
        shapes.append(jax.ShapeDtypeStruct((n_rows, n_cols), BF16))
    return in_specs, out_specs, shapes


def _ffn_kernel(x_ref, g_ref, wg_ref, wu_ref, wd_ref, fin_ref, o_ref, *, final_norm):
    y = _swiglu_residual(x_ref[...], g_ref[...], wg_ref, wu_ref, wd_ref)
    if final_norm:
        y = _rmsnorm(y, fin_ref[...])
    o_ref[...] = y


def _ffn(x, norm_g, wg, wu, wd, fin_g, *, final_norm, tm):
    n = x.shape[0]
    row = pl.BlockSpec((tm, D_MODEL), lambda i: (i, 0))
    return pl.pallas_call(
        functools.partial(_ffn_kernel, final_norm=final_norm),
        grid=(n // tm,),
        in_specs=[row, _resident((1, D_MODEL)), _resident((D_MODEL, D_FF)), _resident((D_MODEL, D_FF)),
                  _resident((D_FF, D_MODEL)), _resident((1, D_MODEL))],
        out_specs=row,
        out_shape=jax.ShapeDtypeStruct((n, D_MODEL), F32),
        compiler_params=pltpu.CompilerParams(dimension_semantics=("arbitrary",),
                                             vmem_limit_bytes=VMEM_LIMIT_BYTES),
        name="ffn_final" if final_norm else "ffn",
    )(x, norm_g, wg, wu, wd, fin_g)


MIX_BLOCK = 256
QKV_BLOCK = 512


def _interleave(primary, n_primary, secondary, n_secondary):
    done = 0
    for i in range(n_primary):
        next(primary)
        while done * n_primary < (i + 1) * n_secondary:
            next(secondary)
            done += 1
    for gen in (primary, secondary):
        for _ in gen:
            pass


def _chunk_time(row):
    return SUBLANES * (row % SUBLANES) + row // SUBLANES


def _attention_qkv_stages(h, wm_ref, cos_ref, sin_ref, att_refs, stage_ref, *, tm):
    cos = cos_ref[...]
    sin = sin_ref[...]
    lane = lax.broadcasted_iota(jnp.int32, (1, LANES), 1)
    first_half = (lane % ATT_HEAD_DIM) < (ATT_HEAD_DIM // 2)
    slot = 0
    for part in range(3):
        for gi, (_, dil) in enumerate(ATT_GROUPS):
            col = part * ATT_QKV_WIDTH + gi * ATT_GROUP_WIDTH
            y = _dot(h, wm_ref[:, col:col + ATT_GROUP_WIDTH])
            yield
            for j in range(ATT_GROUP_WIDTH // LANES):
                blk = y[:, j * LANES:(j + 1) * LANES]
                if part < 2:
                    swapped = jnp.where(first_half, pltpu.roll(blk, LANES - ATT_HEAD_DIM // 2, 1),
                                        pltpu.roll(blk, ATT_HEAD_DIM // 2, 1))
                    blk = blk * cos + swapped * sin
                if part == 0:
                    blk = blk * (ATT_HEAD_DIM ** -0.5)
                dst = part * ATT_GROUP_WIDTH + j * LANES
                if dil == 1:
                    att_refs[gi][:, dst:dst + LANES] = blk.astype(BF16)
                else:
                    stage_ref[slot] = blk
                    for r in range(dil):
                        rows = stage_ref[slot, pl.ds(r, tm // dil, stride=dil), :]
                        lo = r * 3 * ATT_GROUP_WIDTH + dst
                        att_refs[gi][:, lo:lo + LANES] = rows.astype(BF16)
                    slot += 1
                yield


ATT_QKV_STAGES = 3 * len(ATT_GROUPS) * (1 + ATT_GROUP_WIDTH // LANES)
MIXER_IN_STAGED = 3 * (len(ATT_GROUPS) - 1) * (ATT_GROUP_WIDTH // LANES)


def _deltanet_qkv_stages(h_perm, wm_ref, wbd_ref, convw_ref, alog_ref, dtb_ref, gdn_refs, bd_ref, carry_ref, *, tm):
    lane = lax.broadcasted_iota(jnp.int32, (1, LANES), 1)
    raw = _dot(h_perm, wbd_ref[...])
    yield
    z = raw + dtb_ref[...]
    softplus = jnp.maximum(z, 0.0) + jnp.log1p(jnp.exp(-jnp.abs(z)))
    g = -jnp.exp(alog_ref[...]) * softplus
    bd_ref[...] = jnp.where(lane < GDN_HEADS, _sigmoid(raw), jnp.where(lane < 2 * GDN_HEADS, g, 0.0))

    vregs = GDN_CHUNK // SUBLANES
    halo = GDN_CONV - 1
    chunks = tm // GDN_CHUNK
    last_sublane = lax.broadcasted_iota(jnp.int32, (SUBLANES, GDN_HEAD_DIM), 0) == SUBLANES - 1
    heads_per_block = QKV_BLOCK // GDN_HEAD_DIM
    for part in range(3):
        for blk in range(GDN_WIDTH // QKV_BLOCK):
            base = part * GDN_WIDTH + blk * QKV_BLOCK
            y = _dot(h_perm, wm_ref[:, W_IN_GDN + base:W_IN_GDN + base + QKV_BLOCK])
            yield
            for hb in range(heads_per_block):
                hh = blk * heads_per_block + hb
                col = part * GDN_WIDTH + hh * GDN_HEAD_DIM
                cur = y[:, hb * GDN_HEAD_DIM:(hb + 1) * GDN_HEAD_DIM]
                vreg = lambda c, v: cur[c * GDN_CHUNK + v * SUBLANES:c * GDN_CHUNK + (v + 1) * SUBLANES]
                prev_tile = carry_ref[:, col:col + GDN_HEAD_DIM]
                carry_ref[:, col:col + GDN_HEAD_DIM] = cur[tm - halo * SUBLANES:, :]
                wrapped = []
                for c in range(chunks):
                    row = []
                    for i in range(halo):
                        before = (prev_tile[i * SUBLANES:(i + 1) * SUBLANES] if c == 0
                                  else vreg(c - 1, vregs - halo + i))
                        own = vreg(c, vregs - halo + i)
                        row.append(pltpu.roll(jnp.where(last_sublane, before, own), 1, 0))
                    wrapped.append(row)
                acc = cur * convw_ref[halo:halo + 1, col:col + GDN_HEAD_DIM]
                for shift in range(1, GDN_CONV):
                    pieces = []
                    for c in range(chunks):
                        pieces += wrapped[c][halo - shift:]
                        pieces.append(cur[c * GDN_CHUNK:(c + 1) * GDN_CHUNK - shift * SUBLANES])
                    shifted = jnp.concatenate(pieces, axis=0)
                    acc = acc + shifted * convw_ref[halo - shift:halo - shift + 1, col:col + GDN_HEAD_DIM]
                act = acc * _sigmoid(acc)
                if part < 2:
                    act = act * lax.rsqrt(jnp.sum(act * act, axis=-1, keepdims=True) + EPS)
                if part == 0:
                    act = act * (GDN_HEAD_DIM ** -0.5)
                gdn_refs[part][:, hh * GDN_HEAD_DIM:(hh + 1) * GDN_HEAD_DIM] = act
                yield


DELTANET_QKV_STAGES = 1 + 3 * (GDN_WIDTH // QKV_BLOCK) + 3 * GDN_HEADS


def _mixer_in_kernel(x_ref, g_ref, wm_ref, wbd_ref, convw_ref, alog_ref, dtb_ref, cos_ref, sin_ref,
                     a0_ref, a1_ref, a2_ref, qb_ref, kb_ref, vb_ref, bd_ref, carry_ref, perm_ref, stage_ref, *, tm):
    @pl.when(pl.program_id(1) == 0)
    def _():
        carry_ref[...] = jnp.zeros(carry_ref.shape, F32)

    hf = _rmsnorm(x_ref[...], g_ref[...])
    h = hf.astype(BF16)
    vregs = GDN_CHUNK // SUBLANES
    for cb in range(D_MODEL // LANES):
        perm_ref[cb] = hf[:, cb * LANES:(cb + 1) * LANES]
    h_perm = jnp.concatenate(
        [jnp.concatenate([perm_ref[cb, pl.ds(c0 + v, SUBLANES, stride=vregs), :]
                          for c0 in range(0, tm, GDN_CHUNK) for v in range(vregs)], axis=0)
         for cb in range(D_MODEL // LANES)], axis=1).astype(BF16)
    gdn = _deltanet_qkv_stages(h_perm, wm_ref, wbd_ref, convw_ref, alog_ref, dtb_ref, (qb_ref, kb_ref, vb_ref), bd_ref,
                               carry_ref, tm=tm)
    att = _attention_qkv_stages(h, wm_ref, cos_ref, sin_ref, (a0_ref, a1_ref, a2_ref), stage_ref, tm=tm)
    _interleave(gdn, DELTANET_QKV_STAGES, att, ATT_QKV_STAGES)


def _mixer_in(x1, norm_g, w_in, conv_w, a_log, dt_bias, cos_t, sin_t, *, tm, seq):
    n = x1.shape[0]
    tiles_per_seq = seq // tm
    tile = lambda rows, w: pl.BlockSpec((rows, w), lambda bi, i: (bi * tiles_per_seq + i, 0))
    table = pl.BlockSpec((tm, LANES), lambda bi, i: (i, 0))
    wq = 3 * ATT_GROUP_WIDTH
    att_specs = [tile(tm // dil, dil * wq) for _, dil in ATT_GROUPS]
    att_shapes = [jax.ShapeDtypeStruct((n // dil, dil * wq), BF16) for _, dil in ATT_GROUPS]
    gdn = jax.ShapeDtypeStruct((n, GDN_WIDTH), F32)
    return pl.pallas_call(
        functools.partial(_mixer_in_kernel, tm=tm),
        grid=(n // seq, tiles_per_seq),
        in_specs=[tile(tm, D_MODEL), _resident((1, D_MODEL)),
                  pl.BlockSpec((D_MODEL, W_IN_BD), lambda bi, i: (0, 0), pipeline_mode=pl.Buffered(1)),
                  pl.BlockSpec((D_MODEL, LANES), lambda bi, i: (0, W_IN_BD // LANES), pipeline_mode=pl.Buffered(1)),
                  _resident((GDN_CONV, 3 * GDN_WIDTH)), _resident((1, LANES)), _resident((1, LANES)), table, table],
        out_specs=att_specs + [tile(tm, GDN_WIDTH)] * 3 + [tile(tm, LANES)],
        out_shape=att_shapes + [gdn] * 3 + [jax.ShapeDtypeStruct((n, LANES), F32)],
        scratch_shapes=[pltpu.VMEM(((GDN_CONV - 1) * SUBLANES, 3 * GDN_WIDTH), F32),
                        pltpu.VMEM((D_MODEL // LANES, tm, LANES), F32), pltpu.VMEM((MIXER_IN_STAGED, tm, LANES), F32)],
        compiler_params=pltpu.CompilerParams(dimension_semantics=("arbitrary", "arbitrary"),
                                             vmem_limit_bytes=VMEM_LIMIT_BYTES),
        name="mixer_in",
    )(x1, norm_g, w_in, w_in, conv_w, a_log, dt_bias, cos_t, sin_t)


ATT_BATCH = 3


def _attention_blocks(items):
    lane = lax.broadcasted_iota(jnp.int32, (1, ATT_GROUP_WIDTH), 1)
    heads = range(ATT_HEADS_PER_GROUP)
    in_head = [(lane // ATT_HEAD_DIM) == hh for hh in heads]
    keep = [jnp.where(in_head[hh], 1.0, 0.0).astype(BF16) for hh in heads]
    nq = ATT_BLOCK
    s_all = [_dot_nt(jnp.concatenate([q * keep[hh] for hh in heads], axis=0), k) for q, k, _, _ in items]
    stats, p_all = [], []
    for (_, _, _, valid), sa in zip(items, s_all):
        s = [jnp.where(valid, sa[hh * nq:(hh + 1) * nq], NEG_BIG) for hh in heads]
        m = [jnp.max(s[hh], axis=-1, keepdims=True) for hh in heads]
        p = [jnp.exp(s[hh] - m[hh]) for hh in heads]
        l = [jnp.sum(p[hh], axis=-1, keepdims=True) for hh in heads]
        stats.append((m, l))
        p_all.append(jnp.concatenate([p[hh].astype(BF16) for hh in heads], axis=0))
    pv_all = [_dot(ps, v) for ps, (_, _, v, _) in zip(p_all, items)]
    outs = []
    for pv, (m, l) in zip(pv_all, stats):
        o = jnp.zeros((ATT_BLOCK, ATT_GROUP_WIDTH), F32)
        lse = jnp.zeros((ATT_BLOCK, ATT_GROUP_WIDTH), F32)
        for hh in heads:
            o = jnp.where(in_head[hh], pv[hh * nq:(hh + 1) * nq] * (1.0 / l[hh]), o)
            lse = jnp.where(in_head[hh], m[hh] + jnp.log(l[hh]), lse)
        outs.append((o, lse))
    return outs


def _attention_kernel(a0_ref, a1_ref, a2_ref, ya_ref, o0, l0, o1, l1, o2, l2, *, seq):
    in_refs = (a0_ref, a1_ref, a2_ref)
    o_refs = (o0, o1, o2)
    l_refs = (l0, l1, l2)
    qi = lax.broadcasted_iota(jnp.int32, (ATT_BLOCK, ATT_BLOCK), 0)
    kj = lax.broadcasted_iota(jnp.int32, (ATT_BLOCK, ATT_BLOCK), 1)
    causal = kj <= qi
    qi2 = lax.broadcasted_iota(jnp.int32, (ATT_BLOCK, 2 * ATT_BLOCK), 0)
    kj2 = lax.broadcasted_iota(jnp.int32, (ATT_BLOCK, 2 * ATT_BLOCK), 1)
    band = (kj2 >= qi2) & (kj2 - ATT_BLOCK <= qi2)
    wq = 3 * ATT_GROUP_WIDTH

    def load(gi, r, n):
        src, base = in_refs[gi], r * wq
        if isinstance(n, int) and n == 0:
            qrows = krows = slice(0, ATT_BLOCK)
            valid = causal
        else:
            start = lambda x: x if isinstance(x, int) else pl.multiple_of(x, ATT_BLOCK)
            qrows = pl.ds(start(n * ATT_BLOCK), ATT_BLOCK)
            krows = pl.ds(start((n - 1) * ATT_BLOCK), 2 * ATT_BLOCK)
            valid = band
        return (src[0, qrows, base:base + ATT_GROUP_WIDTH],
                src[0, krows, base + ATT_GROUP_WIDTH:base + 2 * ATT_GROUP_WIDTH],
                src[0, krows, base + 2 * ATT_GROUP_WIDTH:base + 3 * ATT_GROUP_WIDTH], valid)

    def store(gi, r, n, o, lse):
        dil = ATT_GROUPS[gi][1]
        if dil == 1:
            first = n * ATT_BLOCK
            rows = pl.ds(first if isinstance(first, int) else pl.multiple_of(first, ATT_BLOCK), ATT_BLOCK)
        else:
            rows = pl.ds(n * ATT_BLOCK * dil + r, ATT_BLOCK, stride=dil)
        for half in range(ATT_GROUP_WIDTH // LANES):
            o_refs[gi][half, rows, :] = o[:, half * LANES:(half + 1) * LANES]
            l_refs[gi][half, rows, :] = lse[:, half * LANES:(half + 1) * LANES]

    def run(blocks):
        for (gi, r, n), (o, lse) in zip(blocks, _attention_blocks([load(*blk) for blk in blocks])):
            store(gi, r, n, o, lse)

    static_blocks = []
    looped = None
    for gi, (window, dil) in enumerate(ATT_GROUPS):
        assert window // dil == ATT_BLOCK
        nblk = seq // dil // ATT_BLOCK
        if dil == 1 and (nblk - 1) % ATT_BATCH == 0:
            static_blocks.append((gi, 0, 0))
            looped = (gi, nblk)
        else:
            static_blocks += [(gi, r, n) for r in range(dil) for n in range(nblk)]
    for i in range(0, len(static_blocks), ATT_BATCH + 1):
        run(static_blocks[i:i + ATT_BATCH + 1])
    if looped is not None:
        gi, nblk = looped

        def body(i, carry):
            run([(gi, 0, 1 + i * ATT_BATCH + j) for j in range(ATT_BATCH)])
            return carry
        lax.fori_loop(0, (nblk - 1) // ATT_BATCH, body, 0)

    rows_per_step = 256

    def merge(i, carry):
        rows = pl.ds(pl.multiple_of(i * rows_per_step, rows_per_step), rows_per_step)
        for half in range(ATT_GROUP_WIDTH // LANES):
            la, lb, lc = l0[half, rows, :], l1[half, rows, :], l2[half, rows, :]
            m = jnp.maximum(jnp.maximum(la, lb), lc)
            ea, eb, ec = jnp.exp(la - m), jnp.exp(lb - m), jnp.exp(lc - m)
            num = ea * o0[half, rows, :] + eb * o1[half, rows, :] + ec * o2[half, rows, :]
            ya_ref[0, rows, half * LANES:(half + 1) * LANES] = num / (ea + eb + ec)
        return carry
    lax.fori_loop(0, seq // rows_per_step, merge, 0)


def _attention(a0, a1, a2, *, batch):
    views = tuple(a.reshape(batch, a.shape[0] // batch, a.shape[1]) for a in (a0, a1, a2))
    b, s, _ = views[0].shape
    specs = [pl.BlockSpec((1,) + arr.shape[1:], lambda bi: (bi, 0, 0)) for arr in views]
    scratch = [pltpu.VMEM((ATT_GROUP_WIDTH // LANES, s, LANES), F32) for _ in range(6)]
    return pl.pallas_call(
        functools.partial(_attention_kernel, seq=s),
        grid=(b,),
        in_specs=specs,
        out_specs=pl.BlockSpec((1, s, ATT_GROUP_WIDTH), lambda bi: (bi, 0, 0)),
        out_shape=jax.ShapeDtypeStruct((b, s, ATT_GROUP_WIDTH), F32),
        scratch_shapes=scratch,
        compiler_params=pltpu.CompilerParams(dimension_semantics=("arbitrary",),
                                             vmem_limit_bytes=VMEM_LIMIT_BYTES),
        name="dilated_attention",
    )(*views)


def _deltanet_stages(q_ref, k_ref, v_ref, bd_ref, gnorm, state_ref, ob_ref, slot, *, tile):
    c = GDN_CHUNK
    d = GDN_HEAD_DIM
    heads = range(GDN_HEADS)
    pairs = [(2 * pp, 2 * pp + 1) for pp in range(GDN_HEADS // 2)]
    cols = [slice(hh * d, (hh + 1) * d) for hh in heads]
    glane = [GDN_HEADS + hh for hh in heads]
    ii = _chunk_time(lax.broadcasted_iota(jnp.int32, (c, 2 * c), 0))
    ll = lax.broadcasted_iota(jnp.int32, (c, 2 * c), 1)
    jj = _chunk_time(ll % c)
    lower = ii >= jj
    strict = ii > jj
    left = ll < c
    left_row = lax.broadcasted_iota(jnp.int32, (1, 2 * c), 1) < c
    keep_left = jnp.where(left, 1.0, 0.0).astype(BF16)
    keep_right = jnp.where(left, 0.0, 1.0).astype(BF16)
    ti = _chunk_time(lax.broadcasted_iota(jnp.int32, (c, c), 0))
    tj = _chunk_time(lax.broadcasted_iota(jnp.int32, (c, c), 1))
    tri_ones = jnp.where(ti >= tj, 1.0, 0.0).astype(BF16)

    def blockdiag(x):
        return jnp.concatenate([x * keep_left, x * keep_right], axis=0)

    def stack_diag(xa, xb):
        zero = jnp.zeros_like(xa)
        return jnp.concatenate([jnp.concatenate([xa, zero], axis=1), jnp.concatenate([zero, xb], axis=1)], axis=0)

    def prepare(ci, out):
        rows = slice(ci * c, (ci + 1) * c)
        bd = bd_ref[rows, :]
        bd_hi = bd.astype(BF16)
        bd_rest = bd - bd_hi.astype(F32)
        bd_mid = bd_rest.astype(BF16)
        bd_lo = (bd_rest - bd_mid.astype(F32)).astype(BF16)
        gcum = _dot(tri_ones, bd_hi) + _dot(tri_ones, bd_mid) + _dot(tri_ones, bd_lo)
        yield
        gtot = jnp.broadcast_to(gcum[c - 1:c, :], (c, LANES))
        gcum_t = jnp.concatenate([gcum, gcum], axis=0).T
        e_cum_all = jnp.exp(gcum)
        e_rest_all = jnp.exp(gtot - gcum)
        e_tot_all = jnp.exp(gtot)
        q = [q_ref[rows, cols[hh]] for hh in heads]
        k = [k_ref[rows, cols[hh]] for hh in heads]
        v = [v_ref[rows, cols[hh]] for hh in heads]
        beta = [bd[:, hh:hh + 1] for hh in heads]
        e_cum = [e_cum_all[:, gl:gl + 1] for gl in glane]
        kbeta = [k[hh] * beta[hh] for hh in heads]
        kq = [_dot_nt(jnp.concatenate([jnp.concatenate([kbeta[a], kbeta[b]], axis=1),
                                       jnp.concatenate([q[a], q[b]], axis=1)], axis=0).astype(BF16),
                      stack_diag(k[a].astype(BF16), k[b].astype(BF16)))
              for a, b in pairs]
        yield
        decay = [jnp.exp(jnp.where(lower,
                                   jnp.where(left, gcum[:, glane[a]:glane[a] + 1], gcum[:, glane[b]:glane[b] + 1])
                                   - jnp.where(left_row, gcum_t[glane[a]:glane[a] + 1, :], gcum_t[glane[b]:glane[b] + 1, :]),
                                   NEG_BIG)) for a, b in pairs]
        m = [jnp.where(strict, kq[pp][0:c] * decay[pp], 0.0) for pp in range(len(pairs))]
        n = [-mm for mm in m]
        pb = [mm.astype(BF16) for mm in m]
        p = [_dot(x, blockdiag(x)) for x in pb]
        yield
        rounds = 5
        for r in range(rounds):
            pb = [x.astype(BF16) for x in p]
            upd = [_dot(x, blockdiag(y.astype(BF16))) for x, y in zip(pb, n)]
            p_next = [_dot(x, blockdiag(x)) for x in pb] if r + 1 < rounds else None
            yield
            n = [y + x + u for y, x, u in zip(n, p, upd)]
            p = p_next
        rhs = [jnp.concatenate([v[hh] * beta[hh], kbeta[hh] * e_cum[hh]], axis=1) for hh in heads]
        nr = [_dot(n[pp].astype(BF16), stack_diag(rhs[a].astype(BF16), rhs[b].astype(BF16)))
              for pp, (a, b) in enumerate(pairs)]
        yield
        sol = [rhs[hh] + nr[hh // 2][:, (hh % 2) * 2 * d:(hh % 2 + 1) * 2 * d] for hh in heads]
        out.update(
            first=ci * c,
            u=[sol[hh][:, 0:d] for hh in heads],
            wq=[jnp.concatenate([sol[hh][:, d:2 * d], q[hh] * e_cum[hh]], axis=0).astype(BF16) for hh in heads],
            a_qk=[(kq[pp][c:2 * c] * decay[pp]).astype(BF16) for pp in range(len(pairs))],
            k_dec=[(k[hh] * e_rest_all[:, gl:gl + 1]).astype(BF16) for hh, gl in zip(heads, glane)],
            e_tot=[e_tot_all[0:1, gl:gl + 1] for gl in glane])

    for first in range(0, tile // c, GDN_GROUP):
        group = [dict() for _ in range(GDN_GROUP)]
        gens = [prepare(first + gi, group[gi]) for gi in range(GDN_GROUP)]
        for _ in range(GDN_PREP_LAYERS):
            for gen in gens:
                next(gen)
            yield
        for gen in gens:
            for _ in gen:
                pass
        for pre in group:
            state = [state_ref[hh] for hh in heads]
            ws = [_dot(pre["wq"][hh], state[hh].astype(BF16)) for hh in heads]
            yield
            v_new = [(pre["u"][hh] - ws[hh][0:c]).astype(BF16) for hh in heads]
            kv = [_dot_tn(pre["k_dec"][hh], v_new[hh]) for hh in heads]
            av = [_dot(pre["a_qk"][pp], stack_diag(v_new[a], v_new[b])) for pp, (a, b) in enumerate(pairs)]
            yield
            for hh in heads:
                state_ref[hh] = state[hh] * pre["e_tot"][hh] + kv[hh]
            for hh in heads:
                o = _rmsnorm(ws[hh][c:2 * c] + av[hh // 2][:, (hh % 2) * d:(hh % 2 + 1) * d], gnorm)
                for vv in range(c // SUBLANES):
                    ob_ref[slot, hh, pl.ds(pre["first"] + vv, SUBLANES, stride=c // SUBLANES), :] = (
                        o[vv * SUBLANES:(vv + 1) * SUBLANES])


GDN_GROUP = 4
GDN_PREP_LAYERS = 9
GDN_LAYERS_PER_GROUP = GDN_PREP_LAYERS + 2 * GDN_GROUP


def _mixer_out_stages(x_ref, ya_ref, ob_ref, slot, g_ref, wgt_ref, wa_ref, wb_ref, wo_ref, o_ref):
    x = x_ref[...]
    h = _rmsnorm(x, g_ref[...]).astype(BF16)
    ya = ya_ref[...].astype(BF16)
    blocks = [slice(j * MIX_BLOCK, (j + 1) * MIX_BLOCK) for j in range(D_MODEL // MIX_BLOCK)]
    gate_cols = lambda which, blk: slice(which * D_MODEL + blk.start, which * D_MODEL + blk.stop)
    yb = []
    for blk in blocks:
        gdn_gate = _dot(h, wgt_ref[:, gate_cols(0, blk)])
        yield
        ob = jnp.concatenate([ob_ref[slot, hh] for hh in range(blk.start // GDN_HEAD_DIM, blk.stop // GDN_HEAD_DIM)],
                             axis=1)
        yb.append((ob * (gdn_gate * _sigmoid(gdn_gate))).astype(BF16))
    yb = jnp.concatenate(yb, axis=1)
    merged = []
    for blk in blocks:
        gate_a = _dot(h, wgt_ref[:, gate_cols(1, blk)])
        branch_a = _dot(ya, wa_ref[:, blk])
        yield
        gate_b = _dot(h, wgt_ref[:, gate_cols(2, blk)])
        yield
        branch_b = _dot(yb, wb_ref[:, blk])
        yield
        merged.append((_sigmoid(gate_a) * branch_a + _sigmoid(gate_b) * branch_b).astype(BF16))
    merged = jnp.concatenate(merged, axis=1)
    for blk in blocks:
        o_ref[:, blk] = x[:, blk] + _dot(merged, wo_ref[:, blk])
        yield


MIX_GRANULES = 5 * (D_MODEL // MIX_BLOCK)


def _mixer_tail_kernel(q_ref, k_ref, v_ref, bd_ref, gn_ref, x_ref, ya_ref, g_ref, wgt_ref, wa_ref, wb_ref, wo_ref,
                       *refs, tile, tiles_per_seq, n_tiles):
    n_cast = (len(refs) - 3) // 2
    o_ref, state_ref, ob_ref = refs[n_cast], refs[2 * n_cast + 1], refs[2 * n_cast + 2]
    step = pl.program_id(0)

    @pl.when(step == 0)
    def _():
        ob_ref[...] = jnp.zeros(ob_ref.shape, F32)

    @pl.when(jnp.minimum(step, n_tiles - 1) % tiles_per_seq == 0)
    def _():
        state_ref[...] = jnp.zeros(state_ref.shape, F32)

    slot = step % 2
    gdn = _deltanet_stages(q_ref, k_ref, v_ref, bd_ref, gn_ref[...], state_ref, ob_ref, slot, tile=tile)
    mix = _mixer_out_stages(x_ref, ya_ref, ob_ref, 1 - slot, g_ref, wgt_ref, wa_ref, wb_ref, wo_ref, o_ref)
    _interleave(gdn, GDN_LAYERS_PER_GROUP * (tile // (GDN_CHUNK * GDN_GROUP)), mix, MIX_GRANULES)
    _cast_rows(refs[:n_cast], refs[n_cast + 1:2 * n_cast + 1])


def _mixer_tail(qb, kb, vb, bd, out_norm, x1, ya, norm_g, w_gates, w_a, w_b, w_o, *, tile, seq, cast=(), layer=0):
    n = x1.shape[0]
    n_tiles = n // tile
    cast_in, cast_out, cast_shapes = _cast_specs(cast, layer, n_tiles)
    cur = lambda w: pl.BlockSpec((tile, w), lambda s: (jnp.minimum(s, n_tiles - 1), 0))
    prev = lambda w: pl.BlockSpec((tile, w), lambda s: (jnp.maximum(s - 1, 0), 0))
    out = pl.pallas_call(
        functools.partial(_mixer_tail_kernel, tile=tile, tiles_per_seq=seq // tile, n_tiles=n_tiles),
        grid=(n_tiles + 1,),
        in_specs=[cur(GDN_WIDTH), cur(GDN_WIDTH), cur(GDN_WIDTH), cur(LANES), _resident((1, GDN_HEAD_DIM)),
                  prev(D_MODEL), prev(ATT_GROUP_WIDTH), _resident((1, D_MODEL)),
                  _resident(w_gates.shape), _resident(w_a.shape), _resident(w_b.shape), _resident(w_o.shape)]
                 + cast_in,
        out_specs=[prev(D_MODEL)] + cast_out,
        out_shape=[jax.ShapeDtypeStruct((n, D_MODEL), F32)] + cast_shapes,
        scratch_shapes=[pltpu.VMEM((GDN_HEADS, GDN_HEAD_DIM, GDN_HEAD_DIM), F32),
                        pltpu.VMEM((2, GDN_HEADS, tile, GDN_HEAD_DIM), F32)],
        compiler_params=pltpu.CompilerParams(dimension_semantics=("arbitrary",),
                                             vmem_limit_bytes=VMEM_LIMIT_BYTES),
        name="deltanet_mixer_out",
    )(qb, kb, vb, bd, out_norm, x1, ya, norm_g, w_gates, w_a, w_b, w_o, *cast)
    return out[0], out[1:]


def _rope_tables(seq):
    half = ATT_HEAD_DIM // 2
    inv_freq = ROPE_THETA ** (-jnp.arange(half, dtype=F32) / half)
    ang = jnp.arange(seq, dtype=F32)[:, None] * inv_freq[None, :]
    cos, sin = jnp.cos(ang), jnp.sin(ang)
    reps = LANES // ATT_HEAD_DIM
    return jnp.tile(jnp.concatenate([cos, cos], axis=-1), (1, reps)), jnp.tile(jnp.concatenate([-sin, sin], axis=-1), (1, reps))


def _pad_lanes(row, offset):
    return jnp.zeros((1, LANES), F32).at[0, offset:offset + row.shape[0]].set(row.astype(F32))


def _layer(x, ffn1_norm, ffn1_w_gate, ffn1_w_up, ffn1_w_down, mix_norm, w_in_all, gdn_conv_w, gdn_a_log, gdn_dt_bias,
           gdn_out_norm, w_branch_a, w_branch_b, w_out, ffn2_norm, ffn2_all, fin_g,
           *, layer, final_norm, tm_ffn, tm_mix, gdn_tile):
    b, s, _ = x.shape
    n = b * s
    row = lambda v: v.reshape(1, -1).astype(F32)
    x1 = _ffn(x.reshape(n, D_MODEL), row(ffn1_norm), ffn1_w_gate.astype(BF16), ffn1_w_up.astype(BF16),
              ffn1_w_down.astype(BF16), fin_g, final_norm=False, tm=tm_ffn)
    w_in = w_in_all[layer].astype(BF16)
    w_gates = w_in[:, W_IN_GATES:]
    cos_t, sin_t = _rope_tables(s)
    a0, a1, a2, qb, kb, vb, bd = _mixer_in(
        x1, row(mix_norm), w_in, gdn_conv_w.astype(F32), _pad_lanes(gdn_a_log, GDN_HEADS),
        _pad_lanes(gdn_dt_bias, GDN_HEADS), cos_t, sin_t, tm=tm_mix, seq=s)

    ya = _attention(a0, a1, a2, batch=b)
    x2, ffn2_w = _mixer_tail(qb, kb, vb, bd, row(gdn_out_norm), x1, ya.reshape(n, ATT_GROUP_WIDTH), row(mix_norm),
                             w_gates, w_branch_a.astype(BF16), w_branch_b.astype(BF16), w_out.astype(BF16),
                             tile=gdn_tile, seq=s, cast=ffn2_all, layer=layer)
    x3 = _ffn(x2, row(ffn2_norm), *ffn2_w, fin_g, final_norm=final_norm, tm=tm_ffn)
    return x3.reshape(b, s, D_MODEL)


def kernel(x, ffn1_norm, ffn1_w_gate, ffn1_w_up, ffn1_w_down, mix_norm, w_in, gdn_conv_w, gdn_a_log, gdn_dt_bias,
           gdn_out_norm, w_branch_a, w_branch_b, w_out, ffn2_norm, ffn2_w_gate, ffn2_w_up, ffn2_w_down, final_norm):
    depth = ffn1_norm.shape[0]
    fin_g = final_norm.reshape(1, -1).astype(F32)
    for layer in range(depth):
        x = _layer(x, ffn1_norm[layer], ffn1_w_gate[layer], ffn1_w_up[layer], ffn1_w_down[layer], mix_norm[layer],
                   w_in, gdn_conv_w[layer], gdn_a_log[layer], gdn_dt_bias[layer], gdn_out_norm[layer],
                   w_branch_a[layer], w_branch_b[layer], w_out[layer], ffn2_norm[layer],
                   (ffn2_w_gate, ffn2_w_up, ffn2_w_down), fin_g, layer=layer, final_norm=(layer == depth - 1),
                   tm_ffn=TOKEN_TILE, tm_mix=TOKEN_TILE, gdn_tile=TOKEN_TILE)
    return x
```

```python
import functools

import jax
import jax.numpy as jnp
from jax import lax
from jax.experimental import pallas as pl
from jax.experimental.pallas import tpu as pltpu

F32 = jnp.float32
BF16 = jnp.bfloat16

D_MODEL = 1024
D_FF = 2816
EPS = 1e-6

ATT_GROUPS = ((128, 1), (512, 4), (2048, 16))
ATT_HEADS_PER_GROUP = 4
ATT_HEAD_DIM = 64
ATT_BLOCK = 128
ATT_GROUP_WIDTH = ATT_HEADS_PER_GROUP * ATT_HEAD_DIM
ATT_QKV_WIDTH = len(ATT_GROUPS) * ATT_GROUP_WIDTH
ROPE_THETA = 10000.0

GDN_HEADS = 8
GDN_HEAD_DIM = 128
GDN_WIDTH = GDN_HEADS * GDN_HEAD_DIM
GDN_CONV = 4
GDN_CHUNK = 64

LANES = 128
SUBLANES = 8
VMEM_LIMIT_BYTES = 56 * 1024 * 1024
TOKEN_TILE = 512

W_IN_GDN = 3 * ATT_QKV_WIDTH
W_IN_BD = W_IN_GDN + 3 * GDN_WIDTH
W_IN_GATES = W_IN_BD + 2 * GDN_HEADS
FFN_CHUNKS = ((0, 768), (768, 1536), (1536, 2304), (2304, 2816))
NEG_BIG = -1e30


def _resident(shape):
    nd = len(shape)
    return pl.BlockSpec(shape, lambda *_: (0,) * nd, pipeline_mode=pl.Buffered(1))


def _rmsnorm(x, g):
    return x * lax.rsqrt(jnp.mean(x * x, axis=-1, keepdims=True) + EPS) * g


def _sigmoid(x):
    return 1.0 / (1.0 + jnp.exp(-x))


def _dot(a, b):
    return jnp.dot(a, b, preferred_element_type=F32)


def _dot_nt(a, b):
    return lax.dot_general(a, b, (((1,), (1,)), ((), ())), preferred_element_type=F32)


def _dot_tn(a, b):
    return lax.dot_general(a, b, (((0,), (0,)), ((), ())), preferred_element_type=F32)


def _swiglu_residual(x, g, wg_ref, wu_ref, wd_ref):
    h = _rmsnorm(x, g).astype(BF16)
    acc = x
    for lo, hi in FFN_CHUNKS:
        gate = _dot(h, wg_ref[:, lo:hi])
        up = _dot(h, wu_ref[:, lo:hi])
        act = (0.5 * gate * _sigmoid(gate) * up).astype(BF16)
        acc = acc + _dot(act, wd_ref[lo:hi, :])
    return acc


def _cast_rows(src_refs, dst_refs):
    for src, dst in zip(src_refs, dst_refs):
        dst[...] = src[...].astype(BF16)


def _cast_specs(arrays, layer, steps):
    in_specs, out_specs, shapes = [], [], []
    for arr in arrays:
        _, n_rows, n_cols = arr.shape
        packed_rows = 2 * SUBLANES
        parts = next(p for p in range(steps, 0, -1) if n_rows % p == 0 and (n_rows // p) % packed_rows == 0)
        in_specs.append(pl.BlockSpec((None, n_rows // parts, n_cols),
                                     lambda s, parts=parts: (layer, jnp.minimum(s, parts - 1), 0)))
        out_specs.append(pl.BlockSpec((n_rows // parts, n_cols), lambda s, parts=parts: (jnp.minimum(s, parts - 1), 0)))
        shapes.append(jax.ShapeDtypeStruct((n_rows, n_cols), BF16))
    return in_specs, out_specs, shapes


def _ffn_kernel(x_ref, g_ref, wg_ref, wu_ref, wd_ref, fin_ref, o_ref, *, final_norm):
    y = _swiglu_residual(x_ref[...], g_ref[...], wg_ref, wu_ref, wd_ref)
    if final_norm:
        y = _rmsnorm(y, fin_ref[...])
    o_ref[...] = y


def _ffn(x, norm_g, wg, wu, wd, fin_g, *, final_norm, tm):
    n = x.shape[0]
    row = pl.BlockSpec((tm, D_MODEL), lambda i: (i, 0))
    return pl.pallas_call(
        functools.partial(_ffn_kernel, final_norm=final_norm),
        grid=(n // tm,),
        in_specs=[row, _resident((1, D_MODEL)), _resident((D_MODEL, D_FF)), _resident((D_MODEL, D_FF)),
                  _resident((D_FF, D_MODEL)), _resident((1, D_MODEL))],
        out_specs=row,
        out_shape=jax.ShapeDtypeStruct((n, D_MODEL), F32),
        compiler_params=pltpu.CompilerParams(dimension_semantics=("arbitrary",),
                                             vmem_limit_bytes=VMEM_LIMIT_BYTES),
        name="ffn_final" if final_norm else "ffn",
    )(x, norm_g, wg, wu, wd, fin_g)


MIX_BLOCK = 256
QKV_BLOCK = 512


def _interleave(primary, n_primary, secondary, n_secondary):
    done = 0
    for i in range(n_primary):
        next(primary)
        while done * n_primary < (i + 1) * n_secondary:
            next(secondary)
            done += 1
    for gen in (primary, secondary):
        for _ in gen:
            pass


def _chunk_time(row):
    return SUBLANES * (row % SUBLANES) + row // SUBLANES


def _attention_qkv_stages(h, wm_ref, cos_ref, sin_ref, att_refs, stage_ref, *, tm):
    cos = cos_ref[...]
    sin = sin_ref[...]
    lane = lax.broadcasted_iota(jnp.int32, (1, LANES), 1)
    first_half = (lane % ATT_HEAD_DIM) < (ATT_HEAD_DIM // 2)
    slot = 0
    for part in range(3):
        for gi, (_, dil) in enumerate(ATT_GROUPS):
            col = part * ATT_QKV_WIDTH + gi * ATT_GROUP_WIDTH
            y = _dot(h, wm_ref[:, col:col + ATT_GROUP_WIDTH])
            yield
            for j in range(ATT_GROUP_WIDTH // LANES):
                blk = y[:, j * LANES:(j + 1) * LANES]
                if part < 2:
                    swapped = jnp.where(first_half, pltpu.roll(blk, LANES - ATT_HEAD_DIM // 2, 1),
                                        pltpu.roll(blk, ATT_HEAD_DIM // 2, 1))
                    blk = blk * cos + swapped * sin
                if part == 0:
                    blk = blk * (ATT_HEAD_DIM ** -0.5)
                dst = part * ATT_GROUP_WIDTH + j * LANES
                if dil == 1:
                    att_refs[gi][:, dst:dst + LANES] = blk.astype(BF16)
                else:
                    stage_ref[slot] = blk
                    for r in range(dil):
                        rows = stage_ref[slot, pl.ds(r, tm // dil, stride=dil), :]
                        lo = r * 3 * ATT_GROUP_WIDTH + dst
                        att_refs[gi][:, lo:lo + LANES] = rows.astype(BF16)
                    slot += 1
                yield


ATT_QKV_STAGES = 3 * len(ATT_GROUPS) * (1 + ATT_GROUP_WIDTH // LANES)
MIXER_IN_STAGED = 3 * (len(ATT_GROUPS) - 1) * (ATT_GROUP_WIDTH // LANES)


def _deltanet_qkv_stages(h_perm, wm_ref, wbd_ref, convw_ref, alog_ref, dtb_ref, gdn_refs, bd_ref, carry_ref, *, tm):
    lane = lax.broadcasted_iota(jnp.int32, (1, LANES), 1)
    raw = _dot(h_perm, wbd_ref[...])
    yield
    z = raw + dtb_ref[...]
    softplus = jnp.maximum(z, 0.0) + jnp.log1p(jnp.exp(-jnp.abs(z)))
    g = -jnp.exp(alog_ref[...]) * softplus
    bd_ref[...] = jnp.where(lane < GDN_HEADS, _sigmoid(raw), jnp.where(lane < 2 * GDN_HEADS, g, 0.0))

    vregs = GDN_CHUNK // SUBLANES
    halo = GDN_CONV - 1
    chunks = tm // GDN_CHUNK
    last_sublane = lax.broadcasted_iota(jnp.int32, (SUBLANES, GDN_HEAD_DIM), 0) == SUBLANES - 1
    heads_per_block = QKV_BLOCK // GDN_HEAD_DIM
    for part in range(3):
        for blk in range(GDN_WIDTH // QKV_BLOCK):
            base = part * GDN_WIDTH + blk * QKV_BLOCK
            y = _dot(h_perm, wm_ref[:, W_IN_GDN + base:W_IN_GDN + base + QKV_BLOCK])
            yield
            for hb in range(heads_per_block):
                hh = blk * heads_per_block + hb
                col = part * GDN_WIDTH + hh * GDN_HEAD_DIM
                cur = y[:, hb * GDN_HEAD_DIM:(hb + 1) * GDN_HEAD_DIM]
                vreg = lambda c, v: cur[c * GDN_CHUNK + v * SUBLANES:c * GDN_CHUNK + (v + 1) * SUBLANES]
                prev_tile = carry_ref[:, col:col + GDN_HEAD_DIM]
                carry_ref[:, col:col + GDN_HEAD_DIM] = cur[tm - halo * SUBLANES:, :]
                wrapped = []
                for c in range(chunks):
                    row = []
                    for i in range(halo):
                        before = (prev_tile[i * SUBLANES:(i + 1) * SUBLANES] if c == 0
                                  else vreg(c - 1, vregs - halo + i))
                        own = vreg(c, vregs - halo + i)
                        row.append(pltpu.roll(jnp.where(last_sublane, before, own), 1, 0))
                    wrapped.append(row)
                acc = cur * convw_ref[halo:halo + 1, col:col + GDN_HEAD_DIM]
                for shift in range(1, GDN_CONV):
                    pieces = []
                    for c in range(chunks):
                        pieces += wrapped[c][halo - shift:]
                        pieces.append(cur[c * GDN_CHUNK:(c + 1) * GDN_CHUNK - shift * SUBLANES])
                    shifted = jnp.concatenate(pieces, axis=0)
                    acc = acc + shifted * convw_ref[halo - shift:halo - shift + 1, col:col + GDN_HEAD_DIM]
                act = acc * _sigmoid(acc)
                if part < 2:
                    act = act * lax.rsqrt(jnp.sum(act * act, axis=-1, keepdims=True) + EPS)
                if part == 0:
                    act = act * (GDN_HEAD_DIM ** -0.5)
                gdn_refs[part][:, hh * GDN_HEAD_DIM:(hh + 1) * GDN_HEAD_DIM] = act
                yield


DELTANET_QKV_STAGES = 1 + 3 * (GDN_WIDTH // QKV_BLOCK) + 3 * GDN_HEADS


def _mixer_in_kernel(x_ref, g_ref, wm_ref, wbd_ref, convw_ref, alog_ref, dtb_ref, cos_ref, sin_ref,
                     a0_ref, a1_ref, a2_ref, qb_ref, kb_ref, vb_ref, bd_ref, carry_ref, perm_ref, stage_ref, *, tm):
    @pl.when(pl.program_id(1) == 0)
    def _():
        carry_ref[...] = jnp.zeros(carry_ref.shape, F32)

    hf = _rmsnorm(x_ref[...], g_ref[...])
    h = hf.astype(BF16)
    vregs = GDN_CHUNK // SUBLANES
    for cb in range(D_MODEL // LANES):
        perm_ref[cb] = hf[:, cb * LANES:(cb + 1) * LANES]
    h_perm = jnp.concatenate(
        [jnp.concatenate([perm_ref[cb, pl.ds(c0 + v, SUBLANES, stride=vregs), :]
                          for c0 in range(0, tm, GDN_CHUNK) for v in range(vregs)], axis=0)
         for cb in range(D_MODEL // LANES)], axis=1).astype(BF16)
    gdn = _deltanet_qkv_stages(h_perm, wm_ref, wbd_ref, convw_ref, alog_ref, dtb_ref, (qb_ref, kb_ref, vb_ref), bd_ref,
                               carry_ref, tm=tm)
    att = _attention_qkv_stages(h, wm_ref, cos_ref, sin_ref, (a0_ref, a1_ref, a2_ref), stage_ref, tm=tm)
    _interleave(gdn, DELTANET_QKV_STAGES, att, ATT_QKV_STAGES)


def _mixer_in(x1, norm_g, w_in, conv_w, a_log, dt_bias, cos_t, sin_t, *, tm, seq):
    n = x1.shape[0]
    tiles_per_seq = seq // tm
    tile = lambda rows, w: pl.BlockSpec((rows, w), lambda bi, i: (bi * tiles_per_seq + i, 0))
    table = pl.BlockSpec((tm, LANES), lambda bi, i: (i, 0))
    wq = 3 * ATT_GROUP_WIDTH
    att_specs = [tile(tm // dil, dil * wq) for _, dil in ATT_GROUPS]
    att_shapes = [jax.ShapeDtypeStruct((n // dil, dil * wq), BF16) for _, dil in ATT_GROUPS]
    gdn = jax.ShapeDtypeStruct((n, GDN_WIDTH), F32)
    return pl.pallas_call(
        functools.partial(_mixer_in_kernel, tm=tm),
        grid=(n // seq, tiles_per_seq),
        in_specs=[tile(tm, D_MODEL), _resident((1, D_MODEL)),
                  pl.BlockSpec((D_MODEL, W_IN_BD), lambda bi, i: (0, 0), pipeline_mode=pl.Buffered(1)),
                  pl.BlockSpec((D_MODEL, LANES), lambda bi, i: (0, W_IN_BD // LANES), pipeline_mode=pl.Buffered(1)),
                  _resident((GDN_CONV, 3 * GDN_WIDTH)), _resident((1, LANES)), _resident((1, LANES)), table, table],
        out_specs=att_specs + [tile(tm, GDN_WIDTH)] * 3 + [tile(tm, LANES)],
        out_shape=att_shapes + [gdn] * 3 + [jax.ShapeDtypeStruct((n, LANES), F32)],
        scratch_shapes=[pltpu.VMEM(((GDN_CONV - 1) * SUBLANES, 3 * GDN_WIDTH), F32),
                        pltpu.VMEM((D_MODEL // LANES, tm, LANES), F32), pltpu.VMEM((MIXER_IN_STAGED, tm, LANES), F32)],
        compiler_params=pltpu.CompilerParams(dimension_semantics=("arbitrary", "arbitrary"),
                                             vmem_limit_bytes=VMEM_LIMIT_BYTES),
        name="mixer_in",
    )(x1, norm_g, w_in, w_in, conv_w, a_log, dt_bias, cos_t, sin_t)


ATT_BATCH = 3


def _attention_blocks(items):
    heads = range(ATT_HEADS_PER_GROUP)
    per_half = LANES // ATT_HEAD_DIM
    halves = range(ATT_GROUP_WIDTH // LANES)
    lane = lax.broadcasted_iota(jnp.int32, (1, LANES), 1)
    in_head = [(lane // ATT_HEAD_DIM) == j for j in range(per_half)]
    keep = [jnp.where(in_head[j], 1.0, 0.0).astype(BF16) for j in range(per_half)]
    nq = ATT_BLOCK
    cut = lambda x, i: x[:, i * LANES:(i + 1) * LANES]
    s_all = [[_dot_nt(jnp.concatenate([cut(q, i) * keep[j] for j in range(per_half)], axis=0), cut(k, i))
              for i in halves] for q, k, _, _ in items]
    stats, p_all = [], []
    for (_, _, _, valid), sa in zip(items, s_all):
        s = [jnp.where(valid, sa[hh // per_half][(hh % per_half) * nq:(hh % per_half + 1) * nq], NEG_BIG)
             for hh in heads]
        m = [jnp.max(s[hh], axis=-1, keepdims=True) for hh in heads]
        p = [jnp.exp(s[hh] - m[hh]) for hh in heads]
        l = [jnp.sum(p[hh], axis=-1, keepdims=True) for hh in heads]
        stats.append((m, l))
        p_all.append([jnp.concatenate([p[i * per_half + j].astype(BF16) for j in range(per_half)], axis=0)
                      for i in halves])
    pv_all = [[_dot(ps[i], cut(v, i)) for i in halves] for ps, (_, _, v, _) in zip(p_all, items)]
    outs = []
    for pv, (m, l) in zip(pv_all, stats):
        o_halves, lse_halves = [], []
        for i in halves:
            o = jnp.zeros((ATT_BLOCK, LANES), F32)
            lse = jnp.zeros((ATT_BLOCK, LANES), F32)
            for j in range(per_half):
                hh = i * per_half + j
                o = jnp.where(in_head[j], pv[i][j * nq:(j + 1) * nq] * (1.0 / l[hh]), o)
                lse = jnp.where(in_head[j], m[hh] + jnp.log(l[hh]), lse)
            o_halves.append(o)
            lse_halves.append(lse)
        outs.append((jnp.concatenate(o_halves, axis=1), jnp.concatenate(lse_halves, axis=1)))
    return outs


def _attention_kernel(a0_ref, a1_ref, a2_ref, ya_ref, o0, l0, o1, l1, o2, l2, *, seq):
    in_refs = (a0_ref, a1_ref, a2_ref)
    o_refs = (o0, o1, o2)
    l_refs = (l0, l1, l2)
    qi = lax.broadcasted_iota(jnp.int32, (ATT_BLOCK, ATT_BLOCK), 0)
    kj = lax.broadcasted_iota(jnp.int32, (ATT_BLOCK, ATT_BLOCK), 1)
    causal = kj <= qi
    qi2 = lax.broadcasted_iota(jnp.int32, (ATT_BLOCK, 2 * ATT_BLOCK), 0)
    kj2 = lax.broadcasted_iota(jnp.int32, (ATT_BLOCK, 2 * ATT_BLOCK), 1)
    band = (kj2 >= qi2) & (kj2 - ATT_BLOCK <= qi2)
    wq = 3 * ATT_GROUP_WIDTH

    def load(gi, r, n):
        src, base = in_refs[gi], r * wq
        if isinstance(n, int) and n == 0:
            qrows = krows = slice(0, ATT_BLOCK)
            valid = causal
        else:
            start = lambda x: x if isinstance(x, int) else pl.multiple_of(x, ATT_BLOCK)
            qrows = pl.ds(start(n * ATT_BLOCK), ATT_BLOCK)
            krows = pl.ds(start((n - 1) * ATT_BLOCK), 2 * ATT_BLOCK)
            valid = band
        return (src[0, qrows, base:base + ATT_GROUP_WIDTH],
                src[0, krows, base + ATT_GROUP_WIDTH:base + 2 * ATT_GROUP_WIDTH],
                src[0, krows, base + 2 * ATT_GROUP_WIDTH:base + 3 * ATT_GROUP_WIDTH], valid)

    def store(gi, r, n, o, lse):
        dil = ATT_GROUPS[gi][1]
        if dil == 1:
            first = n * ATT_BLOCK
            rows = pl.ds(first if isinstance(first, int) else pl.multiple_of(first, ATT_BLOCK), ATT_BLOCK)
        else:
            rows = pl.ds(n * ATT_BLOCK * dil + r, ATT_BLOCK, stride=dil)
        for half in range(ATT_GROUP_WIDTH // LANES):
            o_refs[gi][half, rows, :] = o[:, half * LANES:(half + 1) * LANES]
            l_refs[gi][half, rows, :] = lse[:, half * LANES:(half + 1) * LANES]

    def run(blocks):
        for (gi, r, n), (o, lse) in zip(blocks, _attention_blocks([load(*blk) for blk in blocks])):
            store(gi, r, n, o, lse)

    static_blocks = []
    looped = None
    for gi, (window, dil) in enumerate(ATT_GROUPS):
        assert window // dil == ATT_BLOCK
        nblk = seq // dil // ATT_BLOCK
        if dil == 1 and (nblk - 1) % ATT_BATCH == 0:
            static_blocks.append((gi, 0, 0))
            looped = (gi, nblk)
        else:
            static_blocks += [(gi, r, n) for r in range(dil) for n in range(nblk)]
    for i in range(0, len(static_blocks), ATT_BATCH + 1):
        run(static_blocks[i:i + ATT_BATCH + 1])
    if looped is not None:
        gi, nblk = looped

        def body(i, carry):
            run([(gi, 0, 1 + i * ATT_BATCH + j) for j in range(ATT_BATCH)])
            return carry
        lax.fori_loop(0, (nblk - 1) // ATT_BATCH, body, 0)

    rows_per_step = 256

    def merge(i, carry):
        rows = pl.ds(pl.multiple_of(i * rows_per_step, rows_per_step), rows_per_step)
        for half in range(ATT_GROUP_WIDTH // LANES):
            la, lb, lc = l0[half, rows, :], l1[half, rows, :], l2[half, rows, :]
            m = jnp.maximum(jnp.maximum(la, lb), lc)
            ea, eb, ec = jnp.exp(la - m), jnp.exp(lb - m), jnp.exp(lc - m)
            num = ea * o0[half, rows, :] + eb * o1[half, rows, :] + ec * o2[half, rows, :]
            ya_ref[0, rows, half * LANES:(half + 1) * LANES] = num / (ea + eb + ec)
        return carry
    lax.fori_loop(0, seq // rows_per_step, merge, 0)


def _attention(a0, a1, a2, *, batch):
    views = tuple(a.reshape(batch, a.shape[0] // batch, a.shape[1]) for a in (a0, a1, a2))
    b, s, _ = views[0].shape
    specs = [pl.BlockSpec((1,) + arr.shape[1:], lambda bi: (bi, 0, 0)) for arr in views]
    scratch = [pltpu.VMEM((ATT_GROUP_WIDTH // LANES, s, LANES), F32) for _ in range(6)]
    return pl.pallas_call(
        functools.partial(_attention_kernel, seq=s),
        grid=(b,),
        in_specs=specs,
        out_specs=pl.BlockSpec((1, s, ATT_GROUP_WIDTH), lambda bi: (bi, 0, 0)),
        out_shape=jax.ShapeDtypeStruct((b, s, ATT_GROUP_WIDTH), F32),
        scratch_shapes=scratch,
        compiler_params=pltpu.CompilerParams(dimension_semantics=("arbitrary",),
                                             vmem_limit_bytes=VMEM_LIMIT_BYTES),
        name="dilated_attention",
    )(*views)


def _deltanet_stages(q_ref, k_ref, v_ref, bd_ref, gnorm, state_ref, ob_ref, slot, *, tile):
    c = GDN_CHUNK
    d = GDN_HEAD_DIM
    heads = range(GDN_HEADS)
    pairs = [(2 * pp, 2 * pp + 1) for pp in range(GDN_HEADS // 2)]
    cols = [slice(hh * d, (hh + 1) * d) for hh in heads]
    glane = [GDN_HEADS + hh for hh in heads]
    ii = _chunk_time(lax.broadcasted_iota(jnp.int32, (c, 2 * c), 0))
    ll = lax.broadcasted_iota(jnp.int32, (c, 2 * c), 1)
    jj = _chunk_time(ll % c)
    lower = ii >= jj
    strict = ii > jj
    left = ll < c
    left_row = lax.broadcasted_iota(jnp.int32, (1, 2 * c), 1) < c
    keep_left = jnp.where(left, 1.0, 0.0).astype(BF16)
    keep_right = jnp.where(left, 0.0, 1.0).astype(BF16)
    ti = _chunk_time(lax.broadcasted_iota(jnp.int32, (c, c), 0))
    tj = _chunk_time(lax.broadcasted_iota(jnp.int32, (c, c), 1))
    tri_ones = jnp.where(ti >= tj, 1.0, 0.0).astype(BF16)

    def blockdiag(x):
        return jnp.concatenate([x * keep_left, x * keep_right], axis=0)

    def stack_diag(xa, xb):
        zero = jnp.zeros_like(xa)
        return jnp.concatenate([jnp.concatenate([xa, zero], axis=1), jnp.concatenate([zero, xb], axis=1)], axis=0)

    def prepare(ci, out):
        rows = slice(ci * c, (ci + 1) * c)
        bd = bd_ref[rows, :]
        bd_hi = bd.astype(BF16)
        bd_rest = bd - bd_hi.astype(F32)
        bd_mid = bd_rest.astype(BF16)
        bd_lo = (bd_rest - bd_mid.astype(F32)).astype(BF16)
        gcum = _dot(tri_ones, bd_hi) + _dot(tri_ones, bd_mid) + _dot(tri_ones, bd_lo)
        yield
        gtot = jnp.broadcast_to(gcum[c - 1:c, :], (c, LANES))
        gcum_t = jnp.concatenate([gcum, gcum], axis=0).T
        e_cum_all = jnp.exp(gcum)
        e_rest_all = jnp.exp(gtot - gcum)
        e_tot_all = jnp.exp(gtot)
        q = [q_ref[rows, cols[hh]] for hh in heads]
        k = [k_ref[rows, cols[hh]] for hh in heads]
        v = [v_ref[rows, cols[hh]] for hh in heads]
        beta = [bd[:, hh:hh + 1] for hh in heads]
        e_cum = [e_cum_all[:, gl:gl + 1] for gl in glane]
        kbeta = [k[hh] * beta[hh] for hh in heads]
        kq = [_dot_nt(jnp.concatenate([jnp.concatenate([kbeta[a], kbeta[b]], axis=1),
                                       jnp.concatenate([q[a], q[b]], axis=1)], axis=0).astype(BF16),
                      stack_diag(k[a].astype(BF16), k[b].astype(BF16)))
              for a, b in pairs]
        yield
        decay = [jnp.exp(jnp.where(lower,
                                   jnp.where(left, gcum[:, glane[a]:glane[a] + 1], gcum[:, glane[b]:glane[b] + 1])
                                   - jnp.where(left_row, gcum_t[glane[a]:glane[a] + 1, :], gcum_t[glane[b]:glane[b] + 1, :]),
                                   NEG_BIG)) for a, b in pairs]
        m = [jnp.where(strict, kq[pp][0:c] * decay[pp], 0.0) for pp in range(len(pairs))]
        n = [-mm for mm in m]
        pb = [mm.astype(BF16) for mm in m]
        p = [_dot(x, blockdiag(x)) for x in pb]
        yield
        rounds = 5
        for r in range(rounds):
            pb = [x.astype(BF16) for x in p]
            upd = [_dot(x, blockdiag(y.astype(BF16))) for x, y in zip(pb, n)]
            p_next = [_dot(x, blockdiag(x)) for x in pb] if r + 1 < rounds else None
            yield
            n = [y + x + u for y, x, u in zip(n, p, upd)]
            p = p_next
        rhs = [jnp.concatenate([v[hh] * beta[hh], kbeta[hh] * e_cum[hh]], axis=1) for hh in heads]
        nr = [_dot(n[pp].astype(BF16), stack_diag(rhs[a].astype(BF16), rhs[b].astype(BF16)))
              for pp, (a, b) in enumerate(pairs)]
        yield
        sol = [rhs[hh] + nr[hh // 2][:, (hh % 2) * 2 * d:(hh % 2 + 1) * 2 * d] for hh in heads]
        out.update(
            first=ci * c,
            u=[sol[hh][:, 0:d] for hh in heads],
            wq=[jnp.concatenate([sol[hh][:, d:2 * d], q[hh] * e_cum[hh]], axis=0).astype(BF16) for hh in heads],
            a_qk=[(kq[pp][c:2 * c] * decay[pp]).astype(BF16) for pp in range(len(pairs))],
            k_dec=[(k[hh] * e_rest_all[:, gl:gl + 1]).astype(BF16) for hh, gl in zip(heads, glane)],
            e_tot=[e_tot_all[0:1, gl:gl + 1] for gl in glane])

    for first in range(0, tile // c, GDN_GROUP):
        group = [dict() for _ in range(GDN_GROUP)]
        gens = [prepare(first + gi, group[gi]) for gi in range(GDN_GROUP)]
        for _ in range(GDN_PREP_LAYERS):
            for gen in gens:
                next(gen)
            yield
        for gen in gens:
            for _ in gen:
                pass
        for pre in group:
            state = [state_ref[hh] for hh in heads]
            ws = [_dot(pre["wq"][hh], state[hh].astype(BF16)) for hh in heads]
            yield
            v_new = [(pre["u"][hh] - ws[hh][0:c]).astype(BF16) for hh in heads]
            kv = [_dot_tn(pre["k_dec"][hh], v_new[hh]) for hh in heads]
            av = [_dot(pre["a_qk"][pp], stack_diag(v_new[a], v_new[b])) for pp, (a, b) in enumerate(pairs)]
            yield
            for hh in heads:
                state_ref[hh] = state[hh] * pre["e_tot"][hh] + kv[hh]
            for hh in heads:
                o = _rmsnorm(ws[hh][c:2 * c] + av[hh // 2][:, (hh % 2) * d:(hh % 2 + 1) * d], gnorm)
                for vv in range(c // SUBLANES):
                    ob_ref[slot, hh, pl.ds(pre["first"] + vv, SUBLANES, stride=c // SUBLANES), :] = (
                        o[vv * SUBLANES:(vv + 1) * SUBLANES])


GDN_GROUP = 4
GDN_PREP_LAYERS = 9
GDN_LAYERS_PER_GROUP = GDN_PREP_LAYERS + 2 * GDN_GROUP


def _mixer_out_stages(x_ref, ya_ref, ob_ref, slot, g_ref, wgt_ref, wa_ref, wb_ref, wo_ref, o_ref):
    x = x_ref[...]
    h = _rmsnorm(x, g_ref[...]).astype(BF16)
    ya = ya_ref[...].astype(BF16)
    blocks = [slice(j * MIX_BLOCK, (j + 1) * MIX_BLOCK) for j in range(D_MODEL // MIX_BLOCK)]
    gate_cols = lambda which, blk: slice(which * D_MODEL + blk.start, which * D_MODEL + blk.stop)
    yb = []
    for blk in blocks:
        gdn_gate = _dot(h, wgt_ref[:, gate_cols(0, blk)])
        yield
        ob = jnp.concatenate([ob_ref[slot, hh] for hh in range(blk.start // GDN_HEAD_DIM, blk.stop // GDN_HEAD_DIM)],
                             axis=1)
        yb.append((ob * (gdn_gate * _sigmoid(gdn_gate))).astype(BF16))
    yb = jnp.concatenate(yb, axis=1)
    merged = []
    for blk in blocks:
        gate_a = _dot(h, wgt_ref[:, gate_cols(1, blk)])
        branch_a = _dot(ya, wa_ref[:, blk])
        yield
        gate_b = _dot(h, wgt_ref[:, gate_cols(2, blk)])
        yield
        branch_b = _dot(yb, wb_ref[:, blk])
        yield
        merged.append((_sigmoid(gate_a) * branch_a + _sigmoid(gate_b) * branch_b).astype(BF16))
    merged = jnp.concatenate(merged, axis=1)
    for blk in blocks:
        o_ref[:, blk] = x[:, blk] + _dot(merged, wo_ref[:, blk])
        yield


MIX_GRANULES = 5 * (D_MODEL // MIX_BLOCK)


def _mixer_tail_kernel(q_ref, k_ref, v_ref, bd_ref, gn_ref, x_ref, ya_ref, g_ref, wgt_ref, wa_ref, wb_ref, wo_ref,
                       *refs, tile, tiles_per_seq, n_tiles):
    n_cast = (len(refs) - 3) // 2
    o_ref, state_ref, ob_ref = refs[n_cast], refs[2 * n_cast + 1], refs[2 * n_cast + 2]
    step = pl.program_id(0)

    @pl.when(step == 0)
    def _():
        ob_ref[...] = jnp.zeros(ob_ref.shape, F32)

    @pl.when(jnp.minimum(step, n_tiles - 1) % tiles_per_seq == 0)
    def _():
        state_ref[...] = jnp.zeros(state_ref.shape, F32)

    slot = step % 2
    gdn = _deltanet_stages(q_ref, k_ref, v_ref, bd_ref, gn_ref[...], state_ref, ob_ref, slot, tile=tile)
    mix = _mixer_out_stages(x_ref, ya_ref, ob_ref, 1 - slot, g_ref, wgt_ref, wa_ref, wb_ref, wo_ref, o_ref)
    _interleave(gdn, GDN_LAYERS_PER_GROUP * (tile // (GDN_CHUNK * GDN_GROUP)), mix, MIX_GRANULES)
    _cast_rows(refs[:n_cast], refs[n_cast + 1:2 * n_cast + 1])


def _mixer_tail(qb, kb, vb, bd, out_norm, x1, ya, norm_g, w_gates, w_a, w_b, w_o, *, tile, seq, cast=(), layer=0):
    n = x1.shape[0]
    n_tiles = n // tile
    cast_in, cast_out, cast_shapes = _cast_specs(cast, layer, n_tiles)
    cur = lambda w: pl.BlockSpec((tile, w), lambda s: (jnp.minimum(s, n_tiles - 1), 0))
    prev = lambda w: pl.BlockSpec((tile, w), lambda s: (jnp.maximum(s - 1, 0), 0))
    out = pl.pallas_call(
        functools.partial(_mixer_tail_kernel, tile=tile, tiles_per_seq=seq // tile, n_tiles=n_tiles),
        grid=(n_tiles + 1,),
        in_specs=[cur(GDN_WIDTH), cur(GDN_WIDTH), cur(GDN_WIDTH), cur(LANES), _resident((1, GDN_HEAD_DIM)),
                  prev(D_MODEL), prev(ATT_GROUP_WIDTH), _resident((1, D_MODEL)),
                  _resident(w_gates.shape), _resident(w_a.shape), _resident(w_b.shape), _resident(w_o.shape)]
                 + cast_in,
        out_specs=[prev(D_MODEL)] + cast_out,
        out_shape=[jax.ShapeDtypeStruct((n, D_MODEL), F32)] + cast_shapes,
        scratch_shapes=[pltpu.VMEM((GDN_HEADS, GDN_HEAD_DIM, GDN_HEAD_DIM), F32),
                        pltpu.VMEM((2, GDN_HEADS, tile, GDN_HEAD_DIM), F32)],
        compiler_params=pltpu.CompilerParams(dimension_semantics=("arbitrary",),
                                             vmem_limit_bytes=VMEM_LIMIT_BYTES),
        name="deltanet_mixer_out",
    )(qb, kb, vb, bd, out_norm, x1, ya, norm_g, w_gates, w_a, w_b, w_o, *cast)
    return out[0], out[1:]


def _rope_tables(seq):
    half = ATT_HEAD_DIM // 2
    inv_freq = ROPE_THETA ** (-jnp.arange(half, dtype=F32) / half)
    ang = jnp.arange(seq, dtype=F32)[:, None] * inv_freq[None, :]
    cos, sin = jnp.cos(ang), jnp.sin(ang)
    reps = LANES // ATT_HEAD_DIM
    return jnp.tile(jnp.concatenate([cos, cos], axis=-1), (1, reps)), jnp.tile(jnp.concatenate([-sin, sin], axis=-1), (1, reps))


def _pad_lanes(row, offset):
    return jnp.zeros((1, LANES), F32).at[0, offset:offset + row.shape[0]].set(row.astype(F32))


def _layer(x, ffn1_norm, ffn1_w_gate, ffn1_w_up, ffn1_w_down, mix_norm, w_in_all, gdn_conv_w, gdn_a_log, gdn_dt_bias,
           gdn_out_norm, w_branch_a, w_branch_b, w_out, ffn2_norm, ffn2_all, fin_g,
           *, layer, final_norm, tm_ffn, tm_mix, gdn_tile):
    b, s, _ = x.shape
    n = b * s
    row = lambda v: v.reshape(1, -1).astype(F32)
    x1 = _ffn(x.reshape(n, D_MODEL), row(ffn1_norm), ffn1_w_gate.astype(BF16), ffn1_w_up.astype(BF16),
              ffn1_w_down.astype(BF16), fin_g, final_norm=False, tm=tm_ffn)
    w_in = w_in_all[layer].astype(BF16)
    w_gates = w_in[:, W_IN_GATES:]
    cos_t, sin_t = _rope_tables(s)
    a0, a1, a2, qb, kb, vb, bd = _mixer_in(
        x1, row(mix_norm), w_in, gdn_conv_w.astype(F32), _pad_lanes(gdn_a_log, GDN_HEADS),
        _pad_lanes(gdn_dt_bias, GDN_HEADS), cos_t, sin_t, tm=tm_mix, seq=s)

    ya = _attention(a0, a1, a2, batch=b)
    x2, ffn2_w = _mixer_tail(qb, kb, vb, bd, row(gdn_out_norm), x1, ya.reshape(n, ATT_GROUP_WIDTH), row(mix_norm),
                             w_gates, w_branch_a.astype(BF16), w_branch_b.astype(BF16), w_out.astype(BF16),
                             tile=gdn_tile, seq=s, cast=ffn2_all, layer=layer)
    x3 = _ffn(x2, row(ffn2_norm), *ffn2_w, fin_g, final_norm=final_norm, tm=tm_ffn)
    return x3.reshape(b, s, D_MODEL)


def kernel(x, ffn1_norm, ffn1_w_gate, ffn1_w_up, ffn1_w_down, mix_norm, w_in, gdn_conv_w, gdn_a_log, gdn_dt_bias,
           gdn_out_norm, w_branch_a, w_branch_b, w_out, ffn2_norm, ffn2_w_gate, ffn2_w_up, ffn2_w_down, final_norm):
    depth = ffn1_norm.shape[0]
    fin_g = final_norm.reshape(1, -1).astype(F32)
    for layer in range(depth):
        x = _layer(x, ffn1_norm[layer], ffn1_w_gate[layer], ffn1_w_up[layer], ffn1_w_down[layer], mix_norm[layer],
                   w_in, gdn_conv_w[layer], gdn_a_log[layer], gdn_dt_bias[layer], gdn_out_norm[layer],
                   w_branch_a[layer], w_branch_b[layer], w_out[layer], ffn2_norm[layer],
                   (ffn2_w_gate, ffn2_w_up, ffn2_w_down), fin_g, layer=layer, final_norm=(layer == depth - 1),
                   tm_ffn=TOKEN_TILE, tm_mix=TOKEN_TILE, gdn_tile=TOKEN_TILE)
    return x
```

```python
import functools

import jax
import jax.numpy as jnp
from jax import lax
from jax.experimental import pallas as pl
from jax.experimental.pallas import tpu as pltpu

F32 = jnp.float32
BF16 = jnp.bfloat16

D_MODEL = 1024
D_FF = 2816
EPS = 1e-6

ATT_GROUPS = ((128, 1), (512, 4), (2048, 16))
ATT_HEADS_PER_GROUP = 4
ATT_HEAD_DIM = 64
ATT_BLOCK = 128
ATT_GROUP_WIDTH = ATT_HEADS_PER_GROUP * ATT_HEAD_DIM
ATT_QKV_WIDTH = len(ATT_GROUPS) * ATT_GROUP_WIDTH
ROPE_THETA = 10000.0

GDN_HEADS = 8
GDN_HEAD_DIM = 128
GDN_WIDTH = GDN_HEADS * GDN_HEAD_DIM
GDN_CONV = 4
GDN_CHUNK = 64

LANES = 128
SUBLANES = 8
VMEM_LIMIT_BYTES = 56 * 1024 * 1024
TOKEN_TILE = 512

W_IN_GDN = 3 * ATT_QKV_WIDTH
W_IN_BD = W_IN_GDN + 3 * GDN_WIDTH
W_IN_GATES = W_IN_BD + 2 * GDN_HEADS
FFN_CHUNKS = ((0, 768), (768, 1536), (1536, 2304), (2304, 2816))
NEG_BIG = -1e30


def _resident(shape):
    nd = len(shape)
    return pl.BlockSpec(shape, lambda *_: (0,) * nd, pipeline_mode=pl.Buffered(1))


def _rmsnorm(x, g):
    return x * lax.rsqrt(jnp.mean(x * x, axis=-1, keepdims=True) + EPS) * g


def _sigmoid(x):
    return 1.0 / (1.0 + jnp.exp(-x))


def _dot(a, b):
    return jnp.dot(a, b, preferred_element_type=F32)


def _dot_nt(a, b):
    return lax.dot_general(a, b, (((1,), (1,)), ((), ())), preferred_element_type=F32)


def _dot_tn(a, b):
    return lax.dot_general(a, b, (((0,), (0,)), ((), ())), preferred_element_type=F32)


def _swiglu_residual(x, g, wg_ref, wu_ref, wd_ref):
    h = _rmsnorm(x, g).astype(BF16)
    acc = x
    for lo, hi in FFN_CHUNKS:
        gate = _dot(h, wg_ref[:, lo:hi])
        up = _dot(h, wu_ref[:, lo:hi])
        act = (0.5 * gate * _sigmoid(gate) * up).astype(BF16)
        acc = acc + _dot(act, wd_ref[lo:hi, :])
    return acc


def _cast_rows(src_refs, dst_refs):
    for src, dst in zip(src_refs, dst_refs):
        dst[...] = src[...].astype(BF16)


def _cast_specs(arrays, layer, steps):
    in_specs, out_specs, shapes = [], [], []
    for arr in arrays:
        _, n_rows, n_cols = arr.shape
        packed_rows = 2 * SUBLANES
        parts = next(p for p in range(steps, 0, -1) if n_rows % p == 0 and (n_rows // p) % packed_rows == 0)
        in_specs.append(pl.BlockSpec((None, n_rows // parts, n_cols),
                                     lambda s, parts=parts: (layer, jnp.minimum(s, parts - 1), 0)))
        out_specs.append(pl.BlockSpec((n_rows // parts, n_cols), lambda s, parts=parts: (jnp.minimum(s, parts - 1), 0)))
        shapes.append(jax.ShapeDtypeStruct((n_rows, n_cols), BF16))
    return in_specs, out_specs, shapes


def _ffn_kernel(x_ref, g_ref, wg_ref, wu_ref, wd_ref, fin_ref, o_ref, *, final_norm):
    y = _swiglu_residual(x_ref[...], g_ref[...], wg_ref, wu_ref, wd_ref)
    if final_norm:
        y = _rmsnorm(y, fin_ref[...])
    o_ref[...] = y


def _ffn(x, norm_g, wg, wu, wd, fin_g, *, final_norm, tm):
    n = x.shape[0]
    row = pl.BlockSpec((tm, D_MODEL), lambda i: (i, 0))
    return pl.pallas_call(
        functools.partial(_ffn_kernel, final_norm=final_norm),
        grid=(n // tm,),
        in_specs=[row, _resident((1, D_MODEL)), _resident((D_MODEL, D_FF)), _resident((D_MODEL, D_FF)),
                  _resident((D_FF, D_MODEL)), _resident((1, D_MODEL))],
        out_specs=row,
        out_shape=jax.ShapeDtypeStruct((n, D_MODEL), F32),
        compiler_params=pltpu.CompilerParams(dimension_semantics=("arbitrary",),
                                             vmem_limit_bytes=VMEM_LIMIT_BYTES),
        name="ffn_final" if final_norm else "ffn",
    )(x, norm_g, wg, wu, wd, fin_g)


MIX_BLOCK = 256
QKV_BLOCK = 512


def _interleave(primary, n_primary, secondary, n_secondary):
    done = 0
    for i in range(n_primary):
        next(primary)
        while done * n_primary < (i + 1) * n_secondary:
            next(secondary)
            done += 1
    for gen in (primary, secondary):
        for _ in gen:
            pass


def _chunk_time(row):
    return SUBLANES * (row % SUBLANES) + row // SUBLANES


def _attention_qkv_stages(h, wm_ref, cos_ref, sin_ref, att_refs, stage_ref, *, tm):
    cos = cos_ref[...]
    sin = sin_ref[...]
    lane = lax.broadcasted_iota(jnp.int32, (1, LANES), 1)
    first_half = (lane % ATT_HEAD_DIM) < (ATT_HEAD_DIM // 2)
    slot = 0
    for part in range(3):
        for gi, (_, dil) in enumerate(ATT_GROUPS):
            col = part * ATT_QKV_WIDTH + gi * ATT_GROUP_WIDTH
            y = _dot(h, wm_ref[:, col:col + ATT_GROUP_WIDTH])
            yield
            for j in range(ATT_GROUP_WIDTH // LANES):
                blk = y[:, j * LANES:(j + 1) * LANES]
                if part < 2:
                    swapped = jnp.where(first_half, pltpu.roll(blk, LANES - ATT_HEAD_DIM // 2, 1),
                                        pltpu.roll(blk, ATT_HEAD_DIM // 2, 1))
                    blk = blk * cos + swapped * sin
                if part == 0:
                    blk = blk * (ATT_HEAD_DIM ** -0.5)
                dst = part * ATT_GROUP_WIDTH + j * LANES
                if dil == 1:
                    att_refs[gi][:, dst:dst + LANES] = blk.astype(BF16)
                else:
                    stage_ref[slot] = blk
                    for r in range(dil):
                        rows = stage_ref[slot, pl.ds(r, tm // dil, stride=dil), :]
                        lo = r * 3 * ATT_GROUP_WIDTH + dst
                        att_refs[gi][:, lo:lo + LANES] = rows.astype(BF16)
                    slot += 1
                yield


ATT_QKV_STAGES = 3 * len(ATT_GROUPS) * (1 + ATT_GROUP_WIDTH // LANES)
MIXER_IN_STAGED = 3 * (len(ATT_GROUPS) - 1) * (ATT_GROUP_WIDTH // LANES)


def _deltanet_qkv_stages(h_perm, wm_ref, wbd_ref, convw_ref, alog_ref, dtb_ref, gdn_refs, bd_ref, carry_ref, *, tm):
    lane = lax.broadcasted_iota(jnp.int32, (1, LANES), 1)
    raw = _dot(h_perm, wbd_ref[...])
    yield
    z = raw + dtb_ref[...]
    softplus = jnp.maximum(z, 0.0) + jnp.log1p(jnp.exp(-jnp.abs(z)))
    g = -jnp.exp(alog_ref[...]) * softplus
    bd_ref[...] = jnp.where(lane < GDN_HEADS, _sigmoid(raw), jnp.where(lane < 2 * GDN_HEADS, g, 0.0))

    vregs = GDN_CHUNK // SUBLANES
    halo = GDN_CONV - 1
    chunks = tm // GDN_CHUNK
    last_sublane = lax.broadcasted_iota(jnp.int32, (SUBLANES, GDN_HEAD_DIM), 0) == SUBLANES - 1
    heads_per_block = QKV_BLOCK // GDN_HEAD_DIM
    for part in range(3):
        for blk in range(GDN_WIDTH // QKV_BLOCK):
            base = part * GDN_WIDTH + blk * QKV_BLOCK
            y = _dot(h_perm, wm_ref[:, W_IN_GDN + base:W_IN_GDN + base + QKV_BLOCK])
            yield
            for hb in range(heads_per_block):
                hh = blk * heads_per_block + hb
                col = part * GDN_WIDTH + hh * GDN_HEAD_DIM
                cur = y[:, hb * GDN_HEAD_DIM:(hb + 1) * GDN_HEAD_DIM]
                vreg = lambda c, v: cur[c * GDN_CHUNK + v * SUBLANES:c * GDN_CHUNK + (v + 1) * SUBLANES]
                prev_tile = carry_ref[:, col:col + GDN_HEAD_DIM]
                carry_ref[:, col:col + GDN_HEAD_DIM] = cur[tm - halo * SUBLANES:, :]
                wrapped = []
                for c in range(chunks):
                    row = []
                    for i in range(halo):
                        before = (prev_tile[i * SUBLANES:(i + 1) * SUBLANES] if c == 0
                                  else vreg(c - 1, vregs - halo + i))
                        own = vreg(c, vregs - halo + i)
                        row.append(pltpu.roll(jnp.where(last_sublane, before, own), 1, 0))
                    wrapped.append(row)
                acc = cur * convw_ref[halo:halo + 1, col:col + GDN_HEAD_DIM]
                for shift in range(1, GDN_CONV):
                    pieces = []
                    for c in range(chunks):
                        pieces += wrapped[c][halo - shift:]
                        pieces.append(cur[c * GDN_CHUNK:(c + 1) * GDN_CHUNK - shift * SUBLANES])
                    shifted = jnp.concatenate(pieces, axis=0)
                    acc = acc + shifted * convw_ref[halo - shift:halo - shift + 1, col:col + GDN_HEAD_DIM]
                act = acc * _sigmoid(acc)
                if part < 2:
                    act = act * lax.rsqrt(jnp.sum(act * act, axis=-1, keepdims=True) + EPS)
                if part == 0:
                    act = act * (GDN_HEAD_DIM ** -0.5)
                gdn_refs[part][:, hh * GDN_HEAD_DIM:(hh + 1) * GDN_HEAD_DIM] = act
                yield


DELTANET_QKV_STAGES = 1 + 3 * (GDN_WIDTH // QKV_BLOCK) + 3 * GDN_HEADS


def _mixer_in_kernel(x_ref, g_ref, wm_ref, wbd_ref, convw_ref, alog_ref, dtb_ref, cos_ref, sin_ref,
                     a0_ref, a1_ref, a2_ref, qb_ref, kb_ref, vb_ref, bd_ref, carry_ref, perm_ref, stage_ref, *, tm):
    @pl.when(pl.program_id(1) == 0)
    def _():
        carry_ref[...] = jnp.zeros(carry_ref.shape, F32)

    hf = _rmsnorm(x_ref[...], g_ref[...])
    h = hf.astype(BF16)
    vregs = GDN_CHUNK // SUBLANES
    for cb in range(D_MODEL // LANES):
        perm_ref[cb] = hf[:, cb * LANES:(cb + 1) * LANES]
    h_perm = jnp.concatenate(
        [jnp.concatenate([perm_ref[cb, pl.ds(c0 + v, SUBLANES, stride=vregs), :]
                          for c0 in range(0, tm, GDN_CHUNK) for v in range(vregs)], axis=0)
         for cb in range(D_MODEL // LANES)], axis=1).astype(BF16)
    gdn = _deltanet_qkv_stages(h_perm, wm_ref, wbd_ref, convw_ref, alog_ref, dtb_ref, (qb_ref, kb_ref, vb_ref), bd_ref,
                               carry_ref, tm=tm)
    att = _attention_qkv_stages(h, wm_ref, cos_ref, sin_ref, (a0_ref, a1_ref, a2_ref), stage_ref, tm=tm)
    _interleave(gdn, DELTANET_QKV_STAGES, att, ATT_QKV_STAGES)


def _mixer_in(x1, norm_g, w_in, conv_w, a_log, dt_bias, cos_t, sin_t, *, tm, seq):
    n = x1.shape[0]
    tiles_per_seq = seq // tm
    tile = lambda rows, w: pl.BlockSpec((rows, w), lambda bi, i: (bi * tiles_per_seq + i, 0))
    table = pl.BlockSpec((tm, LANES), lambda bi, i: (i, 0))
    wq = 3 * ATT_GROUP_WIDTH
    att_specs = [tile(tm // dil, dil * wq) for _, dil in ATT_GROUPS]
    att_shapes = [jax.ShapeDtypeStruct((n // dil, dil * wq), BF16) for _, dil in ATT_GROUPS]
    gdn = jax.ShapeDtypeStruct((n, GDN_WIDTH), F32)
    return pl.pallas_call(
        functools.partial(_mixer_in_kernel, tm=tm),
        grid=(n // seq, tiles_per_seq),
        in_specs=[tile(tm, D_MODEL), _resident((1, D_MODEL)),
                  pl.BlockSpec((D_MODEL, W_IN_BD), lambda bi, i: (0, 0), pipeline_mode=pl.Buffered(1)),
                  pl.BlockSpec((D_MODEL, LANES), lambda bi, i: (0, W_IN_BD // LANES), pipeline_mode=pl.Buffered(1)),
                  _resident((GDN_CONV, 3 * GDN_WIDTH)), _resident((1, LANES)), _resident((1, LANES)), table, table],
        out_specs=att_specs + [tile(tm, GDN_WIDTH)] * 3 + [tile(tm, LANES)],
        out_shape=att_shapes + [gdn] * 3 + [jax.ShapeDtypeStruct((n, LANES), F32)],
        scratch_shapes=[pltpu.VMEM(((GDN_CONV - 1) * SUBLANES, 3 * GDN_WIDTH), F32),
                        pltpu.VMEM((D_MODEL // LANES, tm, LANES), F32), pltpu.VMEM((MIXER_IN_STAGED, tm, LANES), F32)],
        compiler_params=pltpu.CompilerParams(dimension_semantics=("arbitrary", "arbitrary"),
                                             vmem_limit_bytes=VMEM_LIMIT_BYTES),
        name="mixer_in",
    )(x1, norm_g, w_in, w_in, conv_w, a_log, dt_bias, cos_t, sin_t)


ATT_BATCH = 3


def _attention_blocks(items):
    heads = range(ATT_HEADS_PER_GROUP)
    per_half = LANES // ATT_HEAD_DIM
    halves = range(ATT_GROUP_WIDTH // LANES)
    lane = lax.broadcasted_iota(jnp.int32, (1, LANES), 1)
    in_head = [(lane // ATT_HEAD_DIM) == j for j in range(per_half)]
    keep = [jnp.where(in_head[j], 1.0, 0.0).astype(BF16) for j in range(per_half)]
    nq = ATT_BLOCK
    cut = lambda x, i: x[:, i * LANES:(i + 1) * LANES]
    s_all = [[_dot_nt(jnp.concatenate([cut(q, i) * keep[j] for j in range(per_half)], axis=0), cut(k, i))
              for i in halves] for q, k, _, _ in items]
    stats, p_all = [], []
    for (_, _, _, valid), sa in zip(items, s_all):
        s = [jnp.where(valid, sa[hh // per_half][(hh % per_half) * nq:(hh % per_half + 1) * nq], NEG_BIG)
             for hh in heads]
        m = [jnp.max(s[hh], axis=-1, keepdims=True) for hh in heads]
        p = [jnp.exp(s[hh] - m[hh]) for hh in heads]
        l = [jnp.sum(p[hh], axis=-1, keepdims=True) for hh in heads]
        stats.append((m, l))
        p_all.append([jnp.concatenate([p[i * per_half + j].astype(BF16) for j in range(per_half)], axis=0)
                      for i in halves])
    pv_all = [[_dot(ps[i], cut(v, i)) for i in halves] for ps, (_, _, v, _) in zip(p_all, items)]
    outs = []
    for pv, (m, l) in zip(pv_all, stats):
        o_halves, lse_halves = [], []
        for i in halves:
            o = jnp.zeros((ATT_BLOCK, LANES), F32)
            lse = jnp.zeros((ATT_BLOCK, LANES), F32)
            for j in range(per_half):
                hh = i * per_half + j
                o = jnp.where(in_head[j], pv[i][j * nq:(j + 1) * nq] * (1.0 / l[hh]), o)
                lse = jnp.where(in_head[j], m[hh] + jnp.log(l[hh]), lse)
            o_halves.append(o)
            lse_halves.append(lse)
        outs.append((jnp.concatenate(o_halves, axis=1), jnp.concatenate(lse_halves, axis=1)))
    return outs


def _attention_kernel(a0_ref, a1_ref, a2_ref, ya_ref, o0, l0, o1, l1, o2, l2, *, seq):
    in_refs = (a0_ref, a1_ref, a2_ref)
    o_refs = (o0, o1, o2)
    l_refs = (l0, l1, l2)
    qi = lax.broadcasted_iota(jnp.int32, (ATT_BLOCK, ATT_BLOCK), 0)
    kj = lax.broadcasted_iota(jnp.int32, (ATT_BLOCK, ATT_BLOCK), 1)
    causal = kj <= qi
    qi2 = lax.broadcasted_iota(jnp.int32, (ATT_BLOCK, 2 * ATT_BLOCK), 0)
    kj2 = lax.broadcasted_iota(jnp.int32, (ATT_BLOCK, 2 * ATT_BLOCK), 1)
    band = (kj2 >= qi2) & (kj2 - ATT_BLOCK <= qi2)
    wq = 3 * ATT_GROUP_WIDTH

    def load(gi, r, n):
        src, base = in_refs[gi], r * wq
        if isinstance(n, int) and n == 0:
            qrows = krows = slice(0, ATT_BLOCK)
            valid = causal
        else:
            start = lambda x: x if isinstance(x, int) else pl.multiple_of(x, ATT_BLOCK)
            qrows = pl.ds(start(n * ATT_BLOCK), ATT_BLOCK)
            krows = pl.ds(start((n - 1) * ATT_BLOCK), 2 * ATT_BLOCK)
            valid = band
        return (src[0, qrows, base:base + ATT_GROUP_WIDTH],
                src[0, krows, base + ATT_GROUP_WIDTH:base + 2 * ATT_GROUP_WIDTH],
                src[0, krows, base + 2 * ATT_GROUP_WIDTH:base + 3 * ATT_GROUP_WIDTH], valid)

    def store(gi, r, n, o, lse):
        dil = ATT_GROUPS[gi][1]
        if dil == 1:
            first = n * ATT_BLOCK
            rows = pl.ds(first if isinstance(first, int) else pl.multiple_of(first, ATT_BLOCK), ATT_BLOCK)
        else:
            rows = pl.ds(n * ATT_BLOCK * dil + r, ATT_BLOCK, stride=dil)
        for half in range(ATT_GROUP_WIDTH // LANES):
            o_refs[gi][half, rows, :] = o[:, half * LANES:(half + 1) * LANES]
            l_refs[gi][half, rows, :] = lse[:, half * LANES:(half + 1) * LANES]

    def run(blocks):
        for (gi, r, n), (o, lse) in zip(blocks, _attention_blocks([load(*blk) for blk in blocks])):
            store(gi, r, n, o, lse)

    static_blocks = []
    looped = None
    for gi, (window, dil) in enumerate(ATT_GROUPS):
        assert window // dil == ATT_BLOCK
        nblk = seq // dil // ATT_BLOCK
        if dil == 1 and (nblk - 1) % ATT_BATCH == 0:
            static_blocks.append((gi, 0, 0))
            looped = (gi, nblk)
        else:
            static_blocks += [(gi, r, n) for r in range(dil) for n in range(nblk)]
    for i in range(0, len(static_blocks), ATT_BATCH + 1):
        run(static_blocks[i:i + ATT_BATCH + 1])
    if looped is not None:
        gi, nblk = looped

        def body(i, carry):
            run([(gi, 0, 1 + i * ATT_BATCH + j) for j in range(ATT_BATCH)])
            return carry
        lax.fori_loop(0, (nblk - 1) // ATT_BATCH, body, 0)

    rows_per_step = 256

    def merge(i, carry):
        rows = pl.ds(pl.multiple_of(i * rows_per_step, rows_per_step), rows_per_step)
        for half in range(ATT_GROUP_WIDTH // LANES):
            la, lb, lc = l0[half, rows, :], l1[half, rows, :], l2[half, rows, :]
            m = jnp.maximum(jnp.maximum(la, lb), lc)
            ea, eb, ec = jnp.exp(la - m), jnp.exp(lb - m), jnp.exp(lc - m)
            num = ea * o0[half, rows, :] + eb * o1[half, rows, :] + ec * o2[half, rows, :]
            ya_ref[0, rows, half * LANES:(half + 1) * LANES] = num / (ea + eb + ec)
        return carry
    lax.fori_loop(0, seq // rows_per_step, merge, 0)


def _attention(a0, a1, a2, *, batch):
    views = tuple(a.reshape(batch, a.shape[0] // batch, a.shape[1]) for a in (a0, a1, a2))
    b, s, _ = views[0].shape
    specs = [pl.BlockSpec((1,) + arr.shape[1:], lambda bi: (bi, 0, 0)) for arr in views]
    scratch = [pltpu.VMEM((ATT_GROUP_WIDTH // LANES, s, LANES), F32) for _ in range(6)]
    return pl.pallas_call(
        functools.partial(_attention_kernel, seq=s),
        grid=(b,),
        in_specs=specs,
        out_specs=pl.BlockSpec((1, s, ATT_GROUP_WIDTH), lambda bi: (bi, 0, 0)),
        out_shape=jax.ShapeDtypeStruct((b, s, ATT_GROUP_WIDTH), F32),
        scratch_shapes=scratch,
        compiler_params=pltpu.CompilerParams(dimension_semantics=("arbitrary",),
                                             vmem_limit_bytes=VMEM_LIMIT_BYTES),
        name="dilated_attention",
    )(*views)


def _deltanet_stages(q_ref, k_ref, v_ref, bd_ref, gnorm, state_ref, ob_ref, slot, *, tile):
    c = GDN_CHUNK
    d = GDN_HEAD_DIM
    heads = range(GDN_HEADS)
    pairs = [(2 * pp, 2 * pp + 1) for pp in range(GDN_HEADS // 2)]
    cols = [slice(hh * d, (hh + 1) * d) for hh in heads]
    glane = [GDN_HEADS + hh for hh in heads]
    ii = _chunk_time(lax.broadcasted_iota(jnp.int32, (c, 2 * c), 0))
    ll = lax.broadcasted_iota(jnp.int32, (c, 2 * c), 1)
    jj = _chunk_time(ll % c)
    lower = ii >= jj
    strict = ii > jj
    left = ll < c
    left_row = lax.broadcasted_iota(jnp.int32, (1, 2 * c), 1) < c
    keep_left = jnp.where(left, 1.0, 0.0).astype(BF16)
    keep_right = jnp.where(left, 0.0, 1.0).astype(BF16)
    ti = _chunk_time(lax.broadcasted_iota(jnp.int32, (c, c), 0))
    tj = _chunk_time(lax.broadcasted_iota(jnp.int32, (c, c), 1))
    tri_ones = jnp.where(ti >= tj, 1.0, 0.0).astype(BF16)

    def blockdiag(x):
        return jnp.concatenate([x * keep_left, x * keep_right], axis=0)

    def stack_diag(xa, xb):
        zero = jnp.zeros_like(xa)
        return jnp.concatenate([jnp.concatenate([xa, zero], axis=1), jnp.concatenate([zero, xb], axis=1)], axis=0)

    def prepare(ci, out):
        rows = slice(ci * c, (ci + 1) * c)
        bd = bd_ref[rows, :]
        bd_hi = bd.astype(BF16)
        bd_rest = bd - bd_hi.astype(F32)
        bd_mid = bd_rest.astype(BF16)
        bd_lo = (bd_rest - bd_mid.astype(F32)).astype(BF16)
        gcum = _dot(tri_ones, bd_hi) + _dot(tri_ones, bd_mid) + _dot(tri_ones, bd_lo)
        yield
        gtot = jnp.broadcast_to(gcum[c - 1:c, :], (c, LANES))
        gcum_t = jnp.concatenate([gcum, gcum], axis=0).T
        e_cum_all = jnp.exp(gcum)
        e_rest_all = jnp.exp(gtot - gcum)
        e_tot_all = jnp.exp(gtot)
        q = [q_ref[rows, cols[hh]] for hh in heads]
        k = [k_ref[rows, cols[hh]] for hh in heads]
        v = [v_ref[rows, cols[hh]] for hh in heads]
        beta = [bd[:, hh:hh + 1] for hh in heads]
        e_cum = [e_cum_all[:, gl:gl + 1] for gl in glane]
        kbeta = [k[hh] * beta[hh] for hh in heads]
        kq = [_dot_nt(jnp.concatenate([jnp.concatenate([kbeta[a], kbeta[b]], axis=1),
                                       jnp.concatenate([q[a], q[b]], axis=1)], axis=0).astype(BF16),
                      stack_diag(k[a].astype(BF16), k[b].astype(BF16)))
              for a, b in pairs]
        yield
        decay = [jnp.exp(jnp.where(lower,
                                   jnp.where(left, gcum[:, glane[a]:glane[a] + 1], gcum[:, glane[b]:glane[b] + 1])
                                   - jnp.where(left_row, gcum_t[glane[a]:glane[a] + 1, :], gcum_t[glane[b]:glane[b] + 1, :]),
                                   NEG_BIG)) for a, b in pairs]
        m = [jnp.where(strict, kq[pp][0:c] * decay[pp], 0.0) for pp in range(len(pairs))]
        n = [-mm for mm in m]
        pb = [mm.astype(BF16) for mm in m]
        p = [_dot(x, blockdiag(x)) for x in pb]
        yield
        rounds = 5
        for r in range(rounds):
            pb = [x.astype(BF16) for x in p]
            upd = [_dot(x, blockdiag(y.astype(BF16))) for x, y in zip(pb, n)]
            p_next = [_dot(x, blockdiag(x)) for x in pb] if r + 1 < rounds else None
            yield
            n = [y + x + u for y, x, u in zip(n, p, upd)]
            p = p_next
        rhs = [jnp.concatenate([v[hh] * beta[hh], kbeta[hh] * e_cum[hh]], axis=1) for hh in heads]
        nr = [_dot(n[pp].astype(BF16), stack_diag(rhs[a].astype(BF16), rhs[b].astype(BF16)))
              for pp, (a, b) in enumerate(pairs)]
        yield
        sol = [rhs[hh] + nr[hh // 2][:, (hh % 2) * 2 * d:(hh % 2 + 1) * 2 * d] for hh in heads]
        out.update(
            first=ci * c,
            u=[sol[hh][:, 0:d] for hh in heads],
            wq=[jnp.concatenate([sol[hh][:, d:2 * d], q[hh] * e_cum[hh]], axis=0).astype(BF16) for hh in heads],
            a_qk=[(kq[pp][c:2 * c] * decay[pp]).astype(BF16) for pp in range(len(pairs))],
            k_dec=[(k[hh] * e_rest_all[:, gl:gl + 1]).astype(BF16) for hh, gl in zip(heads, glane)],
            e_tot=[e_tot_all[0:1, gl:gl + 1] for gl in glane])

    for first in range(0, tile // c, GDN_GROUP):
        group = [dict() for _ in range(GDN_GROUP)]
        gens = [prepare(first + gi, group[gi]) for gi in range(GDN_GROUP)]
        for _ in range(GDN_PREP_LAYERS):
            for gen in gens:
                next(gen)
            yield
        for gen in gens:
            for _ in gen:
                pass
        for pre in group:
            state = [state_ref[hh] for hh in heads]
            ws = [_dot(pre["wq"][hh], state[hh].astype(BF16)) for hh in heads]
            yield
            v_new = [(pre["u"][hh] - ws[hh][0:c]).astype(BF16) for hh in heads]
            kv = [_dot_tn(pre["k_dec"][hh], v_new[hh]) for hh in heads]
            av = [_dot(pre["a_qk"][pp], stack_diag(v_new[a], v_new[b])) for pp, (a, b) in enumerate(pairs)]
            yield
            for hh in heads:
                state_ref[hh] = state[hh] * pre["e_tot"][hh] + kv[hh]
            for hh in heads:
                o = _rmsnorm(ws[hh][c:2 * c] + av[hh // 2][:, (hh % 2) * d:(hh % 2 + 1) * d], gnorm)
                for vv in range(c // SUBLANES):
                    ob_ref[slot, hh, pl.ds(pre["first"] + vv, SUBLANES, stride=c // SUBLANES), :] = (
                        o[vv * SUBLANES:(vv + 1) * SUBLANES])


GDN_GROUP = 4
GDN_PREP_LAYERS = 9
GDN_LAYERS_PER_GROUP = GDN_PREP_LAYERS + 2 * GDN_GROUP


def _mixer_out_stages(x_ref, ya_ref, ob_ref, slot, g_ref, wgt_ref, wa_ref, wb_ref, wo_ref, o_ref):
    x = x_ref[...]
    h = _rmsnorm(x, g_ref[...]).astype(BF16)
    ya = ya_ref[...].astype(BF16)
    blocks = [slice(j * MIX_BLOCK, (j + 1) * MIX_BLOCK) for j in range(D_MODEL // MIX_BLOCK)]
    gate_cols = lambda which, blk: slice(which * D_MODEL + blk.start, which * D_MODEL + blk.stop)
    yb = []
    for blk in blocks:
        gdn_gate = _dot(h, wgt_ref[:, gate_cols(0, blk)])
        yield
        ob = jnp.concatenate([ob_ref[slot, hh] for hh in range(blk.start // GDN_HEAD_DIM, blk.stop // GDN_HEAD_DIM)],
                             axis=1)
        yb.append((ob * (gdn_gate * _sigmoid(gdn_gate))).astype(BF16))
    yb = jnp.concatenate(yb, axis=1)
    merged = []
    for blk in blocks:
        gate_a = _dot(h, wgt_ref[:, gate_cols(1, blk)])
        branch_a = _dot(ya, wa_ref[:, blk])
        yield
        gate_b = _dot(h, wgt_ref[:, gate_cols(2, blk)])
        yield
        branch_b = _dot(yb, wb_ref[:, blk])
        yield
        merged.append((_sigmoid(gate_a) * branch_a + _sigmoid(gate_b) * branch_b).astype(BF16))
    merged = jnp.concatenate(merged, axis=1)
    for blk in blocks:
        o_ref[:, blk] = x[:, blk] + _dot(merged, wo_ref[:, blk])
        yield


MIX_GRANULES = 5 * (D_MODEL // MIX_BLOCK)


def _mixer_tail_kernel(q_ref, k_ref, v_ref, bd_ref, gn_ref, x_ref, ya_ref, g_ref, wgt_ref, wa_ref, wb_ref, wo_ref,
                       *refs, tile, tiles_per_seq, n_tiles):
    n_cast = (len(refs) - 3) // 2
    o_ref, state_ref, ob_ref = refs[n_cast], refs[2 * n_cast + 1], refs[2 * n_cast + 2]
    step = pl.program_id(0)

    @pl.when(jnp.minimum(step, n_tiles - 1) % tiles_per_seq == 0)
    def _():
        state_ref[...] = jnp.zeros(state_ref.shape, F32)

    slot = step % 2
    gdn = lambda: _deltanet_stages(q_ref, k_ref, v_ref, bd_ref, gn_ref[...], state_ref, ob_ref, slot, tile=tile)
    mix = lambda: _mixer_out_stages(x_ref, ya_ref, ob_ref, 1 - slot, g_ref, wgt_ref, wa_ref, wb_ref, wo_ref, o_ref)

    @pl.when(step == 0)
    def _():
        for _ in gdn():
            pass

    @pl.when(step == n_tiles)
    def _():
        for _ in mix():
            pass

    @pl.when(jnp.logical_and(step > 0, step < n_tiles))
    def _():
        _interleave(gdn(), GDN_LAYERS_PER_GROUP * (tile // (GDN_CHUNK * GDN_GROUP)), mix(), MIX_GRANULES)

    _cast_rows(refs[:n_cast], refs[n_cast + 1:2 * n_cast + 1])


def _mixer_tail(qb, kb, vb, bd, out_norm, x1, ya, norm_g, w_gates, w_a, w_b, w_o, *, tile, seq, cast=(), layer=0):
    n = x1.shape[0]
    n_tiles = n // tile
    cast_in, cast_out, cast_shapes = _cast_specs(cast, layer, n_tiles)
    cur = lambda w: pl.BlockSpec((tile, w), lambda s: (jnp.minimum(s, n_tiles - 1), 0))
    prev = lambda w: pl.BlockSpec((tile, w), lambda s: (jnp.maximum(s - 1, 0), 0))
    out = pl.pallas_call(
        functools.partial(_mixer_tail_kernel, tile=tile, tiles_per_seq=seq // tile, n_tiles=n_tiles),
        grid=(n_tiles + 1,),
        in_specs=[cur(GDN_WIDTH), cur(GDN_WIDTH), cur(GDN_WIDTH), cur(LANES), _resident((1, GDN_HEAD_DIM)),
                  prev(D_MODEL), prev(ATT_GROUP_WIDTH), _resident((1, D_MODEL)),
                  _resident(w_gates.shape), _resident(w_a.shape), _resident(w_b.shape), _resident(w_o.shape)]
                 + cast_in,
        out_specs=[prev(D_MODEL)] + cast_out,
        out_shape=[jax.ShapeDtypeStruct((n, D_MODEL), F32)] + cast_shapes,
        scratch_shapes=[pltpu.VMEM((GDN_HEADS, GDN_HEAD_DIM, GDN_HEAD_DIM), F32),
                        pltpu.VMEM((2, GDN_HEADS, tile, GDN_HEAD_DIM), F32)],
        compiler_params=pltpu.CompilerParams(dimension_semantics=("arbitrary",),
                                             vmem_limit_bytes=VMEM_LIMIT_BYTES),
        name="deltanet_mixer_out",
    )(qb, kb, vb, bd, out_norm, x1, ya, norm_g, w_gates, w_a, w_b, w_o, *cast)
    return out[0], out[1:]


def _rope_tables(seq):
    half = ATT_HEAD_DIM // 2
    inv_freq = ROPE_THETA ** (-jnp.arange(half, dtype=F32) / half)
    ang = jnp.arange(seq, dtype=F32)[:, None] * inv_freq[None, :]
    cos, sin = jnp.cos(ang), jnp.sin(ang)
    reps = LANES // ATT_HEAD_DIM
    return jnp.tile(jnp.concatenate([cos, cos], axis=-1), (1, reps)), jnp.tile(jnp.concatenate([-sin, sin], axis=-1), (1, reps))


def _pad_lanes(row, offset):
    return jnp.zeros((1, LANES), F32).at[0, offset:offset + row.shape[0]].set(row.astype(F32))


def _layer(x, ffn1_norm, ffn1_w_gate, ffn1_w_up, ffn1_w_down, mix_norm, w_in_all, gdn_conv_w, gdn_a_log, gdn_dt_bias,
           gdn_out_norm, w_branch_a, w_branch_b, w_out, ffn2_norm, ffn2_all, fin_g,
           *, layer, final_norm, tm_ffn, tm_mix, gdn_tile):
    b, s, _ = x.shape
    n = b * s
    row = lambda v: v.reshape(1, -1).astype(F32)
    x1 = _ffn(x.reshape(n, D_MODEL), row(ffn1_norm), ffn1_w_gate.astype(BF16), ffn1_w_up.astype(BF16),
              ffn1_w_down.astype(BF16), fin_g, final_norm=False, tm=tm_ffn)
    w_in = w_in_all[layer].astype(BF16)
    w_gates = w_in[:, W_IN_GATES:]
    cos_t, sin_t = _rope_tables(s)
    a0, a1, a2, qb, kb, vb, bd = _mixer_in(
        x1, row(mix_norm), w_in, gdn_conv_w.astype(F32), _pad_lanes(gdn_a_log, GDN_HEADS),
        _pad_lanes(gdn_dt_bias, GDN_HEADS), cos_t, sin_t, tm=tm_mix, seq=s)

    ya = _attention(a0, a1, a2, batch=b)
    x2, ffn2_w = _mixer_tail(qb, kb, vb, bd, row(gdn_out_norm), x1, ya.reshape(n, ATT_GROUP_WIDTH), row(mix_norm),
                             w_gates, w_branch_a.astype(BF16), w_branch_b.astype(BF16), w_out.astype(BF16),
                             tile=gdn_tile, seq=s, cast=ffn2_all, layer=layer)
    x3 = _ffn(x2, row(ffn2_norm), *ffn2_w, fin_g, final_norm=final_norm, tm=tm_ffn)
    return x3.reshape(b, s, D_MODEL)


def kernel(x, ffn1_norm, ffn1_w_gate, ffn1_w_up, ffn1_w_down, mix_norm, w_in, gdn_conv_w, gdn_a_log, gdn_dt_bias,
           gdn_out_norm, w_branch_a, w_branch_b, w_out, ffn2_norm, ffn2_w_gate, ffn2_w_up, ffn2_w_down, final_norm):
    depth = ffn1_norm.shape[0]
    fin_g = final_norm.reshape(1, -1).astype(F32)
    for layer in range(depth):
        x = _layer(x, ffn1_norm[layer], ffn1_w_gate[layer], ffn1_w_up[layer], ffn1_w_down[layer], mix_norm[layer],
                   w_in, gdn_conv_w[layer], gdn_a_log[layer], gdn_dt_bias[layer], gdn_out_norm[layer],
                   w_branch_a[layer], w_branch_b[layer], w_out[layer], ffn2_norm[layer],
                   (ffn2_w_gate, ffn2_w_up, ffn2_w_down), fin_g, layer=layer, final_norm=(layer == depth - 1),
                   tm_ffn=TOKEN_TILE, tm_mix=TOKEN_TILE, gdn_tile=TOKEN_TILE)
    return x
```
